```python
import jax, jax.numpy as jnp
from jax import lax
import numpy as np

D_MODEL = 1024
BATCH = 8
SEQ = 16384
DEPTH = 4

CONV_DIM = 512
CONV_WIDTH = 31
SC_DIM = 512
SC_WIDTH = 3
MLA_HEADS = 8
QK_NOPE_DIM = 64
QK_ROPE_DIM = 32
V_HEAD_DIM = 64
Q_LORA_RANK = 256
KV_LORA_RANK = 128
ROPE_THETA = 10000.0
ATTN_BLOCK = 128
POOL_WINDOWS = (2, 4, 8, 16)
POOL_GROUPS = 4
POOL_DIM = 512
POOL_GROUP_DIM = POOL_DIM // POOL_GROUPS
N_BRANCHES = 4
D_FF = 2816
FFN_CONV_WIDTH = 3
LN_EPS = 1e-5
RMS_EPS = 1e-6
DEEPNORM_ALPHA = (2.0 * DEPTH) ** 0.25
DEEPNORM_BETA = (8.0 * DEPTH) ** -0.25

OFF_CONV_A = CONV_DIM
OFF_CONV_B = OFF_CONV_A + CONV_DIM
OFF_SC_B = OFF_CONV_B + SC_DIM
OFF_SC_C = OFF_SC_B + SC_DIM
OFF_SC_X = OFF_SC_C + SC_DIM
OFF_Q_LAT = OFF_SC_X + Q_LORA_RANK
OFF_KV_LAT = OFF_Q_LAT + KV_LORA_RANK
OFF_K_ROPE = OFF_KV_LAT + QK_ROPE_DIM
OFF_POOL = OFF_K_ROPE + POOL_DIM
IN_COLS = OFF_POOL + N_BRANCHES * D_MODEL
IN_OFFSETS = (OFF_CONV_A, OFF_CONV_B, OFF_SC_B, OFF_SC_C, OFF_SC_X, OFF_Q_LAT, OFF_KV_LAT, OFF_K_ROPE, OFF_POOL)

kernel_name = 'hybrid_gated_conv_mla_pool_deepnorm'


def layer_norm(x, g, b):
    xf = x.astype(jnp.float32)
    mu = jnp.mean(xf, axis=-1, keepdims=True)
    var = jnp.mean(jnp.square(xf - mu), axis=-1, keepdims=True)
    y = (xf - mu) * lax.rsqrt(var + LN_EPS)
    return (y * g.astype(jnp.float32) + b.astype(jnp.float32)).astype(x.dtype)


def rms_norm(x, g):
    xf = x.astype(jnp.float32)
    y = xf * lax.rsqrt(jnp.mean(jnp.square(xf), axis=-1, keepdims=True) + RMS_EPS)
    return (y * g.astype(jnp.float32)).astype(x.dtype)


def causal_dwconv(x, w):
    k, c = w.shape
    return lax.conv_general_dilated(
        x, w[:, None, :].astype(x.dtype), window_strides=(1,), padding=[(k - 1, 0)],
        dimension_numbers=('NWC', 'WIO', 'NWC'), feature_group_count=c)


def rope_tables(positions):
    inv = 1.0 / (ROPE_THETA ** (jnp.arange(0, QK_ROPE_DIM, 2, dtype=jnp.float32) / QK_ROPE_DIM))
    ang = positions.astype(jnp.float32)[..., None] * inv
    return jnp.cos(ang), jnp.sin(ang)


def apply_rope(x, cos, sin):
    half = QK_ROPE_DIM // 2
    xf = x.astype(jnp.float32)
    x1, x2 = xf[..., :half], xf[..., half:]
    return jnp.concatenate([x1 * cos - x2 * sin, x2 * cos + x1 * sin], axis=-1).astype(x.dtype)


def mla_attention(q_nope, q_rope, k_nope, k_rope, v):
    b, s, h, _ = q_nope.shape
    nb = s // ATTN_BLOCK
    scale = (QK_NOPE_DIM + QK_ROPE_DIM) ** -0.5
    key_idx = jnp.arange(s)

    def to_blocks(t):
        return jnp.moveaxis(t.reshape(b, nb, ATTN_BLOCK, *t.shape[2:]), 1, 0)

    def one_block(args):
        qn, qr, blk = args
        sc = (jnp.einsum('bqhd,bkhd->bhqk', qn, k_nope, preferred_element_type=jnp.float32)
              + jnp.einsum('bqhd,bkd->bhqk', qr, k_rope, preferred_element_type=jnp.float32))
        q_idx = blk * ATTN_BLOCK + jnp.arange(ATTN_BLOCK)
        mask = key_idx[None, :] <= q_idx[:, None]
        p = jax.nn.softmax(jnp.where(mask, sc * scale, -jnp.inf), axis=-1)
        return jnp.einsum('bhqk,bkhd->bqhd', p.astype(v.dtype), v)

    out = lax.map(one_block, (to_blocks(q_nope), to_blocks(q_rope), jnp.arange(nb)))
    return jnp.moveaxis(out, 0, 1).reshape(b, s, h * V_HEAD_DIM)


def multiscale_pool(u):
    s = u.shape[1]
    uf = u.astype(jnp.float32)
    cs = jnp.cumsum(uf, axis=1)
    t = jnp.arange(s)
    means = []
    for g, w in enumerate(POOL_WINDOWS):
        cg = cs[..., g * POOL_GROUP_DIM:(g + 1) * POOL_GROUP_DIM]
        prev = jnp.pad(cg, ((0, 0), (w, 0), (0, 0)))[:, :s]
        cnt = jnp.minimum(t + 1, w).astype(jnp.float32)[None, :, None]
        means.append((cg - prev) / cnt)
    return (jnp.concatenate(means, axis=-1) - uf).astype(u.dtype)


def _fwd_setup_inputs(seed: int = 0) -> dict:
    key = jax.random.key(seed)
    ks = iter(jax.random.split(key, 40))
    L, D = DEPTH, D_MODEL
    beta = DEEPNORM_BETA

    def nrm(shape, scale):
        return scale * jax.random.normal(next(ks), shape, jnp.float32)

    x = nrm((BATCH, SEQ, D), 1.0)
    c = nrm((BATCH, D), 1.0)
    offsets = jax.random.randint(next(ks), (BATCH, 1), 0, 4096, dtype=jnp.int32)
    positions = offsets + jnp.arange(SEQ, dtype=jnp.int32)[None, :]
    return {
        'x': x,
        'c': c,
        'positions': positions,
        'w_ada': nrm((L, D, 6 * D), 0.1 * D ** -0.5),
        'b_ada': nrm((L, 6 * D), 0.02),
        'w_in': nrm((L, D, IN_COLS), D ** -0.5),
        'b_in': nrm((L, IN_COLS), 0.02),
        'conv_dw': nrm((L, CONV_WIDTH, CONV_DIM), CONV_WIDTH ** -0.5),
        'conv_ln_g': 1.0 + nrm((L, CONV_DIM), 0.1),
        'conv_ln_b': nrm((L, CONV_DIM), 0.02),
        'w_conv_out': nrm((L, CONV_DIM, D), beta * CONV_DIM ** -0.5),
        'sc_dw': nrm((L, SC_WIDTH, SC_DIM), SC_WIDTH ** -0.5),
        'w_sc_out': nrm((L, SC_DIM, D), beta * SC_DIM ** -0.5),
        'q_norm_g': 1.0 + nrm((L, Q_LORA_RANK), 0.1),
        'w_uq': nrm((L, Q_LORA_RANK, MLA_HEADS * (QK_NOPE_DIM + QK_ROPE_DIM)), Q_LORA_RANK ** -0.5),
        'kv_norm_g': 1.0 + nrm((L, KV_LORA_RANK), 0.1),
        'w_ukv': nrm((L, KV_LORA_RANK, MLA_HEADS * (QK_NOPE_DIM + V_HEAD_DIM)), KV_LORA_RANK ** -0.5),
        'w_mla_out': nrm((L, MLA_HEADS * V_HEAD_DIM, D), beta * (MLA_HEADS * V_HEAD_DIM) ** -0.5),
        'w_pool': nrm((L, POOL_GROUPS, POOL_GROUP_DIM, POOL_GROUP_DIM), POOL_GROUP_DIM ** -0.5),
        'pool_scale': 1.0 + nrm((L, POOL_DIM), 0.1),
        'w_pool_out': nrm((L, POOL_DIM, D), beta * POOL_DIM ** -0.5),
        'w_o': nrm((L, D, D), beta * D ** -0.5),
        'ln1_g': 1.0 + nrm((L, D), 0.1),
        'ln1_b': nrm((L, D), 0.02),
        'w_up': nrm((L, D, 2 * D_FF), D ** -0.5),
        'ffn_dw': nrm((L, FFN_CONV_WIDTH, 2 * D_FF), FFN_CONV_WIDTH ** -0.5),
        'w_down': nrm((L, D_FF, D), beta * D_FF ** -0.5),
        'ln2_g': 1.0 + nrm((L, D), 0.1),
        'ln2_b': nrm((L, D), 0.02),
    }


def _fwd_reference(x, c, positions, w_ada, b_ada, w_in, b_in, conv_dw, conv_ln_g, conv_ln_b, w_conv_out,
              sc_dw, w_sc_out, q_norm_g, w_uq, kv_norm_g, w_ukv, w_mla_out, w_pool, pool_scale,
              w_pool_out, w_o, ln1_g, ln1_b, w_up, ffn_dw, w_down, ln2_g, ln2_b):
    b, s, d = x.shape
    cos, sin = rope_tables(positions)
    c_act = jax.nn.silu(c)
    for l in range(DEPTH):
        mod = (c_act @ w_ada[l] + b_ada[l])[:, None, :]
        sh1, sc1, g1, sh2, sc2, g2 = jnp.split(mod, 6, axis=-1)

        h = x * (1.0 + sc1) + sh1
        proj = h @ w_in[l] + b_in[l]
        conv_a, conv_b, sc_bg, sc_cg, sc_x, q_lat, kv_lat, k_rope, pool_u, gates = jnp.split(
            proj, IN_OFFSETS, axis=-1)

        ya = conv_a * jax.nn.sigmoid(conv_b)
        ya = causal_dwconv(ya, conv_dw[l])
        ya = jax.nn.silu(layer_norm(ya, conv_ln_g[l], conv_ln_b[l]))
        ya = ya @ w_conv_out[l]

        yb = (sc_bg * causal_dwconv(sc_cg * sc_x, sc_dw[l])) @ w_sc_out[l]

        q = (rms_norm(q_lat, q_norm_g[l]) @ w_uq[l]).reshape(b, s, MLA_HEADS, QK_NOPE_DIM + QK_ROPE_DIM)
        q_nope = q[..., :QK_NOPE_DIM]
        q_rope = apply_rope(q[..., QK_NOPE_DIM:], cos[:, :, None, :], sin[:, :, None, :])
        kv = (rms_norm(kv_lat, kv_norm_g[l]) @ w_ukv[l]).reshape(b, s, MLA_HEADS, QK_NOPE_DIM + V_HEAD_DIM)
        k_nope, v = kv[..., :QK_NOPE_DIM], kv[..., QK_NOPE_DIM:]
        k_rope_r = apply_rope(k_rope, cos, sin)
        yc = mla_attention(q_nope, q_rope, k_nope, k_rope_r, v) @ w_mla_out[l]

        pd = multiscale_pool(pool_u).reshape(b, s, POOL_GROUPS, POOL_GROUP_DIM)
        yd = jnp.einsum('bsgc,gcd->bsgd', pd, w_pool[l]).reshape(b, s, POOL_DIM) * pool_scale[l]
        yd = yd @ w_pool_out[l]

        gt = jax.nn.sigmoid(gates.astype(jnp.float32)).astype(x.dtype).reshape(b, s, N_BRANCHES, d)
        merged = gt[:, :, 0] * ya + gt[:, :, 1] * yb + gt[:, :, 2] * yc + gt[:, :, 3] * yd
        mix = merged @ w_o[l]
        x = layer_norm(DEEPNORM_ALPHA * x + (1.0 + g1) * mix, ln1_g[l], ln1_b[l])

        h = x * (1.0 + sc2) + sh2
        up = causal_dwconv(h @ w_up[l], ffn_dw[l])
        val, gate = jnp.split(up, 2, axis=-1)
        ffn = (jax.nn.silu(gate) * val) @ w_down[l]
        x = layer_norm(DEEPNORM_ALPHA * x + (1.0 + g2) * ffn, ln2_g[l], ln2_b[l])
    return x


import jax as _jax
import jax.numpy as _jnp

TWIN_FORMAT = 'train_step'
FWD_PARAMS = ['x', 'c', 'positions', 'w_ada', 'b_ada', 'w_in', 'b_in', 'conv_dw', 'conv_ln_g', 'conv_ln_b', 'w_conv_out', 'sc_dw', 'w_sc_out', 'q_norm_g', 'w_uq', 'kv_norm_g', 'w_ukv', 'w_mla_out', 'w_pool', 'pool_scale', 'w_pool_out', 'w_o', 'ln1_g', 'ln1_b', 'w_up', 'ffn_dw', 'w_down', 'ln2_g', 'ln2_b']
TWIN_WEIGHTS = ['w_ada', 'b_ada', 'w_in', 'b_in', 'conv_dw', 'conv_ln_g', 'conv_ln_b', 'w_conv_out', 'sc_dw', 'w_sc_out', 'q_norm_g', 'w_uq', 'kv_norm_g', 'w_ukv', 'w_mla_out', 'w_pool', 'pool_scale', 'w_pool_out', 'w_o', 'ln1_g', 'ln1_b', 'w_up', 'ffn_dw', 'w_down', 'ln2_g', 'ln2_b']
TWIN_DIFF_INPUT = 'x'
TWIN_INPUTS = ['x', 'c', 'positions', 'w_ada', 'b_ada', 'w_in', 'b_in', 'conv_dw', 'conv_ln_g', 'conv_ln_b', 'w_conv_out', 'sc_dw', 'w_sc_out', 'q_norm_g', 'w_uq', 'kv_norm_g', 'w_ukv', 'w_mla_out', 'w_pool', 'pool_scale', 'w_pool_out', 'w_o', 'ln1_g', 'ln1_b', 'w_up', 'ffn_dw', 'w_down', 'ln2_g', 'ln2_b', 'loss_target', 'm_w_ada', 'm_b_ada', 'm_w_in', 'm_b_in', 'm_conv_dw', 'm_conv_ln_g', 'm_conv_ln_b', 'm_w_conv_out', 'm_sc_dw', 'm_w_sc_out', 'm_q_norm_g', 'm_w_uq', 'm_kv_norm_g', 'm_w_ukv', 'm_w_mla_out', 'm_w_pool', 'm_pool_scale', 'm_w_pool_out', 'm_w_o', 'm_ln1_g', 'm_ln1_b', 'm_w_up', 'm_ffn_dw', 'm_w_down', 'm_ln2_g', 'm_ln2_b', 'v_w_ada', 'v_b_ada', 'v_w_in', 'v_b_in', 'v_conv_dw', 'v_conv_ln_g', 'v_conv_ln_b', 'v_w_conv_out', 'v_sc_dw', 'v_w_sc_out', 'v_q_norm_g', 'v_w_uq', 'v_kv_norm_g', 'v_w_ukv', 'v_w_mla_out', 'v_w_pool', 'v_pool_scale', 'v_w_pool_out', 'v_w_o', 'v_ln1_g', 'v_ln1_b', 'v_w_up', 'v_ffn_dw', 'v_w_down', 'v_ln2_g', 'v_ln2_b']
TWIN_OUTPUTS = ['loss', 'grad_x', 'grad_w_ada', 'grad_b_ada', 'grad_w_in', 'grad_b_in', 'grad_conv_dw', 'grad_conv_ln_g', 'grad_conv_ln_b', 'grad_w_conv_out', 'grad_sc_dw', 'grad_w_sc_out', 'grad_q_norm_g', 'grad_w_uq', 'grad_kv_norm_g', 'grad_w_ukv', 'grad_w_mla_out', 'grad_w_pool', 'grad_pool_scale', 'grad_w_pool_out', 'grad_w_o', 'grad_ln1_g', 'grad_ln1_b', 'grad_w_up', 'grad_ffn_dw', 'grad_w_down', 'grad_ln2_g', 'grad_ln2_b', 'delta_w_ada', 'delta_b_ada', 'delta_w_in', 'delta_b_in', 'delta_conv_dw', 'delta_conv_ln_g', 'delta_conv_ln_b', 'delta_w_conv_out', 'delta_sc_dw', 'delta_w_sc_out', 'delta_q_norm_g', 'delta_w_uq', 'delta_kv_norm_g', 'delta_w_ukv', 'delta_w_mla_out', 'delta_w_pool', 'delta_pool_scale', 'delta_w_pool_out', 'delta_w_o', 'delta_ln1_g', 'delta_ln1_b', 'delta_w_up', 'delta_ffn_dw', 'delta_w_down', 'delta_ln2_g', 'delta_ln2_b', 'new_m_w_ada', 'new_m_b_ada', 'new_m_w_in', 'new_m_b_in', 'new_m_conv_dw', 'new_m_conv_ln_g', 'new_m_conv_ln_b', 'new_m_w_conv_out', 'new_m_sc_dw', 'new_m_w_sc_out', 'new_m_q_norm_g', 'new_m_w_uq', 'new_m_kv_norm_g', 'new_m_w_ukv', 'new_m_w_mla_out', 'new_m_w_pool', 'new_m_pool_scale', 'new_m_w_pool_out', 'new_m_w_o', 'new_m_ln1_g', 'new_m_ln1_b', 'new_m_w_up', 'new_m_ffn_dw', 'new_m_w_down', 'new_m_ln2_g', 'new_m_ln2_b', 'new_v_w_ada', 'new_v_b_ada', 'new_v_w_in', 'new_v_b_in', 'new_v_conv_dw', 'new_v_conv_ln_g', 'new_v_conv_ln_b', 'new_v_w_conv_out', 'new_v_sc_dw', 'new_v_w_sc_out', 'new_v_q_norm_g', 'new_v_w_uq', 'new_v_kv_norm_g', 'new_v_w_ukv', 'new_v_w_mla_out', 'new_v_w_pool', 'new_v_pool_scale', 'new_v_w_pool_out', 'new_v_w_o', 'new_v_ln1_g', 'new_v_ln1_b', 'new_v_w_up', 'new_v_ffn_dw', 'new_v_w_down', 'new_v_ln2_g', 'new_v_ln2_b']
TWIN_LEAF_KINDS = {'loss': 'loss', 'grad_x': 'grad_x', 'grad_w_ada': 'grad_w', 'grad_b_ada': 'grad_w', 'grad_w_in': 'grad_w', 'grad_b_in': 'grad_w', 'grad_conv_dw': 'grad_w', 'grad_conv_ln_g': 'grad_w', 'grad_conv_ln_b': 'grad_w', 'grad_w_conv_out': 'grad_w', 'grad_sc_dw': 'grad_w', 'grad_w_sc_out': 'grad_w', 'grad_q_norm_g': 'grad_w', 'grad_w_uq': 'grad_w', 'grad_kv_norm_g': 'grad_w', 'grad_w_ukv': 'grad_w', 'grad_w_mla_out': 'grad_w', 'grad_w_pool': 'grad_w', 'grad_pool_scale': 'grad_w', 'grad_w_pool_out': 'grad_w', 'grad_w_o': 'grad_w', 'grad_ln1_g': 'grad_w', 'grad_ln1_b': 'grad_w', 'grad_w_up': 'grad_w', 'grad_ffn_dw': 'grad_w', 'grad_w_down': 'grad_w', 'grad_ln2_g': 'grad_w', 'grad_ln2_b': 'grad_w', 'delta_w_ada': 'delta_w', 'delta_b_ada': 'delta_w', 'delta_w_in': 'delta_w', 'delta_b_in': 'delta_w', 'delta_conv_dw': 'delta_w', 'delta_conv_ln_g': 'delta_w', 'delta_conv_ln_b': 'delta_w', 'delta_w_conv_out': 'delta_w', 'delta_sc_dw': 'delta_w', 'delta_w_sc_out': 'delta_w', 'delta_q_norm_g': 'delta_w', 'delta_w_uq': 'delta_w', 'delta_kv_norm_g': 'delta_w', 'delta_w_ukv': 'delta_w', 'delta_w_mla_out': 'delta_w', 'delta_w_pool': 'delta_w', 'delta_pool_scale': 'delta_w', 'delta_w_pool_out': 'delta_w', 'delta_w_o': 'delta_w', 'delta_ln1_g': 'delta_w', 'delta_ln1_b': 'delta_w', 'delta_w_up': 'delta_w', 'delta_ffn_dw': 'delta_w', 'delta_w_down': 'delta_w', 'delta_ln2_g': 'delta_w', 'delta_ln2_b': 'delta_w', 'new_m_w_ada': 'new_m', 'new_m_b_ada': 'new_m', 'new_m_w_in': 'new_m', 'new_m_b_in': 'new_m', 'new_m_conv_dw': 'new_m', 'new_m_conv_ln_g': 'new_m', 'new_m_conv_ln_b': 'new_m', 'new_m_w_conv_out': 'new_m', 'new_m_sc_dw': 'new_m', 'new_m_w_sc_out': 'new_m', 'new_m_q_norm_g': 'new_m', 'new_m_w_uq': 'new_m', 'new_m_kv_norm_g': 'new_m', 'new_m_w_ukv': 'new_m', 'new_m_w_mla_out': 'new_m', 'new_m_w_pool': 'new_m', 'new_m_pool_scale': 'new_m', 'new_m_w_pool_out': 'new_m', 'new_m_w_o': 'new_m', 'new_m_ln1_g': 'new_m', 'new_m_ln1_b': 'new_m', 'new_m_w_up': 'new_m', 'new_m_ffn_dw': 'new_m', 'new_m_w_down': 'new_m', 'new_m_ln2_g': 'new_m', 'new_m_ln2_b': 'new_m', 'new_v_w_ada': 'new_v', 'new_v_b_ada': 'new_v', 'new_v_w_in': 'new_v', 'new_v_b_in': 'new_v', 'new_v_conv_dw': 'new_v', 'new_v_conv_ln_g': 'new_v', 'new_v_conv_ln_b': 'new_v', 'new_v_w_conv_out': 'new_v', 'new_v_sc_dw': 'new_v', 'new_v_w_sc_out': 'new_v', 'new_v_q_norm_g': 'new_v', 'new_v_w_uq': 'new_v', 'new_v_kv_norm_g': 'new_v', 'new_v_w_ukv': 'new_v', 'new_v_w_mla_out': 'new_v', 'new_v_w_pool': 'new_v', 'new_v_pool_scale': 'new_v', 'new_v_w_pool_out': 'new_v', 'new_v_w_o': 'new_v', 'new_v_ln1_g': 'new_v', 'new_v_ln1_b': 'new_v', 'new_v_w_up': 'new_v', 'new_v_ffn_dw': 'new_v', 'new_v_w_down': 'new_v', 'new_v_ln2_g': 'new_v', 'new_v_ln2_b': 'new_v'}


def _forward(args):
    return _fwd_reference(*[args[k] for k in FWD_PARAMS])


def _output_shape():
    def fwd():
        inp = _fwd_setup_inputs(0)
        return _fwd_reference(*[inp[k] for k in FWD_PARAMS])
    out = _jax.eval_shape(fwd)
    return out.shape, out.dtype

N_MICROBATCH = 1
ADAM_LR = 0.001
ADAM_B1 = 0.9
ADAM_B2 = 0.999
ADAM_EPS = 1e-08
ADAM_WD = 0.01
ADAM_STEP = 10
PER_EXAMPLE_BATCH_AXIS = {'x': 0, 'c': 0, 'positions': 0, 'loss_target': 0}
SHARED_INPUTS = []
_WEIGHT_DTYPES = {'w_ada': _jnp.float32, 'b_ada': _jnp.float32, 'w_in': _jnp.float32, 'b_in': _jnp.float32, 'conv_dw': _jnp.float32, 'conv_ln_g': _jnp.float32, 'conv_ln_b': _jnp.float32, 'w_conv_out': _jnp.float32, 'sc_dw': _jnp.float32, 'w_sc_out': _jnp.float32, 'q_norm_g': _jnp.float32, 'w_uq': _jnp.float32, 'kv_norm_g': _jnp.float32, 'w_ukv': _jnp.float32, 'w_mla_out': _jnp.float32, 'w_pool': _jnp.float32, 'pool_scale': _jnp.float32, 'w_pool_out': _jnp.float32, 'w_o': _jnp.float32, 'ln1_g': _jnp.float32, 'ln1_b': _jnp.float32, 'w_up': _jnp.float32, 'ffn_dw': _jnp.float32, 'w_down': _jnp.float32, 'ln2_g': _jnp.float32, 'ln2_b': _jnp.float32}
MOMENT_SCALE = {'w_ada': 5.147302e-02, 'b_ada': 1.688335e-01, 'w_in': 1.794239e-02, 'b_in': 3.630493e-02, 'conv_dw': 2.015917e-02, 'conv_ln_g': 4.538671e-02, 'conv_ln_b': 7.068092e-02, 'w_conv_out': 4.639387e-02, 'sc_dw': 3.209984e-02, 'w_sc_out': 5.309013e-02, 'q_norm_g': 8.302174e-03, 'w_uq': 4.775195e-03, 'kv_norm_g': 1.780993e-02, 'w_ukv': 6.356239e-03, 'w_mla_out': 1.259780e-02, 'w_pool': 3.444334e-02, 'pool_scale': 3.659719e-02, 'w_pool_out': 5.871272e-02, 'w_o': 9.041654e-02, 'ln1_g': 2.694305e+01, 'ln1_b': 2.066341e+00, 'w_up': 3.587313e-02, 'ffn_dw': 3.588686e-02, 'w_down': 1.424367e-01, 'ln2_g': 7.662072e+01, 'ln2_b': 3.713658e+00}


def _to_microbatches(a, axis):
    t = _jnp.moveaxis(a, axis, 0)
    t = t.reshape((N_MICROBATCH, t.shape[0] // N_MICROBATCH) + t.shape[1:])
    return _jnp.moveaxis(t, 1, axis + 1)


def setup_inputs(seed: int = 0) -> dict:
    inp = _fwd_setup_inputs(seed)
    key = _jax.random.fold_in(_jax.random.key(seed), 7919)
    shape, _ = _output_shape()
    out = dict(inp)
    out["loss_target"] = _jax.random.normal(_jax.random.fold_in(key, 0), shape, _jnp.float32)
    for i, name in enumerate(TWIN_WEIGHTS):
        w = inp[name].astype(_jnp.float32)
        if MOMENT_SCALE is None:
            s = _jnp.sqrt(_jnp.mean(_jnp.square(w)) + 1e-30)
        else:
            s = MOMENT_SCALE[name]
        km, kv = _jax.random.split(_jax.random.fold_in(key, i + 1))
        out[name] = w
        out["m_" + name] = s * _jax.random.normal(km, w.shape, _jnp.float32)
        out["v_" + name] = (s * s) * _jax.random.uniform(kv, w.shape, _jnp.float32, 0.5, 1.5)
    if N_MICROBATCH > 1:
        for name, axis in PER_EXAMPLE_BATCH_AXIS.items():
            out[name] = _to_microbatches(out[name], axis)
    return {'x': out['x'], 'c': out['c'], 'positions': out['positions'], 'w_ada': out['w_ada'], 'b_ada': out['b_ada'], 'w_in': out['w_in'], 'b_in': out['b_in'], 'conv_dw': out['conv_dw'], 'conv_ln_g': out['conv_ln_g'], 'conv_ln_b': out['conv_ln_b'], 'w_conv_out': out['w_conv_out'], 'sc_dw': out['sc_dw'], 'w_sc_out': out['w_sc_out'], 'q_norm_g': out['q_norm_g'], 'w_uq': out['w_uq'], 'kv_norm_g': out['kv_norm_g'], 'w_ukv': out['w_ukv'], 'w_mla_out': out['w_mla_out'], 'w_pool': out['w_pool'], 'pool_scale': out['pool_scale'], 'w_pool_out': out['w_pool_out'], 'w_o': out['w_o'], 'ln1_g': out['ln1_g'], 'ln1_b': out['ln1_b'], 'w_up': out['w_up'], 'ffn_dw': out['ffn_dw'], 'w_down': out['w_down'], 'ln2_g': out['ln2_g'], 'ln2_b': out['ln2_b'], 'loss_target': out['loss_target'], 'm_w_ada': out['m_w_ada'], 'm_b_ada': out['m_b_ada'], 'm_w_in': out['m_w_in'], 'm_b_in': out['m_b_in'], 'm_conv_dw': out['m_conv_dw'], 'm_conv_ln_g': out['m_conv_ln_g'], 'm_conv_ln_b': out['m_conv_ln_b'], 'm_w_conv_out': out['m_w_conv_out'], 'm_sc_dw': out['m_sc_dw'], 'm_w_sc_out': out['m_w_sc_out'], 'm_q_norm_g': out['m_q_norm_g'], 'm_w_uq': out['m_w_uq'], 'm_kv_norm_g': out['m_kv_norm_g'], 'm_w_ukv': out['m_w_ukv'], 'm_w_mla_out': out['m_w_mla_out'], 'm_w_pool': out['m_w_pool'], 'm_pool_scale': out['m_pool_scale'], 'm_w_pool_out': out['m_w_pool_out'], 'm_w_o': out['m_w_o'], 'm_ln1_g': out['m_ln1_g'], 'm_ln1_b': out['m_ln1_b'], 'm_w_up': out['m_w_up'], 'm_ffn_dw': out['m_ffn_dw'], 'm_w_down': out['m_w_down'], 'm_ln2_g': out['m_ln2_g'], 'm_ln2_b': out['m_ln2_b'], 'v_w_ada': out['v_w_ada'], 'v_b_ada': out['v_b_ada'], 'v_w_in': out['v_w_in'], 'v_b_in': out['v_b_in'], 'v_conv_dw': out['v_conv_dw'], 'v_conv_ln_g': out['v_conv_ln_g'], 'v_conv_ln_b': out['v_conv_ln_b'], 'v_w_conv_out': out['v_w_conv_out'], 'v_sc_dw': out['v_sc_dw'], 'v_w_sc_out': out['v_w_sc_out'], 'v_q_norm_g': out['v_q_norm_g'], 'v_w_uq': out['v_w_uq'], 'v_kv_norm_g': out['v_kv_norm_g'], 'v_w_ukv': out['v_w_ukv'], 'v_w_mla_out': out['v_w_mla_out'], 'v_w_pool': out['v_w_pool'], 'v_pool_scale': out['v_pool_scale'], 'v_w_pool_out': out['v_w_pool_out'], 'v_w_o': out['v_w_o'], 'v_ln1_g': out['v_ln1_g'], 'v_ln1_b': out['v_ln1_b'], 'v_w_up': out['v_w_up'], 'v_ffn_dw': out['v_ffn_dw'], 'v_w_down': out['v_w_down'], 'v_ln2_g': out['v_ln2_g'], 'v_ln2_b': out['v_ln2_b']}


def _loss(weights, diff, rest, loss_target):
    with _jax.named_scope("forward"):
        args = {**rest, TWIN_DIFF_INPUT: diff, **{k: w.astype(_WEIGHT_DTYPES[k]) for k, w in weights.items()}}
        y = _forward(args)
    with _jax.named_scope("loss_head"):
        err = _jnp.square(y.astype(_jnp.float32) - loss_target)
        return 0.5 * _jnp.sum(_jnp.mean(err, axis=-1)) if err.ndim else 0.5 * err


def _adamw(w, g, m, v):
    m = ADAM_B1 * m + (1.0 - ADAM_B1) * g
    v = ADAM_B2 * v + (1.0 - ADAM_B2) * _jnp.square(g)
    m_hat = m / (1.0 - ADAM_B1 ** ADAM_STEP)
    v_hat = v / (1.0 - ADAM_B2 ** ADAM_STEP)
    delta = -ADAM_LR * (m_hat / (_jnp.sqrt(v_hat) + ADAM_EPS) + ADAM_WD * w)
    return delta, m, v


def reference(x, c, positions, w_ada, b_ada, w_in, b_in, conv_dw, conv_ln_g, conv_ln_b, w_conv_out, sc_dw, w_sc_out, q_norm_g, w_uq, kv_norm_g, w_ukv, w_mla_out, w_pool, pool_scale, w_pool_out, w_o, ln1_g, ln1_b, w_up, ffn_dw, w_down, ln2_g, ln2_b, loss_target, m_w_ada, m_b_ada, m_w_in, m_b_in, m_conv_dw, m_conv_ln_g, m_conv_ln_b, m_w_conv_out, m_sc_dw, m_w_sc_out, m_q_norm_g, m_w_uq, m_kv_norm_g, m_w_ukv, m_w_mla_out, m_w_pool, m_pool_scale, m_w_pool_out, m_w_o, m_ln1_g, m_ln1_b, m_w_up, m_ffn_dw, m_w_down, m_ln2_g, m_ln2_b, v_w_ada, v_b_ada, v_w_in, v_b_in, v_conv_dw, v_conv_ln_g, v_conv_ln_b, v_w_conv_out, v_sc_dw, v_w_sc_out, v_q_norm_g, v_w_uq, v_kv_norm_g, v_w_ukv, v_w_mla_out, v_w_pool, v_pool_scale, v_w_pool_out, v_w_o, v_ln1_g, v_ln1_b, v_w_up, v_ffn_dw, v_w_down, v_ln2_g, v_ln2_b):
    given = dict(x=x, c=c, positions=positions, w_ada=w_ada, b_ada=b_ada, w_in=w_in, b_in=b_in, conv_dw=conv_dw, conv_ln_g=conv_ln_g, conv_ln_b=conv_ln_b, w_conv_out=w_conv_out, sc_dw=sc_dw, w_sc_out=w_sc_out, q_norm_g=q_norm_g, w_uq=w_uq, kv_norm_g=kv_norm_g, w_ukv=w_ukv, w_mla_out=w_mla_out, w_pool=w_pool, pool_scale=pool_scale, w_pool_out=w_pool_out, w_o=w_o, ln1_g=ln1_g, ln1_b=ln1_b, w_up=w_up, ffn_dw=ffn_dw, w_down=w_down, ln2_g=ln2_g, ln2_b=ln2_b, loss_target=loss_target, m_w_ada=m_w_ada, m_b_ada=m_b_ada, m_w_in=m_w_in, m_b_in=m_b_in, m_conv_dw=m_conv_dw, m_conv_ln_g=m_conv_ln_g, m_conv_ln_b=m_conv_ln_b, m_w_conv_out=m_w_conv_out, m_sc_dw=m_sc_dw, m_w_sc_out=m_w_sc_out, m_q_norm_g=m_q_norm_g, m_w_uq=m_w_uq, m_kv_norm_g=m_kv_norm_g, m_w_ukv=m_w_ukv, m_w_mla_out=m_w_mla_out, m_w_pool=m_w_pool, m_pool_scale=m_pool_scale, m_w_pool_out=m_w_pool_out, m_w_o=m_w_o, m_ln1_g=m_ln1_g, m_ln1_b=m_ln1_b, m_w_up=m_w_up, m_ffn_dw=m_ffn_dw, m_w_down=m_w_down, m_ln2_g=m_ln2_g, m_ln2_b=m_ln2_b, v_w_ada=v_w_ada, v_b_ada=v_b_ada, v_w_in=v_w_in, v_b_in=v_b_in, v_conv_dw=v_conv_dw, v_conv_ln_g=v_conv_ln_g, v_conv_ln_b=v_conv_ln_b, v_w_conv_out=v_w_conv_out, v_sc_dw=v_sc_dw, v_w_sc_out=v_w_sc_out, v_q_norm_g=v_q_norm_g, v_w_uq=v_w_uq, v_kv_norm_g=v_kv_norm_g, v_w_ukv=v_w_ukv, v_w_mla_out=v_w_mla_out, v_w_pool=v_w_pool, v_pool_scale=v_pool_scale, v_w_pool_out=v_w_pool_out, v_w_o=v_w_o, v_ln1_g=v_ln1_g, v_ln1_b=v_ln1_b, v_w_up=v_w_up, v_ffn_dw=v_ffn_dw, v_w_down=v_w_down, v_ln2_g=v_ln2_g, v_ln2_b=v_ln2_b)
    weights = {n: given[n] for n in TWIN_WEIGHTS}
    shared = {n: given[n] for n in SHARED_INPUTS}
    per_example = {n: given[n] for n in ['x', 'c', 'positions']}
    grad_fn = _jax.value_and_grad(_loss, argnums=(0, 1))

    def one_microbatch(ex, loss_target):
        ex = dict(ex)
        diff = ex.pop(TWIN_DIFF_INPUT)
        return grad_fn(weights, diff, {**shared, **ex}, loss_target)

    if N_MICROBATCH == 1:
        loss, (grad_w, grad_x) = one_microbatch(per_example, given["loss_target"])
    else:
        def body(carry, xs):
            loss_sum, grad_sum = carry
            l_k, (gw_k, gx_k) = one_microbatch(xs[0], xs[1])
            with _jax.named_scope("update"):
                return (loss_sum + l_k, _jax.tree.map(_jnp.add, grad_sum, gw_k)), gx_k

        init = (_jnp.zeros((), _jnp.float32), _jax.tree.map(_jnp.zeros_like, weights))
        (loss, grad_w), grad_x = _jax.lax.scan(body, init, (per_example, given["loss_target"]))
    with _jax.named_scope("update"):
        delta_w, new_m, new_v = {}, {}, {}
        for n in TWIN_WEIGHTS:
            delta_w[n], new_m[n], new_v[n] = _adamw(weights[n], grad_w[n], given["m_" + n], given["v_" + n])
    return (loss, grad_x, *[grad_w[n] for n in TWIN_WEIGHTS], *[delta_w[n] for n in TWIN_WEIGHTS],
            *[new_m[n] for n in TWIN_WEIGHTS], *[new_v[n] for n in TWIN_WEIGHTS])
```

```python
import functools

import jax
import jax.numpy as jnp
from jax import lax
from jax.experimental import pallas as pl
from jax.experimental.pallas import tpu as pltpu

F32 = jnp.float32
BF16 = jnp.bfloat16

N_DEV = 8
DEPTH = 4
D = 1024
CONV_W = 31
HEADS = 8
HEAD_PAD = 128
QK_DIM = 96
NOPE = 64
ROPE = 32
ROPE_THETA = 10000.0
D_FF = 2816
LN_EPS = 1e-5
RMS_EPS = 1e-6
ALPHA = (2.0 * DEPTH) ** 0.25
ATT_SCALE = QK_DIM ** -0.5
POOL_WINDOWS = (2, 4, 8, 16)

ADAM_LR = 0.001
ADAM_B1 = 0.9
ADAM_B2 = 0.999
ADAM_EPS = 1e-08
ADAM_WD = 0.01
ADAM_STEP = 10

GATES0 = 0
CA0 = 4096
CB0 = 4608
SBG0 = 5120
SCG0 = 5632
SX0 = 6144
QL0 = 6656
KVL0 = 6912
KR0 = 7040
PU0 = 7168
NPROJ = 7680

LANE = 128
TM = 512
TM_WIDE = 256
TQ = 512
FFN_TC = 1408
VMEM_LIMIT = 56 * 1024 * 1024

MESH = pl.DeviceIdType.MESH


def _sig(x):
    return 1.0 / (1.0 + jnp.exp(-x))


def _tile(n, pref):
    if n <= pref:
        return n
    t = (pref // LANE) * LANE
    while t >= LANE:
        if n % t == 0:
            return t
        t -= LANE
    raise ValueError(f"no lane-aligned tile for {n}")


def _params(sem):
    return pltpu.CompilerParams(dimension_semantics=sem, vmem_limit_bytes=VMEM_LIMIT)


def _full(shape):
    nd = len(shape)
    return pl.BlockSpec(shape, lambda *_: (0,) * nd)


def _rows(tm, cw, cb):
    return pl.BlockSpec((tm, cw), lambda i: (i, cb))


def _prev(tm, hb, cw, cb):
    r = tm // hb
    return pl.BlockSpec((hb, cw), lambda i: (jnp.maximum(i * r - 1, 0), cb))


def _next(tm, hb, cw, cb, s):
    r = tm // hb
    last = s // hb - 1
    return pl.BlockSpec((hb, cw), lambda i: (jnp.minimum((i + 1) * r, last), cb))


def _acc_rows(ref, first, rows):
    @pl.when(first)
    def _():
        ref[...] = jnp.zeros_like(ref)
    for r, v in enumerate(rows):
        ref[r:r + 1, :] += v


def _colsum(v):
    return jnp.sum(v, axis=0, keepdims=True)


def _ln_stats(r):
    mu = jnp.mean(r, axis=-1, keepdims=True)
    xc = r - mu
    var = jnp.mean(xc * xc, axis=-1, keepdims=True)
    rstd = lax.rsqrt(var + LN_EPS)
    return xc * rstd, rstd


def _ln_bwd(dxh, xh, rstd):
    return rstd * (dxh - jnp.mean(dxh, axis=-1, keepdims=True) - xh * jnp.mean(dxh * xh, axis=-1, keepdims=True))


_DIMS = {"nn": ((1,), (0,)), "nt": ((1,), (1,)), "tn": ((0,), (0,))}


def _mm(a, b, mode, name, *, out_dtype=F32, bias=None, tm=512, tn=1024, tk=2048):
    if mode == "nn":
        (m, k), (k2, n) = a.shape, b.shape
    elif mode == "nt":
        (m, k), (n, k2) = a.shape, b.shape
    else:
        (k, m), (k2, n) = a.shape, b.shape
    assert k == k2, (a.shape, b.shape, mode)
    tm, tn, tk = _tile(m, tm), _tile(n, tn), _tile(k, tk)
    nk = k // tk
    dims = (_DIMS[mode], ((), ()))
    has_bias = bias is not None

    def body(*refs):
        if has_bias:
            a_ref, b_ref, bias_ref, o_ref = refs[:4]
        else:
            a_ref, b_ref, o_ref = refs[:3]
        p = lax.dot_general(a_ref[...].astype(BF16), b_ref[...].astype(BF16), dims, preferred_element_type=F32)

        def finish(r):
            if has_bias:
                r = r + bias_ref[...]
            o_ref[...] = r.astype(out_dtype)

        if nk == 1:
            finish(p)
        else:
            acc = refs[-1]
            kk = pl.program_id(2)

            @pl.when(kk == 0)
            def _():
                acc[...] = p

            @pl.when(kk > 0)
            def _():
                acc[...] += p

            @pl.when(kk == nk - 1)
            def _():
                finish(acc[...])

    if mode == "nn":
        a_spec = pl.BlockSpec((tm, tk), lambda i, j, kk: (i, kk))
        b_spec = pl.BlockSpec((tk, tn), lambda i, j, kk: (kk, j))
    elif mode == "nt":
        a_spec = pl.BlockSpec((tm, tk), lambda i, j, kk: (i, kk))
        b_spec = pl.BlockSpec((tn, tk), lambda i, j, kk: (j, kk))
    else:
        a_spec = pl.BlockSpec((tk, tm), lambda i, j, kk: (kk, i))
        b_spec = pl.BlockSpec((tk, tn), lambda i, j, kk: (kk, j))
    in_specs = [a_spec, b_spec]
    args = [a, b]
    if has_bias:
        in_specs.append(pl.BlockSpec((1, tn), lambda i, j, kk: (0, j)))
        args.append(bias)
    return pl.pallas_call(
        body, name=name, grid=(m // tm, n // tn, nk),
        in_specs=in_specs, out_specs=pl.BlockSpec((tm, tn), lambda i, j, kk: (i, j)),
        out_shape=jax.ShapeDtypeStruct((m, n), out_dtype),
        scratch_shapes=[pltpu.VMEM((tm, tn), F32)] if nk > 1 else [],
        compiler_params=_params(("parallel", "parallel", "arbitrary")),
    )(*args)


def _mod_fwd(x, prm):
    s = x.shape[0]
    tm = _tile(s, TM)

    def body(x_ref, p_ref, h_ref):
        h_ref[...] = (x_ref[...] * (1.0 + p_ref[1:2, :]) + p_ref[2:3, :]).astype(BF16)

    return pl.pallas_call(
        body, name="mod_fwd", grid=(s // tm,),
        in_specs=[_rows(tm, D, 0), _full((8, D))], out_specs=_rows(tm, D, 0),
        out_shape=jax.ShapeDtypeStruct((s, D), BF16), compiler_params=_params(("parallel",)),
    )(x, prm)


def _mod_bwd(dres, dh, x, prm):
    s = x.shape[0]
    tm = _tile(s, TM)

    def body(dres_ref, dh_ref, x_ref, p_ref, dx_ref, acc_ref):
        dh_v = dh_ref[...]
        dx_ref[...] = dres_ref[...] + dh_v * (1.0 + p_ref[1:2, :])
        _acc_rows(acc_ref, pl.program_id(0) == 0, [_colsum(dh_v * x_ref[...]), _colsum(dh_v)])

    return pl.pallas_call(
        body, name="mod_bwd", grid=(s // tm,),
        in_specs=[_rows(tm, D, 0)] * 3 + [_full((8, D))],
        out_specs=[_rows(tm, D, 0), _full((8, D))],
        out_shape=[jax.ShapeDtypeStruct((s, D), F32), jax.ShapeDtypeStruct((8, D), F32)],
        compiler_params=_params(("arbitrary",)),
    )(dres, dh, x, prm)


CA_HALO = 32
C512 = 512


def _glu_buf(buf, ap, am, bp, bm, first, tm):
    glu_p = ap[...] * _sig(bp[...])
    buf[0:CA_HALO, :] = jnp.where(first, jnp.zeros_like(glu_p), glu_p)
    buf[CA_HALO:CA_HALO + tm, :] = am[...] * _sig(bm[...])


def _conv_a_fwd(proj, w32, ln_g, ln_b):
    s = proj.shape[0]
    tm = _tile(s, TM)
    ca, cb = CA0 // C512, CB0 // C512

    def body(ap, am, bp, bm, w_ref, g_ref, b_ref, yc_ref, za_ref, buf):
        _glu_buf(buf, ap, am, bp, bm, pl.program_id(0) == 0, tm)
        acc = w_ref[0:1, :] * buf[pl.ds(CA_HALO - CONV_W + 1, tm), :]
        for k in range(1, CONV_W):
            acc = acc + w_ref[k:k + 1, :] * buf[pl.ds(CA_HALO - CONV_W + 1 + k, tm), :]
        yc_ref[...] = acc
        xh, _ = _ln_stats(acc)
        y = xh * g_ref[...] + b_ref[...]
        za_ref[...] = (y * _sig(y)).astype(BF16)

    return pl.pallas_call(
        body, name="conv_a_fwd", grid=(s // tm,),
        in_specs=[_prev(tm, CA_HALO, C512, ca), _rows(tm, C512, ca), _prev(tm, CA_HALO, C512, cb), _rows(tm, C512, cb),
                  _full((32, C512)), _full((1, C512)), _full((1, C512))],
        out_specs=[_rows(tm, C512, 0), _rows(tm, C512, 0)],
        out_shape=[jax.ShapeDtypeStruct((s, C512), F32), jax.ShapeDtypeStruct((s, C512), BF16)],
        scratch_shapes=[pltpu.VMEM((tm + CA_HALO, C512), F32)],
        compiler_params=_params(("parallel",)),
    )(proj, proj, proj, proj, w32, ln_g, ln_b)


def _conv_a_bwd1(dza, yconv, proj, ln_g, ln_b):
    s = proj.shape[0]
    tm = _tile(s, TM)
    ca, cb = CA0 // C512, CB0 // C512

    def body(dza_ref, yc_ref, ap, am, bp, bm, g_ref, b_ref, dyc_ref, dw_ref, acc_ref, buf):
        first = pl.program_id(0) == 0
        xh, rstd = _ln_stats(yc_ref[...])
        g = g_ref[...]
        y = xh * g + b_ref[...]
        sg = _sig(y)
        dy = dza_ref[...] * (sg * (1.0 + y * (1.0 - sg)))
        _acc_rows(acc_ref, first, [_colsum(dy * xh), _colsum(dy)])
        dyc = _ln_bwd(dy * g, xh, rstd)
        dyc_ref[...] = dyc
        _glu_buf(buf, ap, am, bp, bm, first, tm)

        @pl.when(first)
        def _():
            dw_ref[...] = jnp.zeros_like(dw_ref)
        for k in range(CONV_W):
            dw_ref[k:k + 1, :] += _colsum(dyc * buf[pl.ds(CA_HALO - CONV_W + 1 + k, tm), :])

    return pl.pallas_call(
        body, name="conv_a_bwd1", grid=(s // tm,),
        in_specs=[_rows(tm, C512, 0), _rows(tm, C512, 0),
                  _prev(tm, CA_HALO, C512, ca), _rows(tm, C512, ca), _prev(tm, CA_HALO, C512, cb), _rows(tm, C512, cb),
                  _full((1, C512)), _full((1, C512))],
        out_specs=[_rows(tm, C512, 0), _full((32, C512)), _full((8, C512))],
        out_shape=[jax.ShapeDtypeStruct((s, C512), F32), jax.ShapeDtypeStruct((32, C512), F32),
                   jax.ShapeDtypeStruct((8, C512), F32)],
        scratch_shapes=[pltpu.VMEM((tm + CA_HALO, C512), F32)],
        compiler_params=_params(("arbitrary",)),
    )(dza, yconv, proj, proj, proj, proj, ln_g, ln_b)


def _conv_a_bwd2(dyc, proj, w32):
    s = proj.shape[0]
    tm = _tile(s, TM)
    ca, cb = CA0 // C512, CB0 // C512
    nt = s // tm

    def body(dm, dn, am, bm, w_ref, da_ref, db_ref, acc_ref, buf):
        i = pl.program_id(0)
        buf[0:tm, :] = dm[...]
        nxt = dn[...]
        buf[tm:tm + CA_HALO, :] = jnp.where(i == nt - 1, jnp.zeros_like(nxt), nxt)
        dglu = w_ref[0:1, :] * buf[pl.ds(CONV_W - 1, tm), :]
        for k in range(1, CONV_W):
            dglu = dglu + w_ref[k:k + 1, :] * buf[pl.ds(CONV_W - 1 - k, tm), :]
        sb = _sig(bm[...])
        da = dglu * sb
        db = dglu * am[...] * sb * (1.0 - sb)
        da_ref[...] = da.astype(BF16)
        db_ref[...] = db.astype(BF16)
        _acc_rows(acc_ref, i == 0, [_colsum(da), _colsum(db)])

    return pl.pallas_call(
        body, name="conv_a_bwd2", grid=(nt,),
        in_specs=[_rows(tm, C512, 0), _next(tm, CA_HALO, C512, 0, s), _rows(tm, C512, ca), _rows(tm, C512, cb),
                  _full((32, C512))],
        out_specs=[_rows(tm, C512, 0), _rows(tm, C512, 0), _full((8, C512))],
        out_shape=[jax.ShapeDtypeStruct((s, C512), BF16), jax.ShapeDtypeStruct((s, C512), BF16),
                   jax.ShapeDtypeStruct((8, C512), F32)],
        scratch_shapes=[pltpu.VMEM((tm + CA_HALO, C512), F32)],
        compiler_params=_params(("arbitrary",)),
    )(dyc, dyc, proj, proj, w32)


H8 = 8
SC_W = 3


def _sc_ubuf(buf, cp, cm, xp, xm, first, tm):
    up = cp[...] * xp[...]
    buf[0:H8, :] = jnp.where(first, jnp.zeros_like(up), up)
    buf[H8:H8 + tm, :] = cm[...] * xm[...]


def _conv3(w_ref, buf, tm):
    acc = w_ref[0:1, :] * buf[pl.ds(H8 - SC_W + 1, tm), :]
    for k in range(1, SC_W):
        acc = acc + w_ref[k:k + 1, :] * buf[pl.ds(H8 - SC_W + 1 + k, tm), :]
    return acc


def _conv3_t(w_ref, buf, tm):
    acc = w_ref[0:1, :] * buf[pl.ds(SC_W - 1, tm), :]
    for k in range(1, SC_W):
        acc = acc + w_ref[k:k + 1, :] * buf[pl.ds(SC_W - 1 - k, tm), :]
    return acc


def _sc_fwd(proj, w8):
    s = proj.shape[0]
    tm = _tile(s, TM)
    c_bg, c_cg, c_x = SBG0 // C512, SCG0 // C512, SX0 // C512

    def body(bg, cp, cm, xp, xm, w_ref, zb_ref, buf):
        _sc_ubuf(buf, cp, cm, xp, xm, pl.program_id(0) == 0, tm)
        zb_ref[...] = (bg[...] * _conv3(w_ref, buf, tm)).astype(BF16)

    return pl.pallas_call(
        body, name="sc_fwd", grid=(s // tm,),
        in_specs=[_rows(tm, C512, c_bg), _prev(tm, H8, C512, c_cg), _rows(tm, C512, c_cg),
                  _prev(tm, H8, C512, c_x), _rows(tm, C512, c_x), _full((8, C512))],
        out_specs=_rows(tm, C512, 0), out_shape=jax.ShapeDtypeStruct((s, C512), BF16),
        scratch_shapes=[pltpu.VMEM((tm + H8, C512), F32)], compiler_params=_params(("parallel",)),
    )(proj, proj, proj, proj, proj, w8)


def _sc_bwd1(dzb, proj, w8):
    s = proj.shape[0]
    tm = _tile(s, TM)
    c_bg, c_cg, c_x = SBG0 // C512, SCG0 // C512, SX0 // C512

    def body(dz_ref, bg, cp, cm, xp, xm, w_ref, dconv_ref, dbg_ref, dw_ref, acc_ref, buf):
        first = pl.program_id(0) == 0
        _sc_ubuf(buf, cp, cm, xp, xm, first, tm)
        dz = dz_ref[...]
        dbg = dz * _conv3(w_ref, buf, tm)
        dconv = dz * bg[...]
        dconv_ref[...] = dconv
        dbg_ref[...] = dbg.astype(BF16)
        _acc_rows(acc_ref, first, [_colsum(dbg)])
        _acc_rows(dw_ref, first, [_colsum(dconv * buf[pl.ds(H8 - SC_W + 1 + k, tm), :]) for k in range(SC_W)])

    return pl.pallas_call(
        body, name="sc_bwd1", grid=(s // tm,),
        in_specs=[_rows(tm, C512, 0), _rows(tm, C512, c_bg), _prev(tm, H8, C512, c_cg), _rows(tm, C512, c_cg),
                  _prev(tm, H8, C512, c_x), _rows(tm, C512, c_x), _full((8, C512))],
        out_specs=[_rows(tm, C512, 0), _rows(tm, C512, 0), _full((8, C512)), _full((8, C512))],
        out_shape=[jax.ShapeDtypeStruct((s, C512), F32), jax.ShapeDtypeStruct((s, C512), BF16),
                   jax.ShapeDtypeStruct((8, C512), F32), jax.ShapeDtypeStruct((8, C512), F32)],
        scratch_shapes=[pltpu.VMEM((tm + H8, C512), F32)], compiler_params=_params(("arbitrary",)),
    )(dzb, proj, proj, proj, proj, proj, w8)


def _sc_bwd2(dconv, proj, w8):
    s = proj.shape[0]
    tm = _tile(s, TM)
    c_cg, c_x = SCG0 // C512, SX0 // C512
    nt = s // tm

    def body(dm, dn, cm, xm, w_ref, dcg_ref, dx_ref, acc_ref, buf):
        i = pl.program_id(0)
        buf[0:tm, :] = dm[...]
        nxt = dn[...]
        buf[tm:tm + H8, :] = jnp.where(i == nt - 1, jnp.zeros_like(nxt), nxt)
        du = _conv3_t(w_ref, buf, tm)
        dcg = du * xm[...]
        dx = du * cm[...]
        dcg_ref[...] = dcg.astype(BF16)
        dx_ref[...] = dx.astype(BF16)
        _acc_rows(acc_ref, i == 0, [_colsum(dcg), _colsum(dx)])

    return pl.pallas_call(
        body, name="sc_bwd2", grid=(nt,),
        in_specs=[_rows(tm, C512, 0), _next(tm, H8, C512, 0, s), _rows(tm, C512, c_cg), _rows(tm, C512, c_x),
                  _full((8, C512))],
        out_specs=[_rows(tm, C512, 0), _rows(tm, C512, 0), _full((8, C512))],
        out_shape=[jax.ShapeDtypeStruct((s, C512), BF16), jax.ShapeDtypeStruct((s, C512), BF16),
                   jax.ShapeDtypeStruct((8, C512), F32)],
        scratch_shapes=[pltpu.VMEM((tm + H8, C512), F32)], compiler_params=_params(("arbitrary",)),
    )(dconv, dconv, proj, proj, w8)


QLAT = 256
KVLAT = 128


def _rms(x, g):
    r = lax.rsqrt(jnp.mean(x * x, axis=-1, keepdims=True) + RMS_EPS)
    return x * r * g, r


def _rms_bwd(dy, x, g, r):
    u = dy * g
    dx = r * u - x * (r * r * r) * jnp.mean(u * x, axis=-1, keepdims=True)
    return dx, _colsum(dy * x * r)


def _lat_fwd(proj, gq, gkv):
    s = proj.shape[0]
    tm = _tile(s, TM)

    def body(q_ref, kv_ref, gq_ref, gkv_ref, qn_ref, kvn_ref):
        qn_ref[...] = _rms(q_ref[...], gq_ref[...])[0].astype(BF16)
        kvn_ref[...] = _rms(kv_ref[...], gkv_ref[...])[0].astype(BF16)

    return pl.pallas_call(
        body, name="lat_fwd", grid=(s // tm,),
        in_specs=[_rows(tm, QLAT, QL0 // QLAT), _rows(tm, KVLAT, KVL0 // KVLAT), _full((1, QLAT)), _full((1, KVLAT))],
        out_specs=[_rows(tm, QLAT, 0), _rows(tm, KVLAT, 0)],
        out_shape=[jax.ShapeDtypeStruct((s, QLAT), BF16), jax.ShapeDtypeStruct((s, KVLAT), BF16)],
        compiler_params=_params(("parallel",)),
    )(proj, proj, gq, gkv)


def _lat_bwd(dqn, dkvn, proj, gq, gkv):
    s = proj.shape[0]
    tm = _tile(s, TM)

    def body(dqn_ref, dkvn_ref, q_ref, kv_ref, gq_ref, gkv_ref, dq_ref, dkv_ref, accq_ref, acckv_ref):
        first = pl.program_id(0) == 0
        q, kv = q_ref[...], kv_ref[...]
        gqv, gkvv = gq_ref[...], gkv_ref[...]
        dq, dgq = _rms_bwd(dqn_ref[...], q, gqv, _rms(q, gqv)[1])
        dkv, dgkv = _rms_bwd(dkvn_ref[...], kv, gkvv, _rms(kv, gkvv)[1])
        dq_ref[...] = dq.astype(BF16)
        dkv_ref[...] = dkv.astype(BF16)
        _acc_rows(accq_ref, first, [dgq, _colsum(dq)])
        _acc_rows(acckv_ref, first, [dgkv, _colsum(dkv)])

    return pl.pallas_call(
        body, name="lat_bwd", grid=(s // tm,),
        in_specs=[_rows(tm, QLAT, 0), _rows(tm, KVLAT, 0), _rows(tm, QLAT, QL0 // QLAT), _rows(tm, KVLAT, KVL0 // KVLAT),
                  _full((1, QLAT)), _full((1, KVLAT))],
        out_specs=[_rows(tm, QLAT, 0), _rows(tm, KVLAT, 0), _full((8, QLAT)), _full((8, KVLAT))],
        out_shape=[jax.ShapeDtypeStruct((s, QLAT), BF16), jax.ShapeDtypeStruct((s, KVLAT), BF16),
                   jax.ShapeDtypeStruct((8, QLAT), F32), jax.ShapeDtypeStruct((8, KVLAT), F32)],
        compiler_params=_params(("arbitrary",)),
    )(dqn, dkvn, proj, proj, gq, gkv)


def _rope(x, c, s1, s2):
    return x * c + pltpu.roll(x, HEAD_PAD - ROPE // 2, 1) * s1 + pltpu.roll(x, ROPE // 2, 1) * s2


def _rope_t(d, c, s1, s2):
    return d * c + pltpu.roll(d * s1, ROPE // 2, 1) + pltpu.roll(d * s2, HEAD_PAD - ROPE // 2, 1)


def _rope_fwd(qp, kvp, proj, ct, s1t, s2t):
    s = proj.shape[0]
    tm = _tile(s, TM)

    def body(q_ref, kn_ref, v_ref, kr_ref, c_ref, s1_ref, s2_ref, qo, ko, vo):
        c, s1, s2 = c_ref[...], s1_ref[...], s2_ref[...]
        qo[...] = _rope(q_ref[...], c, s1, s2).astype(BF16)
        ko[...] = (kn_ref[...] + _rope(kr_ref[...], c, s1, s2)).astype(BF16)
        vo[...] = v_ref[...].astype(BF16)

    blk = lambda f: pl.BlockSpec((tm, HEAD_PAD), f)
    tab = blk(lambda i, h: (i, 0))
    return pl.pallas_call(
        body, name="rope_fwd", grid=(s // tm, HEADS),
        in_specs=[blk(lambda i, h: (i, h)), blk(lambda i, h: (i, 2 * h)), blk(lambda i, h: (i, 2 * h + 1)),
                  blk(lambda i, h: (i, KR0 // HEAD_PAD)), tab, tab, tab],
        out_specs=[blk(lambda i, h: (i, h))] * 3,
        out_shape=[jax.ShapeDtypeStruct((s, HEADS * HEAD_PAD), BF16)] * 3,
        compiler_params=_params(("parallel", "parallel")),
    )(qp, kvp, kvp, proj, ct, s1t, s2t)


def _rope_bwd(dq, dkv, ct, s1t, s2t):
    s = dq.shape[0]
    tm = _tile(s, TM)

    def body(dq_ref, dk_ref, c_ref, s1_ref, s2_ref, dqo, dkr_ref, acc_ref, kacc):
        i, h = pl.program_id(0), pl.program_id(1)
        c, s1, s2 = c_ref[...], s1_ref[...], s2_ref[...]
        dqo[...] = _rope_t(dq_ref[...], c, s1, s2).astype(BF16)
        lane = lax.broadcasted_iota(jnp.int32, (tm, HEAD_PAD), 1)
        contrib = jnp.where((lane >= NOPE) & (lane < QK_DIM), _rope_t(dk_ref[...], c, s1, s2), 0.0)

        @pl.when(h == 0)
        def _():
            kacc[...] = contrib

        @pl.when(h > 0)
        def _():
            kacc[...] += contrib

        @pl.when((i == 0) & (h == 0))
        def _():
            acc_ref[...] = jnp.zeros_like(acc_ref)

        @pl.when(h == HEADS - 1)
        def _():
            tot = kacc[...]
            dkr_ref[...] = tot.astype(BF16)
            acc_ref[0:1, :] += _colsum(tot)

    blk = lambda f: pl.BlockSpec((tm, HEAD_PAD), f)
    tab = blk(lambda i, h: (i, 0))
    return pl.pallas_call(
        body, name="rope_bwd", grid=(s // tm, HEADS),
        in_specs=[blk(lambda i, h: (i, h)), blk(lambda i, h: (i, 2 * h)), tab, tab, tab],
        out_specs=[blk(lambda i, h: (i, h)), blk(lambda i, h: (i, 0)), pl.BlockSpec((8, HEAD_PAD), lambda i, h: (0, 0))],
        out_shape=[jax.ShapeDtypeStruct((s, HEADS * HEAD_PAD), BF16), jax.ShapeDtypeStruct((s, HEAD_PAD), BF16),
                   jax.ShapeDtypeStruct((8, HEAD_PAD), F32)],
        scratch_shapes=[pltpu.VMEM((tm, HEAD_PAD), F32)],
        compiler_params=_params(("arbitrary", "arbitrary")),
    )(dq, dkv, ct, s1t, s2t)


_NT = (((1,), (1,)), ((), ()))
_TN = (((0,), (0,)), ((), ()))
_NN = (((1,), (0,)), ((), ()))


def _scores(q, k, masked, tq):
    sc = lax.dot_general(q, k, _NT, preferred_element_type=F32) * ATT_SCALE
    if masked:
        row = lax.broadcasted_iota(jnp.int32, (tq, tq), 0)
        col = lax.broadcasted_iota(jnp.int32, (tq, tq), 1)
        sc = jnp.where(col <= row, sc, -jnp.inf)
    return sc


def _attn_fwd(q, k, v):
    s = q.shape[0]
    tq = _tile(s, TQ)
    nq = s // tq

    def body(q_ref, k_ref, v_ref, o_ref, lse_ref, m_sc, l_sc, acc_sc):
        qi, ki = pl.program_id(1), pl.program_id(2)

        @pl.when(ki == 0)
        def _():
            m_sc[...] = jnp.full_like(m_sc, -jnp.inf)
            l_sc[...] = jnp.zeros_like(l_sc)
            acc_sc[...] = jnp.zeros_like(acc_sc)

        def step(masked):
            sc = _scores(q_ref[...], k_ref[...], masked, tq)
            m_prev = m_sc[...]
            m_cur = jnp.maximum(m_prev, jnp.max(sc, axis=-1, keepdims=True))
            p = jnp.exp(sc - m_cur[:, 0:1])
            a = jnp.exp(m_prev - m_cur)
            l_sc[...] = a * l_sc[...] + jnp.sum(p, axis=-1, keepdims=True)
            acc_sc[...] = a * acc_sc[...] + lax.dot_general(p.astype(BF16), v_ref[...], _NN, preferred_element_type=F32)
            m_sc[...] = m_cur

        @pl.when(ki < qi)
        def _():
            step(False)

        @pl.when(ki == qi)
        def _():
            step(True)
            l = l_sc[...]
            o_ref[...] = acc_sc[...] / l
            lse_ref[0] = m_sc[...] + jnp.log(l)

    qspec = pl.BlockSpec((tq, HEAD_PAD), lambda h, qi, ki: (qi, h))
    kspec = pl.BlockSpec((tq, HEAD_PAD), lambda h, qi, ki: (jnp.minimum(ki, qi), h))
    return pl.pallas_call(
        body, name="attn_fwd", grid=(HEADS, nq, nq),
        in_specs=[qspec, kspec, kspec],
        out_specs=[qspec, pl.BlockSpec((1, tq, HEAD_PAD), lambda h, qi, ki: (h, qi, 0))],
        out_shape=[jax.ShapeDtypeStruct((s, HEADS * HEAD_PAD), F32), jax.ShapeDtypeStruct((HEADS, s, HEAD_PAD), F32)],
        scratch_shapes=[pltpu.VMEM((tq, HEAD_PAD), F32)] * 3,
        compiler_params=_params(("parallel", "parallel", "arbitrary")),
    )(q, k, v)


def _attn_prep(d_o, o, lse):
    s = o.shape[0]
    tm = _tile(s, TM)

    def body(do_ref, o_ref, lse_ref, st_ref, dob_ref):
        dov = do_ref[...]
        delta = jnp.sum(dov * o_ref[...], axis=-1, keepdims=True)
        lane = lax.broadcasted_iota(jnp.int32, (tm, HEAD_PAD), 1)
        st_ref[0] = jnp.where(lane < NOPE, lse_ref[0], delta)
        dob_ref[...] = dov.astype(BF16)

    blk = pl.BlockSpec((tm, HEAD_PAD), lambda i, h: (i, h))
    hblk = pl.BlockSpec((1, tm, HEAD_PAD), lambda i, h: (h, i, 0))
    return pl.pallas_call(
        body, name="attn_prep", grid=(s // tm, HEADS),
        in_specs=[blk, blk, hblk], out_specs=[hblk, blk],
        out_shape=[jax.ShapeDtypeStruct((HEADS, s, HEAD_PAD), F32), jax.ShapeDtypeStruct((s, HEADS * HEAD_PAD), BF16)],
        compiler_params=_params(("parallel", "parallel")),
    )(d_o, o, lse)


def _p_ds(q, k, v, d_o, st, masked, tq):
    sc = _scores(q, k, masked, tq)
    p = jnp.exp(sc - st[:, 0:1])
    dp = lax.dot_general(d_o, v, _NT, preferred_element_type=F32)
    ds = p * (dp - st[:, NOPE:NOPE + 1]) * ATT_SCALE
    return p, ds


def _attn_bwd_dq(q, k, v, d_o, stats):
    s = q.shape[0]
    tq = _tile(s, TQ)
    nq = s // tq

    def body(q_ref, k_ref, v_ref, do_ref, st_ref, dq_ref, acc):
        qi, ki = pl.program_id(1), pl.program_id(2)

        @pl.when(ki == 0)
        def _():
            acc[...] = jnp.zeros_like(acc)

        def step(masked):
            _, ds = _p_ds(q_ref[...], k_ref[...], v_ref[...], do_ref[...], st_ref[0], masked, tq)
            acc[...] += lax.dot_general(ds.astype(BF16), k_ref[...], _NN, preferred_element_type=F32)

        @pl.when(ki < qi)
        def _():
            step(False)

        @pl.when(ki == qi)
        def _():
            step(True)
            dq_ref[...] = acc[...]

    qspec = pl.BlockSpec((tq, HEAD_PAD), lambda h, qi, ki: (qi, h))
    kspec = pl.BlockSpec((tq, HEAD_PAD), lambda h, qi, ki: (jnp.minimum(ki, qi), h))
    return pl.pallas_call(
        body, name="attn_bwd_dq", grid=(HEADS, nq, nq),
        in_specs=[qspec, kspec, kspec, qspec, pl.BlockSpec((1, tq, HEAD_PAD), lambda h, qi, ki: (h, qi, 0))],
        out_specs=qspec, out_shape=jax.ShapeDtypeStruct((s, HEADS * HEAD_PAD), F32),
        scratch_shapes=[pltpu.VMEM((tq, HEAD_PAD), F32)],
        compiler_params=_params(("parallel", "parallel", "arbitrary")),
    )(q, k, v, d_o, stats)


def _attn_bwd_dkv(q, k, v, d_o, stats):
    s = q.shape[0]
    tq = _tile(s, TQ)
    nq = s // tq

    def body(q_ref, k_ref, v_ref, do_ref, st_ref, dkv_ref, dk_acc, dv_acc):
        ki, qi = pl.program_id(1), pl.program_id(2)

        @pl.when(qi == 0)
        def _():
            dk_acc[...] = jnp.zeros_like(dk_acc)
            dv_acc[...] = jnp.zeros_like(dv_acc)

        def step(masked):
            p, ds = _p_ds(q_ref[...], k_ref[...], v_ref[...], do_ref[...], st_ref[0], masked, tq)
            dv_acc[...] += lax.dot_general(p.astype(BF16), do_ref[...], _TN, preferred_element_type=F32)
            dk_acc[...] += lax.dot_general(ds.astype(BF16), q_ref[...], _TN, preferred_element_type=F32)

        @pl.when(qi == ki)
        def _():
            step(True)

        @pl.when(qi > ki)
        def _():
            step(False)

        @pl.when(qi == nq - 1)
        def _():
            dkv_ref[:, 0:HEAD_PAD] = dk_acc[...]
            dkv_ref[:, HEAD_PAD:2 * HEAD_PAD] = dv_acc[...]

    qspec = pl.BlockSpec((tq, HEAD_PAD), lambda h, ki, qi: (jnp.maximum(qi, ki), h))
    kspec = pl.BlockSpec((tq, HEAD_PAD), lambda h, ki, qi: (ki, h))
    return pl.pallas_call(
        body, name="attn_bwd_dkv", grid=(HEADS, nq, nq),
        in_specs=[qspec, kspec, kspec, qspec,
                  pl.BlockSpec((1, tq, HEAD_PAD), lambda h, ki, qi: (h, jnp.maximum(qi, ki), 0))],
        out_specs=pl.BlockSpec((tq, 2 * HEAD_PAD), lambda h, ki, qi: (ki, h)),
        out_shape=jax.ShapeDtypeStruct((s, HEADS * 2 * HEAD_PAD), F32),
        scratch_shapes=[pltpu.VMEM((tq, HEAD_PAD), F32)] * 2,
        compiler_params=_params(("parallel", "parallel", "arbitrary")),
    )(q, k, v, d_o, stats)


PH = 16
PG = 128


def _pool_pd(buf, u_main_ref, g, i, tm):
    w = POOL_WINDOWS[g]
    cs = pl.ds(g * PG, PG)
    tot = buf[pl.ds(PH, tm), cs]
    for j in range(1, w):
        tot = tot + buf[pl.ds(PH - j, tm), cs]
    t = i * tm + lax.broadcasted_iota(jnp.int32, (tm, PG), 0)
    cnt = jnp.minimum(t + 1, w).astype(F32)
    return tot / cnt - u_main_ref[:, cs]


def _pool_ubuf(buf, up, um, first, tm):
    p = up[...]
    buf[0:PH, :] = jnp.where(first, jnp.zeros_like(p), p)
    buf[PH:PH + tm, :] = um[...]


def _pool_fwd(proj, w_pool, scale):
    s = proj.shape[0]
    tm = _tile(s, TM)
    cu = PU0 // C512

    def body(up, um, w_ref, sc_ref, zd_ref, buf):
        i = pl.program_id(0)
        _pool_ubuf(buf, up, um, i == 0, tm)
        for g in range(4):
            pd = _pool_pd(buf, um, g, i, tm).astype(BF16)
            e = lax.dot_general(pd, w_ref[g], _NN, preferred_element_type=F32)
            zd_ref[:, g * PG:(g + 1) * PG] = (e * sc_ref[:, g * PG:(g + 1) * PG]).astype(BF16)

    return pl.pallas_call(
        body, name="pool_fwd", grid=(s // tm,),
        in_specs=[_prev(tm, PH, C512, cu), _rows(tm, C512, cu), _full((4, PG, PG)), _full((1, C512))],
        out_specs=_rows(tm, C512, 0), out_shape=jax.ShapeDtypeStruct((s, C512), BF16),
        scratch_shapes=[pltpu.VMEM((tm + PH, C512), F32)], compiler_params=_params(("parallel",)),
    )(proj, proj, w_pool, scale)


def _pool_bwd1(dzd, proj, w_pool, scale):
    s = proj.shape[0]
    tm = _tile(s, TM)
    cu = PU0 // C512

    def body(dz_ref, up, um, w_ref, sc_ref, dpd_ref, dw_ref, acc_ref, buf):
        i = pl.program_id(0)
        first = i == 0
        _pool_ubuf(buf, up, um, first, tm)

        @pl.when(first)
        def _():
            dw_ref[...] = jnp.zeros_like(dw_ref)
            acc_ref[...] = jnp.zeros_like(acc_ref)
        for g in range(4):
            cs = slice(g * PG, (g + 1) * PG)
            pd = _pool_pd(buf, um, g, i, tm).astype(BF16)
            wg = w_ref[g]
            e = lax.dot_general(pd, wg, _NN, preferred_element_type=F32)
            dz = dz_ref[:, cs]
            acc_ref[0:1, cs] += _colsum(dz * e)
            de = (dz * sc_ref[:, cs]).astype(BF16)
            dw_ref[g] += lax.dot_general(pd, de, _TN, preferred_element_type=F32)
            dpd_ref[:, cs] = lax.dot_general(de, wg, _NT, preferred_element_type=F32)

    return pl.pallas_call(
        body, name="pool_bwd1", grid=(s // tm,),
        in_specs=[_rows(tm, C512, 0), _prev(tm, PH, C512, cu), _rows(tm, C512, cu), _full((4, PG, PG)), _full((1, C512))],
        out_specs=[_rows(tm, C512, 0), _full((4, PG, PG)), _full((8, C512))],
        out_shape=[jax.ShapeDtypeStruct((s, C512), F32), jax.ShapeDtypeStruct((4, PG, PG), F32),
                   jax.ShapeDtypeStruct((8, C512), F32)],
        scratch_shapes=[pltpu.VMEM((tm + PH, C512), F32)], compiler_params=_params(("arbitrary",)),
    )(dzd, proj, proj, w_pool, scale)


def _pool_bwd2(dpd, s):
    tm = _tile(s, TM)
    nt = s // tm

    def body(dm, dn, du_ref, acc_ref, buf):
        i = pl.program_id(0)
        buf[0:tm, :] = dm[...]
        nxt = dn[...]
        buf[tm:tm + PH, :] = jnp.where(i == nt - 1, jnp.zeros_like(nxt), nxt)
        t = i * tm + lax.broadcasted_iota(jnp.int32, (tm + PH, PG), 0)
        cols = []
        for g, w in enumerate(POOL_WINDOWS):
            cs = pl.ds(g * PG, PG)
            cnt = jnp.minimum(t + 1, w).astype(F32)
            buf[:, cs] = buf[:, cs] / cnt
        for g, w in enumerate(POOL_WINDOWS):
            cs = pl.ds(g * PG, PG)
            tot = buf[pl.ds(0, tm), cs]
            for j in range(1, w):
                tot = tot + buf[pl.ds(j, tm), cs]
            du = tot - dm[:, cs]
            du_ref[:, cs] = du.astype(BF16)
            cols.append(_colsum(du))
        _acc_rows(acc_ref, i == 0, [jnp.concatenate(cols, axis=1)])

    return pl.pallas_call(
        body, name="pool_bwd2", grid=(nt,),
        in_specs=[_rows(tm, C512, 0), _next(tm, PH, C512, 0, s)],
        out_specs=[_rows(tm, C512, 0), _full((8, C512))],
        out_shape=[jax.ShapeDtypeStruct((s, C512), BF16), jax.ShapeDtypeStruct((8, C512), F32)],
        scratch_shapes=[pltpu.VMEM((tm + PH, C512), F32)], compiler_params=_params(("arbitrary",)),
    )(dpd, dpd)


def _merge_specs(tm):
    return [_rows(tm, C512, 0), _rows(tm, C512, 0), _rows(tm, D, 0), _rows(tm, C512, 0),
            _rows(tm, 4 * D, GATES0 // (4 * D)),
            _full((C512, D)), _full((C512, D)), _full((D, D)), _full((C512, D))]


def _branch_ys(za, zb, o, zd, wa, wb, wc, wd):
    zs = (za[...], zb[...], o[...].astype(BF16), zd[...])
    return [lax.dot_general(z, w[...], _NN, preferred_element_type=F32) for z, w in zip(zs, (wa, wb, wc, wd))]


def _merge_fwd(za, zb, o, zd, proj, wa, wb, wc, wd):
    s = proj.shape[0]
    tm = _tile(s, TM_WIDE)

    def body(za_r, zb_r, o_r, zd_r, g_ref, wa_r, wb_r, wc_r, wd_r, m_ref):
        ys = _branch_ys(za_r, zb_r, o_r, zd_r, wa_r, wb_r, wc_r, wd_r)
        acc = _sig(g_ref[:, 0:D]) * ys[0]
        for b in range(1, 4):
            acc = acc + _sig(g_ref[:, b * D:(b + 1) * D]) * ys[b]
        m_ref[...] = acc.astype(BF16)

    return pl.pallas_call(
        body, name="merge_fwd", grid=(s // tm,), in_specs=_merge_specs(tm),
        out_specs=_rows(tm, D, 0), out_shape=jax.ShapeDtypeStruct((s, D), BF16),
        compiler_params=_params(("parallel",)),
    )(za, zb, o, zd, proj, wa, wb, wc, wd)


def _merge_bwd(dmerged, za, zb, o, zd, proj, wa, wb, wc, wd):
    s = proj.shape[0]
    tm = _tile(s, TM_WIDE)

    def body(dm_ref, za_r, zb_r, o_r, zd_r, g_ref, wa_r, wb_r, wc_r, wd_r,
             dya, dyb, dyc, dyd, dza, dzb, d_o, dzd, dg_ref, acc_ref):
        ys = _branch_ys(za_r, zb_r, o_r, zd_r, wa_r, wb_r, wc_r, wd_r)
        dm = dm_ref[...]
        sums = []
        for b, (dy_ref, dz_ref, w_r) in enumerate(((dya, dza, wa_r), (dyb, dzb, wb_r), (dyc, d_o, wc_r), (dyd, dzd, wd_r))):
            gt = _sig(g_ref[:, b * D:(b + 1) * D])
            dg = dm * ys[b] * gt * (1.0 - gt)
            dg_ref[:, b * D:(b + 1) * D] = dg.astype(BF16)
            sums.append(_colsum(dg))
            dy = (dm * gt).astype(BF16)
            dy_ref[...] = dy
            dz_ref[...] = lax.dot_general(dy, w_r[...], _NT, preferred_element_type=F32)
        _acc_rows(acc_ref, pl.program_id(0) == 0, [jnp.concatenate(sums, axis=1)])

    bf = lambda c: jax.ShapeDtypeStruct((s, c), BF16)
    f32 = lambda c: jax.ShapeDtypeStruct((s, c), F32)
    return pl.pallas_call(
        body, name="merge_bwd", grid=(s // tm,), in_specs=[_rows(tm, D, 0)] + _merge_specs(tm),
        out_specs=[_rows(tm, D, 0)] * 4 + [_rows(tm, C512, 0), _rows(tm, C512, 0), _rows(tm, D, 0), _rows(tm, C512, 0),
                                           _rows(tm, 4 * D, 0), _full((8, 4 * D))],
        out_shape=[bf(D)] * 4 + [f32(C512), f32(C512), f32(D), f32(C512), bf(4 * D), jax.ShapeDtypeStruct((8, 4 * D), F32)],
        compiler_params=_params(("arbitrary",)),
    )(dmerged, za, zb, o, zd, proj, wa, wb, wc, wd)


def _mm_res_ln(a, w, xres, prm, name):
    s, k = a.shape
    tm = _tile(s, TM_WIDE)

    def body(a_ref, w_ref, x_ref, p_ref, y_ref, xn_ref, hn_ref):
        y = lax.dot_general(a_ref[...], w_ref[...], _NN, preferred_element_type=F32)
        y_ref[...] = y
        xh, _ = _ln_stats(ALPHA * x_ref[...] + (1.0 + p_ref[0:1, :]) * y)
        xn = xh * p_ref[3:4, :] + p_ref[4:5, :]
        xn_ref[...] = xn
        hn_ref[...] = (xn * (1.0 + p_ref[1:2, :]) + p_ref[2:3, :]).astype(BF16)

    return pl.pallas_call(
        body, name=name, grid=(s // tm,),
        in_specs=[_rows(tm, k, 0), _full((k, D)), _rows(tm, D, 0), _full((8, D))],
        out_specs=[_rows(tm, D, 0)] * 3,
        out_shape=[jax.ShapeDtypeStruct((s, D), F32), jax.ShapeDtypeStruct((s, D), F32), jax.ShapeDtypeStruct((s, D), BF16)],
        compiler_params=_params(("parallel",)),
    )(a, w, xres, prm)


def _ln_res_bwd(dres_next, dh, xres, y, prm):
    s = xres.shape[0]
    tm = _tile(s, TM)

    def body(dn_ref, dh_ref, x_ref, y_ref, p_ref, dres_ref, dy_ref, acc_ref):
        gam, lng = p_ref[0:1, :], p_ref[3:4, :]
        yv = y_ref[...]
        xh, rstd = _ln_stats(ALPHA * x_ref[...] + (1.0 + gam) * yv)
        xn = xh * lng + p_ref[4:5, :]
        dh_v = dh_ref[...]
        dxn = dn_ref[...] + dh_v * (1.0 + p_ref[1:2, :])
        dr = _ln_bwd(dxn * lng, xh, rstd)
        dres_ref[...] = ALPHA * dr
        dy_ref[...] = ((1.0 + gam) * dr).astype(BF16)
        _acc_rows(acc_ref, pl.program_id(0) == 0,
                  [_colsum(dr * yv), _colsum(dh_v * xn), _colsum(dh_v), _colsum(dxn * xh), _colsum(dxn)])

    return pl.pallas_call(
        body, name="ln_res_bwd", grid=(s // tm,),
        in_specs=[_rows(tm, D, 0)] * 4 + [_full((8, D))],
        out_specs=[_rows(tm, D, 0), _rows(tm, D, 0), _full((8, D))],
        out_shape=[jax.ShapeDtypeStruct((s, D), F32), jax.ShapeDtypeStruct((s, D), BF16), jax.ShapeDtypeStruct((8, D), F32)],
        compiler_params=_params(("arbitrary",)),
    )(dres_next, dh, xres, y, prm)


def _ffn_specs(tm, tc, nc):
    pv = pl.BlockSpec((H8, tc), lambda j, i: (jnp.maximum(i * (tm // H8) - 1, 0), j))
    mv = pl.BlockSpec((tm, tc), lambda j, i: (i, j))
    pg = pl.BlockSpec((H8, tc), lambda j, i: (jnp.maximum(i * (tm // H8) - 1, 0), j + nc))
    mg = pl.BlockSpec((tm, tc), lambda j, i: (i, j + nc))
    wv = pl.BlockSpec((8, tc), lambda j, i: (0, j))
    wg = pl.BlockSpec((8, tc), lambda j, i: (0, j + nc))
    return pv, mv, pg, mg, wv, wg


def _ffn_bufs(bv, bg, pv, mv, pg, mg, first, tm):
    for buf, p, m in ((bv, pv, mv), (bg, pg, mg)):
        pp = p[...]
        buf[0:H8, :] = jnp.where(first, jnp.zeros_like(pp), pp)
        buf[H8:H8 + tm, :] = m[...]


def _ffn_act_fwd(up, w8):
    s = up.shape[0]
    tm, tc = _tile(s, TM_WIDE), FFN_TC
    nc = D_FF // tc

    def body(pv, mv, pg, mg, wv, wg, a_ref, bv, bg):
        _ffn_bufs(bv, bg, pv, mv, pg, mg, pl.program_id(1) == 0, tm)
        val, gate = _conv3(wv, bv, tm), _conv3(wg, bg, tm)
        a_ref[...] = (gate * _sig(gate) * val).astype(BF16)

    return pl.pallas_call(
        body, name="ffn_act_fwd", grid=(nc, s // tm), in_specs=list(_ffn_specs(tm, tc, nc)),
        out_specs=pl.BlockSpec((tm, tc), lambda j, i: (i, j)), out_shape=jax.ShapeDtypeStruct((s, D_FF), BF16),
        scratch_shapes=[pltpu.VMEM((tm + H8, tc), F32)] * 2, compiler_params=_params(("parallel", "parallel")),
    )(up, up, up, up, w8, w8)


def _ffn_act_bwd1(da, up, w8):
    s = up.shape[0]
    tm, tc = _tile(s, TM_WIDE), FFN_TC
    nc = D_FF // tc

    def body(da_ref, pv, mv, pg, mg, wv, wg, dv_ref, dg_ref, dwv_ref, dwg_ref, bv, bg):
        first = pl.program_id(1) == 0
        _ffn_bufs(bv, bg, pv, mv, pg, mg, first, tm)
        val, gate = _conv3(wv, bv, tm), _conv3(wg, bg, tm)
        sg = _sig(gate)
        dav = da_ref[...]
        dval = dav * gate * sg
        dgate = dav * val * (sg * (1.0 + gate * (1.0 - sg)))
        dv_ref[...] = dval
        dg_ref[...] = dgate
        _acc_rows(dwv_ref, first, [_colsum(dval * bv[pl.ds(H8 - SC_W + 1 + k, tm), :]) for k in range(SC_W)])
        _acc_rows(dwg_ref, first, [_colsum(dgate * bg[pl.ds(H8 - SC_W + 1 + k, tm), :]) for k in range(SC_W)])

    blk = pl.BlockSpec((tm, tc), lambda j, i: (i, j))
    return pl.pallas_call(
        body, name="ffn_act_bwd1", grid=(nc, s // tm), in_specs=[blk] + list(_ffn_specs(tm, tc, nc)),
        out_specs=[blk, blk, pl.BlockSpec((8, tc), lambda j, i: (0, j)), pl.BlockSpec((8, tc), lambda j, i: (0, j))],
        out_shape=[jax.ShapeDtypeStruct((s, D_FF), F32)] * 2 + [jax.ShapeDtypeStruct((8, D_FF), F32)] * 2,
        scratch_shapes=[pltpu.VMEM((tm + H8, tc), F32)] * 2, compiler_params=_params(("parallel", "arbitrary")),
    )(da, up, up, up, up, w8, w8)


def _ffn_act_bwd2(dval, dgate, w8):
    s = dval.shape[0]
    tm, tc = _tile(s, TM_WIDE), FFN_TC
    nc = D_FF // tc
    nt = s // tm
    last = s // H8 - 1

    def body(vm, vn, gm, gn, wv, wg, dup_ref, buf):
        i = pl.program_id(1)
        half = pl.program_id(0) // nc

        def run(m, n, w):
            buf[0:tm, :] = m[...]
            nx = n[...]
            buf[tm:tm + H8, :] = jnp.where(i == nt - 1, jnp.zeros_like(nx), nx)
            dup_ref[...] = _conv3_t(w, buf, tm).astype(BF16)

        @pl.when(half == 0)
        def _():
            run(vm, vn, wv)

        @pl.when(half == 1)
        def _():
            run(gm, gn, wg)

    col = lambda j: j % nc
    main = pl.BlockSpec((tm, tc), lambda j, i: (i, col(j)))
    nxt = pl.BlockSpec((H8, tc), lambda j, i: (jnp.minimum((i + 1) * (tm // H8), last), col(j)))
    wv = pl.BlockSpec((8, tc), lambda j, i: (0, col(j)))
    wg = pl.BlockSpec((8, tc), lambda j, i: (0, col(j) + nc))
    return pl.pallas_call(
        body, name="ffn_act_bwd2", grid=(2 * nc, nt), in_specs=[main, nxt, main, nxt, wv, wg],
        out_specs=pl.BlockSpec((tm, tc), lambda j, i: (i, j)), out_shape=jax.ShapeDtypeStruct((s, 2 * D_FF), BF16),
        scratch_shapes=[pltpu.VMEM((tm + H8, tc), F32)], compiler_params=_params(("parallel", "parallel")),
    )(dval, dval, dgate, dgate, w8, w8)


def _loss_head(y, target):
    s = y.shape[0]
    tm = _tile(s, TM)

    def body(y_ref, t_ref, dy_ref, l_ref):
        err = y_ref[...] - t_ref[...]
        dy_ref[...] = err * (1.0 / D)
        part = 0.5 * jnp.sum(jnp.mean(err * err, axis=-1, keepdims=True), axis=0, keepdims=True)

        @pl.when(pl.program_id(0) == 0)
        def _():
            l_ref[...] = jnp.zeros_like(l_ref)
        l_ref[...] += part

    return pl.pallas_call(
        body, name="loss_head", grid=(s // tm,),
        in_specs=[_rows(tm, D, 0)] * 2, out_specs=[_rows(tm, D, 0), _full((8, LANE))],
        out_shape=[jax.ShapeDtypeStruct((s, D), F32), jax.ShapeDtypeStruct((8, LANE), F32)],
        compiler_params=_params(("arbitrary",)),
    )(y, target)


def _silu_rows(c_all):
    def body(c_ref, o_ref):
        cv = c_ref[...]
        o_ref[...] = jnp.concatenate([cv * _sig(cv), jnp.zeros((N_DEV, D), F32)], axis=0).astype(BF16)

    return pl.pallas_call(
        body, name="silu_rows", grid=(1,), in_specs=[_full((N_DEV, D))], out_specs=_full((2 * N_DEV, D)),
        out_shape=jax.ShapeDtypeStruct((2 * N_DEV, D), BF16), compiler_params=_params(("arbitrary",)),
    )(c_all)


GRAD_ROWS = 512


def _adamw(parts, w, m, v, name):
    n, r, _ = parts.shape
    tr = GRAD_ROWS
    assert r % tr == 0, r

    def body(p_ref, w_ref, m_ref, v_ref, g_out, d_out, m_out, v_out):
        g = p_ref[0]
        for j in range(1, n):
            g = g + p_ref[j]
        mn = ADAM_B1 * m_ref[...] + (1.0 - ADAM_B1) * g
        vn = ADAM_B2 * v_ref[...] + (1.0 - ADAM_B2) * (g * g)
        m_hat = mn / (1.0 - ADAM_B1 ** ADAM_STEP)
        v_hat = vn / (1.0 - ADAM_B2 ** ADAM_STEP)
        g_out[...] = g
        d_out[...] = -ADAM_LR * (m_hat / (jnp.sqrt(v_hat) + ADAM_EPS) + ADAM_WD * w_ref[...])
        m_out[...] = mn
        v_out[...] = vn

    blk = pl.BlockSpec((tr, LANE), lambda i: (i, 0))
    return pl.pallas_call(
        body, name=name, grid=(r // tr,),
        in_specs=[pl.BlockSpec((n, tr, LANE), lambda i: (0, i, 0)), blk, blk, blk], out_specs=[blk] * 4,
        out_shape=[jax.ShapeDtypeStruct((r, LANE), F32)] * 4, compiler_params=_params(("parallel",)),
    )(parts, w, m, v)


def _peers():
    ix, iy, ic = lax.axis_index("x"), lax.axis_index("y"), lax.axis_index("c")
    me = 4 * ix + 2 * iy + ic
    out = []
    for k in range(1, N_DEV):
        px = 1 - ix if (k >> 2) & 1 else ix
        py = 1 - iy if (k >> 1) & 1 else iy
        pc = 1 - ic if k & 1 else ic
        out.append(((px, py, pc), 4 * px + 2 * py + pc))
    return me, out


_HBM = pl.BlockSpec(memory_space=pltpu.HBM)


def _exchange(x, name, gather):
    shape = ((N_DEV,) + x.shape) if gather else x.shape

    def body(x_ref, o_ref, send_sems, recv_sems, local_sem):
        me, peers = _peers()
        src_of = (lambda p: x_ref) if gather else (lambda p: x_ref.at[p])
        local = pltpu.make_async_copy(src_of(me), o_ref.at[me], local_sem)
        local.start()
        sends = []
        for k, (dev, p) in enumerate(peers):
            cp = pltpu.make_async_remote_copy(src_ref=src_of(p), dst_ref=o_ref.at[me], send_sem=send_sems.at[k],
                                              recv_sem=recv_sems.at[k], device_id=dev, device_id_type=MESH)
            cp.start()
            sends.append(cp)
        for k, (dev, p) in enumerate(peers):
            pltpu.make_async_remote_copy(src_ref=src_of(p), dst_ref=o_ref.at[p], send_sem=send_sems.at[k],
                                         recv_sem=recv_sems.at[k], device_id=dev, device_id_type=MESH).wait_recv()
        for cp in sends:
            cp.wait_send()
        local.wait()

    return pl.pallas_call(
        body, name=name, in_specs=[_HBM], out_specs=_HBM, out_shape=jax.ShapeDtypeStruct(shape, x.dtype),
        scratch_shapes=[pltpu.SemaphoreType.DMA((N_DEV - 1,)), pltpu.SemaphoreType.DMA((N_DEV - 1,)),
                        pltpu.SemaphoreType.DMA],
    )(x)


def _pack_rows(arrs, row_mult):
    flat = jnp.concatenate([a.reshape(-1) for a in arrs])
    n = flat.shape[0]
    pad = (-n) % (LANE * row_mult)
    if pad:
        flat = jnp.concatenate([flat, jnp.zeros((pad,), flat.dtype)])
    return flat.reshape(-1, LANE)


def _pack_cols(arrs, row_mult):
    flat = jnp.concatenate(arrs, axis=1)
    n = flat.shape[1]
    pad = (-n) % (LANE * row_mult)
    if pad:
        flat = jnp.concatenate([flat, jnp.zeros((flat.shape[0], pad), flat.dtype)], axis=1)
    return flat.reshape(flat.shape[0], -1, LANE)


def _unpack(flat, shapes):
    out, off = [], 0
    lead = flat.shape[:-1]
    for shp in shapes:
        n = 1
        for d_ in shp:
            n *= d_
        out.append(flat[..., off:off + n].reshape(lead + tuple(shp)))
        off += n
    return out


BIG = (("w_in", 2), ("w_conv_out", 2), ("w_sc_out", 2), ("w_uq", 2), ("w_ukv", 2), ("w_mla_out", 2),
       ("w_pool_out", 2), ("w_o", 1), ("w_up", 2), ("w_down", 1))
TAPS = (("conv_dw", 2), ("sc_dw", 2), ("ffn_dw", 2))
SMALL = ("b_ada", "b_in", "conv_ln_g", "conv_ln_b", "q_norm_g", "kv_norm_g", "w_pool", "pool_scale",
         "ln1_g", "ln1_b", "ln2_g", "ln2_b")
WEIGHTS = ("w_ada", "b_ada", "w_in", "b_in", "conv_dw", "conv_ln_g", "conv_ln_b", "w_conv_out", "sc_dw", "w_sc_out",
           "q_norm_g", "w_uq", "kv_norm_g", "w_ukv", "w_mla_out", "w_pool", "pool_scale", "w_pool_out", "w_o",
           "ln1_g", "ln1_b", "w_up", "ffn_dw", "w_down", "ln2_g", "ln2_b")


def _join(g, axis):
    g = jnp.moveaxis(g, 0, axis)
    shp = list(g.shape)
    shp[axis:axis + 2] = [shp[axis] * shp[axis + 1]]
    return g.reshape(shp)


def _split(full, axis):
    shp = list(full.shape)
    shp[axis:axis + 1] = [N_DEV, shp[axis] // N_DEV]
    return jnp.moveaxis(full.reshape(shp), axis, 0)


def _pad_rows(a, rows):
    return jnp.concatenate([a, jnp.zeros((rows - a.shape[0],) + a.shape[1:], a.dtype)], axis=0)


def _proj_cols(w):
    z = lambda n: jnp.zeros(w.shape[:-1] + (n,), w.dtype)
    return jnp.concatenate([w[..., 3488:7584], w[..., 0:2944], z(NOPE), w[..., 2944:2976], z(HEAD_PAD - QK_DIM),
                            w[..., 2976:3488]], axis=-1)


def _proj_cols_inv(w):
    return jnp.concatenate([w[..., CA0:KR0], w[..., KR0 + NOPE:KR0 + QK_DIM], w[..., PU0:NPROJ], w[..., 0:CA0]], axis=-1)


def _layer_weights(full, small, l):
    w = {}
    w["w_in"] = _proj_cols(full["w_in"][l])
    w["b_in"] = _proj_cols(small["b_in"][l][None, :])
    w["conv_w"] = _pad_rows(full["conv_dw"][l], 32)
    w["sc_w"] = _pad_rows(full["sc_dw"][l], 8)
    w["ffn_w"] = _pad_rows(full["ffn_dw"][l], 8)
    uq = full["w_uq"][l].reshape(256, HEADS, QK_DIM)
    w["w_uq"] = jnp.concatenate([uq, jnp.zeros((256, HEADS, HEAD_PAD - QK_DIM), uq.dtype)], axis=-1).reshape(256, -1)
    ukv = full["w_ukv"][l].reshape(128, HEADS, 2, NOPE)
    w["w_ukv"] = jnp.concatenate([ukv, jnp.zeros_like(ukv)], axis=-1).reshape(128, -1)
    mo = full["w_mla_out"][l].reshape(HEADS, NOPE, D)
    w["w_c"] = jnp.concatenate([mo, jnp.zeros_like(mo)], axis=1).reshape(HEADS * HEAD_PAD, D)
    w["w_a"], w["w_b"], w["w_d"] = full["w_conv_out"][l], full["w_sc_out"][l], full["w_pool_out"][l]
    w["w_o"], w["w_up"], w["w_down"] = full["w_o"][l], full["w_up"][l], full["w_down"][l]
    w["w_pool"] = small["w_pool"][l].astype(BF16)
    for n in ("conv_ln_g", "conv_ln_b", "q_norm_g", "kv_norm_g", "pool_scale"):
        w[n] = small[n][l][None, :]
    return w


def _prm(rows):
    z = jnp.zeros((D,), F32)
    rows = list(rows) + [z] * (8 - len(rows))
    return jnp.stack(rows)


def _layer_fwd(x, h1, w, prm1, prm2, tabs):
    ct, s1t, s2t = tabs
    proj = _mm(h1, w["w_in"], "nn", "proj_fwd", bias=w["b_in"], tn=768)
    yconv, za = _conv_a_fwd(proj, w["conv_w"], w["conv_ln_g"], w["conv_ln_b"])
    zb = _sc_fwd(proj, w["sc_w"])
    qn, kvn = _lat_fwd(proj, w["q_norm_g"], w["kv_norm_g"])
    qp = _mm(qn, w["w_uq"], "nn", "uq_fwd")
    kvp = _mm(kvn, w["w_ukv"], "nn", "ukv_fwd")
    q, k, v = _rope_fwd(qp, kvp, proj, ct, s1t, s2t)
    o, lse = _attn_fwd(q, k, v)
    zd = _pool_fwd(proj, w["w_pool"], w["pool_scale"])
    merged = _merge_fwd(za, zb, o, zd, proj, w["w_a"], w["w_b"], w["w_c"], w["w_d"])
    mix, x1, h2 = _mm_res_ln(merged, w["w_o"], x, prm1, "o_res_ln")
    up = _mm(h2, w["w_up"], "nn", "up_fwd", tn=1408)
    a = _ffn_act_fwd(up, w["ffn_w"])
    ffn, x2, h_next = _mm_res_ln(a, w["w_down"], x1, prm2, "down_res_ln")
    res = dict(x=x, h1=h1, proj=proj, yconv=yconv, za=za, zb=zb, qn=qn, kvn=kvn, q=q, k=k, v=v, o=o, lse=lse, zd=zd,
               merged=merged, mix=mix, x1=x1, h2=h2, up=up, a=a, ffn=ffn)
    return x2, h_next, res


def _layer_bwd(dres_next, dh_next, r, w, prm1, prm2, tabs):
    ct, s1t, s2t = tabs
    s = r["x"].shape[0]
    g = {}
    dres2, dffn, acc2 = _ln_res_bwd(dres_next, dh_next, r["x1"], r["ffn"], prm2)
    da = _mm(dffn, w["w_down"], "nt", "down_bwd_x", tn=1408)
    g["w_down"] = _mm(r["a"], dffn, "tn", "down_bwd_w", tm=1408)
    dval, dgate, dwv, dwg = _ffn_act_bwd1(da, r["up"], w["ffn_w"])
    g["ffn_dw"] = jnp.concatenate([dwv[:3], dwg[:3]], axis=1)
    dup = _ffn_act_bwd2(dval, dgate, w["ffn_w"])
    dh2 = _mm(dup, w["w_up"], "nt", "up_bwd_x", tk=1408)
    g["w_up"] = _mm(r["h2"], dup, "tn", "up_bwd_w", tn=1408)
    dres1, dmix, acc1 = _ln_res_bwd(dres2, dh2, r["x"], r["mix"], prm1)
    dmerged = _mm(dmix, w["w_o"], "nt", "o_bwd_x")
    g["w_o"] = _mm(r["merged"], dmix, "tn", "o_bwd_w")
    (dya, dyb, dyc, dyd, dza, dzb, d_o, dzd, dgates, accg) = _merge_bwd(
        dmerged, r["za"], r["zb"], r["o"], r["zd"], r["proj"], w["w_a"], w["w_b"], w["w_c"], w["w_d"])
    g["w_conv_out"] = _mm(r["za"], dya, "tn", "branch_bwd_w")
    g["w_sc_out"] = _mm(r["zb"], dyb, "tn", "branch_bwd_w")
    g["w_pool_out"] = _mm(r["zd"], dyd, "tn", "branch_bwd_w")
    gwc = _mm(r["o"], dyc, "tn", "mla_out_bwd_w")
    g["w_mla_out"] = gwc.reshape(HEADS, HEAD_PAD, D)[:, :NOPE].reshape(HEADS * NOPE, D)
    dyconv, dconv_w, acca = _conv_a_bwd1(dza, r["yconv"], r["proj"], w["conv_ln_g"], w["conv_ln_b"])
    g["conv_dw"], g["conv_ln_g"], g["conv_ln_b"] = dconv_w[:CONV_W], acca[0], acca[1]
    d_ca, d_cb, acca2 = _conv_a_bwd2(dyconv, r["proj"], w["conv_w"])
    dconv, d_bg, dsc_w, accb1 = _sc_bwd1(dzb, r["proj"], w["sc_w"])
    g["sc_dw"] = dsc_w[:SC_W]
    d_cg, d_sx, accb2 = _sc_bwd2(dconv, r["proj"], w["sc_w"])
    stats, dob = _attn_prep(d_o, r["o"], r["lse"])
    dq = _attn_bwd_dq(r["q"], r["k"], r["v"], dob, stats)
    dkv = _attn_bwd_dkv(r["q"], r["k"], r["v"], dob, stats)
    dqp, d_kr, acckr = _rope_bwd(dq, dkv, ct, s1t, s2t)
    dqn = _mm(dqp, w["w_uq"], "nt", "uq_bwd_x")
    guq = _mm(r["qn"], dqp, "tn", "uq_bwd_w")
    g["w_uq"] = guq.reshape(256, HEADS, HEAD_PAD)[:, :, :QK_DIM].reshape(256, HEADS * QK_DIM)
    dkvn = _mm(dkv, w["w_ukv"], "nt", "ukv_bwd_x")
    gukv = _mm(r["kvn"], dkv, "tn", "ukv_bwd_w")
    g["w_ukv"] = gukv.reshape(128, HEADS, 2, HEAD_PAD)[..., :NOPE].reshape(128, HEADS * 2 * NOPE)
    d_ql, d_kvl, accq, acckv = _lat_bwd(dqn, dkvn, r["proj"], w["q_norm_g"], w["kv_norm_g"])
    g["q_norm_g"], g["kv_norm_g"] = accq[0], acckv[0]
    dpd, g["w_pool"], accd1 = _pool_bwd1(dzd, r["proj"], w["w_pool"], w["pool_scale"])
    g["pool_scale"] = accd1[0]
    d_pu, accd2 = _pool_bwd2(dpd, s)
    dproj = jnp.concatenate([dgates, d_ca, d_cb, d_bg, d_cg, d_sx, d_ql, d_kvl, d_kr, d_pu], axis=1)
    db = jnp.concatenate([accg[0], acca2[0], acca2[1], accb1[0], accb2[0], accb2[1], accq[1], acckv[1], acckr[0], accd2[0]])
    g["b_in"] = _proj_cols_inv(db)
    dh1 = _mm(dproj, w["w_in"], "nt", "proj_bwd_x", tk=1536)
    g["w_in"] = _proj_cols_inv(_mm(r["h1"], dproj, "tn", "proj_bwd_w", tn=768))
    g["ln1_g"], g["ln1_b"], g["ln2_g"], g["ln2_b"] = acc1[3], acc1[4], acc2[3], acc2[4]
    return dres1, dh1, g, (acc1, acc2)


def _rope_tables(positions):
    half = ROPE // 2
    inv = 1.0 / (ROPE_THETA ** (jnp.arange(0, ROPE, 2, dtype=F32) / ROPE))
    ang = positions.astype(F32)[:, None] * inv
    cos, sin = jnp.cos(ang), jnp.sin(ang)
    s = positions.shape[0]
    z = lambda n: jnp.zeros((s, n), F32)
    ct = jnp.concatenate([jnp.ones((s, NOPE), F32), cos, cos, z(HEAD_PAD - QK_DIM)], axis=1)
    s1t = jnp.concatenate([z(NOPE), -sin, z(half), z(HEAD_PAD - QK_DIM)], axis=1)
    s2t = jnp.concatenate([z(NOPE), z(half), sin, z(HEAD_PAD - QK_DIM)], axis=1)
    return ct, s1t, s2t


def _local_step(x, mod, positions, full, small, target):
    tabs = _rope_tables(positions)
    ws = [_layer_weights(full, small, l) for l in range(DEPTH)]
    zero = jnp.zeros((D,), F32)
    prm1s, prm2s = [], []
    for l in range(DEPTH):
        sh1, sc1, g1, sh2, sc2, g2 = (mod[l, j] for j in range(6))
        nxt = (mod[l + 1, 1], mod[l + 1, 0]) if l + 1 < DEPTH else (zero, zero)
        prm1s.append(_prm([g1, sc2, sh2, small["ln1_g"][l], small["ln1_b"][l]]))
        prm2s.append(_prm([g2, nxt[0], nxt[1], small["ln2_g"][l], small["ln2_b"][l]]))
    prm0 = _prm([zero, mod[0, 1], mod[0, 0]])
    h = _mod_fwd(x, prm0)
    res = []
    xc = x
    for l in range(DEPTH):
        xc, h, r = _layer_fwd(xc, h, ws[l], prm1s[l], prm2s[l], tabs)
        res.append(r)
    dy, lacc = _loss_head(xc, target)
    loss = lacc[0, 0]
    dres, dh = dy, jnp.zeros_like(dy)
    grads = [None] * DEPTH
    accs = [None] * DEPTH
    for l in reversed(range(DEPTH)):
        dres, dh, grads[l], accs[l] = _layer_bwd(dres, dh, res[l], ws[l], prm1s[l], prm2s[l], tabs)
    dx, acc0 = _mod_bwd(dres, dh, x, prm0)
    dmod = []
    for l in range(DEPTH):
        acc1, acc2 = accs[l]
        dsc1, dsh1 = (acc0[0], acc0[1]) if l == 0 else (accs[l - 1][1][1], accs[l - 1][1][2])
        dmod.append(jnp.stack([dsh1, dsc1, acc1[0], acc1[2], acc1[1], acc2[0]]))
    return loss, dx, grads, jnp.stack(dmod)


ADA_SHARD = 6 * D // N_DEV


def _step(p):
    me = 4 * lax.axis_index("x") + 2 * lax.axis_index("y") + lax.axis_index("c")
    x, target, positions = p["x"][0], p["loss_target"][0], p["positions"][0]

    tap_shapes = [p[n].shape for n, _ in TAPS] + [(D,)]
    small_g = _exchange(_pack_rows([p[n] for n, _ in TAPS] + [p["c"][0]], 8), "gather_taps", True)
    parts = _unpack(small_g.reshape(N_DEV, -1), tap_shapes)
    full = {n: _join(g, ax) for (n, ax), g in zip(TAPS, parts[:-1])}
    c_all = parts[-1]
    big_shapes = [p[n].shape for n, _ in BIG]
    big_g = _exchange(_pack_rows([p[n].astype(BF16) for n, _ in BIG], 16), "gather_weights", True)
    for (n, ax), g in zip(BIG, _unpack(big_g.reshape(N_DEV, -1), big_shapes)):
        full[n] = _join(g, ax)
    small = {n: p[n] for n in SMALL}

    c_act = _silu_rows(c_all)
    w_ada_cols = jnp.moveaxis(p["w_ada"], 0, 1).reshape(D, DEPTH * ADA_SHARD)
    b_shard = lax.dynamic_slice_in_dim(p["b_ada"], me * ADA_SHARD, ADA_SHARD, axis=1).reshape(1, DEPTH * ADA_SHARD)
    mod_sh = _mm(c_act, w_ada_cols, "nn", "ada_fwd", bias=b_shard)[:N_DEV]
    mod_x = _exchange(_pack_cols([mod_sh], 8), "scatter_mod", False)
    mod = mod_x.reshape(N_DEV, -1)[:, :DEPTH * ADA_SHARD].reshape(N_DEV, DEPTH, ADA_SHARD)
    mod = jnp.moveaxis(mod, 0, 1).reshape(DEPTH, 6, D)

    loss_local, dx, grads, dmod = _local_step(x, mod, positions, full, small, target)
    loss = lax.psum(loss_local, ("x", "y", "c"))
    gfull = {n: jnp.stack([grads[l][n] for l in range(DEPTH)]) for n in grads[0]}

    out = {"loss": loss, "grad_x": dx[None]}

    def emit(names, g, dlt, mn, vn, shapes):
        for n, gi, di, mi, vi in zip(names, _unpack(g, shapes), _unpack(dlt, shapes), _unpack(mn, shapes), _unpack(vn, shapes)):
            out["grad_" + n], out["delta_" + n], out["new_m_" + n], out["new_v_" + n] = gi, di, mi, vi

    small_parts = [dmod.reshape(DEPTH, 6 * D)] + [gfull[n] for n in SMALL[1:]]
    small_all = _exchange(_pack_rows(small_parts, GRAD_ROWS), "gather_small_grads", True)
    small_shapes = [p[n].shape for n in SMALL]
    sg, sd, sm, sv = _adamw(small_all, *[_pack_rows([p[pre + n] for n in SMALL], GRAD_ROWS) for pre in ("", "m_", "v_")],
                            name="adamw_small")
    emit(SMALL, *[t.reshape(-1) for t in (sg, sd, sm, sv)], small_shapes)

    dmod_all = small_all.reshape(N_DEV, -1)[:, :DEPTH * 6 * D].reshape(N_DEV, DEPTH, 6 * D)
    dmod_sh = lax.dynamic_slice_in_dim(dmod_all, me * ADA_SHARD, ADA_SHARD, axis=2).reshape(N_DEV, DEPTH * ADA_SHARD)
    g_ada = _mm(c_act, _pad_rows(dmod_sh, 2 * N_DEV), "tn", "ada_bwd_w")
    g_ada = jnp.moveaxis(g_ada.reshape(D, DEPTH, ADA_SHARD), 1, 0)
    ag, ad, am, av = _adamw(_pack_rows([g_ada], GRAD_ROWS)[None], *[_pack_rows([p[pre + "w_ada"]], GRAD_ROWS) for pre in ("", "m_", "v_")],
                            name="adamw_ada")
    emit(("w_ada",), *[t.reshape(-1) for t in (ag, ad, am, av)], [p["w_ada"].shape])

    shard_names = [n for n, _ in BIG + TAPS]
    pieces = [_split(gfull[n], ax).reshape(N_DEV, -1) for n, ax in BIG + TAPS]
    recv = _exchange(_pack_cols(pieces, GRAD_ROWS), "scatter_grads", False)
    bg, bd, bm, bv = _adamw(recv, *[_pack_rows([p[pre + n] for n in shard_names], GRAD_ROWS) for pre in ("", "m_", "v_")],
                            name="adamw_sharded")
    emit(shard_names, *[t.reshape(-1) for t in (bg, bd, bm, bv)], [p[n].shape for n in shard_names])
    return out


_ARG_NAMES = ("x", "c", "positions") + WEIGHTS + ("loss_target",) + tuple("m_" + n for n in WEIGHTS) + tuple("v_" + n for n in WEIGHTS)
_OUT_NAMES = ("loss", "grad_x") + tuple(pre + n for pre in ("grad_", "delta_", "new_m_", "new_v_") for n in WEIGHTS)


def kernel(x, c, positions, w_ada, b_ada, w_in, b_in, conv_dw, conv_ln_g, conv_ln_b, w_conv_out, sc_dw, w_sc_out, q_norm_g, w_uq, kv_norm_g, w_ukv, w_mla_out, w_pool, pool_scale, w_pool_out, w_o, ln1_g, ln1_b, w_up, ffn_dw, w_down, ln2_g, ln2_b, loss_target, m_w_ada, m_b_ada, m_w_in, m_b_in, m_conv_dw, m_conv_ln_g, m_conv_ln_b, m_w_conv_out, m_sc_dw, m_w_sc_out, m_q_norm_g, m_w_uq, m_kv_norm_g, m_w_ukv, m_w_mla_out, m_w_pool, m_pool_scale, m_w_pool_out, m_w_o, m_ln1_g, m_ln1_b, m_w_up, m_ffn_dw, m_w_down, m_ln2_g, m_ln2_b, v_w_ada, v_b_ada, v_w_in, v_b_in, v_conv_dw, v_conv_ln_g, v_conv_ln_b, v_w_conv_out, v_sc_dw, v_w_sc_out, v_q_norm_g, v_w_uq, v_kv_norm_g, v_w_ukv, v_w_mla_out, v_w_pool, v_pool_scale, v_w_pool_out, v_w_o, v_ln1_g, v_ln1_b, v_w_up, v_ffn_dw, v_w_down, v_ln2_g, v_ln2_b):
    args = locals()
    out = _step({n: args[n] for n in _ARG_NAMES})
    return tuple(out[n] for n in _OUT_NAMES)
```

```python
import functools

import jax
import jax.numpy as jnp
from jax import lax
from jax.experimental import pallas as pl
from jax.experimental.pallas import tpu as pltpu

F32 = jnp.float32
BF16 = jnp.bfloat16

N_DEV = 8
DEPTH = 4
D = 1024
CONV_W = 31
HEADS = 8
HEAD_PAD = 128
QK_DIM = 96
NOPE = 64
ROPE = 32
ROPE_THETA = 10000.0
D_FF = 2816
LN_EPS = 1e-5
RMS_EPS = 1e-6
ALPHA = (2.0 * DEPTH) ** 0.25
ATT_SCALE = QK_DIM ** -0.5
POOL_WINDOWS = (2, 4, 8, 16)

ADAM_LR = 0.001
ADAM_B1 = 0.9
ADAM_B2 = 0.999
ADAM_EPS = 1e-08
ADAM_WD = 0.01
ADAM_STEP = 10

GATES0 = 0
CA0 = 4096
CB0 = 4608
SBG0 = 5120
SCG0 = 5632
SX0 = 6144
QL0 = 6656
KVL0 = 6912
KR0 = 7040
PU0 = 7168
NPROJ = 7680

LANE = 128
TM = 512
TM_WIDE = 256
TQ = 512
FFN_TC = 1408
VMEM_LIMIT = 56 * 1024 * 1024

MESH = pl.DeviceIdType.MESH


def _sig(x):
    return 1.0 / (1.0 + jnp.exp(-x))


def _tile(n, pref):
    if n <= pref:
        return n
    t = (pref // LANE) * LANE
    while t >= LANE:
        if n % t == 0:
            return t
        t -= LANE
    raise ValueError(f"no lane-aligned tile for {n}")


def _params(sem):
    return pltpu.CompilerParams(dimension_semantics=sem, vmem_limit_bytes=VMEM_LIMIT)


def _full(shape):
    nd = len(shape)
    return pl.BlockSpec(shape, lambda *_: (0,) * nd)


def _rows(tm, cw, cb):
    return pl.BlockSpec((tm, cw), lambda i: (i, cb))


def _prev(tm, hb, cw, cb):
    r = tm // hb
    return pl.BlockSpec((hb, cw), lambda i: (jnp.maximum(i * r - 1, 0), cb))


def _next(tm, hb, cw, cb, s):
    r = tm // hb
    last = s // hb - 1
    return pl.BlockSpec((hb, cw), lambda i: (jnp.minimum((i + 1) * r, last), cb))


def _acc_rows(ref, first, rows):
    @pl.when(first)
    def _():
        ref[...] = jnp.zeros_like(ref)
    for r, v in enumerate(rows):
        ref[r:r + 1, :] += v


def _colsum(v):
    return jnp.sum(v, axis=0, keepdims=True)


def _ln_stats(r):
    mu = jnp.mean(r, axis=-1, keepdims=True)
    xc = r - mu
    var = jnp.mean(xc * xc, axis=-1, keepdims=True)
    rstd = lax.rsqrt(var + LN_EPS)
    return xc * rstd, rstd


def _ln_bwd(dxh, xh, rstd):
    return rstd * (dxh - jnp.mean(dxh, axis=-1, keepdims=True) - xh * jnp.mean(dxh * xh, axis=-1, keepdims=True))


_DIMS = {"nn": ((1,), (0,)), "nt": ((1,), (1,)), "tn": ((0,), (0,))}


def _mm(a, b, mode, name, *, out_dtype=F32, bias=None, tm=512, tn=1024, tk=2048):
    if mode == "nn":
        (m, k), (k2, n) = a.shape, b.shape
    elif mode == "nt":
        (m, k), (n, k2) = a.shape, b.shape
    else:
        (k, m), (k2, n) = a.shape, b.shape
    assert k == k2, (a.shape, b.shape, mode)
    tm, tn, tk = _tile(m, tm), _tile(n, tn), _tile(k, tk)
    nk = k // tk
    dims = (_DIMS[mode], ((), ()))
    has_bias = bias is not None

    def body(*refs):
        if has_bias:
            a_ref, b_ref, bias_ref, o_ref = refs[:4]
        else:
            a_ref, b_ref, o_ref = refs[:3]
        p = lax.dot_general(a_ref[...].astype(BF16), b_ref[...].astype(BF16), dims, preferred_element_type=F32)

        def finish(r):
            if has_bias:
                r = r + bias_ref[...]
            o_ref[...] = r.astype(out_dtype)

        if nk == 1:
            finish(p)
        else:
            acc = refs[-1]
            kk = pl.program_id(2)

            @pl.when(kk == 0)
            def _():
                acc[...] = p

            @pl.when(kk > 0)
            def _():
                acc[...] += p

            @pl.when(kk == nk - 1)
            def _():
                finish(acc[...])

    if mode == "nn":
        a_spec = pl.BlockSpec((tm, tk), lambda i, j, kk: (i, kk))
        b_spec = pl.BlockSpec((tk, tn), lambda i, j, kk: (kk, j))
    elif mode == "nt":
        a_spec = pl.BlockSpec((tm, tk), lambda i, j, kk: (i, kk))
        b_spec = pl.BlockSpec((tn, tk), lambda i, j, kk: (j, kk))
    else:
        a_spec = pl.BlockSpec((tk, tm), lambda i, j, kk: (kk, i))
        b_spec = pl.BlockSpec((tk, tn), lambda i, j, kk: (kk, j))
    in_specs = [a_spec, b_spec]
    args = [a, b]
    if has_bias:
        in_specs.append(pl.BlockSpec((1, tn), lambda i, j, kk: (0, j)))
        args.append(bias)
    return pl.pallas_call(
        body, name=name, grid=(m // tm, n // tn, nk),
        in_specs=in_specs, out_specs=pl.BlockSpec((tm, tn), lambda i, j, kk: (i, j)),
        out_shape=jax.ShapeDtypeStruct((m, n), out_dtype),
        scratch_shapes=[pltpu.VMEM((tm, tn), F32)] if nk > 1 else [],
        compiler_params=_params(("parallel", "parallel", "arbitrary")),
    )(*args)


def _mod_fwd(x, prm):
    s = x.shape[0]
    tm = _tile(s, TM)

    def body(x_ref, p_ref, h_ref):
        h_ref[...] = (x_ref[...] * (1.0 + p_ref[1:2, :]) + p_ref[2:3, :]).astype(BF16)

    return pl.pallas_call(
        body, name="mod_fwd", grid=(s // tm,),
        in_specs=[_rows(tm, D, 0), _full((8, D))], out_specs=_rows(tm, D, 0),
        out_shape=jax.ShapeDtypeStruct((s, D), BF16), compiler_params=_params(("parallel",)),
    )(x, prm)


def _mod_bwd(dres, dh, x, prm):
    s = x.shape[0]
    tm = _tile(s, TM)

    def body(dres_ref, dh_ref, x_ref, p_ref, dx_ref, acc_ref):
        dh_v = dh_ref[...]
        dx_ref[...] = dres_ref[...] + dh_v * (1.0 + p_ref[1:2, :])
        _acc_rows(acc_ref, pl.program_id(0) == 0, [_colsum(dh_v * x_ref[...]), _colsum(dh_v)])

    return pl.pallas_call(
        body, name="mod_bwd", grid=(s // tm,),
        in_specs=[_rows(tm, D, 0)] * 3 + [_full((8, D))],
        out_specs=[_rows(tm, D, 0), _full((8, D))],
        out_shape=[jax.ShapeDtypeStruct((s, D), F32), jax.ShapeDtypeStruct((8, D), F32)],
        compiler_params=_params(("arbitrary",)),
    )(dres, dh, x, prm)


CA_HALO = 32
C512 = 512


def _glu_buf(buf, ap, am, bp, bm, first, tm):
    glu_p = ap[...] * _sig(bp[...])
    buf[0:CA_HALO, :] = jnp.where(first, jnp.zeros_like(glu_p), glu_p)
    buf[CA_HALO:CA_HALO + tm, :] = am[...] * _sig(bm[...])


def _conv_a_fwd(proj, w32, ln_g, ln_b):
    s = proj.shape[0]
    tm = _tile(s, TM)
    ca, cb = CA0 // C512, CB0 // C512

    def body(ap, am, bp, bm, w_ref, g_ref, b_ref, yc_ref, za_ref, buf):
        _glu_buf(buf, ap, am, bp, bm, pl.program_id(0) == 0, tm)
        acc = w_ref[0:1, :] * buf[pl.ds(CA_HALO - CONV_W + 1, tm), :]
        for k in range(1, CONV_W):
            acc = acc + w_ref[k:k + 1, :] * buf[pl.ds(CA_HALO - CONV_W + 1 + k, tm), :]
        yc_ref[...] = acc
        xh, _ = _ln_stats(acc)
        y = xh * g_ref[...] + b_ref[...]
        za_ref[...] = (y * _sig(y)).astype(BF16)

    return pl.pallas_call(
        body, name="conv_a_fwd", grid=(s // tm,),
        in_specs=[_prev(tm, CA_HALO, C512, ca), _rows(tm, C512, ca), _prev(tm, CA_HALO, C512, cb), _rows(tm, C512, cb),
                  _full((32, C512)), _full((1, C512)), _full((1, C512))],
        out_specs=[_rows(tm, C512, 0), _rows(tm, C512, 0)],
        out_shape=[jax.ShapeDtypeStruct((s, C512), F32), jax.ShapeDtypeStruct((s, C512), BF16)],
        scratch_shapes=[pltpu.VMEM((tm + CA_HALO, C512), F32)],
        compiler_params=_params(("parallel",)),
    )(proj, proj, proj, proj, w32, ln_g, ln_b)


def _conv_a_bwd1(dza, yconv, proj, ln_g, ln_b):
    s = proj.shape[0]
    tm = _tile(s, TM)
    ca, cb = CA0 // C512, CB0 // C512

    def body(dza_ref, yc_ref, ap, am, bp, bm, g_ref, b_ref, dyc_ref, dw_ref, acc_ref, buf):
        first = pl.program_id(0) == 0
        xh, rstd = _ln_stats(yc_ref[...])
        g = g_ref[...]
        y = xh * g + b_ref[...]
        sg = _sig(y)
        dy = dza_ref[...] * (sg * (1.0 + y * (1.0 - sg)))
        _acc_rows(acc_ref, first, [_colsum(dy * xh), _colsum(dy)])
        dyc = _ln_bwd(dy * g, xh, rstd)
        dyc_ref[...] = dyc
        _glu_buf(buf, ap, am, bp, bm, first, tm)

        @pl.when(first)
        def _():
            dw_ref[...] = jnp.zeros_like(dw_ref)
        for k in range(CONV_W):
            dw_ref[k:k + 1, :] += _colsum(dyc * buf[pl.ds(CA_HALO - CONV_W + 1 + k, tm), :])

    return pl.pallas_call(
        body, name="conv_a_bwd1", grid=(s // tm,),
        in_specs=[_rows(tm, C512, 0), _rows(tm, C512, 0),
                  _prev(tm, CA_HALO, C512, ca), _rows(tm, C512, ca), _prev(tm, CA_HALO, C512, cb), _rows(tm, C512, cb),
                  _full((1, C512)), _full((1, C512))],
        out_specs=[_rows(tm, C512, 0), _full((32, C512)), _full((8, C512))],
        out_shape=[jax.ShapeDtypeStruct((s, C512), F32), jax.ShapeDtypeStruct((32, C512), F32),
                   jax.ShapeDtypeStruct((8, C512), F32)],
        scratch_shapes=[pltpu.VMEM((tm + CA_HALO, C512), F32)],
        compiler_params=_params(("arbitrary",)),
    )(dza, yconv, proj, proj, proj, proj, ln_g, ln_b)


def _conv_a_bwd2(dyc, proj, w32):
    s = proj.shape[0]
    tm = _tile(s, TM)
    ca, cb = CA0 // C512, CB0 // C512
    nt = s // tm

    def body(dm, dn, am, bm, w_ref, da_ref, db_ref, acc_ref, buf):
        i = pl.program_id(0)
        buf[0:tm, :] = dm[...]
        nxt = dn[...]
        buf[tm:tm + CA_HALO, :] = jnp.where(i == nt - 1, jnp.zeros_like(nxt), nxt)
        dglu = w_ref[0:1, :] * buf[pl.ds(CONV_W - 1, tm), :]
        for k in range(1, CONV_W):
            dglu = dglu + w_ref[k:k + 1, :] * buf[pl.ds(CONV_W - 1 - k, tm), :]
        sb = _sig(bm[...])
        da = dglu * sb
        db = dglu * am[...] * sb * (1.0 - sb)
        da_ref[...] = da.astype(BF16)
        db_ref[...] = db.astype(BF16)
        _acc_rows(acc_ref, i == 0, [_colsum(da), _colsum(db)])

    return pl.pallas_call(
        body, name="conv_a_bwd2", grid=(nt,),
        in_specs=[_rows(tm, C512, 0), _next(tm, CA_HALO, C512, 0, s), _rows(tm, C512, ca), _rows(tm, C512, cb),
                  _full((32, C512))],
        out_specs=[_rows(tm, C512, 0), _rows(tm, C512, 0), _full((8, C512))],
        out_shape=[jax.ShapeDtypeStruct((s, C512), BF16), jax.ShapeDtypeStruct((s, C512), BF16),
                   jax.ShapeDtypeStruct((8, C512), F32)],
        scratch_shapes=[pltpu.VMEM((tm + CA_HALO, C512), F32)],
        compiler_params=_params(("arbitrary",)),
    )(dyc, dyc, proj, proj, w32)


H8 = 8
SC_W = 3


def _sc_ubuf(buf, cp, cm, xp, xm, first, tm):
    up = cp[...] * xp[...]
    buf[0:H8, :] = jnp.where(first, jnp.zeros_like(up), up)
    buf[H8:H8 + tm, :] = cm[...] * xm[...]


def _conv3(w_ref, buf, tm):
    acc = w_ref[0:1, :] * buf[pl.ds(H8 - SC_W + 1, tm), :]
    for k in range(1, SC_W):
        acc = acc + w_ref[k:k + 1, :] * buf[pl.ds(H8 - SC_W + 1 + k, tm), :]
    return acc


def _conv3_t(w_ref, buf, tm):
    acc = w_ref[0:1, :] * buf[pl.ds(SC_W - 1, tm), :]
    for k in range(1, SC_W):
        acc = acc + w_ref[k:k + 1, :] * buf[pl.ds(SC_W - 1 - k, tm), :]
    return acc


def _sc_fwd(proj, w8):
    s = proj.shape[0]
    tm = _tile(s, TM)
    c_bg, c_cg, c_x = SBG0 // C512, SCG0 // C512, SX0 // C512

    def body(bg, cp, cm, xp, xm, w_ref, zb_ref, buf):
        _sc_ubuf(buf, cp, cm, xp, xm, pl.program_id(0) == 0, tm)
        zb_ref[...] = (bg[...] * _conv3(w_ref, buf, tm)).astype(BF16)

    return pl.pallas_call(
        body, name="sc_fwd", grid=(s // tm,),
        in_specs=[_rows(tm, C512, c_bg), _prev(tm, H8, C512, c_cg), _rows(tm, C512, c_cg),
                  _prev(tm, H8, C512, c_x), _rows(tm, C512, c_x), _full((8, C512))],
        out_specs=_rows(tm, C512, 0), out_shape=jax.ShapeDtypeStruct((s, C512), BF16),
        scratch_shapes=[pltpu.VMEM((tm + H8, C512), F32)], compiler_params=_params(("parallel",)),
    )(proj, proj, proj, proj, proj, w8)


def _sc_bwd1(dzb, proj, w8):
    s = proj.shape[0]
    tm = _tile(s, TM)
    c_bg, c_cg, c_x = SBG0 // C512, SCG0 // C512, SX0 // C512

    def body(dz_ref, bg, cp, cm, xp, xm, w_ref, dconv_ref, dbg_ref, dw_ref, acc_ref, buf):
        first = pl.program_id(0) == 0
        _sc_ubuf(buf, cp, cm, xp, xm, first, tm)
        dz = dz_ref[...]
        dbg = dz * _conv3(w_ref, buf, tm)
        dconv = dz * bg[...]
        dconv_ref[...] = dconv
        dbg_ref[...] = dbg.astype(BF16)
        _acc_rows(acc_ref, first, [_colsum(dbg)])
        _acc_rows(dw_ref, first, [_colsum(dconv * buf[pl.ds(H8 - SC_W + 1 + k, tm), :]) for k in range(SC_W)])

    return pl.pallas_call(
        body, name="sc_bwd1", grid=(s // tm,),
        in_specs=[_rows(tm, C512, 0), _rows(tm, C512, c_bg), _prev(tm, H8, C512, c_cg), _rows(tm, C512, c_cg),
                  _prev(tm, H8, C512, c_x), _rows(tm, C512, c_x), _full((8, C512))],
        out_specs=[_rows(tm, C512, 0), _rows(tm, C512, 0), _full((8, C512)), _full((8, C512))],
        out_shape=[jax.ShapeDtypeStruct((s, C512), F32), jax.ShapeDtypeStruct((s, C512), BF16),
                   jax.ShapeDtypeStruct((8, C512), F32), jax.ShapeDtypeStruct((8, C512), F32)],
        scratch_shapes=[pltpu.VMEM((tm + H8, C512), F32)], compiler_params=_params(("arbitrary",)),
    )(dzb, proj, proj, proj, proj, proj, w8)


def _sc_bwd2(dconv, proj, w8):
    s = proj.shape[0]
    tm = _tile(s, TM)
    c_cg, c_x = SCG0 // C512, SX0 // C512
    nt = s // tm

    def body(dm, dn, cm, xm, w_ref, dcg_ref, dx_ref, acc_ref, buf):
        i = pl.program_id(0)
        buf[0:tm, :] = dm[...]
        nxt = dn[...]
        buf[tm:tm + H8, :] = jnp.where(i == nt - 1, jnp.zeros_like(nxt), nxt)
        du = _conv3_t(w_ref, buf, tm)
        dcg = du * xm[...]
        dx = du * cm[...]
        dcg_ref[...] = dcg.astype(BF16)
        dx_ref[...] = dx.astype(BF16)
        _acc_rows(acc_ref, i == 0, [_colsum(dcg), _colsum(dx)])

    return pl.pallas_call(
        body, name="sc_bwd2", grid=(nt,),
        in_specs=[_rows(tm, C512, 0), _next(tm, H8, C512, 0, s), _rows(tm, C512, c_cg), _rows(tm, C512, c_x),
                  _full((8, C512))],
        out_specs=[_rows(tm, C512, 0), _rows(tm, C512, 0), _full((8, C512))],
        out_shape=[jax.ShapeDtypeStruct((s, C512), BF16), jax.ShapeDtypeStruct((s, C512), BF16),
                   jax.ShapeDtypeStruct((8, C512), F32)],
        scratch_shapes=[pltpu.VMEM((tm + H8, C512), F32)], compiler_params=_params(("arbitrary",)),
    )(dconv, dconv, proj, proj, w8)


QLAT = 256
KVLAT = 128


def _rms(x, g):
    r = lax.rsqrt(jnp.mean(x * x, axis=-1, keepdims=True) + RMS_EPS)
    return x * r * g, r


def _rms_bwd(dy, x, g, r):
    u = dy * g
    dx = r * u - x * (r * r * r) * jnp.mean(u * x, axis=-1, keepdims=True)
    return dx, _colsum(dy * x * r)


def _lat_fwd(proj, gq, gkv):
    s = proj.shape[0]
    tm = _tile(s, TM)

    def body(q_ref, kv_ref, gq_ref, gkv_ref, qn_ref, kvn_ref):
        qn_ref[...] = _rms(q_ref[...], gq_ref[...])[0].astype(BF16)
        kvn_ref[...] = _rms(kv_ref[...], gkv_ref[...])[0].astype(BF16)

    return pl.pallas_call(
        body, name="lat_fwd", grid=(s // tm,),
        in_specs=[_rows(tm, QLAT, QL0 // QLAT), _rows(tm, KVLAT, KVL0 // KVLAT), _full((1, QLAT)), _full((1, KVLAT))],
        out_specs=[_rows(tm, QLAT, 0), _rows(tm, KVLAT, 0)],
        out_shape=[jax.ShapeDtypeStruct((s, QLAT), BF16), jax.ShapeDtypeStruct((s, KVLAT), BF16)],
        compiler_params=_params(("parallel",)),
    )(proj, proj, gq, gkv)


def _lat_bwd(dqn, dkvn, proj, gq, gkv):
    s = proj.shape[0]
    tm = _tile(s, TM)

    def body(dqn_ref, dkvn_ref, q_ref, kv_ref, gq_ref, gkv_ref, dq_ref, dkv_ref, accq_ref, acckv_ref):
        first = pl.program_id(0) == 0
        q, kv = q_ref[...], kv_ref[...]
        gqv, gkvv = gq_ref[...], gkv_ref[...]
        dq, dgq = _rms_bwd(dqn_ref[...], q, gqv, _rms(q, gqv)[1])
        dkv, dgkv = _rms_bwd(dkvn_ref[...], kv, gkvv, _rms(kv, gkvv)[1])
        dq_ref[...] = dq.astype(BF16)
        dkv_ref[...] = dkv.astype(BF16)
        _acc_rows(accq_ref, first, [dgq, _colsum(dq)])
        _acc_rows(acckv_ref, first, [dgkv, _colsum(dkv)])

    return pl.pallas_call(
        body, name="lat_bwd", grid=(s // tm,),
        in_specs=[_rows(tm, QLAT, 0), _rows(tm, KVLAT, 0), _rows(tm, QLAT, QL0 // QLAT), _rows(tm, KVLAT, KVL0 // KVLAT),
                  _full((1, QLAT)), _full((1, KVLAT))],
        out_specs=[_rows(tm, QLAT, 0), _rows(tm, KVLAT, 0), _full((8, QLAT)), _full((8, KVLAT))],
        out_shape=[jax.ShapeDtypeStruct((s, QLAT), BF16), jax.ShapeDtypeStruct((s, KVLAT), BF16),
                   jax.ShapeDtypeStruct((8, QLAT), F32), jax.ShapeDtypeStruct((8, KVLAT), F32)],
        compiler_params=_params(("arbitrary",)),
    )(dqn, dkvn, proj, proj, gq, gkv)


def _rope(x, c, s1, s2):
    return x * c + pltpu.roll(x, HEAD_PAD - ROPE // 2, 1) * s1 + pltpu.roll(x, ROPE // 2, 1) * s2


def _rope_t(d, c, s1, s2):
    return d * c + pltpu.roll(d * s1, ROPE // 2, 1) + pltpu.roll(d * s2, HEAD_PAD - ROPE // 2, 1)


def _rope_fwd(qp, kvp, proj, ct, s1t, s2t):
    s = proj.shape[0]
    tm = _tile(s, TM)

    def body(q_ref, kv_ref, kr_ref, c_ref, s1_ref, s2_ref, qo, ko, vo):
        c, s1, s2 = c_ref[...], s1_ref[...], s2_ref[...]
        kr = _rope(kr_ref[...], c, s1, s2)
        for h in range(HEADS):
            cs = slice(h * HEAD_PAD, (h + 1) * HEAD_PAD)
            qo[:, cs] = _rope(q_ref[:, cs], c, s1, s2).astype(BF16)
            ko[:, cs] = (kv_ref[:, 2 * h * HEAD_PAD:(2 * h + 1) * HEAD_PAD] + kr).astype(BF16)
            vo[:, cs] = kv_ref[:, (2 * h + 1) * HEAD_PAD:(2 * h + 2) * HEAD_PAD].astype(BF16)

    tab = _rows(tm, HEAD_PAD, 0)
    return pl.pallas_call(
        body, name="rope_fwd", grid=(s // tm,),
        in_specs=[_rows(tm, HEADS * HEAD_PAD, 0), _rows(tm, 2 * HEADS * HEAD_PAD, 0), _rows(tm, HEAD_PAD, KR0 // HEAD_PAD),
                  tab, tab, tab],
        out_specs=[_rows(tm, HEADS * HEAD_PAD, 0)] * 3,
        out_shape=[jax.ShapeDtypeStruct((s, HEADS * HEAD_PAD), BF16)] * 3,
        compiler_params=_params(("parallel",)),
    )(qp, kvp, proj, ct, s1t, s2t)


def _rope_bwd(dq, dkv, ct, s1t, s2t):
    s = dq.shape[0]
    tm = _tile(s, TM)

    def body(dq_ref, dkv_ref, c_ref, s1_ref, s2_ref, dqo, dkr_ref, acc_ref):
        c, s1, s2 = c_ref[...], s1_ref[...], s2_ref[...]
        tot = dkv_ref[:, 0:HEAD_PAD]
        for h in range(HEADS):
            cs = slice(h * HEAD_PAD, (h + 1) * HEAD_PAD)
            dqo[:, cs] = _rope_t(dq_ref[:, cs], c, s1, s2).astype(BF16)
            if h:
                tot = tot + dkv_ref[:, 2 * h * HEAD_PAD:(2 * h + 1) * HEAD_PAD]
        lane = lax.broadcasted_iota(jnp.int32, (tm, HEAD_PAD), 1)
        dkr = jnp.where((lane >= NOPE) & (lane < QK_DIM), _rope_t(tot, c, s1, s2), 0.0)
        dkr_ref[...] = dkr.astype(BF16)
        _acc_rows(acc_ref, pl.program_id(0) == 0, [_colsum(dkr)])

    tab = _rows(tm, HEAD_PAD, 0)
    return pl.pallas_call(
        body, name="rope_bwd", grid=(s // tm,),
        in_specs=[_rows(tm, HEADS * HEAD_PAD, 0), _rows(tm, 2 * HEADS * HEAD_PAD, 0), tab, tab, tab],
        out_specs=[_rows(tm, HEADS * HEAD_PAD, 0), tab, _full((8, HEAD_PAD))],
        out_shape=[jax.ShapeDtypeStruct((s, HEADS * HEAD_PAD), BF16), jax.ShapeDtypeStruct((s, HEAD_PAD), BF16),
                   jax.ShapeDtypeStruct((8, HEAD_PAD), F32)],
        compiler_params=_params(("arbitrary",)),
    )(dq, dkv, ct, s1t, s2t)


_NT = (((1,), (1,)), ((), ()))
_TN = (((0,), (0,)), ((), ()))
_NN = (((1,), (0,)), ((), ()))


def _tile_rows(ref, t, tq):
    return ref[pl.ds(pl.multiple_of(t * tq, tq), tq), :]


def _attn_fwd(q, k, v):
    s = q.shape[0]
    tq = _tile(s, TQ)
    nq = s // tq

    def body(q_ref, k_ref, v_ref, o_ref, lse_ref):
        qi = pl.program_id(1)
        q_t = q_ref[...]

        def raw(ki):
            return lax.dot_general(q_t, _tile_rows(k_ref, ki, tq), _NT, preferred_element_type=F32)

        def process(ki, sc, m, l, acc, masked):
            sc = sc * ATT_SCALE
            if masked:
                row = lax.broadcasted_iota(jnp.int32, (tq, tq), 0)
                col = lax.broadcasted_iota(jnp.int32, (tq, tq), 1)
                sc = jnp.where(col <= row, sc, -jnp.inf)
            m_new = jnp.maximum(m, jnp.max(sc, axis=-1, keepdims=True))
            p = jnp.exp(sc - m_new)
            a = jnp.exp(m - m_new)
            pv = lax.dot_general(p.astype(BF16), _tile_rows(v_ref, ki, tq), _NN, preferred_element_type=F32)
            return m_new, a * l + jnp.sum(p, axis=-1, keepdims=True), a * acc + pv

        def loop_body(ki, c):
            nxt = raw(ki + 1)
            return (nxt,) + process(ki, c[0], c[1], c[2], c[3], False)

        init = (raw(0), jnp.full((tq, 1), -jnp.inf, F32), jnp.zeros((tq, 1), F32), jnp.zeros((tq, HEAD_PAD), F32))
        c = lax.fori_loop(0, qi, loop_body, init)
        m, l, acc = process(qi, c[0], c[1], c[2], c[3], True)
        o_ref[...] = acc / l
        lse = jnp.broadcast_to(m + jnp.log(l), (tq, HEAD_PAD))
        lse_ref[0] = jnp.transpose(lse)[0:8, :]

    qspec = pl.BlockSpec((tq, HEAD_PAD), lambda h, qi: (qi, h))
    kspec = pl.BlockSpec((s, HEAD_PAD), lambda h, qi: (0, h))
    return pl.pallas_call(
        body, name="attn_fwd", grid=(HEADS, nq),
        in_specs=[qspec, kspec, kspec],
        out_specs=[qspec, pl.BlockSpec((1, 8, tq), lambda h, qi: (h, 0, qi))],
        out_shape=[jax.ShapeDtypeStruct((s, HEADS * HEAD_PAD), F32), jax.ShapeDtypeStruct((HEADS, 8, s), F32)],
        compiler_params=_params(("parallel", "parallel")),
    )(q, k, v)


def _attn_prep(d_o, o):
    s = o.shape[0]
    tm = _tile(s, TM)

    def body(do_ref, o_ref, dl_ref, dob_ref):
        for h in range(HEADS):
            cs = slice(h * HEAD_PAD, (h + 1) * HEAD_PAD)
            dov = do_ref[:, cs]
            row = jnp.sum(jnp.transpose(dov * o_ref[:, cs]), axis=0, keepdims=True)
            dl_ref[h] = jnp.broadcast_to(row, (8, tm))
            dob_ref[:, cs] = dov.astype(BF16)

    blk = _rows(tm, HEADS * HEAD_PAD, 0)
    return pl.pallas_call(
        body, name="attn_prep", grid=(s // tm,),
        in_specs=[blk, blk], out_specs=[pl.BlockSpec((HEADS, 8, tm), lambda i: (0, 0, i)), blk],
        out_shape=[jax.ShapeDtypeStruct((HEADS, 8, s), F32), jax.ShapeDtypeStruct((s, HEADS * HEAD_PAD), BF16)],
        compiler_params=_params(("parallel",)),
    )(d_o, o)


def _attn_bwd(q, k, v, d_o, lse, delta):
    s = q.shape[0]
    tq = _tile(s, TQ)
    nq = s // tq

    def body(q_ref, k_ref, v_ref, do_ref, lse_ref, dl_ref, dq_ref, dkv_ref):
        ki = pl.program_id(1)
        k_t, v_t = k_ref[...], v_ref[...]

        @pl.when(ki == 0)
        def _():
            dq_ref[...] = jnp.zeros_like(dq_ref)

        def raw(qi):
            return (lax.dot_general(k_t, _tile_rows(q_ref, qi, tq), _NT, preferred_element_type=F32),
                    lax.dot_general(v_t, _tile_rows(do_ref, qi, tq), _NT, preferred_element_type=F32))

        def process(qi, st, dpt, dk, dv, masked):
            cols = pl.ds(pl.multiple_of(qi * tq, tq), tq)
            sc = st * ATT_SCALE
            if masked:
                key = lax.broadcasted_iota(jnp.int32, (tq, tq), 0)
                qry = lax.broadcasted_iota(jnp.int32, (tq, tq), 1)
                sc = jnp.where(key <= qry, sc, -jnp.inf)
            pt = jnp.exp(sc - lse_ref[0, 0:1, cols])
            dsb = (pt * (dpt - dl_ref[0, 0:1, cols]) * ATT_SCALE).astype(BF16)
            dv = dv + lax.dot_general(pt.astype(BF16), _tile_rows(do_ref, qi, tq), _NN, preferred_element_type=F32)
            dk = dk + lax.dot_general(dsb, _tile_rows(q_ref, qi, tq), _NN, preferred_element_type=F32)
            dq_ref[cols, :] += lax.dot_general(dsb, k_t, _TN, preferred_element_type=F32)
            return dk, dv

        zero = jnp.zeros((tq, HEAD_PAD), F32)
        dk, dv = process(ki, *raw(ki), zero, zero, True)

        def loop_body(qi, c):
            nxt = raw(jnp.minimum(qi + 1, nq - 1))
            return nxt + process(qi, c[0], c[1], c[2], c[3], False)

        c = lax.fori_loop(ki + 1, nq, loop_body, raw(jnp.minimum(ki + 1, nq - 1)) + (dk, dv))
        dkv_ref[:, 0:HEAD_PAD] = c[2]
        dkv_ref[:, HEAD_PAD:2 * HEAD_PAD] = c[3]

    full = pl.BlockSpec((s, HEAD_PAD), lambda h, ki: (0, h))
    tile = pl.BlockSpec((tq, HEAD_PAD), lambda h, ki: (ki, h))
    stat = pl.BlockSpec((1, 8, s), lambda h, ki: (h, 0, 0))
    return pl.pallas_call(
        body, name="attn_bwd", grid=(HEADS, nq),
        in_specs=[full, tile, tile, full, stat, stat],
        out_specs=[full, pl.BlockSpec((tq, 2 * HEAD_PAD), lambda h, ki: (ki, h))],
        out_shape=[jax.ShapeDtypeStruct((s, HEADS * HEAD_PAD), F32), jax.ShapeDtypeStruct((s, HEADS * 2 * HEAD_PAD), F32)],
        compiler_params=_params(("parallel", "arbitrary")),
    )(q, k, v, d_o, lse, delta)


PH = 16
PG = 128


def _pool_pd(buf, u_main_ref, g, i, tm):
    w = POOL_WINDOWS[g]
    cs = pl.ds(g * PG, PG)
    tot = buf[pl.ds(PH, tm), cs]
    for j in range(1, w):
        tot = tot + buf[pl.ds(PH - j, tm), cs]
    t = i * tm + lax.broadcasted_iota(jnp.int32, (tm, PG), 0)
    cnt = jnp.minimum(t + 1, w).astype(F32)
    return tot / cnt - u_main_ref[:, cs]


def _pool_ubuf(buf, up, um, first, tm):
    p = up[...]
    buf[0:PH, :] = jnp.where(first, jnp.zeros_like(p), p)
    buf[PH:PH + tm, :] = um[...]


def _pool_fwd(proj, w_pool, scale):
    s = proj.shape[0]
    tm = _tile(s, TM)
    cu = PU0 // C512

    def body(up, um, w_ref, sc_ref, zd_ref, buf):
        i = pl.program_id(0)
        _pool_ubuf(buf, up, um, i == 0, tm)
        for g in range(4):
            pd = _pool_pd(buf, um, g, i, tm).astype(BF16)
            e = lax.dot_general(pd, w_ref[g], _NN, preferred_element_type=F32)
            zd_ref[:, g * PG:(g + 1) * PG] = (e * sc_ref[:, g * PG:(g + 1) * PG]).astype(BF16)

    return pl.pallas_call(
        body, name="pool_fwd", grid=(s // tm,),
        in_specs=[_prev(tm, PH, C512, cu), _rows(tm, C512, cu), _full((4, PG, PG)), _full((1, C512))],
        out_specs=_rows(tm, C512, 0), out_shape=jax.ShapeDtypeStruct((s, C512), BF16),
        scratch_shapes=[pltpu.VMEM((tm + PH, C512), F32)], compiler_params=_params(("parallel",)),
    )(proj, proj, w_pool, scale)


def _pool_bwd1(dzd, proj, w_pool, scale):
    s = proj.shape[0]
    tm = _tile(s, TM)
    cu = PU0 // C512

    def body(dz_ref, up, um, w_ref, sc_ref, dpd_ref, dw_ref, acc_ref, buf):
        i = pl.program_id(0)
        first = i == 0
        _pool_ubuf(buf, up, um, first, tm)

        @pl.when(first)
        def _():
            dw_ref[...] = jnp.zeros_like(dw_ref)
            acc_ref[...] = jnp.zeros_like(acc_ref)
        for g in range(4):
            cs = slice(g * PG, (g + 1) * PG)
            pd = _pool_pd(buf, um, g, i, tm).astype(BF16)
            wg = w_ref[g]
            e = lax.dot_general(pd, wg, _NN, preferred_element_type=F32)
            dz = dz_ref[:, cs]
            acc_ref[0:1, cs] += _colsum(dz * e)
            de = (dz * sc_ref[:, cs]).astype(BF16)
            dw_ref[g] += lax.dot_general(pd, de, _TN, preferred_element_type=F32)
            dpd_ref[:, cs] = lax.dot_general(de, wg, _NT, preferred_element_type=F32)

    return pl.pallas_call(
        body, name="pool_bwd1", grid=(s // tm,),
        in_specs=[_rows(tm, C512, 0), _prev(tm, PH, C512, cu), _rows(tm, C512, cu), _full((4, PG, PG)), _full((1, C512))],
        out_specs=[_rows(tm, C512, 0), _full((4, PG, PG)), _full((8, C512))],
        out_shape=[jax.ShapeDtypeStruct((s, C512), F32), jax.ShapeDtypeStruct((4, PG, PG), F32),
                   jax.ShapeDtypeStruct((8, C512), F32)],
        scratch_shapes=[pltpu.VMEM((tm + PH, C512), F32)], compiler_params=_params(("arbitrary",)),
    )(dzd, proj, proj, w_pool, scale)


def _pool_bwd2(dpd, s):
    tm = _tile(s, TM)
    nt = s // tm

    def body(dm, dn, du_ref, acc_ref, buf):
        i = pl.program_id(0)
        buf[0:tm, :] = dm[...]
        nxt = dn[...]
        buf[tm:tm + PH, :] = jnp.where(i == nt - 1, jnp.zeros_like(nxt), nxt)
        t = i * tm + lax.broadcasted_iota(jnp.int32, (tm + PH, PG), 0)
        cols = []
        for g, w in enumerate(POOL_WINDOWS):
            cs = pl.ds(g * PG, PG)
            cnt = jnp.minimum(t + 1, w).astype(F32)
            buf[:, cs] = buf[:, cs] / cnt
        for g, w in enumerate(POOL_WINDOWS):
            cs = pl.ds(g * PG, PG)
            tot = buf[pl.ds(0, tm), cs]
            for j in range(1, w):
                tot = tot + buf[pl.ds(j, tm), cs]
            du = tot - dm[:, cs]
            du_ref[:, cs] = du.astype(BF16)
            cols.append(_colsum(du))
        _acc_rows(acc_ref, i == 0, [jnp.concatenate(cols, axis=1)])

    return pl.pallas_call(
        body, name="pool_bwd2", grid=(nt,),
        in_specs=[_rows(tm, C512, 0), _next(tm, PH, C512, 0, s)],
        out_specs=[_rows(tm, C512, 0), _full((8, C512))],
        out_shape=[jax.ShapeDtypeStruct((s, C512), BF16), jax.ShapeDtypeStruct((8, C512), F32)],
        scratch_shapes=[pltpu.VMEM((tm + PH, C512), F32)], compiler_params=_params(("arbitrary",)),
    )(dpd, dpd)


def _merge_specs(tm):
    return [_rows(tm, C512, 0), _rows(tm, C512, 0), _rows(tm, D, 0), _rows(tm, C512, 0),
            _rows(tm, 4 * D, GATES0 // (4 * D)),
            _full((C512, D)), _full((C512, D)), _full((D, D)), _full((C512, D))]


def _branch_ys(za, zb, o, zd, wa, wb, wc, wd):
    zs = (za[...], zb[...], o[...].astype(BF16), zd[...])
    return [lax.dot_general(z, w[...], _NN, preferred_element_type=F32) for z, w in zip(zs, (wa, wb, wc, wd))]


def _merge_fwd(za, zb, o, zd, proj, wa, wb, wc, wd):
    s = proj.shape[0]
    tm = _tile(s, TM_WIDE)

    def body(za_r, zb_r, o_r, zd_r, g_ref, wa_r, wb_r, wc_r, wd_r, m_ref):
        ys = _branch_ys(za_r, zb_r, o_r, zd_r, wa_r, wb_r, wc_r, wd_r)
        acc = _sig(g_ref[:, 0:D]) * ys[0]
        for b in range(1, 4):
            acc = acc + _sig(g_ref[:, b * D:(b + 1) * D]) * ys[b]
        m_ref[...] = acc.astype(BF16)

    return pl.pallas_call(
        body, name="merge_fwd", grid=(s // tm,), in_specs=_merge_specs(tm),
        out_specs=_rows(tm, D, 0), out_shape=jax.ShapeDtypeStruct((s, D), BF16),
        compiler_params=_params(("parallel",)),
    )(za, zb, o, zd, proj, wa, wb, wc, wd)


def _merge_bwd(dmerged, za, zb, o, zd, proj, wa, wb, wc, wd):
    s = proj.shape[0]
    tm = _tile(s, TM_WIDE)

    def body(dm_ref, za_r, zb_r, o_r, zd_r, g_ref, wa_r, wb_r, wc_r, wd_r,
             dya, dyb, dyc, dyd, dza, dzb, d_o, dzd, dg_ref, acc_ref):
        ys = _branch_ys(za_r, zb_r, o_r, zd_r, wa_r, wb_r, wc_r, wd_r)
        dm = dm_ref[...]
        sums = []
        for b, (dy_ref, dz_ref, w_r) in enumerate(((dya, dza, wa_r), (dyb, dzb, wb_r), (dyc, d_o, wc_r), (dyd, dzd, wd_r))):
            gt = _sig(g_ref[:, b * D:(b + 1) * D])
            dg = dm * ys[b] * gt * (1.0 - gt)
            dg_ref[:, b * D:(b + 1) * D] = dg.astype(BF16)
            sums.append(_colsum(dg))
            dy = (dm * gt).astype(BF16)
            dy_ref[...] = dy
            dz_ref[...] = lax.dot_general(dy, w_r[...], _NT, preferred_element_type=F32)
        _acc_rows(acc_ref, pl.program_id(0) == 0, [jnp.concatenate(sums, axis=1)])

    bf = lambda c: jax.ShapeDtypeStruct((s, c), BF16)
    f32 = lambda c: jax.ShapeDtypeStruct((s, c), F32)
    return pl.pallas_call(
        body, name="merge_bwd", grid=(s // tm,), in_specs=[_rows(tm, D, 0)] + _merge_specs(tm),
        out_specs=[_rows(tm, D, 0)] * 4 + [_rows(tm, C512, 0), _rows(tm, C512, 0), _rows(tm, D, 0), _rows(tm, C512, 0),
                                           _rows(tm, 4 * D, 0), _full((8, 4 * D))],
        out_shape=[bf(D)] * 4 + [f32(C512), f32(C512), f32(D), f32(C512), bf(4 * D), jax.ShapeDtypeStruct((8, 4 * D), F32)],
        compiler_params=_params(("arbitrary",)),
    )(dmerged, za, zb, o, zd, proj, wa, wb, wc, wd)


def _mm_res_ln(a, w, xres, prm, name):
    s, k = a.shape
    tm = _tile(s, TM_WIDE)

    def body(a_ref, w_ref, x_ref, p_ref, y_ref, xn_ref, hn_ref):
        y = lax.dot_general(a_ref[...], w_ref[...], _NN, preferred_element_type=F32)
        y_ref[...] = y
        xh, _ = _ln_stats(ALPHA * x_ref[...] + (1.0 + p_ref[0:1, :]) * y)
        xn = xh * p_ref[3:4, :] + p_ref[4:5, :]
        xn_ref[...] = xn
        hn_ref[...] = (xn * (1.0 + p_ref[1:2, :]) + p_ref[2:3, :]).astype(BF16)

    return pl.pallas_call(
        body, name=name, grid=(s // tm,),
        in_specs=[_rows(tm, k, 0), _full((k, D)), _rows(tm, D, 0), _full((8, D))],
        out_specs=[_rows(tm, D, 0)] * 3,
        out_shape=[jax.ShapeDtypeStruct((s, D), F32), jax.ShapeDtypeStruct((s, D), F32), jax.ShapeDtypeStruct((s, D), BF16)],
        compiler_params=_params(("parallel",)),
    )(a, w, xres, prm)


def _ln_res_bwd(dres_next, dh, xres, y, prm):
    s = xres.shape[0]
    tm = _tile(s, TM)

    def body(dn_ref, dh_ref, x_ref, y_ref, p_ref, dres_ref, dy_ref, acc_ref):
        gam, lng = p_ref[0:1, :], p_ref[3:4, :]
        yv = y_ref[...]
        xh, rstd = _ln_stats(ALPHA * x_ref[...] + (1.0 + gam) * yv)
        xn = xh * lng + p_ref[4:5, :]
        dh_v = dh_ref[...]
        dxn = dn_ref[...] + dh_v * (1.0 + p_ref[1:2, :])
        dr = _ln_bwd(dxn * lng, xh, rstd)
        dres_ref[...] = ALPHA * dr
        dy_ref[...] = ((1.0 + gam) * dr).astype(BF16)
        _acc_rows(acc_ref, pl.program_id(0) == 0,
                  [_colsum(dr * yv), _colsum(dh_v * xn), _colsum(dh_v), _colsum(dxn * xh), _colsum(dxn)])

    return pl.pallas_call(
        body, name="ln_res_bwd", grid=(s // tm,),
        in_specs=[_rows(tm, D, 0)] * 4 + [_full((8, D))],
        out_specs=[_rows(tm, D, 0), _rows(tm, D, 0), _full((8, D))],
        out_shape=[jax.ShapeDtypeStruct((s, D), F32), jax.ShapeDtypeStruct((s, D), BF16), jax.ShapeDtypeStruct((8, D), F32)],
        compiler_params=_params(("arbitrary",)),
    )(dres_next, dh, xres, y, prm)


def _ffn_specs(tm, tc, nc):
    pv = pl.BlockSpec((H8, tc), lambda j, i: (jnp.maximum(i * (tm // H8) - 1, 0), j))
    mv = pl.BlockSpec((tm, tc), lambda j, i: (i, j))
    pg = pl.BlockSpec((H8, tc), lambda j, i: (jnp.maximum(i * (tm // H8) - 1, 0), j + nc))
    mg = pl.BlockSpec((tm, tc), lambda j, i: (i, j + nc))
    wv = pl.BlockSpec((8, tc), lambda j, i: (0, j))
    wg = pl.BlockSpec((8, tc), lambda j, i: (0, j + nc))
    return pv, mv, pg, mg, wv, wg


def _ffn_bufs(bv, bg, pv, mv, pg, mg, first, tm):
    for buf, p, m in ((bv, pv, mv), (bg, pg, mg)):
        pp = p[...]
        buf[0:H8, :] = jnp.where(first, jnp.zeros_like(pp), pp)
        buf[H8:H8 + tm, :] = m[...]


def _ffn_act_fwd(up, w8):
    s = up.shape[0]
    tm, tc = _tile(s, TM_WIDE), FFN_TC
    nc = D_FF // tc

    def body(pv, mv, pg, mg, wv, wg, a_ref, bv, bg):
        _ffn_bufs(bv, bg, pv, mv, pg, mg, pl.program_id(1) == 0, tm)
        val, gate = _conv3(wv, bv, tm), _conv3(wg, bg, tm)
        a_ref[...] = (gate * _sig(gate) * val).astype(BF16)

    return pl.pallas_call(
        body, name="ffn_act_fwd", grid=(nc, s // tm), in_specs=list(_ffn_specs(tm, tc, nc)),
        out_specs=pl.BlockSpec((tm, tc), lambda j, i: (i, j)), out_shape=jax.ShapeDtypeStruct((s, D_FF), BF16),
        scratch_shapes=[pltpu.VMEM((tm + H8, tc), F32)] * 2, compiler_params=_params(("parallel", "parallel")),
    )(up, up, up, up, w8, w8)


def _ffn_act_bwd1(da, up, w8):
    s = up.shape[0]
    tm, tc = _tile(s, TM_WIDE), FFN_TC
    nc = D_FF // tc

    def body(da_ref, pv, mv, pg, mg, wv, wg, dv_ref, dg_ref, dwv_ref, dwg_ref, bv, bg):
        first = pl.program_id(1) == 0
        _ffn_bufs(bv, bg, pv, mv, pg, mg, first, tm)
        val, gate = _conv3(wv, bv, tm), _conv3(wg, bg, tm)
        sg = _sig(gate)
        dav = da_ref[...]
        dval = dav * gate * sg
        dgate = dav * val * (sg * (1.0 + gate * (1.0 - sg)))
        dv_ref[...] = dval
        dg_ref[...] = dgate
        _acc_rows(dwv_ref, first, [_colsum(dval * bv[pl.ds(H8 - SC_W + 1 + k, tm), :]) for k in range(SC_W)])
        _acc_rows(dwg_ref, first, [_colsum(dgate * bg[pl.ds(H8 - SC_W + 1 + k, tm), :]) for k in range(SC_W)])

    blk = pl.BlockSpec((tm, tc), lambda j, i: (i, j))
    return pl.pallas_call(
        body, name="ffn_act_bwd1", grid=(nc, s // tm), in_specs=[blk] + list(_ffn_specs(tm, tc, nc)),
        out_specs=[blk, blk, pl.BlockSpec((8, tc), lambda j, i: (0, j)), pl.BlockSpec((8, tc), lambda j, i: (0, j))],
        out_shape=[jax.ShapeDtypeStruct((s, D_FF), F32)] * 2 + [jax.ShapeDtypeStruct((8, D_FF), F32)] * 2,
        scratch_shapes=[pltpu.VMEM((tm + H8, tc), F32)] * 2, compiler_params=_params(("parallel", "arbitrary")),
    )(da, up, up, up, up, w8, w8)


def _ffn_act_bwd2(dval, dgate, w8):
    s = dval.shape[0]
    tm, tc = _tile(s, TM_WIDE), FFN_TC
    nc = D_FF // tc
    nt = s // tm
    last = s // H8 - 1

    def body(vm, vn, gm, gn, wv, wg, dup_ref, buf):
        i = pl.program_id(1)
        half = pl.program_id(0) // nc

        def run(m, n, w):
            buf[0:tm, :] = m[...]
            nx = n[...]
            buf[tm:tm + H8, :] = jnp.where(i == nt - 1, jnp.zeros_like(nx), nx)
            dup_ref[...] = _conv3_t(w, buf, tm).astype(BF16)

        @pl.when(half == 0)
        def _():
            run(vm, vn, wv)

        @pl.when(half == 1)
        def _():
            run(gm, gn, wg)

    col = lambda j: j % nc
    main = pl.BlockSpec((tm, tc), lambda j, i: (i, col(j)))
    nxt = pl.BlockSpec((H8, tc), lambda j, i: (jnp.minimum((i + 1) * (tm // H8), last), col(j)))
    wv = pl.BlockSpec((8, tc), lambda j, i: (0, col(j)))
    wg = pl.BlockSpec((8, tc), lambda j, i: (0, col(j) + nc))
    return pl.pallas_call(
        body, name="ffn_act_bwd2", grid=(2 * nc, nt), in_specs=[main, nxt, main, nxt, wv, wg],
        out_specs=pl.BlockSpec((tm, tc), lambda j, i: (i, j)), out_shape=jax.ShapeDtypeStruct((s, 2 * D_FF), BF16),
        scratch_shapes=[pltpu.VMEM((tm + H8, tc), F32)], compiler_params=_params(("parallel", "parallel")),
    )(dval, dval, dgate, dgate, w8, w8)


def _loss_head(y, target):
    s = y.shape[0]
    tm = _tile(s, TM)

    def body(y_ref, t_ref, dy_ref, l_ref):
        err = y_ref[...] - t_ref[...]
        dy_ref[...] = err * (1.0 / D)
        part = 0.5 * jnp.sum(jnp.mean(err * err, axis=-1, keepdims=True), axis=0, keepdims=True)

        @pl.when(pl.program_id(0) == 0)
        def _():
            l_ref[...] = jnp.zeros_like(l_ref)
        l_ref[...] += part

    return pl.pallas_call(
        body, name="loss_head", grid=(s // tm,),
        in_specs=[_rows(tm, D, 0)] * 2, out_specs=[_rows(tm, D, 0), _full((8, LANE))],
        out_shape=[jax.ShapeDtypeStruct((s, D), F32), jax.ShapeDtypeStruct((8, LANE), F32)],
        compiler_params=_params(("arbitrary",)),
    )(y, target)


def _silu_rows(c_all):
    def body(c_ref, o_ref):
        cv = c_ref[...]
        o_ref[...] = jnp.concatenate([cv * _sig(cv), jnp.zeros((N_DEV, D), F32)], axis=0).astype(BF16)

    return pl.pallas_call(
        body, name="silu_rows", grid=(1,), in_specs=[_full((N_DEV, D))], out_specs=_full((2 * N_DEV, D)),
        out_shape=jax.ShapeDtypeStruct((2 * N_DEV, D), BF16), compiler_params=_params(("arbitrary",)),
    )(c_all)


GRAD_ROWS = 512


def _sum_parts(p_ref, n):
    g = p_ref[0]
    for j in range(1, n):
        g = g + p_ref[j]
    return g


def _adamw(parts, w, m, v, name):
    n = parts.shape[0]
    r, c = w.shape
    tr = GRAD_ROWS
    assert r % tr == 0 and parts.shape[2] == c, (parts.shape, w.shape)

    def body(p_ref, w_ref, m_ref, v_ref, g_out, d_out, m_out, v_out):
        g = _sum_parts(p_ref, n)
        mn = ADAM_B1 * m_ref[...] + (1.0 - ADAM_B1) * g
        vn = ADAM_B2 * v_ref[...] + (1.0 - ADAM_B2) * (g * g)
        m_hat = mn / (1.0 - ADAM_B1 ** ADAM_STEP)
        v_hat = vn / (1.0 - ADAM_B2 ** ADAM_STEP)
        g_out[...] = g
        d_out[...] = -ADAM_LR * (m_hat / (jnp.sqrt(v_hat) + ADAM_EPS) + ADAM_WD * w_ref[...])
        m_out[...] = mn
        v_out[...] = vn

    blk = pl.BlockSpec((tr, c), lambda i: (i, 0))
    return pl.pallas_call(
        body, name=name, grid=(r // tr,),
        in_specs=[pl.BlockSpec((n, tr, c), lambda i: (0, i, 0)), blk, blk, blk], out_specs=[blk] * 4,
        out_shape=[jax.ShapeDtypeStruct((r, c), F32)] * 4, compiler_params=_params(("parallel",)),
    )(parts, w, m, v)


def _sum8(parts, name):
    n, r, c = parts.shape
    tr = GRAD_ROWS
    assert r % tr == 0, r

    def body(p_ref, g_out):
        g_out[...] = _sum_parts(p_ref, n)

    return pl.pallas_call(
        body, name=name, grid=(r // tr,),
        in_specs=[pl.BlockSpec((n, tr, c), lambda i: (0, i, 0))], out_specs=pl.BlockSpec((tr, c), lambda i: (i, 0)),
        out_shape=jax.ShapeDtypeStruct((r, c), F32), compiler_params=_params(("parallel",)),
    )(parts)


def _peers():
    ix, iy, ic = lax.axis_index("x"), lax.axis_index("y"), lax.axis_index("c")
    me = 4 * ix + 2 * iy + ic
    out = []
    for k in range(1, N_DEV):
        px = 1 - ix if (k >> 2) & 1 else ix
        py = 1 - iy if (k >> 1) & 1 else iy
        pc = 1 - ic if k & 1 else ic
        out.append(((px, py, pc), 4 * px + 2 * py + pc))
    return me, out


_HBM = pl.BlockSpec(memory_space=pltpu.HBM)


def _exchange(x, name, gather):
    shape = ((N_DEV,) + x.shape) if gather else x.shape

    def body(x_ref, o_ref, send_sems, recv_sems, local_sem):
        me, peers = _peers()
        src_of = (lambda p: x_ref) if gather else (lambda p: x_ref.at[p])
        local = pltpu.make_async_copy(src_of(me), o_ref.at[me], local_sem)
        local.start()
        sends = []
        for k, (dev, p) in enumerate(peers):
            cp = pltpu.make_async_remote_copy(src_ref=src_of(p), dst_ref=o_ref.at[me], send_sem=send_sems.at[k],
                                              recv_sem=recv_sems.at[k], device_id=dev, device_id_type=MESH)
            cp.start()
            sends.append(cp)
        for k, (dev, p) in enumerate(peers):
            pltpu.make_async_remote_copy(src_ref=src_of(p), dst_ref=o_ref.at[p], send_sem=send_sems.at[k],
                                         recv_sem=recv_sems.at[k], device_id=dev, device_id_type=MESH).wait_recv()
        for cp in sends:
            cp.wait_send()
        local.wait()

    return pl.pallas_call(
        body, name=name, in_specs=[_HBM], out_specs=_HBM, out_shape=jax.ShapeDtypeStruct(shape, x.dtype),
        scratch_shapes=[pltpu.SemaphoreType.DMA((N_DEV - 1,)), pltpu.SemaphoreType.DMA((N_DEV - 1,)),
                        pltpu.SemaphoreType.DMA],
    )(x)


def _pack_rows(arrs, row_mult):
    flat = jnp.concatenate([a.reshape(-1) for a in arrs])
    n = flat.shape[0]
    pad = (-n) % (LANE * row_mult)
    if pad:
        flat = jnp.concatenate([flat, jnp.zeros((pad,), flat.dtype)])
    return flat.reshape(-1, LANE)


def _pack_cols(arrs, row_mult):
    flat = jnp.concatenate(arrs, axis=1)
    n = flat.shape[1]
    pad = (-n) % (LANE * row_mult)
    if pad:
        flat = jnp.concatenate([flat, jnp.zeros((flat.shape[0], pad), flat.dtype)], axis=1)
    return flat.reshape(flat.shape[0], -1, LANE)


def _unpack(flat, shapes):
    out, off = [], 0
    lead = flat.shape[:-1]
    for shp in shapes:
        n = 1
        for d_ in shp:
            n *= d_
        out.append(flat[..., off:off + n].reshape(lead + tuple(shp)))
        off += n
    return out


BIG = (("w_conv_out", 2), ("w_sc_out", 2), ("w_uq", 2), ("w_ukv", 2), ("w_mla_out", 2),
       ("w_pool_out", 2), ("w_o", 1), ("w_down", 1))
TAPS = (("conv_dw", 2), ("sc_dw", 2), ("ffn_dw", 2))
SMALL = ("b_ada", "b_in", "conv_ln_g", "conv_ln_b", "q_norm_g", "kv_norm_g", "w_pool", "pool_scale",
         "ln1_g", "ln1_b", "ln2_g", "ln2_b")
WEIGHTS = ("w_ada", "b_ada", "w_in", "b_in", "conv_dw", "conv_ln_g", "conv_ln_b", "w_conv_out", "sc_dw", "w_sc_out",
           "q_norm_g", "w_uq", "kv_norm_g", "w_ukv", "w_mla_out", "w_pool", "pool_scale", "w_pool_out", "w_o",
           "ln1_g", "ln1_b", "w_up", "ffn_dw", "w_down", "ln2_g", "ln2_b")


def _join(g, axis):
    g = jnp.moveaxis(g, 0, axis)
    shp = list(g.shape)
    shp[axis:axis + 2] = [shp[axis] * shp[axis + 1]]
    return g.reshape(shp)


def _split(full, axis):
    shp = list(full.shape)
    shp[axis:axis + 1] = [N_DEV, shp[axis] // N_DEV]
    return jnp.moveaxis(full.reshape(shp), axis, 0)


def _pad_rows(a, rows):
    return jnp.concatenate([a, jnp.zeros((rows - a.shape[0],) + a.shape[1:], a.dtype)], axis=0)


def _proj_cols(w):
    z = lambda n: jnp.zeros(w.shape[:-1] + (n,), w.dtype)
    return jnp.concatenate([w[..., 3488:7584], w[..., 0:2944], z(NOPE), w[..., 2944:2976], z(HEAD_PAD - QK_DIM),
                            w[..., 2976:3488]], axis=-1)


def _proj_cols_inv(w):
    return jnp.concatenate([w[..., CA0:KR0], w[..., KR0 + NOPE:KR0 + QK_DIM], w[..., PU0:NPROJ], w[..., 0:CA0]], axis=-1)


IN_COLS = 7584
IN_SHARD = IN_COLS // N_DEV
IN_SHARD_PAD = 960
UP_SHARD = 2 * D_FF // N_DEV
_IN_CUTS = (2944, 2976, 3488)
_IN_SEGS = ((3488, IN_COLS), (0, 2944), NOPE, (2944, 2976), HEAD_PAD - QK_DIM, (2976, 3488))


def _proj_rows(g):
    pieces = []
    for seg in _IN_SEGS:
        if isinstance(seg, int):
            pieces.append(jnp.zeros((seg, g.shape[1]), g.dtype))
            continue
        j = seg[0]
        while j < seg[1]:
            dev = j // IN_SHARD
            e = min(seg[1], (dev + 1) * IN_SHARD)
            r0 = dev * IN_SHARD_PAD + j - dev * IN_SHARD
            pieces.append(g[r0:r0 + e - j])
            j = e
    return jnp.concatenate(pieces, axis=0)


def _aligned_row(j):
    if j < 2944:
        return CA0 + j
    if j < 2976:
        return KR0 + NOPE + j - 2944
    if j < 3488:
        return PU0 + j - 2976
    return j - 3488


def _proj_rows_split(gt):
    out = []
    for dev in range(N_DEV):
        j0, j1 = dev * IN_SHARD, (dev + 1) * IN_SHARD
        cuts = [j0] + [c for c in _IN_CUTS if j0 < c < j1] + [j1]
        out.append(jnp.concatenate([gt[_aligned_row(a):_aligned_row(a) + b - a] for a, b in zip(cuts[:-1], cuts[1:])], axis=0))
    return jnp.stack(out)


def _layer_weights(full, small, l):
    w = {}
    w["w_inT"] = _proj_rows(full["w_inT"][l])
    w["b_in"] = _proj_cols(small["b_in"][l][None, :])
    w["conv_w"] = _pad_rows(full["conv_dw"][l], 32)
    w["sc_w"] = _pad_rows(full["sc_dw"][l], 8)
    w["ffn_w"] = _pad_rows(full["ffn_dw"][l], 8)
    uq = full["w_uq"][l].reshape(256, HEADS, QK_DIM)
    w["w_uq"] = jnp.concatenate([uq, jnp.zeros((256, HEADS, HEAD_PAD - QK_DIM), uq.dtype)], axis=-1).reshape(256, -1)
    ukv = full["w_ukv"][l].reshape(128, HEADS, 2, NOPE)
    w["w_ukv"] = jnp.concatenate([ukv, jnp.zeros_like(ukv)], axis=-1).reshape(128, -1)
    mo = full["w_mla_out"][l].reshape(HEADS, NOPE, D)
    w["w_c"] = jnp.concatenate([mo, jnp.zeros_like(mo)], axis=1).reshape(HEADS * HEAD_PAD, D)
    w["w_a"], w["w_b"], w["w_d"] = full["w_conv_out"][l], full["w_sc_out"][l], full["w_pool_out"][l]
    w["w_o"], w["w_upT"], w["w_down"] = full["w_o"][l], full["w_upT"][l], full["w_down"][l]
    w["w_pool"] = small["w_pool"][l].astype(BF16)
    for n in ("conv_ln_g", "conv_ln_b", "q_norm_g", "kv_norm_g", "pool_scale"):
        w[n] = small[n][l][None, :]
    return w


def _prm(rows):
    z = jnp.zeros((D,), F32)
    rows = list(rows) + [z] * (8 - len(rows))
    return jnp.stack(rows)


def _layer_fwd(x, h1, w, prm1, prm2, tabs):
    ct, s1t, s2t = tabs
    proj = _mm(h1, w["w_inT"], "nt", "proj_fwd", bias=w["b_in"], tn=768)
    yconv, za = _conv_a_fwd(proj, w["conv_w"], w["conv_ln_g"], w["conv_ln_b"])
    zb = _sc_fwd(proj, w["sc_w"])
    qn, kvn = _lat_fwd(proj, w["q_norm_g"], w["kv_norm_g"])
    qp = _mm(qn, w["w_uq"], "nn", "uq_fwd")
    kvp = _mm(kvn, w["w_ukv"], "nn", "ukv_fwd")
    q, k, v = _rope_fwd(qp, kvp, proj, ct, s1t, s2t)
    o, lse = _attn_fwd(q, k, v)
    zd = _pool_fwd(proj, w["w_pool"], w["pool_scale"])
    merged = _merge_fwd(za, zb, o, zd, proj, w["w_a"], w["w_b"], w["w_c"], w["w_d"])
    mix, x1, h2 = _mm_res_ln(merged, w["w_o"], x, prm1, "o_res_ln")
    up = _mm(h2, w["w_upT"], "nt", "up_fwd", tn=1408)
    a = _ffn_act_fwd(up, w["ffn_w"])
    ffn, x2, h_next = _mm_res_ln(a, w["w_down"], x1, prm2, "down_res_ln")
    res = dict(x=x, h1=h1, proj=proj, yconv=yconv, za=za, zb=zb, qn=qn, kvn=kvn, q=q, k=k, v=v, o=o, lse=lse, zd=zd,
               merged=merged, mix=mix, x1=x1, h2=h2, up=up, a=a, ffn=ffn)
    return x2, h_next, res


def _layer_bwd(dres_next, dh_next, r, w, prm1, prm2, tabs):
    ct, s1t, s2t = tabs
    s = r["x"].shape[0]
    g = {}
    dres2, dffn, acc2 = _ln_res_bwd(dres_next, dh_next, r["x1"], r["ffn"], prm2)
    da = _mm(dffn, w["w_down"], "nt", "down_bwd_x", tn=1408)
    g["w_down"] = _mm(r["a"], dffn, "tn", "down_bwd_w", tm=1408)
    dval, dgate, dwv, dwg = _ffn_act_bwd1(da, r["up"], w["ffn_w"])
    g["ffn_dw"] = jnp.concatenate([dwv[:3], dwg[:3]], axis=1)
    dup = _ffn_act_bwd2(dval, dgate, w["ffn_w"])
    dh2 = _mm(dup, w["w_upT"], "nn", "up_bwd_x", tk=1408)
    g["w_upT"] = _mm(dup, r["h2"], "tn", "up_bwd_w", tm=1408)
    dres1, dmix, acc1 = _ln_res_bwd(dres2, dh2, r["x"], r["mix"], prm1)
    dmerged = _mm(dmix, w["w_o"], "nt", "o_bwd_x")
    g["w_o"] = _mm(r["merged"], dmix, "tn", "o_bwd_w")
    (dya, dyb, dyc, dyd, dza, dzb, d_o, dzd, dgates, accg) = _merge_bwd(
        dmerged, r["za"], r["zb"], r["o"], r["zd"], r["proj"], w["w_a"], w["w_b"], w["w_c"], w["w_d"])
    g["w_conv_out"] = _mm(r["za"], dya, "tn", "branch_bwd_w")
    g["w_sc_out"] = _mm(r["zb"], dyb, "tn", "branch_bwd_w")
    g["w_pool_out"] = _mm(r["zd"], dyd, "tn", "branch_bwd_w")
    gwc = _mm(r["o"], dyc, "tn", "mla_out_bwd_w")
    g["w_mla_out"] = gwc.reshape(HEADS, HEAD_PAD, D)[:, :NOPE].reshape(HEADS * NOPE, D)
    dyconv, dconv_w, acca = _conv_a_bwd1(dza, r["yconv"], r["proj"], w["conv_ln_g"], w["conv_ln_b"])
    g["conv_dw"], g["conv_ln_g"], g["conv_ln_b"] = dconv_w[:CONV_W], acca[0], acca[1]
    d_ca, d_cb, acca2 = _conv_a_bwd2(dyconv, r["proj"], w["conv_w"])
    dconv, d_bg, dsc_w, accb1 = _sc_bwd1(dzb, r["proj"], w["sc_w"])
    g["sc_dw"] = dsc_w[:SC_W]
    d_cg, d_sx, accb2 = _sc_bwd2(dconv, r["proj"], w["sc_w"])
    delta, dob = _attn_prep(d_o, r["o"])
    dq, dkv = _attn_bwd(r["q"], r["k"], r["v"], dob, r["lse"], delta)
    dqp, d_kr, acckr = _rope_bwd(dq, dkv, ct, s1t, s2t)
    dqn = _mm(dqp, w["w_uq"], "nt", "uq_bwd_x")
    guq = _mm(r["qn"], dqp, "tn", "uq_bwd_w")
    g["w_uq"] = guq.reshape(256, HEADS, HEAD_PAD)[:, :, :QK_DIM].reshape(256, HEADS * QK_DIM)
    dkvn = _mm(dkv, w["w_ukv"], "nt", "ukv_bwd_x")
    gukv = _mm(r["kvn"], dkv, "tn", "ukv_bwd_w")
    g["w_ukv"] = gukv.reshape(128, HEADS, 2, HEAD_PAD)[..., :NOPE].reshape(128, HEADS * 2 * NOPE)
    d_ql, d_kvl, accq, acckv = _lat_bwd(dqn, dkvn, r["proj"], w["q_norm_g"], w["kv_norm_g"])
    g["q_norm_g"], g["kv_norm_g"] = accq[0], acckv[0]
    dpd, g["w_pool"], accd1 = _pool_bwd1(dzd, r["proj"], w["w_pool"], w["pool_scale"])
    g["pool_scale"] = accd1[0]
    d_pu, accd2 = _pool_bwd2(dpd, s)
    dproj = jnp.concatenate([dgates, d_ca, d_cb, d_bg, d_cg, d_sx, d_ql, d_kvl, d_kr, d_pu], axis=1)
    db = jnp.concatenate([accg[0], acca2[0], acca2[1], accb1[0], accb2[0], accb2[1], accq[1], acckv[1], acckr[0], accd2[0]])
    g["b_in"] = _proj_cols_inv(db)
    dh1 = _mm(dproj, w["w_inT"], "nn", "proj_bwd_x", tk=1536)
    g["w_inT"] = _proj_rows_split(_mm(dproj, r["h1"], "tn", "proj_bwd_w", tm=768))
    g["ln1_g"], g["ln1_b"], g["ln2_g"], g["ln2_b"] = acc1[3], acc1[4], acc2[3], acc2[4]
    return dres1, dh1, g, (acc1, acc2)


def _rope_tables(positions):
    half = ROPE // 2
    inv = 1.0 / (ROPE_THETA ** (jnp.arange(0, ROPE, 2, dtype=F32) / ROPE))
    ang = positions.astype(F32)[:, None] * inv
    cos, sin = jnp.cos(ang), jnp.sin(ang)
    s = positions.shape[0]
    z = lambda n: jnp.zeros((s, n), F32)
    ct = jnp.concatenate([jnp.ones((s, NOPE), F32), cos, cos, z(HEAD_PAD - QK_DIM)], axis=1)
    s1t = jnp.concatenate([z(NOPE), -sin, z(half), z(HEAD_PAD - QK_DIM)], axis=1)
    s2t = jnp.concatenate([z(NOPE), z(half), sin, z(HEAD_PAD - QK_DIM)], axis=1)
    return ct, s1t, s2t


def _local_step(x, mod, positions, full, small, target):
    tabs = _rope_tables(positions)
    ws = [_layer_weights(full, small, l) for l in range(DEPTH)]
    zero = jnp.zeros((D,), F32)
    prm1s, prm2s = [], []
    for l in range(DEPTH):
        sh1, sc1, g1, sh2, sc2, g2 = (mod[l, j] for j in range(6))
        nxt = (mod[l + 1, 1], mod[l + 1, 0]) if l + 1 < DEPTH else (zero, zero)
        prm1s.append(_prm([g1, sc2, sh2, small["ln1_g"][l], small["ln1_b"][l]]))
        prm2s.append(_prm([g2, nxt[0], nxt[1], small["ln2_g"][l], small["ln2_b"][l]]))
    prm0 = _prm([zero, mod[0, 1], mod[0, 0]])
    h = _mod_fwd(x, prm0)
    res = []
    xc = x
    for l in range(DEPTH):
        xc, h, r = _layer_fwd(xc, h, ws[l], prm1s[l], prm2s[l], tabs)
        res.append(r)
    dy, lacc = _loss_head(xc, target)
    loss = lacc[0, 0]
    dres, dh = dy, jnp.zeros_like(dy)
    grads = [None] * DEPTH
    accs = [None] * DEPTH
    for l in reversed(range(DEPTH)):
        dres, dh, grads[l], accs[l] = _layer_bwd(dres, dh, res[l], ws[l], prm1s[l], prm2s[l], tabs)
    dx, acc0 = _mod_bwd(dres, dh, x, prm0)
    dmod = []
    for l in range(DEPTH):
        acc1, acc2 = accs[l]
        dsc1, dsh1 = (acc0[0], acc0[1]) if l == 0 else (accs[l - 1][1][1], accs[l - 1][1][2])
        dmod.append(jnp.stack([dsh1, dsc1, acc1[0], acc1[2], acc1[1], acc2[0]]))
    return loss, dx, grads, jnp.stack(dmod)


ADA_SHARD = 6 * D // N_DEV


def _step(p):
    me = 4 * lax.axis_index("x") + 2 * lax.axis_index("y") + lax.axis_index("c")
    x, target, positions = p["x"][0], p["loss_target"][0], p["positions"][0]

    tap_shapes = [p[n].shape for n, _ in TAPS] + [(D,)]
    small_g = _exchange(_pack_rows([p[n] for n, _ in TAPS] + [p["c"][0]], 8), "gather_taps", True)
    parts = _unpack(small_g.reshape(N_DEV, -1), tap_shapes)
    full = {n: _join(g, ax) for (n, ax), g in zip(TAPS, parts[:-1])}
    c_all = parts[-1]
    w_in_t = jnp.swapaxes(p["w_in"], 1, 2).astype(BF16)
    w_in_t = jnp.concatenate([w_in_t, jnp.zeros((DEPTH, IN_SHARD_PAD - IN_SHARD, D), BF16)], axis=1)
    w_up_t = jnp.swapaxes(p["w_up"], 1, 2).astype(BF16)
    big_shapes = [p[n].shape for n, _ in BIG] + [w_in_t.shape, w_up_t.shape]
    big_g = _exchange(_pack_rows([p[n].astype(BF16) for n, _ in BIG] + [w_in_t, w_up_t], 16), "gather_weights", True)
    big_parts = _unpack(big_g.reshape(N_DEV, -1), big_shapes)
    for (n, ax), g in zip(BIG, big_parts):
        full[n] = _join(g, ax)
    full["w_inT"] = _join(big_parts[-2], 1)
    full["w_upT"] = _join(big_parts[-1], 1)
    small = {n: p[n] for n in SMALL}

    c_act = _silu_rows(c_all)
    w_ada_cols = jnp.moveaxis(p["w_ada"], 0, 1).reshape(D, DEPTH * ADA_SHARD)
    b_shard = lax.dynamic_slice_in_dim(p["b_ada"], me * ADA_SHARD, ADA_SHARD, axis=1).reshape(1, DEPTH * ADA_SHARD)
    mod_sh = _mm(c_act, w_ada_cols, "nn", "ada_fwd", bias=b_shard)[:N_DEV]
    mod_x = _exchange(_pack_cols([mod_sh], 8), "scatter_mod", False)
    mod = mod_x.reshape(N_DEV, -1)[:, :DEPTH * ADA_SHARD].reshape(N_DEV, DEPTH, ADA_SHARD)
    mod = jnp.moveaxis(mod, 0, 1).reshape(DEPTH, 6, D)

    loss_local, dx, grads, dmod = _local_step(x, mod, positions, full, small, target)
    loss = lax.psum(loss_local, ("x", "y", "c"))
    gfull = {n: jnp.stack([grads[l][n] for l in range(DEPTH)]) for n in grads[0]}

    out = {"loss": loss, "grad_x": dx[None]}

    def emit(names, g, dlt, mn, vn, shapes):
        for n, gi, di, mi, vi in zip(names, _unpack(g, shapes), _unpack(dlt, shapes), _unpack(mn, shapes), _unpack(vn, shapes)):
            out["grad_" + n], out["delta_" + n], out["new_m_" + n], out["new_v_" + n] = gi, di, mi, vi

    small_parts = [dmod.reshape(DEPTH, 6 * D)] + [gfull[n] for n in SMALL[1:]]
    small_all = _exchange(_pack_rows(small_parts, GRAD_ROWS), "gather_small_grads", True)
    small_shapes = [p[n].shape for n in SMALL]
    sg, sd, sm, sv = _adamw(small_all, *[_pack_rows([p[pre + n] for n in SMALL], GRAD_ROWS) for pre in ("", "m_", "v_")],
                            name="adamw_small")
    emit(SMALL, *[t.reshape(-1) for t in (sg, sd, sm, sv)], small_shapes)

    dmod_all = small_all.reshape(N_DEV, -1)[:, :DEPTH * 6 * D].reshape(N_DEV, DEPTH, 6 * D)
    dmod_sh = lax.dynamic_slice_in_dim(dmod_all, me * ADA_SHARD, ADA_SHARD, axis=2).reshape(N_DEV, DEPTH * ADA_SHARD)
    g_ada = _mm(c_act, _pad_rows(dmod_sh, 2 * N_DEV), "tn", "ada_bwd_w")
    g_ada = jnp.moveaxis(g_ada.reshape(D, DEPTH, ADA_SHARD), 1, 0)
    ag, ad, am, av = _adamw(_pack_rows([g_ada], GRAD_ROWS)[None], *[_pack_rows([p[pre + "w_ada"]], GRAD_ROWS) for pre in ("", "m_", "v_")],
                            name="adamw_ada")
    emit(("w_ada",), *[t.reshape(-1) for t in (ag, ad, am, av)], [p["w_ada"].shape])

    shard_names = [n for n, _ in BIG + TAPS]
    pieces = [_split(gfull[n], ax).reshape(N_DEV, -1) for n, ax in BIG + TAPS]
    recv = _exchange(_pack_cols(pieces, GRAD_ROWS), "scatter_grads", False)
    bg, bd, bm, bv = _adamw(recv, *[_pack_rows([p[pre + n] for n in shard_names], GRAD_ROWS) for pre in ("", "m_", "v_")],
                            name="adamw_sharded")
    emit(shard_names, *[t.reshape(-1) for t in (bg, bd, bm, bv)], [p[n].shape for n in shard_names])

    t_pieces = [jnp.moveaxis(gfull["w_inT"], 1, 0).reshape(N_DEV, -1),
                jnp.moveaxis(gfull["w_upT"].reshape(DEPTH, N_DEV, UP_SHARD, D), 1, 0).reshape(N_DEV, -1)]
    g_t = _sum8(_exchange(_pack_cols(t_pieces, GRAD_ROWS), "scatter_grads_t", False), "sum_grads_t")
    g_in_t, g_up_t = _unpack(g_t.reshape(-1), [(DEPTH, IN_SHARD, D), (DEPTH, UP_SHARD, D)])
    for n, gt in (("w_in", g_in_t), ("w_up", g_up_t)):
        shp = p[n].shape
        two_d = (shp[0] * shp[1], shp[2])
        res = _adamw(jnp.swapaxes(gt, 1, 2).reshape((1,) + two_d), *[p[pre + n].reshape(two_d) for pre in ("", "m_", "v_")],
                     name="adamw_" + n)
        for key, t in zip(("grad_", "delta_", "new_m_", "new_v_"), res):
            out[key + n] = t.reshape(shp)
    return out


_ARG_NAMES = ("x", "c", "positions") + WEIGHTS + ("loss_target",) + tuple("m_" + n for n in WEIGHTS) + tuple("v_" + n for n in WEIGHTS)
_OUT_NAMES = ("loss", "grad_x") + tuple(pre + n for pre in ("grad_", "delta_", "new_m_", "new_v_") for n in WEIGHTS)


def kernel(x, c, positions, w_ada, b_ada, w_in, b_in, conv_dw, conv_ln_g, conv_ln_b, w_conv_out, sc_dw, w_sc_out, q_norm_g, w_uq, kv_norm_g, w_ukv, w_mla_out, w_pool, pool_scale, w_pool_out, w_o, ln1_g, ln1_b, w_up, ffn_dw, w_down, ln2_g, ln2_b, loss_target, m_w_ada, m_b_ada, m_w_in, m_b_in, m_conv_dw, m_conv_ln_g, m_conv_ln_b, m_w_conv_out, m_sc_dw, m_w_sc_out, m_q_norm_g, m_w_uq, m_kv_norm_g, m_w_ukv, m_w_mla_out, m_w_pool, m_pool_scale, m_w_pool_out, m_w_o, m_ln1_g, m_ln1_b, m_w_up, m_ffn_dw, m_w_down, m_ln2_g, m_ln2_b, v_w_ada, v_b_ada, v_w_in, v_b_in, v_conv_dw, v_conv_ln_g, v_conv_ln_b, v_w_conv_out, v_sc_dw, v_w_sc_out, v_q_norm_g, v_w_uq, v_kv_norm_g, v_w_ukv, v_w_mla_out, v_w_pool, v_pool_scale, v_w_pool_out, v_w_o, v_ln1_g, v_ln1_b, v_w_up, v_ffn_dw, v_w_down, v_ln2_g, v_ln2_b):
    args = locals()
    out = _step({n: args[n] for n in _ARG_NAMES})
    return tuple(out[n] for n in _OUT_NAMES)
```

```python
import functools

import jax
import jax.numpy as jnp
from jax import lax
from jax.experimental import pallas as pl
from jax.experimental.pallas import tpu as pltpu

F32 = jnp.float32
BF16 = jnp.bfloat16

N_DEV = 8
DEPTH = 4
D = 1024
CONV_W = 31
HEADS = 8
HEAD_PAD = 128
QK_DIM = 96
NOPE = 64
ROPE = 32
ROPE_THETA = 10000.0
D_FF = 2816
LN_EPS = 1e-5
RMS_EPS = 1e-6
ALPHA = (2.0 * DEPTH) ** 0.25
ATT_SCALE = QK_DIM ** -0.5
POOL_WINDOWS = (2, 4, 8, 16)

ADAM_LR = 0.001
ADAM_B1 = 0.9
ADAM_B2 = 0.999
ADAM_EPS = 1e-08
ADAM_WD = 0.01
ADAM_STEP = 10

GATES0 = 0
CA0 = 4096
CB0 = 4608
SBG0 = 5120
SCG0 = 5632
SX0 = 6144
QL0 = 6656
KVL0 = 6912
KR0 = 7040
PU0 = 7168
NPROJ = 7680

LANE = 128
TM = 512
TM_WIDE = 256
TQ = 1024
TQ_BWD = 512
VMEM_LIMIT = 56 * 1024 * 1024

MESH = pl.DeviceIdType.MESH


def _sig(x):
    return 1.0 / (1.0 + jnp.exp(-x))


def _tile(n, pref):
    if n <= pref:
        return n
    t = (pref // LANE) * LANE
    while t >= LANE:
        if n % t == 0:
            return t
        t -= LANE
    raise ValueError(f"no lane-aligned tile for {n}")


def _params(sem):
    return pltpu.CompilerParams(dimension_semantics=sem, vmem_limit_bytes=VMEM_LIMIT)


def _full(shape):
    nd = len(shape)
    return pl.BlockSpec(shape, lambda *_: (0,) * nd)


def _rows(tm, cw, cb):
    return pl.BlockSpec((tm, cw), lambda i: (i, cb))


def _prev(tm, hb, cw, cb):
    r = tm // hb
    return pl.BlockSpec((hb, cw), lambda i: (jnp.maximum(i * r - 1, 0), cb))


def _next(tm, hb, cw, cb, s):
    r = tm // hb
    last = s // hb - 1
    return pl.BlockSpec((hb, cw), lambda i: (jnp.minimum((i + 1) * r, last), cb))


def _acc_rows(ref, first, rows):
    @pl.when(first)
    def _():
        ref[...] = jnp.zeros_like(ref)
    for r, v in enumerate(rows):
        ref[r:r + 1, :] += v


def _colsum(v):
    return jnp.sum(v, axis=0, keepdims=True)


def _ln_stats(r):
    mu = jnp.mean(r, axis=-1, keepdims=True)
    xc = r - mu
    var = jnp.mean(xc * xc, axis=-1, keepdims=True)
    rstd = lax.rsqrt(var + LN_EPS)
    return xc * rstd, rstd


def _ln_bwd(dxh, xh, rstd):
    return rstd * (dxh - jnp.mean(dxh, axis=-1, keepdims=True) - xh * jnp.mean(dxh * xh, axis=-1, keepdims=True))


_DIMS = {"nn": ((1,), (0,)), "nt": ((1,), (1,)), "tn": ((0,), (0,))}


def _mm(a, b, mode, name, *, out_dtype=F32, bias=None, tm=512, tn=1024, tk=2048):
    if mode == "nn":
        (m, k), (k2, n) = a.shape, b.shape
    elif mode == "nt":
        (m, k), (n, k2) = a.shape, b.shape
    else:
        (k, m), (k2, n) = a.shape, b.shape
    assert k == k2, (a.shape, b.shape, mode)
    tm, tn, tk = _tile(m, tm), _tile(n, tn), _tile(k, tk)
    nk = k // tk
    dims = (_DIMS[mode], ((), ()))
    has_bias = bias is not None

    def body(*refs):
        if has_bias:
            a_ref, b_ref, bias_ref, o_ref = refs[:4]
        else:
            a_ref, b_ref, o_ref = refs[:3]
        p = lax.dot_general(a_ref[...].astype(BF16), b_ref[...].astype(BF16), dims, preferred_element_type=F32)

        def finish(r):
            if has_bias:
                r = r + bias_ref[...]
            o_ref[...] = r.astype(out_dtype)

        if nk == 1:
            finish(p)
        else:
            acc = refs[-1]
            kk = pl.program_id(2)

            @pl.when(kk == 0)
            def _():
                acc[...] = p

            @pl.when(kk > 0)
            def _():
                acc[...] += p

            @pl.when(kk == nk - 1)
            def _():
                finish(acc[...])

    if mode == "nn":
        a_spec = pl.BlockSpec((tm, tk), lambda i, j, kk: (i, kk))
        b_spec = pl.BlockSpec((tk, tn), lambda i, j, kk: (kk, j))
    elif mode == "nt":
        a_spec = pl.BlockSpec((tm, tk), lambda i, j, kk: (i, kk))
        b_spec = pl.BlockSpec((tn, tk), lambda i, j, kk: (j, kk))
    else:
        a_spec = pl.BlockSpec((tk, tm), lambda i, j, kk: (kk, i))
        b_spec = pl.BlockSpec((tk, tn), lambda i, j, kk: (kk, j))
    in_specs = [a_spec, b_spec]
    args = [a, b]
    if has_bias:
        in_specs.append(pl.BlockSpec((1, tn), lambda i, j, kk: (0, j)))
        args.append(bias)
    return pl.pallas_call(
        body, name=name, grid=(m // tm, n // tn, nk),
        in_specs=in_specs, out_specs=pl.BlockSpec((tm, tn), lambda i, j, kk: (i, j)),
        out_shape=jax.ShapeDtypeStruct((m, n), out_dtype),
        scratch_shapes=[pltpu.VMEM((tm, tn), F32)] if nk > 1 else [],
        compiler_params=_params(("parallel", "parallel", "arbitrary")),
    )(*args)


def _mod_fwd(x, prm):
    s = x.shape[0]
    tm = _tile(s, TM)

    def body(x_ref, p_ref, h_ref):
        h_ref[...] = (x_ref[...] * (1.0 + p_ref[1:2, :]) + p_ref[2:3, :]).astype(BF16)

    return pl.pallas_call(
        body, name="mod_fwd", grid=(s // tm,),
        in_specs=[_rows(tm, D, 0), _full((8, D))], out_specs=_rows(tm, D, 0),
        out_shape=jax.ShapeDtypeStruct((s, D), BF16), compiler_params=_params(("parallel",)),
    )(x, prm)


def _mod_bwd(dres, dh, x, prm):
    s = x.shape[0]
    tm = _tile(s, TM)

    def body(dres_ref, dh_ref, x_ref, p_ref, dx_ref, acc_ref):
        dh_v = dh_ref[...]
        dx_ref[...] = dres_ref[...] + dh_v * (1.0 + p_ref[1:2, :])
        _acc_rows(acc_ref, pl.program_id(0) == 0, [_colsum(dh_v * x_ref[...]), _colsum(dh_v)])

    return pl.pallas_call(
        body, name="mod_bwd", grid=(s // tm,),
        in_specs=[_rows(tm, D, 0)] * 3 + [_full((8, D))],
        out_specs=[_rows(tm, D, 0), _full((8, D))],
        out_shape=[jax.ShapeDtypeStruct((s, D), F32), jax.ShapeDtypeStruct((8, D), F32)],
        compiler_params=_params(("arbitrary",)),
    )(dres, dh, x, prm)


CA_HALO = 32
C512 = 512


def _glu_buf(buf, ap, am, bp, bm, first, tm):
    glu_p = ap[...] * _sig(bp[...])
    buf[0:CA_HALO, :] = jnp.where(first, jnp.zeros_like(glu_p), glu_p)
    buf[CA_HALO:CA_HALO + tm, :] = am[...] * _sig(bm[...])


def _conv_a_fwd(proj, w32, ln_g, ln_b):
    s = proj.shape[0]
    tm = _tile(s, TM)
    ca, cb = CA0 // C512, CB0 // C512

    def body(ap, am, bp, bm, w_ref, g_ref, b_ref, yc_ref, za_ref, buf):
        _glu_buf(buf, ap, am, bp, bm, pl.program_id(0) == 0, tm)
        acc = w_ref[0:1, :] * buf[pl.ds(CA_HALO - CONV_W + 1, tm), :]
        for k in range(1, CONV_W):
            acc = acc + w_ref[k:k + 1, :] * buf[pl.ds(CA_HALO - CONV_W + 1 + k, tm), :]
        yc_ref[...] = acc
        xh, _ = _ln_stats(acc)
        y = xh * g_ref[...] + b_ref[...]
        za_ref[...] = (y * _sig(y)).astype(BF16)

    return pl.pallas_call(
        body, name="conv_a_fwd", grid=(s // tm,),
        in_specs=[_prev(tm, CA_HALO, C512, ca), _rows(tm, C512, ca), _prev(tm, CA_HALO, C512, cb), _rows(tm, C512, cb),
                  _full((32, C512)), _full((1, C512)), _full((1, C512))],
        out_specs=[_rows(tm, C512, 0), _rows(tm, C512, 0)],
        out_shape=[jax.ShapeDtypeStruct((s, C512), F32), jax.ShapeDtypeStruct((s, C512), BF16)],
        scratch_shapes=[pltpu.VMEM((tm + CA_HALO, C512), F32)],
        compiler_params=_params(("parallel",)),
    )(proj, proj, proj, proj, w32, ln_g, ln_b)


def _conv_a_bwd1(dza, yconv, proj, ln_g, ln_b):
    s = proj.shape[0]
    tm = _tile(s, TM)
    ca, cb = CA0 // C512, CB0 // C512

    def body(dza_ref, yc_ref, ap, am, bp, bm, g_ref, b_ref, dyc_ref, dw_ref, acc_ref, buf):
        first = pl.program_id(0) == 0
        xh, rstd = _ln_stats(yc_ref[...])
        g = g_ref[...]
        y = xh * g + b_ref[...]
        sg = _sig(y)
        dy = dza_ref[...] * (sg * (1.0 + y * (1.0 - sg)))
        _acc_rows(acc_ref, first, [_colsum(dy * xh), _colsum(dy)])
        dyc = _ln_bwd(dy * g, xh, rstd)
        dyc_ref[...] = dyc
        _glu_buf(buf, ap, am, bp, bm, first, tm)

        @pl.when(first)
        def _():
            dw_ref[...] = jnp.zeros_like(dw_ref)
        for k in range(CONV_W):
            dw_ref[k:k + 1, :] += _colsum(dyc * buf[pl.ds(CA_HALO - CONV_W + 1 + k, tm), :])

    return pl.pallas_call(
        body, name="conv_a_bwd1", grid=(s // tm,),
        in_specs=[_rows(tm, C512, 0), _rows(tm, C512, 0),
                  _prev(tm, CA_HALO, C512, ca), _rows(tm, C512, ca), _prev(tm, CA_HALO, C512, cb), _rows(tm, C512, cb),
                  _full((1, C512)), _full((1, C512))],
        out_specs=[_rows(tm, C512, 0), _full((32, C512)), _full((8, C512))],
        out_shape=[jax.ShapeDtypeStruct((s, C512), F32), jax.ShapeDtypeStruct((32, C512), F32),
                   jax.ShapeDtypeStruct((8, C512), F32)],
        scratch_shapes=[pltpu.VMEM((tm + CA_HALO, C512), F32)],
        compiler_params=_params(("arbitrary",)),
    )(dza, yconv, proj, proj, proj, proj, ln_g, ln_b)


def _conv_a_bwd2(dyc, proj, w32):
    s = proj.shape[0]
    tm = _tile(s, TM)
    ca, cb = CA0 // C512, CB0 // C512
    nt = s // tm

    def body(dm, dn, am, bm, w_ref, da_ref, db_ref, acc_ref, buf):
        i = pl.program_id(0)
        buf[0:tm, :] = dm[...]
        nxt = dn[...]
        buf[tm:tm + CA_HALO, :] = jnp.where(i == nt - 1, jnp.zeros_like(nxt), nxt)
        dglu = w_ref[0:1, :] * buf[pl.ds(CONV_W - 1, tm), :]
        for k in range(1, CONV_W):
            dglu = dglu + w_ref[k:k + 1, :] * buf[pl.ds(CONV_W - 1 - k, tm), :]
        sb = _sig(bm[...])
        da = dglu * sb
        db = dglu * am[...] * sb * (1.0 - sb)
        da_ref[...] = da.astype(BF16)
        db_ref[...] = db.astype(BF16)
        _acc_rows(acc_ref, i == 0, [_colsum(da), _colsum(db)])

    return pl.pallas_call(
        body, name="conv_a_bwd2", grid=(nt,),
        in_specs=[_rows(tm, C512, 0), _next(tm, CA_HALO, C512, 0, s), _rows(tm, C512, ca), _rows(tm, C512, cb),
                  _full((32, C512))],
        out_specs=[_rows(tm, C512, 0), _rows(tm, C512, 0), _full((8, C512))],
        out_shape=[jax.ShapeDtypeStruct((s, C512), BF16), jax.ShapeDtypeStruct((s, C512), BF16),
                   jax.ShapeDtypeStruct((8, C512), F32)],
        scratch_shapes=[pltpu.VMEM((tm + CA_HALO, C512), F32)],
        compiler_params=_params(("arbitrary",)),
    )(dyc, dyc, proj, proj, w32)


H8 = 8
SC_W = 3


def _sc_ubuf(buf, cp, cm, xp, xm, first, tm):
    up = cp[...] * xp[...]
    buf[0:H8, :] = jnp.where(first, jnp.zeros_like(up), up)
    buf[H8:H8 + tm, :] = cm[...] * xm[...]


def _conv3(w_ref, buf, tm):
    acc = w_ref[0:1, :] * buf[pl.ds(H8 - SC_W + 1, tm), :]
    for k in range(1, SC_W):
        acc = acc + w_ref[k:k + 1, :] * buf[pl.ds(H8 - SC_W + 1 + k, tm), :]
    return acc


def _conv3_t(w_ref, buf, tm):
    acc = w_ref[0:1, :] * buf[pl.ds(SC_W - 1, tm), :]
    for k in range(1, SC_W):
        acc = acc + w_ref[k:k + 1, :] * buf[pl.ds(SC_W - 1 - k, tm), :]
    return acc


def _sc_fwd(proj, w8):
    s = proj.shape[0]
    tm = _tile(s, TM)
    c_bg, c_cg, c_x = SBG0 // C512, SCG0 // C512, SX0 // C512

    def body(bg, cp, cm, xp, xm, w_ref, zb_ref, buf):
        _sc_ubuf(buf, cp, cm, xp, xm, pl.program_id(0) == 0, tm)
        zb_ref[...] = (bg[...] * _conv3(w_ref, buf, tm)).astype(BF16)

    return pl.pallas_call(
        body, name="sc_fwd", grid=(s // tm,),
        in_specs=[_rows(tm, C512, c_bg), _prev(tm, H8, C512, c_cg), _rows(tm, C512, c_cg),
                  _prev(tm, H8, C512, c_x), _rows(tm, C512, c_x), _full((8, C512))],
        out_specs=_rows(tm, C512, 0), out_shape=jax.ShapeDtypeStruct((s, C512), BF16),
        scratch_shapes=[pltpu.VMEM((tm + H8, C512), F32)], compiler_params=_params(("parallel",)),
    )(proj, proj, proj, proj, proj, w8)


def _sc_bwd1(dzb, proj, w8):
    s = proj.shape[0]
    tm = _tile(s, TM)
    c_bg, c_cg, c_x = SBG0 // C512, SCG0 // C512, SX0 // C512

    def body(dz_ref, bg, cp, cm, xp, xm, w_ref, dconv_ref, dbg_ref, dw_ref, acc_ref, buf):
        first = pl.program_id(0) == 0
        _sc_ubuf(buf, cp, cm, xp, xm, first, tm)
        dz = dz_ref[...]
        dbg = dz * _conv3(w_ref, buf, tm)
        dconv = dz * bg[...]
        dconv_ref[...] = dconv
        dbg_ref[...] = dbg.astype(BF16)
        _acc_rows(acc_ref, first, [_colsum(dbg)])
        _acc_rows(dw_ref, first, [_colsum(dconv * buf[pl.ds(H8 - SC_W + 1 + k, tm), :]) for k in range(SC_W)])

    return pl.pallas_call(
        body, name="sc_bwd1", grid=(s // tm,),
        in_specs=[_rows(tm, C512, 0), _rows(tm, C512, c_bg), _prev(tm, H8, C512, c_cg), _rows(tm, C512, c_cg),
                  _prev(tm, H8, C512, c_x), _rows(tm, C512, c_x), _full((8, C512))],
        out_specs=[_rows(tm, C512, 0), _rows(tm, C512, 0), _full((8, C512)), _full((8, C512))],
        out_shape=[jax.ShapeDtypeStruct((s, C512), F32), jax.ShapeDtypeStruct((s, C512), BF16),
                   jax.ShapeDtypeStruct((8, C512), F32), jax.ShapeDtypeStruct((8, C512), F32)],
        scratch_shapes=[pltpu.VMEM((tm + H8, C512), F32)], compiler_params=_params(("arbitrary",)),
    )(dzb, proj, proj, proj, proj, proj, w8)


def _sc_bwd2(dconv, proj, w8):
    s = proj.shape[0]
    tm = _tile(s, TM)
    c_cg, c_x = SCG0 // C512, SX0 // C512
    nt = s // tm

    def body(dm, dn, cm, xm, w_ref, dcg_ref, dx_ref, acc_ref, buf):
        i = pl.program_id(0)
        buf[0:tm, :] = dm[...]
        nxt = dn[...]
        buf[tm:tm + H8, :] = jnp.where(i == nt - 1, jnp.zeros_like(nxt), nxt)
        du = _conv3_t(w_ref, buf, tm)
        dcg = du * xm[...]
        dx = du * cm[...]
        dcg_ref[...] = dcg.astype(BF16)
        dx_ref[...] = dx.astype(BF16)
        _acc_rows(acc_ref, i == 0, [_colsum(dcg), _colsum(dx)])

    return pl.pallas_call(
        body, name="sc_bwd2", grid=(nt,),
        in_specs=[_rows(tm, C512, 0), _next(tm, H8, C512, 0, s), _rows(tm, C512, c_cg), _rows(tm, C512, c_x),
                  _full((8, C512))],
        out_specs=[_rows(tm, C512, 0), _rows(tm, C512, 0), _full((8, C512))],
        out_shape=[jax.ShapeDtypeStruct((s, C512), BF16), jax.ShapeDtypeStruct((s, C512), BF16),
                   jax.ShapeDtypeStruct((8, C512), F32)],
        scratch_shapes=[pltpu.VMEM((tm + H8, C512), F32)], compiler_params=_params(("arbitrary",)),
    )(dconv, dconv, proj, proj, w8)


QLAT = 256
KVLAT = 128


def _rms(x, g):
    r = lax.rsqrt(jnp.mean(x * x, axis=-1, keepdims=True) + RMS_EPS)
    return x * r * g, r


def _rms_bwd(dy, x, g, r):
    u = dy * g
    dx = r * u - x * (r * r * r) * jnp.mean(u * x, axis=-1, keepdims=True)
    return dx, _colsum(dy * x * r)


def _lat_fwd(proj, gq, gkv):
    s = proj.shape[0]
    tm = _tile(s, TM)

    def body(q_ref, kv_ref, gq_ref, gkv_ref, qn_ref, kvn_ref):
        qn_ref[...] = _rms(q_ref[...], gq_ref[...])[0].astype(BF16)
        kvn_ref[...] = _rms(kv_ref[...], gkv_ref[...])[0].astype(BF16)

    return pl.pallas_call(
        body, name="lat_fwd", grid=(s // tm,),
        in_specs=[_rows(tm, QLAT, QL0 // QLAT), _rows(tm, KVLAT, KVL0 // KVLAT), _full((1, QLAT)), _full((1, KVLAT))],
        out_specs=[_rows(tm, QLAT, 0), _rows(tm, KVLAT, 0)],
        out_shape=[jax.ShapeDtypeStruct((s, QLAT), BF16), jax.ShapeDtypeStruct((s, KVLAT), BF16)],
        compiler_params=_params(("parallel",)),
    )(proj, proj, gq, gkv)


def _lat_bwd(dqn, dkvn, proj, gq, gkv):
    s = proj.shape[0]
    tm = _tile(s, TM)

    def body(dqn_ref, dkvn_ref, q_ref, kv_ref, gq_ref, gkv_ref, dq_ref, dkv_ref, accq_ref, acckv_ref):
        first = pl.program_id(0) == 0
        q, kv = q_ref[...], kv_ref[...]
        gqv, gkvv = gq_ref[...], gkv_ref[...]
        dq, dgq = _rms_bwd(dqn_ref[...], q, gqv, _rms(q, gqv)[1])
        dkv, dgkv = _rms_bwd(dkvn_ref[...], kv, gkvv, _rms(kv, gkvv)[1])
        dq_ref[...] = dq.astype(BF16)
        dkv_ref[...] = dkv.astype(BF16)
        _acc_rows(accq_ref, first, [dgq, _colsum(dq)])
        _acc_rows(acckv_ref, first, [dgkv, _colsum(dkv)])

    return pl.pallas_call(
        body, name="lat_bwd", grid=(s // tm,),
        in_specs=[_rows(tm, QLAT, 0), _rows(tm, KVLAT, 0), _rows(tm, QLAT, QL0 // QLAT), _rows(tm, KVLAT, KVL0 // KVLAT),
                  _full((1, QLAT)), _full((1, KVLAT))],
        out_specs=[_rows(tm, QLAT, 0), _rows(tm, KVLAT, 0), _full((8, QLAT)), _full((8, KVLAT))],
        out_shape=[jax.ShapeDtypeStruct((s, QLAT), BF16), jax.ShapeDtypeStruct((s, KVLAT), BF16),
                   jax.ShapeDtypeStruct((8, QLAT), F32), jax.ShapeDtypeStruct((8, KVLAT), F32)],
        compiler_params=_params(("arbitrary",)),
    )(dqn, dkvn, proj, proj, gq, gkv)


def _rope(x, c, s1, s2):
    return x * c + pltpu.roll(x, HEAD_PAD - ROPE // 2, 1) * s1 + pltpu.roll(x, ROPE // 2, 1) * s2


def _rope_t(d, c, s1, s2):
    return d * c + pltpu.roll(d * s1, ROPE // 2, 1) + pltpu.roll(d * s2, HEAD_PAD - ROPE // 2, 1)


def _rope_fwd(qp, kvp, proj, ct, s1t, s2t):
    s = proj.shape[0]
    tm = _tile(s, TM)

    def body(q_ref, kv_ref, kr_ref, c_ref, s1_ref, s2_ref, qo, ko, vo):
        c, s1, s2 = c_ref[...], s1_ref[...], s2_ref[...]
        kr = _rope(kr_ref[...], c, s1, s2)
        for h in range(HEADS):
            cs = slice(h * HEAD_PAD, (h + 1) * HEAD_PAD)
            qo[:, cs] = _rope(q_ref[:, cs], c, s1, s2).astype(BF16)
            ko[:, cs] = (kv_ref[:, 2 * h * HEAD_PAD:(2 * h + 1) * HEAD_PAD] + kr).astype(BF16)
            vo[:, cs] = kv_ref[:, (2 * h + 1) * HEAD_PAD:(2 * h + 2) * HEAD_PAD].astype(BF16)

    tab = _rows(tm, HEAD_PAD, 0)
    return pl.pallas_call(
        body, name="rope_fwd", grid=(s // tm,),
        in_specs=[_rows(tm, HEADS * HEAD_PAD, 0), _rows(tm, 2 * HEADS * HEAD_PAD, 0), _rows(tm, HEAD_PAD, KR0 // HEAD_PAD),
                  tab, tab, tab],
        out_specs=[_rows(tm, HEADS * HEAD_PAD, 0)] * 3,
        out_shape=[jax.ShapeDtypeStruct((s, HEADS * HEAD_PAD), BF16)] * 3,
        compiler_params=_params(("parallel",)),
    )(qp, kvp, proj, ct, s1t, s2t)


def _rope_bwd(dq, dkv, ct, s1t, s2t):
    s = dq.shape[0]
    tm = _tile(s, TM)

    def body(dq_ref, dkv_ref, c_ref, s1_ref, s2_ref, dqo, dkr_ref, acc_ref):
        c, s1, s2 = c_ref[...], s1_ref[...], s2_ref[...]
        tot = dkv_ref[:, 0:HEAD_PAD]
        for h in range(HEADS):
            cs = slice(h * HEAD_PAD, (h + 1) * HEAD_PAD)
            dqo[:, cs] = _rope_t(dq_ref[:, cs], c, s1, s2).astype(BF16)
            if h:
                tot = tot + dkv_ref[:, 2 * h * HEAD_PAD:(2 * h + 1) * HEAD_PAD]
        lane = lax.broadcasted_iota(jnp.int32, (tm, HEAD_PAD), 1)
        dkr = jnp.where((lane >= NOPE) & (lane < QK_DIM), _rope_t(tot, c, s1, s2), 0.0)
        dkr_ref[...] = dkr.astype(BF16)
        _acc_rows(acc_ref, pl.program_id(0) == 0, [_colsum(dkr)])

    tab = _rows(tm, HEAD_PAD, 0)
    return pl.pallas_call(
        body, name="rope_bwd", grid=(s // tm,),
        in_specs=[_rows(tm, HEADS * HEAD_PAD, 0), _rows(tm, 2 * HEADS * HEAD_PAD, 0), tab, tab, tab],
        out_specs=[_rows(tm, HEADS * HEAD_PAD, 0), tab, _full((8, HEAD_PAD))],
        out_shape=[jax.ShapeDtypeStruct((s, HEADS * HEAD_PAD), BF16), jax.ShapeDtypeStruct((s, HEAD_PAD), BF16),
                   jax.ShapeDtypeStruct((8, HEAD_PAD), F32)],
        compiler_params=_params(("arbitrary",)),
    )(dq, dkv, ct, s1t, s2t)


_NT = (((1,), (1,)), ((), ()))
_TN = (((0,), (0,)), ((), ()))
_NN = (((1,), (0,)), ((), ()))


def _tile_rows(ref, t, tq):
    return ref[pl.ds(pl.multiple_of(t * tq, tq), tq), :]


def _attn_fwd(q, k, v):
    s = q.shape[0]
    tq = _tile(s, TQ)
    nq = s // tq

    def body(q_ref, k_ref, v_ref, o_ref, lse_ref):
        qi = pl.program_id(1)
        q_t = q_ref[...]

        def raw(ki):
            return lax.dot_general(_tile_rows(k_ref, ki, tq), q_t, _NT, preferred_element_type=F32)

        def process(ki, st, m, l, acc, masked):
            sc = st * ATT_SCALE
            if masked:
                key = lax.broadcasted_iota(jnp.int32, (tq, tq), 0)
                qry = lax.broadcasted_iota(jnp.int32, (tq, tq), 1)
                sc = jnp.where(key <= qry, sc, -jnp.inf)
            m_new = jnp.maximum(m, jnp.max(sc, axis=0, keepdims=True))
            pt = jnp.exp(sc - m_new)
            a = jnp.exp(m - m_new)
            pv = lax.dot_general(_tile_rows(v_ref, ki, tq), pt.astype(BF16), _TN, preferred_element_type=F32)
            return m_new, a * l + jnp.sum(pt, axis=0, keepdims=True), a * acc + pv

        def loop_body(ki, c):
            nxt = raw(ki + 1)
            return (nxt,) + process(ki, c[0], c[1], c[2], c[3], False)

        init = (raw(0), jnp.full((1, tq), -jnp.inf, F32), jnp.zeros((1, tq), F32), jnp.zeros((HEAD_PAD, tq), F32))
        c = lax.fori_loop(0, qi, loop_body, init)
        m, l, acc = process(qi, c[0], c[1], c[2], c[3], True)
        o_ref[...] = jnp.transpose(acc / l)
        lse_ref[0] = jnp.broadcast_to(m + jnp.log(l), (8, tq))

    qspec = pl.BlockSpec((tq, HEAD_PAD), lambda h, qi: (qi, h))
    kspec = pl.BlockSpec((s, HEAD_PAD), lambda h, qi: (0, h))
    return pl.pallas_call(
        body, name="attn_fwd", grid=(HEADS, nq),
        in_specs=[qspec, kspec, kspec],
        out_specs=[qspec, pl.BlockSpec((1, 8, tq), lambda h, qi: (h, 0, qi))],
        out_shape=[jax.ShapeDtypeStruct((s, HEADS * HEAD_PAD), F32), jax.ShapeDtypeStruct((HEADS, 8, s), F32)],
        compiler_params=_params(("parallel", "parallel")),
    )(q, k, v)


def _attn_prep(d_o, o):
    s = o.shape[0]
    tm = _tile(s, TM)

    def body(do_ref, o_ref, dl_ref, dob_ref):
        for h in range(HEADS):
            cs = slice(h * HEAD_PAD, (h + 1) * HEAD_PAD)
            dov = do_ref[:, cs]
            row = jnp.sum(jnp.transpose(dov * o_ref[:, cs]), axis=0, keepdims=True)
            dl_ref[h] = jnp.broadcast_to(row, (8, tm))
            dob_ref[:, cs] = dov.astype(BF16)

    blk = _rows(tm, HEADS * HEAD_PAD, 0)
    return pl.pallas_call(
        body, name="attn_prep", grid=(s // tm,),
        in_specs=[blk, blk], out_specs=[pl.BlockSpec((HEADS, 8, tm), lambda i: (0, 0, i)), blk],
        out_shape=[jax.ShapeDtypeStruct((HEADS, 8, s), F32), jax.ShapeDtypeStruct((s, HEADS * HEAD_PAD), BF16)],
        compiler_params=_params(("parallel",)),
    )(d_o, o)


def _attn_bwd(q, k, v, d_o, lse, delta):
    s = q.shape[0]
    tq = _tile(s, TQ_BWD)
    nq = s // tq

    def body(q_ref, k_ref, v_ref, do_ref, lse_ref, dl_ref, dq_ref, dkv_ref):
        ki = pl.program_id(1)
        k_t, v_t = k_ref[...], v_ref[...]

        @pl.when(ki == 0)
        def _():
            dq_ref[...] = jnp.zeros_like(dq_ref)

        def raw(qi):
            return (lax.dot_general(k_t, _tile_rows(q_ref, qi, tq), _NT, preferred_element_type=F32),
                    lax.dot_general(v_t, _tile_rows(do_ref, qi, tq), _NT, preferred_element_type=F32))

        def process(qi, st, dpt, dk, dv, masked):
            cols = pl.ds(pl.multiple_of(qi * tq, tq), tq)
            sc = st * ATT_SCALE
            if masked:
                key = lax.broadcasted_iota(jnp.int32, (tq, tq), 0)
                qry = lax.broadcasted_iota(jnp.int32, (tq, tq), 1)
                sc = jnp.where(key <= qry, sc, -jnp.inf)
            pt = jnp.exp(sc - lse_ref[0, 0:1, cols])
            dsb = (pt * (dpt - dl_ref[0, 0:1, cols]) * ATT_SCALE).astype(BF16)
            dv = dv + lax.dot_general(pt.astype(BF16), _tile_rows(do_ref, qi, tq), _NN, preferred_element_type=F32)
            dk = dk + lax.dot_general(dsb, _tile_rows(q_ref, qi, tq), _NN, preferred_element_type=F32)
            dq_ref[cols, :] += lax.dot_general(dsb, k_t, _TN, preferred_element_type=F32)
            return dk, dv

        zero = jnp.zeros((tq, HEAD_PAD), F32)
        dk, dv = process(ki, *raw(ki), zero, zero, True)

        def loop_body(qi, c):
            nxt = raw(jnp.minimum(qi + 1, nq - 1))
            return nxt + process(qi, c[0], c[1], c[2], c[3], False)

        c = lax.fori_loop(ki + 1, nq, loop_body, raw(jnp.minimum(ki + 1, nq - 1)) + (dk, dv))
        dkv_ref[:, 0:HEAD_PAD] = c[2]
        dkv_ref[:, HEAD_PAD:2 * HEAD_PAD] = c[3]

    full = pl.BlockSpec((s, HEAD_PAD), lambda h, ki: (0, h))
    tile = pl.BlockSpec((tq, HEAD_PAD), lambda h, ki: (ki, h))
    stat = pl.BlockSpec((1, 8, s), lambda h, ki: (h, 0, 0))
    return pl.pallas_call(
        body, name="attn_bwd", grid=(HEADS, nq),
        in_specs=[full, tile, tile, full, stat, stat],
        out_specs=[full, pl.BlockSpec((tq, 2 * HEAD_PAD), lambda h, ki: (ki, h))],
        out_shape=[jax.ShapeDtypeStruct((s, HEADS * HEAD_PAD), F32), jax.ShapeDtypeStruct((s, HEADS * 2 * HEAD_PAD), F32)],
        compiler_params=_params(("parallel", "arbitrary")),
    )(q, k, v, d_o, lse, delta)


PH = 16
PG = 128


def _pool_pd(buf, u_main_ref, g, i, tm):
    w = POOL_WINDOWS[g]
    cs = pl.ds(g * PG, PG)
    tot = buf[pl.ds(PH, tm), cs]
    for j in range(1, w):
        tot = tot + buf[pl.ds(PH - j, tm), cs]
    t = i * tm + lax.broadcasted_iota(jnp.int32, (tm, PG), 0)
    cnt = jnp.minimum(t + 1, w).astype(F32)
    return tot / cnt - u_main_ref[:, cs]


def _pool_ubuf(buf, up, um, first, tm):
    p = up[...]
    buf[0:PH, :] = jnp.where(first, jnp.zeros_like(p), p)
    buf[PH:PH + tm, :] = um[...]


def _pool_fwd(proj, w_pool, scale):
    s = proj.shape[0]
    tm = _tile(s, TM)
    cu = PU0 // C512

    def body(up, um, w_ref, sc_ref, zd_ref, buf):
        i = pl.program_id(0)
        _pool_ubuf(buf, up, um, i == 0, tm)
        for g in range(4):
            pd = _pool_pd(buf, um, g, i, tm).astype(BF16)
            e = lax.dot_general(pd, w_ref[g], _NN, preferred_element_type=F32)
            zd_ref[:, g * PG:(g + 1) * PG] = (e * sc_ref[:, g * PG:(g + 1) * PG]).astype(BF16)

    return pl.pallas_call(
        body, name="pool_fwd", grid=(s // tm,),
        in_specs=[_prev(tm, PH, C512, cu), _rows(tm, C512, cu), _full((4, PG, PG)), _full((1, C512))],
        out_specs=_rows(tm, C512, 0), out_shape=jax.ShapeDtypeStruct((s, C512), BF16),
        scratch_shapes=[pltpu.VMEM((tm + PH, C512), F32)], compiler_params=_params(("parallel",)),
    )(proj, proj, w_pool, scale)


def _pool_bwd1(dzd, proj, w_pool, scale):
    s = proj.shape[0]
    tm = _tile(s, TM)
    cu = PU0 // C512

    def body(dz_ref, up, um, w_ref, sc_ref, dpd_ref, dw_ref, acc_ref, buf):
        i = pl.program_id(0)
        first = i == 0
        _pool_ubuf(buf, up, um, first, tm)

        @pl.when(first)
        def _():
            dw_ref[...] = jnp.zeros_like(dw_ref)
            acc_ref[...] = jnp.zeros_like(acc_ref)
        for g in range(4):
            cs = slice(g * PG, (g + 1) * PG)
            pd = _pool_pd(buf, um, g, i, tm).astype(BF16)
            wg = w_ref[g]
            e = lax.dot_general(pd, wg, _NN, preferred_element_type=F32)
            dz = dz_ref[:, cs]
            acc_ref[0:1, cs] += _colsum(dz * e)
            de = (dz * sc_ref[:, cs]).astype(BF16)
            dw_ref[g] += lax.dot_general(pd, de, _TN, preferred_element_type=F32)
            dpd_ref[:, cs] = lax.dot_general(de, wg, _NT, preferred_element_type=F32)

    return pl.pallas_call(
        body, name="pool_bwd1", grid=(s // tm,),
        in_specs=[_rows(tm, C512, 0), _prev(tm, PH, C512, cu), _rows(tm, C512, cu), _full((4, PG, PG)), _full((1, C512))],
        out_specs=[_rows(tm, C512, 0), _full((4, PG, PG)), _full((8, C512))],
        out_shape=[jax.ShapeDtypeStruct((s, C512), F32), jax.ShapeDtypeStruct((4, PG, PG), F32),
                   jax.ShapeDtypeStruct((8, C512), F32)],
        scratch_shapes=[pltpu.VMEM((tm + PH, C512), F32)], compiler_params=_params(("arbitrary",)),
    )(dzd, proj, proj, w_pool, scale)


def _pool_bwd2(dpd, s):
    tm = _tile(s, TM)
    nt = s // tm

    def body(dm, dn, du_ref, acc_ref, buf):
        i = pl.program_id(0)
        buf[0:tm, :] = dm[...]
        nxt = dn[...]
        buf[tm:tm + PH, :] = jnp.where(i == nt - 1, jnp.zeros_like(nxt), nxt)
        t = i * tm + lax.broadcasted_iota(jnp.int32, (tm + PH, PG), 0)
        cols = []
        for g, w in enumerate(POOL_WINDOWS):
            cs = pl.ds(g * PG, PG)
            cnt = jnp.minimum(t + 1, w).astype(F32)
            buf[:, cs] = buf[:, cs] / cnt
        for g, w in enumerate(POOL_WINDOWS):
            cs = pl.ds(g * PG, PG)
            tot = buf[pl.ds(0, tm), cs]
            for j in range(1, w):
                tot = tot + buf[pl.ds(j, tm), cs]
            du = tot - dm[:, cs]
            du_ref[:, cs] = du.astype(BF16)
            cols.append(_colsum(du))
        _acc_rows(acc_ref, i == 0, [jnp.concatenate(cols, axis=1)])

    return pl.pallas_call(
        body, name="pool_bwd2", grid=(nt,),
        in_specs=[_rows(tm, C512, 0), _next(tm, PH, C512, 0, s)],
        out_specs=[_rows(tm, C512, 0), _full((8, C512))],
        out_shape=[jax.ShapeDtypeStruct((s, C512), BF16), jax.ShapeDtypeStruct((8, C512), F32)],
        scratch_shapes=[pltpu.VMEM((tm + PH, C512), F32)], compiler_params=_params(("arbitrary",)),
    )(dpd, dpd)


def _merge_specs(tm):
    return [_rows(tm, C512, 0), _rows(tm, C512, 0), _rows(tm, D, 0), _rows(tm, C512, 0),
            _rows(tm, 4 * D, GATES0 // (4 * D)),
            _full((C512, D)), _full((C512, D)), _full((D, D)), _full((C512, D))]


def _branch_ys(za, zb, o, zd, wa, wb, wc, wd):
    zs = (za[...], zb[...], o[...].astype(BF16), zd[...])
    return [lax.dot_general(z, w[...], _NN, preferred_element_type=F32) for z, w in zip(zs, (wa, wb, wc, wd))]


def _merge_fwd(za, zb, o, zd, proj, wa, wb, wc, wd):
    s = proj.shape[0]
    tm = _tile(s, TM_WIDE)

    def body(za_r, zb_r, o_r, zd_r, g_ref, wa_r, wb_r, wc_r, wd_r, m_ref):
        ys = _branch_ys(za_r, zb_r, o_r, zd_r, wa_r, wb_r, wc_r, wd_r)
        acc = _sig(g_ref[:, 0:D]) * ys[0]
        for b in range(1, 4):
            acc = acc + _sig(g_ref[:, b * D:(b + 1) * D]) * ys[b]
        m_ref[...] = acc.astype(BF16)

    return pl.pallas_call(
        body, name="merge_fwd", grid=(s // tm,), in_specs=_merge_specs(tm),
        out_specs=_rows(tm, D, 0), out_shape=jax.ShapeDtypeStruct((s, D), BF16),
        compiler_params=_params(("parallel",)),
    )(za, zb, o, zd, proj, wa, wb, wc, wd)


def _merge_bwd(dmerged, za, zb, o, zd, proj, wa, wb, wc, wd):
    s = proj.shape[0]
    tm = _tile(s, TM_WIDE)

    def body(dm_ref, za_r, zb_r, o_r, zd_r, g_ref, wa_r, wb_r, wc_r, wd_r,
             dya, dyb, dyc, dyd, dza, dzb, d_o, dzd, dg_ref, acc_ref):
        ys = _branch_ys(za_r, zb_r, o_r, zd_r, wa_r, wb_r, wc_r, wd_r)
        dm = dm_ref[...]
        sums = []
        for b, (dy_ref, dz_ref, w_r) in enumerate(((dya, dza, wa_r), (dyb, dzb, wb_r), (dyc, d_o, wc_r), (dyd, dzd, wd_r))):
            gt = _sig(g_ref[:, b * D:(b + 1) * D])
            dg = dm * ys[b] * gt * (1.0 - gt)
            dg_ref[:, b * D:(b + 1) * D] = dg.astype(BF16)
            sums.append(_colsum(dg))
            dy = (dm * gt).astype(BF16)
            dy_ref[...] = dy
            dz_ref[...] = lax.dot_general(dy, w_r[...], _NT, preferred_element_type=F32)
        _acc_rows(acc_ref, pl.program_id(0) == 0, [jnp.concatenate(sums, axis=1)])

    bf = lambda c: jax.ShapeDtypeStruct((s, c), BF16)
    f32 = lambda c: jax.ShapeDtypeStruct((s, c), F32)
    return pl.pallas_call(
        body, name="merge_bwd", grid=(s // tm,), in_specs=[_rows(tm, D, 0)] + _merge_specs(tm),
        out_specs=[_rows(tm, D, 0)] * 4 + [_rows(tm, C512, 0), _rows(tm, C512, 0), _rows(tm, D, 0), _rows(tm, C512, 0),
                                           _rows(tm, 4 * D, 0), _full((8, 4 * D))],
        out_shape=[bf(D)] * 4 + [f32(C512), f32(C512), f32(D), f32(C512), bf(4 * D), jax.ShapeDtypeStruct((8, 4 * D), F32)],
        compiler_params=_params(("arbitrary",)),
    )(dmerged, za, zb, o, zd, proj, wa, wb, wc, wd)


def _mm_res_ln(a, w, xres, prm, name):
    s, k = a.shape
    tm = _tile(s, TM_WIDE)

    def body(a_ref, w_ref, x_ref, p_ref, y_ref, xn_ref, hn_ref):
        y = lax.dot_general(a_ref[...], w_ref[...], _NN, preferred_element_type=F32)
        y_ref[...] = y
        xh, _ = _ln_stats(ALPHA * x_ref[...] + (1.0 + p_ref[0:1, :]) * y)
        xn = xh * p_ref[3:4, :] + p_ref[4:5, :]
        xn_ref[...] = xn
        hn_ref[...] = (xn * (1.0 + p_ref[1:2, :]) + p_ref[2:3, :]).astype(BF16)

    return pl.pallas_call(
        body, name=name, grid=(s // tm,),
        in_specs=[_rows(tm, k, 0), _full((k, D)), _rows(tm, D, 0), _full((8, D))],
        out_specs=[_rows(tm, D, 0)] * 3,
        out_shape=[jax.ShapeDtypeStruct((s, D), F32), jax.ShapeDtypeStruct((s, D), F32), jax.ShapeDtypeStruct((s, D), BF16)],
        compiler_params=_params(("parallel",)),
    )(a, w, xres, prm)


def _ln_res_bwd(dres_next, dh, xres, y, prm):
    s = xres.shape[0]
    tm = _tile(s, TM)

    def body(dn_ref, dh_ref, x_ref, y_ref, p_ref, dres_ref, dy_ref, acc_ref):
        gam, lng = p_ref[0:1, :], p_ref[3:4, :]
        yv = y_ref[...]
        xh, rstd = _ln_stats(ALPHA * x_ref[...] + (1.0 + gam) * yv)
        xn = xh * lng + p_ref[4:5, :]
        dh_v = dh_ref[...]
        dxn = dn_ref[...] + dh_v * (1.0 + p_ref[1:2, :])
        dr = _ln_bwd(dxn * lng, xh, rstd)
        dres_ref[...] = ALPHA * dr
        dy_ref[...] = ((1.0 + gam) * dr).astype(BF16)
        _acc_rows(acc_ref, pl.program_id(0) == 0,
                  [_colsum(dr * yv), _colsum(dh_v * xn), _colsum(dh_v), _colsum(dxn * xh), _colsum(dxn)])

    return pl.pallas_call(
        body, name="ln_res_bwd", grid=(s // tm,),
        in_specs=[_rows(tm, D, 0)] * 4 + [_full((8, D))],
        out_specs=[_rows(tm, D, 0), _rows(tm, D, 0), _full((8, D))],
        out_shape=[jax.ShapeDtypeStruct((s, D), F32), jax.ShapeDtypeStruct((s, D), BF16), jax.ShapeDtypeStruct((8, D), F32)],
        compiler_params=_params(("arbitrary",)),
    )(dres_next, dh, xres, y, prm)


FC = 16


def _shift_down(cur, prev, k, rowi):
    return jnp.where(rowi >= k, pltpu.roll(cur, k, 0), pltpu.roll(prev, k, 0))


def _shift_up(cur, nxt, k, rowi):
    return jnp.where(rowi < FC - k, pltpu.roll(cur, FC - k, 0), pltpu.roll(nxt, FC - k, 0))


def _conv3_chunk(w, cur, prev, rowi):
    return w[2] * cur + w[1] * _shift_down(cur, prev, 1, rowi) + w[0] * _shift_down(cur, prev, 2, rowi)


def _conv3_t_chunk(w, cur, nxt, rowi):
    return w[2] * cur + w[1] * _shift_up(cur, nxt, 1, rowi) + w[0] * _shift_up(cur, nxt, 2, rowi)


def _chunk_rows(j):
    return pl.ds(pl.multiple_of(j * FC, FC), FC)


def _ffn_chunk_specs(tm, s):
    r = tm // FC
    last = s // FC - 1
    out = []
    for half in (0, 1):
        out.append((pl.BlockSpec((FC, D_FF), lambda i, half=half: (jnp.maximum(i * r - 1, 0), half)),
                    pl.BlockSpec((tm, D_FF), lambda i, half=half: (i, half)),
                    pl.BlockSpec((FC, D_FF), lambda i, half=half: (jnp.minimum((i + 1) * r, last), half)),
                    pl.BlockSpec((8, D_FF), lambda i, half=half: (0, half))))
    return out


def _ffn_fwd(up, w8):
    s = up.shape[0]
    tm = _tile(s, TM_WIDE)
    (pv_s, mv_s, _, wv_s), (pg_s, mg_s, _, wg_s) = _ffn_chunk_specs(tm, s)

    def body(pv, mv, pg, mg, wv_ref, wg_ref, a_ref):
        first = pl.program_id(0) == 0
        rowi = lax.broadcasted_iota(jnp.int32, (FC, LANE), 0)
        zero = jnp.zeros((FC, LANE), F32)
        for cg in range(D_FF // LANE):
            cs = slice(cg * LANE, (cg + 1) * LANE)
            wv = [wv_ref[k:k + 1, cs] for k in range(SC_W)]
            wg = [wg_ref[k:k + 1, cs] for k in range(SC_W)]

            def step(j, carry, cs=cs, wv=wv, wg=wg):
                rows = _chunk_rows(j)
                xv, xg = mv[rows, cs], mg[rows, cs]
                gate = _conv3_chunk(wg, xg, carry[1], rowi)
                a_ref[rows, cs] = (gate * _sig(gate) * _conv3_chunk(wv, xv, carry[0], rowi)).astype(BF16)
                return xv, xg

            lax.fori_loop(0, tm // FC, step, (jnp.where(first, zero, pv[:, cs]), jnp.where(first, zero, pg[:, cs])))

    return pl.pallas_call(
        body, name="ffn_fwd", grid=(s // tm,), in_specs=[pv_s, mv_s, pg_s, mg_s, wv_s, wg_s],
        out_specs=_rows(tm, D_FF, 0), out_shape=jax.ShapeDtypeStruct((s, D_FF), BF16),
        compiler_params=_params(("parallel",)),
    )(up, up, up, up, w8, w8)


def _ffn_bwd(da, up, w8):
    s = up.shape[0]
    tm = _tile(s, TM_WIDE)
    n, nt = tm // FC, s // tm
    (pv_s, mv_s, nv_s, wv_s), (pg_s, mg_s, ng_s, wg_s) = _ffn_chunk_specs(tm, s)

    def body(dam, dan, pv, mv, nv, pg, mg, ng, wv_ref, wg_ref, dup_ref, dw_ref):
        i = pl.program_id(0)
        first, last = i == 0, i == nt - 1
        rowi = lax.broadcasted_iota(jnp.int32, (FC, LANE), 0)
        zero = jnp.zeros((FC, LANE), F32)

        @pl.when(first)
        def _():
            dw_ref[...] = jnp.zeros_like(dw_ref)

        for cg in range(D_FF // LANE):
            cs = slice(cg * LANE, (cg + 1) * LANE)
            cs_g = slice(D_FF + cg * LANE, D_FF + (cg + 1) * LANE)
            wv = [wv_ref[k:k + 1, cs] for k in range(SC_W)]
            wg = [wg_ref[k:k + 1, cs] for k in range(SC_W)]

            def conv_grads(xv, xg, xpv, xpg, dav, wv=wv, wg=wg):
                val, gate = _conv3_chunk(wv, xv, xpv, rowi), _conv3_chunk(wg, xg, xpg, rowi)
                sg = _sig(gate)
                return dav * gate * sg, dav * val * (sg * (1.0 + gate * (1.0 - sg)))

            def step(j, c, cs=cs, cs_g=cs_g, wv=wv, wg=wg, conv_grads=conv_grads):
                xpv, xpg, dvp, dgp = c[:4]
                rows = _chunk_rows(j)
                xv, xg = mv[rows, cs], mg[rows, cs]
                dv, dg = conv_grads(xv, xg, xpv, xpg, dam[rows, cs])
                prow = _chunk_rows(jnp.maximum(j - 1, 0))
                dup_ref[prow, cs] = _conv3_t_chunk(wv, dvp, dv, rowi).astype(BF16)
                dup_ref[prow, cs_g] = _conv3_t_chunk(wg, dgp, dg, rowi).astype(BF16)
                accs = (c[4] + dv * _shift_down(xv, xpv, 2, rowi), c[5] + dv * _shift_down(xv, xpv, 1, rowi), c[6] + dv * xv,
                        c[7] + dg * _shift_down(xg, xpg, 2, rowi), c[8] + dg * _shift_down(xg, xpg, 1, rowi), c[9] + dg * xg)
                return (xv, xg, dv, dg) + accs

            init = (jnp.where(first, zero, pv[:, cs]), jnp.where(first, zero, pg[:, cs]), zero, zero) + (zero,) * 6
            c = lax.fori_loop(0, n, step, init)
            dv_n, dg_n = conv_grads(nv[:, cs], ng[:, cs], c[0], c[1], jnp.where(last, zero, dan[:, cs]))
            dup_ref[tm - FC:tm, cs] = _conv3_t_chunk(wv, c[2], dv_n, rowi).astype(BF16)
            dup_ref[tm - FC:tm, cs_g] = _conv3_t_chunk(wg, c[3], dg_n, rowi).astype(BF16)
            for k in range(SC_W):
                dw_ref[k:k + 1, cs] += _colsum(c[4 + k])
                dw_ref[k:k + 1, cs_g] += _colsum(c[7 + k])

    r = tm // FC
    da_next = pl.BlockSpec((FC, D_FF), lambda i: (jnp.minimum((i + 1) * r, s // FC - 1), 0))
    return pl.pallas_call(
        body, name="ffn_bwd", grid=(nt,),
        in_specs=[_rows(tm, D_FF, 0), da_next, pv_s, mv_s, nv_s, pg_s, mg_s, ng_s, wv_s, wg_s],
        out_specs=[_rows(tm, 2 * D_FF, 0), _full((8, 2 * D_FF))],
        out_shape=[jax.ShapeDtypeStruct((s, 2 * D_FF), BF16), jax.ShapeDtypeStruct((8, 2 * D_FF), F32)],
        compiler_params=_params(("arbitrary",)),
    )(da, da, up, up, up, up, up, up, w8, w8)


def _loss_head(y, target):
    s = y.shape[0]
    tm = _tile(s, TM)

    def body(y_ref, t_ref, dy_ref, l_ref):
        err = y_ref[...] - t_ref[...]
        dy_ref[...] = err * (1.0 / D)
        part = 0.5 * jnp.sum(jnp.mean(err * err, axis=-1, keepdims=True), axis=0, keepdims=True)

        @pl.when(pl.program_id(0) == 0)
        def _():
            l_ref[...] = jnp.zeros_like(l_ref)
        l_ref[...] += part

    return pl.pallas_call(
        body, name="loss_head", grid=(s // tm,),
        in_specs=[_rows(tm, D, 0)] * 2, out_specs=[_rows(tm, D, 0), _full((8, LANE))],
        out_shape=[jax.ShapeDtypeStruct((s, D), F32), jax.ShapeDtypeStruct((8, LANE), F32)],
        compiler_params=_params(("arbitrary",)),
    )(y, target)


def _silu_rows(c_all):
    def body(c_ref, o_ref):
        cv = c_ref[...]
        o_ref[...] = jnp.concatenate([cv * _sig(cv), jnp.zeros((N_DEV, D), F32)], axis=0).astype(BF16)

    return pl.pallas_call(
        body, name="silu_rows", grid=(1,), in_specs=[_full((N_DEV, D))], out_specs=_full((2 * N_DEV, D)),
        out_shape=jax.ShapeDtypeStruct((2 * N_DEV, D), BF16), compiler_params=_params(("arbitrary",)),
    )(c_all)


GRAD_ROWS = 512


def _sum_parts(p_ref, n):
    g = p_ref[0].astype(F32)
    for j in range(1, n):
        g = g + p_ref[j].astype(F32)
    return g


def _adamw(parts, w, m, v, name):
    n = parts.shape[0]
    r, c = w.shape
    tr = GRAD_ROWS
    assert r % tr == 0 and parts.shape[2] == c, (parts.shape, w.shape)

    def body(p_ref, w_ref, m_ref, v_ref, g_out, d_out, m_out, v_out):
        g = _sum_parts(p_ref, n)
        mn = ADAM_B1 * m_ref[...] + (1.0 - ADAM_B1) * g
        vn = ADAM_B2 * v_ref[...] + (1.0 - ADAM_B2) * (g * g)
        m_hat = mn / (1.0 - ADAM_B1 ** ADAM_STEP)
        v_hat = vn / (1.0 - ADAM_B2 ** ADAM_STEP)
        g_out[...] = g
        d_out[...] = -ADAM_LR * (m_hat / (jnp.sqrt(v_hat) + ADAM_EPS) + ADAM_WD * w_ref[...])
        m_out[...] = mn
        v_out[...] = vn

    blk = pl.BlockSpec((tr, c), lambda i: (i, 0))
    return pl.pallas_call(
        body, name=name, grid=(r // tr,),
        in_specs=[pl.BlockSpec((n, tr, c), lambda i: (0, i, 0)), blk, blk, blk], out_specs=[blk] * 4,
        out_shape=[jax.ShapeDtypeStruct((r, c), F32)] * 4, compiler_params=_params(("parallel",)),
    )(parts, w, m, v)


def _sum8(parts, name):
    n, r, c = parts.shape
    tr = GRAD_ROWS
    assert r % tr == 0, r

    def body(p_ref, g_out):
        g_out[...] = _sum_parts(p_ref, n)

    return pl.pallas_call(
        body, name=name, grid=(r // tr,),
        in_specs=[pl.BlockSpec((n, tr, c), lambda i: (0, i, 0))], out_specs=pl.BlockSpec((tr, c), lambda i: (i, 0)),
        out_shape=jax.ShapeDtypeStruct((r, c), F32), compiler_params=_params(("parallel",)),
    )(parts)


def _peers():
    ix, iy, ic = lax.axis_index("x"), lax.axis_index("y"), lax.axis_index("c")
    me = 4 * ix + 2 * iy + ic
    out = []
    for k in range(1, N_DEV):
        px = 1 - ix if (k >> 2) & 1 else ix
        py = 1 - iy if (k >> 1) & 1 else iy
        pc = 1 - ic if k & 1 else ic
        out.append(((px, py, pc), 4 * px + 2 * py + pc))
    return me, out


_HBM = pl.BlockSpec(memory_space=pltpu.HBM)


def _exchange(x, name, gather):
    shape = ((N_DEV,) + x.shape) if gather else x.shape

    def body(x_ref, o_ref, send_sems, recv_sems, local_sem):
        me, peers = _peers()
        src_of = (lambda p: x_ref) if gather else (lambda p: x_ref.at[p])
        local = pltpu.make_async_copy(src_of(me), o_ref.at[me], local_sem)
        local.start()
        sends = []
        for k, (dev, p) in enumerate(peers):
            cp = pltpu.make_async_remote_copy(src_ref=src_of(p), dst_ref=o_ref.at[me], send_sem=send_sems.at[k],
                                              recv_sem=recv_sems.at[k], device_id=dev, device_id_type=MESH)
            cp.start()
            sends.append(cp)
        for k, (dev, p) in enumerate(peers):
            pltpu.make_async_remote_copy(src_ref=src_of(p), dst_ref=o_ref.at[p], send_sem=send_sems.at[k],
                                         recv_sem=recv_sems.at[k], device_id=dev, device_id_type=MESH).wait_recv()
        for cp in sends:
            cp.wait_send()
        local.wait()

    return pl.pallas_call(
        body, name=name, in_specs=[_HBM], out_specs=_HBM, out_shape=jax.ShapeDtypeStruct(shape, x.dtype),
        scratch_shapes=[pltpu.SemaphoreType.DMA((N_DEV - 1,)), pltpu.SemaphoreType.DMA((N_DEV - 1,)),
                        pltpu.SemaphoreType.DMA],
    )(x)


def _pack_rows(arrs, row_mult):
    flat = jnp.concatenate([a.reshape(-1) for a in arrs])
    n = flat.shape[0]
    pad = (-n) % (LANE * row_mult)
    if pad:
        flat = jnp.concatenate([flat, jnp.zeros((pad,), flat.dtype)])
    return flat.reshape(-1, LANE)


def _pack_cols(arrs, row_mult):
    flat = jnp.concatenate(arrs, axis=1)
    n = flat.shape[1]
    pad = (-n) % (LANE * row_mult)
    if pad:
        flat = jnp.concatenate([flat, jnp.zeros((flat.shape[0], pad), flat.dtype)], axis=1)
    return flat.reshape(flat.shape[0], -1, LANE)


def _unpack(flat, shapes):
    out, off = [], 0
    lead = flat.shape[:-1]
    for shp in shapes:
        n = 1
        for d_ in shp:
            n *= d_
        out.append(flat[..., off:off + n].reshape(lead + tuple(shp)))
        off += n
    return out


BIG = (("w_conv_out", 2), ("w_sc_out", 2), ("w_uq", 2), ("w_ukv", 2), ("w_mla_out", 2),
       ("w_pool_out", 2), ("w_o", 1), ("w_down", 1))
TAPS = (("conv_dw", 2), ("sc_dw", 2), ("ffn_dw", 2))
SMALL = ("b_ada", "b_in", "conv_ln_g", "conv_ln_b", "q_norm_g", "kv_norm_g", "w_pool", "pool_scale",
         "ln1_g", "ln1_b", "ln2_g", "ln2_b")
WEIGHTS = ("w_ada", "b_ada", "w_in", "b_in", "conv_dw", "conv_ln_g", "conv_ln_b", "w_conv_out", "sc_dw", "w_sc_out",
           "q_norm_g", "w_uq", "kv_norm_g", "w_ukv", "w_mla_out", "w_pool", "pool_scale", "w_pool_out", "w_o",
           "ln1_g", "ln1_b", "w_up", "ffn_dw", "w_down", "ln2_g", "ln2_b")


def _join(g, axis):
    g = jnp.moveaxis(g, 0, axis)
    shp = list(g.shape)
    shp[axis:axis + 2] = [shp[axis] * shp[axis + 1]]
    return g.reshape(shp)


def _split(full, axis):
    shp = list(full.shape)
    shp[axis:axis + 1] = [N_DEV, shp[axis] // N_DEV]
    return jnp.moveaxis(full.reshape(shp), axis, 0)


def _pad_rows(a, rows):
    return jnp.concatenate([a, jnp.zeros((rows - a.shape[0],) + a.shape[1:], a.dtype)], axis=0)


def _proj_cols(w):
    z = lambda n: jnp.zeros(w.shape[:-1] + (n,), w.dtype)
    return jnp.concatenate([w[..., 3488:7584], w[..., 0:2944], z(NOPE), w[..., 2944:2976], z(HEAD_PAD - QK_DIM),
                            w[..., 2976:3488]], axis=-1)


def _proj_cols_inv(w):
    return jnp.concatenate([w[..., CA0:KR0], w[..., KR0 + NOPE:KR0 + QK_DIM], w[..., PU0:NPROJ], w[..., 0:CA0]], axis=-1)


IN_COLS = 7584
IN_SHARD = IN_COLS // N_DEV
IN_SHARD_PAD = 960
UP_SHARD = 2 * D_FF // N_DEV
_IN_CUTS = (2944, 2976, 3488)
_IN_SEGS = ((3488, IN_COLS), (0, 2944), NOPE, (2944, 2976), HEAD_PAD - QK_DIM, (2976, 3488))


def _proj_rows(g):
    pieces = []
    for seg in _IN_SEGS:
        if isinstance(seg, int):
            pieces.append(jnp.zeros((seg, g.shape[1]), g.dtype))
            continue
        j = seg[0]
        while j < seg[1]:
            dev = j // IN_SHARD
            e = min(seg[1], (dev + 1) * IN_SHARD)
            r0 = dev * IN_SHARD_PAD + j - dev * IN_SHARD
            pieces.append(g[r0:r0 + e - j])
            j = e
    return jnp.concatenate(pieces, axis=0)


def _aligned_row(j):
    if j < 2944:
        return CA0 + j
    if j < 2976:
        return KR0 + NOPE + j - 2944
    if j < 3488:
        return PU0 + j - 2976
    return j - 3488


def _proj_rows_split(gt):
    out = []
    for dev in range(N_DEV):
        j0, j1 = dev * IN_SHARD, (dev + 1) * IN_SHARD
        cuts = [j0] + [c for c in _IN_CUTS if j0 < c < j1] + [j1]
        out.append(jnp.concatenate([gt[_aligned_row(a):_aligned_row(a) + b - a] for a, b in zip(cuts[:-1], cuts[1:])], axis=0))
    return jnp.stack(out)


def _layer_weights(full, small, l):
    w = {}
    w["w_inT"] = _proj_rows(full["w_inT"][l])
    w["b_in"] = _proj_cols(small["b_in"][l][None, :])
    w["conv_w"] = _pad_rows(full["conv_dw"][l], 32)
    w["sc_w"] = _pad_rows(full["sc_dw"][l], 8)
    w["ffn_w"] = _pad_rows(full["ffn_dw"][l], 8)
    uq = full["w_uq"][l].reshape(256, HEADS, QK_DIM)
    w["w_uq"] = jnp.concatenate([uq, jnp.zeros((256, HEADS, HEAD_PAD - QK_DIM), uq.dtype)], axis=-1).reshape(256, -1)
    ukv = full["w_ukv"][l].reshape(128, HEADS, 2, NOPE)
    w["w_ukv"] = jnp.concatenate([ukv, jnp.zeros_like(ukv)], axis=-1).reshape(128, -1)
    mo = full["w_mla_out"][l].reshape(HEADS, NOPE, D)
    w["w_c"] = jnp.concatenate([mo, jnp.zeros_like(mo)], axis=1).reshape(HEADS * HEAD_PAD, D)
    w["w_a"], w["w_b"], w["w_d"] = full["w_conv_out"][l], full["w_sc_out"][l], full["w_pool_out"][l]
    w["w_o"], w["w_upT"], w["w_down"] = full["w_o"][l], full["w_upT"][l], full["w_down"][l]
    w["w_pool"] = small["w_pool"][l].astype(BF16)
    for n in ("conv_ln_g", "conv_ln_b", "q_norm_g", "kv_norm_g", "pool_scale"):
        w[n] = small[n][l][None, :]
    return w


def _prm(rows):
    z = jnp.zeros((D,), F32)
    rows = list(rows) + [z] * (8 - len(rows))
    return jnp.stack(rows)


def _layer_fwd(x, h1, w, prm1, prm2, tabs):
    ct, s1t, s2t = tabs
    proj = _mm(h1, w["w_inT"], "nt", "proj_fwd", bias=w["b_in"], tm=1024, tn=768)
    yconv, za = _conv_a_fwd(proj, w["conv_w"], w["conv_ln_g"], w["conv_ln_b"])
    zb = _sc_fwd(proj, w["sc_w"])
    qn, kvn = _lat_fwd(proj, w["q_norm_g"], w["kv_norm_g"])
    qp = _mm(qn, w["w_uq"], "nn", "uq_fwd")
    kvp = _mm(kvn, w["w_ukv"], "nn", "ukv_fwd")
    q, k, v = _rope_fwd(qp, kvp, proj, ct, s1t, s2t)
    o, lse = _attn_fwd(q, k, v)
    zd = _pool_fwd(proj, w["w_pool"], w["pool_scale"])
    merged = _merge_fwd(za, zb, o, zd, proj, w["w_a"], w["w_b"], w["w_c"], w["w_d"])
    mix, x1, h2 = _mm_res_ln(merged, w["w_o"], x, prm1, "o_res_ln")
    up = _mm(h2, w["w_upT"], "nt", "up_fwd", tm=1024, tn=1408)
    a = _ffn_fwd(up, w["ffn_w"])
    ffn, x2, h_next = _mm_res_ln(a, w["w_down"], x1, prm2, "down_res_ln")
    res = dict(x=x, h1=h1, proj=proj, yconv=yconv, za=za, zb=zb, qn=qn, kvn=kvn, q=q, k=k, v=v, o=o, lse=lse, zd=zd,
               merged=merged, mix=mix, x1=x1, h2=h2, up=up, a=a, ffn=ffn)
    return x2, h_next, res


def _layer_bwd(dres_next, dh_next, r, w, prm1, prm2, tabs):
    ct, s1t, s2t = tabs
    s = r["x"].shape[0]
    g = {}
    dres2, dffn, acc2 = _ln_res_bwd(dres_next, dh_next, r["x1"], r["ffn"], prm2)
    da = _mm(dffn, w["w_down"], "nt", "down_bwd_x", tm=1024, tn=1408)
    g["w_down"] = _mm(r["a"], dffn, "tn", "down_bwd_w", tm=1408)
    dup, dffn_w = _ffn_bwd(da, r["up"], w["ffn_w"])
    g["ffn_dw"] = dffn_w[:SC_W]
    dh2 = _mm(dup, w["w_upT"], "nn", "up_bwd_x", tm=1024, tk=1408)
    g["w_upT"] = _mm(dup, r["h2"], "tn", "up_bwd_w", tm=1408)
    dres1, dmix, acc1 = _ln_res_bwd(dres2, dh2, r["x"], r["mix"], prm1)
    dmerged = _mm(dmix, w["w_o"], "nt", "o_bwd_x")
    g["w_o"] = _mm(r["merged"], dmix, "tn", "o_bwd_w")
    (dya, dyb, dyc, dyd, dza, dzb, d_o, dzd, dgates, accg) = _merge_bwd(
        dmerged, r["za"], r["zb"], r["o"], r["zd"], r["proj"], w["w_a"], w["w_b"], w["w_c"], w["w_d"])
    g["w_conv_out"] = _mm(r["za"], dya, "tn", "branch_bwd_w")
    g["w_sc_out"] = _mm(r["zb"], dyb, "tn", "branch_bwd_w")
    g["w_pool_out"] = _mm(r["zd"], dyd, "tn", "branch_bwd_w")
    gwc = _mm(r["o"], dyc, "tn", "mla_out_bwd_w")
    g["w_mla_out"] = gwc.reshape(HEADS, HEAD_PAD, D)[:, :NOPE].reshape(HEADS * NOPE, D)
    dyconv, dconv_w, acca = _conv_a_bwd1(dza, r["yconv"], r["proj"], w["conv_ln_g"], w["conv_ln_b"])
    g["conv_dw"], g["conv_ln_g"], g["conv_ln_b"] = dconv_w[:CONV_W], acca[0], acca[1]
    d_ca, d_cb, acca2 = _conv_a_bwd2(dyconv, r["proj"], w["conv_w"])
    dconv, d_bg, dsc_w, accb1 = _sc_bwd1(dzb, r["proj"], w["sc_w"])
    g["sc_dw"] = dsc_w[:SC_W]
    d_cg, d_sx, accb2 = _sc_bwd2(dconv, r["proj"], w["sc_w"])
    delta, dob = _attn_prep(d_o, r["o"])
    dq, dkv = _attn_bwd(r["q"], r["k"], r["v"], dob, r["lse"], delta)
    dqp, d_kr, acckr = _rope_bwd(dq, dkv, ct, s1t, s2t)
    dqn = _mm(dqp, w["w_uq"], "nt", "uq_bwd_x")
    guq = _mm(r["qn"], dqp, "tn", "uq_bwd_w")
    g["w_uq"] = guq.reshape(256, HEADS, HEAD_PAD)[:, :, :QK_DIM].reshape(256, HEADS * QK_DIM)
    dkvn = _mm(dkv, w["w_ukv"], "nt", "ukv_bwd_x")
    gukv = _mm(r["kvn"], dkv, "tn", "ukv_bwd_w")
    g["w_ukv"] = gukv.reshape(128, HEADS, 2, HEAD_PAD)[..., :NOPE].reshape(128, HEADS * 2 * NOPE)
    d_ql, d_kvl, accq, acckv = _lat_bwd(dqn, dkvn, r["proj"], w["q_norm_g"], w["kv_norm_g"])
    g["q_norm_g"], g["kv_norm_g"] = accq[0], acckv[0]
    dpd, g["w_pool"], accd1 = _pool_bwd1(dzd, r["proj"], w["w_pool"], w["pool_scale"])
    g["pool_scale"] = accd1[0]
    d_pu, accd2 = _pool_bwd2(dpd, s)
    dproj = jnp.concatenate([dgates, d_ca, d_cb, d_bg, d_cg, d_sx, d_ql, d_kvl, d_kr, d_pu], axis=1)
    db = jnp.concatenate([accg[0], acca2[0], acca2[1], accb1[0], accb2[0], accb2[1], accq[1], acckv[1], acckr[0], accd2[0]])
    g["b_in"] = _proj_cols_inv(db)
    dh1 = _mm(dproj, w["w_inT"], "nn", "proj_bwd_x", tm=1024, tk=1536)
    g["w_inT"] = _proj_rows_split(_mm(dproj, r["h1"], "tn", "proj_bwd_w", tm=768))
    g["ln1_g"], g["ln1_b"], g["ln2_g"], g["ln2_b"] = acc1[3], acc1[4], acc2[3], acc2[4]
    return dres1, dh1, g, (acc1, acc2)


def _rope_tables(positions):
    half = ROPE // 2
    inv = 1.0 / (ROPE_THETA ** (jnp.arange(0, ROPE, 2, dtype=F32) / ROPE))
    ang = positions.astype(F32)[:, None] * inv
    cos, sin = jnp.cos(ang), jnp.sin(ang)
    s = positions.shape[0]
    z = lambda n: jnp.zeros((s, n), F32)
    ct = jnp.concatenate([jnp.ones((s, NOPE), F32), cos, cos, z(HEAD_PAD - QK_DIM)], axis=1)
    s1t = jnp.concatenate([z(NOPE), -sin, z(half), z(HEAD_PAD - QK_DIM)], axis=1)
    s2t = jnp.concatenate([z(NOPE), z(half), sin, z(HEAD_PAD - QK_DIM)], axis=1)
    return ct, s1t, s2t


def _local_step(x, mod, positions, full, small, target):
    tabs = _rope_tables(positions)
    ws = [_layer_weights(full, small, l) for l in range(DEPTH)]
    zero = jnp.zeros((D,), F32)
    prm1s, prm2s = [], []
    for l in range(DEPTH):
        sh1, sc1, g1, sh2, sc2, g2 = (mod[l, j] for j in range(6))
        nxt = (mod[l + 1, 1], mod[l + 1, 0]) if l + 1 < DEPTH else (zero, zero)
        prm1s.append(_prm([g1, sc2, sh2, small["ln1_g"][l], small["ln1_b"][l]]))
        prm2s.append(_prm([g2, nxt[0], nxt[1], small["ln2_g"][l], small["ln2_b"][l]]))
    prm0 = _prm([zero, mod[0, 1], mod[0, 0]])
    h = _mod_fwd(x, prm0)
    res = []
    xc = x
    for l in range(DEPTH):
        xc, h, r = _layer_fwd(xc, h, ws[l], prm1s[l], prm2s[l], tabs)
        res.append(r)
    dy, lacc = _loss_head(xc, target)
    loss = lacc[0, 0]
    dres, dh = dy, jnp.zeros_like(dy)
    grads = [None] * DEPTH
    accs = [None] * DEPTH
    for l in reversed(range(DEPTH)):
        dres, dh, grads[l], accs[l] = _layer_bwd(dres, dh, res[l], ws[l], prm1s[l], prm2s[l], tabs)
    dx, acc0 = _mod_bwd(dres, dh, x, prm0)
    dmod = []
    for l in range(DEPTH):
        acc1, acc2 = accs[l]
        dsc1, dsh1 = (acc0[0], acc0[1]) if l == 0 else (accs[l - 1][1][1], accs[l - 1][1][2])
        dmod.append(jnp.stack([dsh1, dsc1, acc1[0], acc1[2], acc1[1], acc2[0]]))
    return loss, dx, grads, jnp.stack(dmod)


ADA_SHARD = 6 * D // N_DEV


def _step(p):
    me = 4 * lax.axis_index("x") + 2 * lax.axis_index("y") + lax.axis_index("c")
    x, target, positions = p["x"][0], p["loss_target"][0], p["positions"][0]

    tap_shapes = [p[n].shape for n, _ in TAPS] + [(D,)]
    small_g = _exchange(_pack_rows([p[n] for n, _ in TAPS] + [p["c"][0]], 8), "gather_taps", True)
    parts = _unpack(small_g.reshape(N_DEV, -1), tap_shapes)
    full = {n: _join(g, ax) for (n, ax), g in zip(TAPS, parts[:-1])}
    c_all = parts[-1]
    w_in_t = jnp.swapaxes(p["w_in"], 1, 2).astype(BF16)
    w_in_t = jnp.concatenate([w_in_t, jnp.zeros((DEPTH, IN_SHARD_PAD - IN_SHARD, D), BF16)], axis=1)
    w_up_t = jnp.swapaxes(p["w_up"], 1, 2).astype(BF16)
    big_shapes = [p[n].shape for n, _ in BIG] + [w_in_t.shape, w_up_t.shape]
    big_g = _exchange(_pack_rows([p[n].astype(BF16) for n, _ in BIG] + [w_in_t, w_up_t], 16), "gather_weights", True)
    big_parts = _unpack(big_g.reshape(N_DEV, -1), big_shapes)
    for (n, ax), g in zip(BIG, big_parts):
        full[n] = _join(g, ax)
    full["w_inT"] = _join(big_parts[-2], 1)
    full["w_upT"] = _join(big_parts[-1], 1)
    small = {n: p[n] for n in SMALL}

    c_act = _silu_rows(c_all)
    w_ada_cols = jnp.moveaxis(p["w_ada"], 0, 1).reshape(D, DEPTH * ADA_SHARD)
    b_shard = lax.dynamic_slice_in_dim(p["b_ada"], me * ADA_SHARD, ADA_SHARD, axis=1).reshape(1, DEPTH * ADA_SHARD)
    mod_sh = _mm(c_act, w_ada_cols, "nn", "ada_fwd", bias=b_shard)[:N_DEV]
    mod_x = _exchange(_pack_cols([mod_sh], 8), "scatter_mod", False)
    mod = mod_x.reshape(N_DEV, -1)[:, :DEPTH * ADA_SHARD].reshape(N_DEV, DEPTH, ADA_SHARD)
    mod = jnp.moveaxis(mod, 0, 1).reshape(DEPTH, 6, D)

    loss_local, dx, grads, dmod = _local_step(x, mod, positions, full, small, target)
    loss = lax.psum(loss_local, ("x", "y", "c"))
    gfull = {n: jnp.stack([grads[l][n] for l in range(DEPTH)]) for n in grads[0]}

    out = {"loss": loss, "grad_x": dx[None]}

    def emit(names, g, dlt, mn, vn, shapes):
        for n, gi, di, mi, vi in zip(names, _unpack(g, shapes), _unpack(dlt, shapes), _unpack(mn, shapes), _unpack(vn, shapes)):
            out["grad_" + n], out["delta_" + n], out["new_m_" + n], out["new_v_" + n] = gi, di, mi, vi

    small_parts = [dmod.reshape(DEPTH, 6 * D)] + [gfull[n] for n in SMALL[1:]]
    small_all = _exchange(_pack_rows(small_parts, GRAD_ROWS), "gather_small_grads", True)
    small_shapes = [p[n].shape for n in SMALL]
    sg, sd, sm, sv = _adamw(small_all, *[_pack_rows([p[pre + n] for n in SMALL], GRAD_ROWS) for pre in ("", "m_", "v_")],
                            name="adamw_small")
    emit(SMALL, *[t.reshape(-1) for t in (sg, sd, sm, sv)], small_shapes)

    dmod_all = small_all.reshape(N_DEV, -1)[:, :DEPTH * 6 * D].reshape(N_DEV, DEPTH, 6 * D)
    dmod_sh = lax.dynamic_slice_in_dim(dmod_all, me * ADA_SHARD, ADA_SHARD, axis=2).reshape(N_DEV, DEPTH * ADA_SHARD)
    g_ada = _mm(c_act, _pad_rows(dmod_sh, 2 * N_DEV), "tn", "ada_bwd_w")
    g_ada = jnp.moveaxis(g_ada.reshape(D, DEPTH, ADA_SHARD), 1, 0)
    ag, ad, am, av = _adamw(_pack_rows([g_ada], GRAD_ROWS)[None], *[_pack_rows([p[pre + "w_ada"]], GRAD_ROWS) for pre in ("", "m_", "v_")],
                            name="adamw_ada")
    emit(("w_ada",), *[t.reshape(-1) for t in (ag, ad, am, av)], [p["w_ada"].shape])

    shard_names = [n for n, _ in BIG + TAPS]
    pieces = [_split(gfull[n], ax).reshape(N_DEV, -1) for n, ax in BIG + TAPS]
    recv = _exchange(_pack_cols(pieces, GRAD_ROWS).astype(BF16), "scatter_grads", False)
    bg, bd, bm, bv = _adamw(recv, *[_pack_rows([p[pre + n] for n in shard_names], GRAD_ROWS) for pre in ("", "m_", "v_")],
                            name="adamw_sharded")
    emit(shard_names, *[t.reshape(-1) for t in (bg, bd, bm, bv)], [p[n].shape for n in shard_names])

    t_pieces = [jnp.moveaxis(gfull["w_inT"], 1, 0).reshape(N_DEV, -1),
                jnp.moveaxis(gfull["w_upT"].reshape(DEPTH, N_DEV, UP_SHARD, D), 1, 0).reshape(N_DEV, -1)]
    g_t = _sum8(_exchange(_pack_cols(t_pieces, GRAD_ROWS).astype(BF16), "scatter_grads_t", False), "sum_grads_t")
    g_in_t, g_up_t = _unpack(g_t.reshape(-1), [(DEPTH, IN_SHARD, D), (DEPTH, UP_SHARD, D)])
    for n, gt in (("w_in", g_in_t), ("w_up", g_up_t)):
        shp = p[n].shape
        two_d = (shp[0] * shp[1], shp[2])
        res = _adamw(jnp.swapaxes(gt, 1, 2).reshape((1,) + two_d), *[p[pre + n].reshape(two_d) for pre in ("", "m_", "v_")],
                     name="adamw_" + n)
        for key, t in zip(("grad_", "delta_", "new_m_", "new_v_"), res):
            out[key + n] = t.reshape(shp)
    return out


_ARG_NAMES = ("x", "c", "positions") + WEIGHTS + ("loss_target",) + tuple("m_" + n for n in WEIGHTS) + tuple("v_" + n for n in WEIGHTS)
_OUT_NAMES = ("loss", "grad_x") + tuple(pre + n for pre in ("grad_", "delta_", "new_m_", "new_v_") for n in WEIGHTS)


def kernel(x, c, positions, w_ada, b_ada, w_in, b_in, conv_dw, conv_ln_g, conv_ln_b, w_conv_out, sc_dw, w_sc_out, q_norm_g, w_uq, kv_norm_g, w_ukv, w_mla_out, w_pool, pool_scale, w_pool_out, w_o, ln1_g, ln1_b, w_up, ffn_dw, w_down, ln2_g, ln2_b, loss_target, m_w_ada, m_b_ada, m_w_in, m_b_in, m_conv_dw, m_conv_ln_g, m_conv_ln_b, m_w_conv_out, m_sc_dw, m_w_sc_out, m_q_norm_g, m_w_uq, m_kv_norm_g, m_w_ukv, m_w_mla_out, m_w_pool, m_pool_scale, m_w_pool_out, m_w_o, m_ln1_g, m_ln1_b, m_w_up, m_ffn_dw, m_w_down, m_ln2_g, m_ln2_b, v_w_ada, v_b_ada, v_w_in, v_b_in, v_conv_dw, v_conv_ln_g, v_conv_ln_b, v_w_conv_out, v_sc_dw, v_w_sc_out, v_q_norm_g, v_w_uq, v_kv_norm_g, v_w_ukv, v_w_mla_out, v_w_pool, v_pool_scale, v_w_pool_out, v_w_o, v_ln1_g, v_ln1_b, v_w_up, v_ffn_dw, v_w_down, v_ln2_g, v_ln2_b):
    args = locals()
    out = _step({n: args[n] for n in _ARG_NAMES})
    return tuple(out[n] for n in _OUT_NAMES)
```

```python
import functools

import jax
import jax.numpy as jnp
from jax import lax
from jax.experimental import pallas as pl
from jax.experimental.pallas import tpu as pltpu

F32 = jnp.float32
BF16 = jnp.bfloat16

N_DEV = 8
DEPTH = 4
D = 1024
CONV_W = 31
HEADS = 8
HEAD_PAD = 128
QK_DIM = 96
NOPE = 64
ROPE = 32
ROPE_THETA = 10000.0
D_FF = 2816
LN_EPS = 1e-5
RMS_EPS = 1e-6
ALPHA = (2.0 * DEPTH) ** 0.25
ATT_SCALE = QK_DIM ** -0.5
POOL_WINDOWS = (2, 4, 8, 16)

ADAM_LR = 0.001
ADAM_B1 = 0.9
ADAM_B2 = 0.999
ADAM_EPS = 1e-08
ADAM_WD = 0.01
ADAM_STEP = 10

GATES0 = 0
CA0 = 4096
CB0 = 4608
SBG0 = 5120
SCG0 = 5632
SX0 = 6144
QL0 = 6656
KVL0 = 6912
KR0 = 7040
PU0 = 7168
NPROJ = 7680

LANE = 128
TM = 512
TM_WIDE = 256
TQ = 1024
TQ_BWD = 512
VMEM_LIMIT = 56 * 1024 * 1024

MESH = pl.DeviceIdType.MESH


def _sig(x):
    return 1.0 / (1.0 + jnp.exp(-x))


def _tile(n, pref):
    if n <= pref:
        return n
    t = (pref // LANE) * LANE
    while t >= LANE:
        if n % t == 0:
            return t
        t -= LANE
    raise ValueError(f"no lane-aligned tile for {n}")


def _params(sem):
    return pltpu.CompilerParams(dimension_semantics=sem, vmem_limit_bytes=VMEM_LIMIT)


def _full(shape):
    nd = len(shape)
    return pl.BlockSpec(shape, lambda *_: (0,) * nd)


def _rows(tm, cw, cb):
    return pl.BlockSpec((tm, cw), lambda i: (i, cb))


def _prev(tm, hb, cw, cb):
    r = tm // hb
    return pl.BlockSpec((hb, cw), lambda i: (jnp.maximum(i * r - 1, 0), cb))


def _next(tm, hb, cw, cb, s):
    r = tm // hb
    last = s // hb - 1
    return pl.BlockSpec((hb, cw), lambda i: (jnp.minimum((i + 1) * r, last), cb))


def _acc_rows(ref, first, rows):
    @pl.when(first)
    def _():
        ref[...] = jnp.zeros_like(ref)
    for r, v in enumerate(rows):
        ref[r:r + 1, :] += v


def _colsum(v):
    return jnp.sum(v, axis=0, keepdims=True)


def _ln_stats(r):
    mu = jnp.mean(r, axis=-1, keepdims=True)
    xc = r - mu
    var = jnp.mean(xc * xc, axis=-1, keepdims=True)
    rstd = lax.rsqrt(var + LN_EPS)
    return xc * rstd, rstd


def _ln_bwd(dxh, xh, rstd):
    return rstd * (dxh - jnp.mean(dxh, axis=-1, keepdims=True) - xh * jnp.mean(dxh * xh, axis=-1, keepdims=True))


_DIMS = {"nn": ((1,), (0,)), "nt": ((1,), (1,)), "tn": ((0,), (0,))}


def _mm(a, b, mode, name, *, out_dtype=F32, bias=None, tm=512, tn=1024, tk=2048):
    if mode == "nn":
        (m, k), (k2, n) = a.shape, b.shape
    elif mode == "nt":
        (m, k), (n, k2) = a.shape, b.shape
    else:
        (k, m), (k2, n) = a.shape, b.shape
    assert k == k2, (a.shape, b.shape, mode)
    tm, tn, tk = _tile(m, tm), _tile(n, tn), _tile(k, tk)
    nk = k // tk
    dims = (_DIMS[mode], ((), ()))
    has_bias = bias is not None

    def body(*refs):
        if has_bias:
            a_ref, b_ref, bias_ref, o_ref = refs[:4]
        else:
            a_ref, b_ref, o_ref = refs[:3]
        p = lax.dot_general(a_ref[...].astype(BF16), b_ref[...].astype(BF16), dims, preferred_element_type=F32)

        def finish(r):
            if has_bias:
                r = r + bias_ref[...]
            o_ref[...] = r.astype(out_dtype)

        if nk == 1:
            finish(p)
        else:
            acc = refs[-1]
            kk = pl.program_id(2)

            @pl.when(kk == 0)
            def _():
                acc[...] = p

            @pl.when(kk > 0)
            def _():
                acc[...] += p

            @pl.when(kk == nk - 1)
            def _():
                finish(acc[...])

    if mode == "nn":
        a_spec = pl.BlockSpec((tm, tk), lambda i, j, kk: (i, kk))
        b_spec = pl.BlockSpec((tk, tn), lambda i, j, kk: (kk, j))
    elif mode == "nt":
        a_spec = pl.BlockSpec((tm, tk), lambda i, j, kk: (i, kk))
        b_spec = pl.BlockSpec((tn, tk), lambda i, j, kk: (j, kk))
    else:
        a_spec = pl.BlockSpec((tk, tm), lambda i, j, kk: (kk, i))
        b_spec = pl.BlockSpec((tk, tn), lambda i, j, kk: (kk, j))
    in_specs = [a_spec, b_spec]
    args = [a, b]
    if has_bias:
        in_specs.append(pl.BlockSpec((1, tn), lambda i, j, kk: (0, j)))
        args.append(bias)
    return pl.pallas_call(
        body, name=name, grid=(m // tm, n // tn, nk),
        in_specs=in_specs, out_specs=pl.BlockSpec((tm, tn), lambda i, j, kk: (i, j)),
        out_shape=jax.ShapeDtypeStruct((m, n), out_dtype),
        scratch_shapes=[pltpu.VMEM((tm, tn), F32)] if nk > 1 else [],
        compiler_params=_params(("parallel", "parallel", "arbitrary")),
    )(*args)


def _mod_fwd(x, prm):
    s = x.shape[0]
    tm = _tile(s, TM)

    def body(x_ref, p_ref, h_ref):
        h_ref[...] = (x_ref[...] * (1.0 + p_ref[1:2, :]) + p_ref[2:3, :]).astype(BF16)

    return pl.pallas_call(
        body, name="mod_fwd", grid=(s // tm,),
        in_specs=[_rows(tm, D, 0), _full((8, D))], out_specs=_rows(tm, D, 0),
        out_shape=jax.ShapeDtypeStruct((s, D), BF16), compiler_params=_params(("parallel",)),
    )(x, prm)


def _mod_bwd(dres, dh, x, prm):
    s = x.shape[0]
    tm = _tile(s, TM)

    def body(dres_ref, dh_ref, x_ref, p_ref, dx_ref, acc_ref):
        dh_v = dh_ref[...]
        dx_ref[...] = dres_ref[...] + dh_v * (1.0 + p_ref[1:2, :])
        _acc_rows(acc_ref, pl.program_id(0) == 0, [_colsum(dh_v * x_ref[...]), _colsum(dh_v)])

    return pl.pallas_call(
        body, name="mod_bwd", grid=(s // tm,),
        in_specs=[_rows(tm, D, 0)] * 3 + [_full((8, D))],
        out_specs=[_rows(tm, D, 0), _full((8, D))],
        out_shape=[jax.ShapeDtypeStruct((s, D), F32), jax.ShapeDtypeStruct((8, D), F32)],
        compiler_params=_params(("arbitrary",)),
    )(dres, dh, x, prm)


CA_HALO = 32
C512 = 512


def _glu_buf(buf, ap, am, bp, bm, first, tm):
    glu_p = ap[...] * _sig(bp[...])
    buf[0:CA_HALO, :] = jnp.where(first, jnp.zeros_like(glu_p), glu_p)
    buf[CA_HALO:CA_HALO + tm, :] = am[...] * _sig(bm[...])


def _conv_a_fwd(proj, w32, ln_g, ln_b):
    s = proj.shape[0]
    tm = _tile(s, TM)
    ca, cb = CA0 // C512, CB0 // C512

    def body(ap, am, bp, bm, w_ref, g_ref, b_ref, yc_ref, za_ref, buf):
        _glu_buf(buf, ap, am, bp, bm, pl.program_id(0) == 0, tm)
        acc = w_ref[0:1, :] * buf[pl.ds(CA_HALO - CONV_W + 1, tm), :]
        for k in range(1, CONV_W):
            acc = acc + w_ref[k:k + 1, :] * buf[pl.ds(CA_HALO - CONV_W + 1 + k, tm), :]
        yc_ref[...] = acc
        xh, _ = _ln_stats(acc)
        y = xh * g_ref[...] + b_ref[...]
        za_ref[...] = (y * _sig(y)).astype(BF16)

    return pl.pallas_call(
        body, name="conv_a_fwd", grid=(s // tm,),
        in_specs=[_prev(tm, CA_HALO, C512, ca), _rows(tm, C512, ca), _prev(tm, CA_HALO, C512, cb), _rows(tm, C512, cb),
                  _full((32, C512)), _full((1, C512)), _full((1, C512))],
        out_specs=[_rows(tm, C512, 0), _rows(tm, C512, 0)],
        out_shape=[jax.ShapeDtypeStruct((s, C512), F32), jax.ShapeDtypeStruct((s, C512), BF16)],
        scratch_shapes=[pltpu.VMEM((tm + CA_HALO, C512), F32)],
        compiler_params=_params(("parallel",)),
    )(proj, proj, proj, proj, w32, ln_g, ln_b)


def _conv_a_bwd1(dza, yconv, proj, ln_g, ln_b):
    s = proj.shape[0]
    tm = _tile(s, TM)
    ca, cb = CA0 // C512, CB0 // C512

    def body(dza_ref, yc_ref, ap, am, bp, bm, g_ref, b_ref, dyc_ref, dw_ref, acc_ref, buf):
        first = pl.program_id(0) == 0
        xh, rstd = _ln_stats(yc_ref[...])
        g = g_ref[...]
        y = xh * g + b_ref[...]
        sg = _sig(y)
        dy = dza_ref[...] * (sg * (1.0 + y * (1.0 - sg)))
        _acc_rows(acc_ref, first, [_colsum(dy * xh), _colsum(dy)])
        dyc = _ln_bwd(dy * g, xh, rstd)
        dyc_ref[...] = dyc
        _glu_buf(buf, ap, am, bp, bm, first, tm)

        @pl.when(first)
        def _():
            dw_ref[...] = jnp.zeros_like(dw_ref)
        for k in range(CONV_W):
            dw_ref[k:k + 1, :] += _colsum(dyc * buf[pl.ds(CA_HALO - CONV_W + 1 + k, tm), :])

    return pl.pallas_call(
        body, name="conv_a_bwd1", grid=(s // tm,),
        in_specs=[_rows(tm, C512, 0), _rows(tm, C512, 0),
                  _prev(tm, CA_HALO, C512, ca), _rows(tm, C512, ca), _prev(tm, CA_HALO, C512, cb), _rows(tm, C512, cb),
                  _full((1, C512)), _full((1, C512))],
        out_specs=[_rows(tm, C512, 0), _full((32, C512)), _full((8, C512))],
        out_shape=[jax.ShapeDtypeStruct((s, C512), F32), jax.ShapeDtypeStruct((32, C512), F32),
                   jax.ShapeDtypeStruct((8, C512), F32)],
        scratch_shapes=[pltpu.VMEM((tm + CA_HALO, C512), F32)],
        compiler_params=_params(("arbitrary",)),
    )(dza, yconv, proj, proj, proj, proj, ln_g, ln_b)


def _conv_a_bwd2(dyc, proj, w32):
    s = proj.shape[0]
    tm = _tile(s, TM)
    ca, cb = CA0 // C512, CB0 // C512
    nt = s // tm

    def body(dm, dn, am, bm, w_ref, da_ref, db_ref, acc_ref, buf):
        i = pl.program_id(0)
        buf[0:tm, :] = dm[...]
        nxt = dn[...]
        buf[tm:tm + CA_HALO, :] = jnp.where(i == nt - 1, jnp.zeros_like(nxt), nxt)
        dglu = w_ref[0:1, :] * buf[pl.ds(CONV_W - 1, tm), :]
        for k in range(1, CONV_W):
            dglu = dglu + w_ref[k:k + 1, :] * buf[pl.ds(CONV_W - 1 - k, tm), :]
        sb = _sig(bm[...])
        da = dglu * sb
        db = dglu * am[...] * sb * (1.0 - sb)
        da_ref[...] = da.astype(BF16)
        db_ref[...] = db.astype(BF16)
        _acc_rows(acc_ref, i == 0, [_colsum(da), _colsum(db)])

    return pl.pallas_call(
        body, name="conv_a_bwd2", grid=(nt,),
        in_specs=[_rows(tm, C512, 0), _next(tm, CA_HALO, C512, 0, s), _rows(tm, C512, ca), _rows(tm, C512, cb),
                  _full((32, C512))],
        out_specs=[_rows(tm, C512, 0), _rows(tm, C512, 0), _full((8, C512))],
        out_shape=[jax.ShapeDtypeStruct((s, C512), BF16), jax.ShapeDtypeStruct((s, C512), BF16),
                   jax.ShapeDtypeStruct((8, C512), F32)],
        scratch_shapes=[pltpu.VMEM((tm + CA_HALO, C512), F32)],
        compiler_params=_params(("arbitrary",)),
    )(dyc, dyc, proj, proj, w32)


H8 = 8
SC_W = 3


def _sc_ubuf(buf, cp, cm, xp, xm, first, tm):
    up = cp[...] * xp[...]
    buf[0:H8, :] = jnp.where(first, jnp.zeros_like(up), up)
    buf[H8:H8 + tm, :] = cm[...] * xm[...]


def _conv3(w_ref, buf, tm):
    acc = w_ref[0:1, :] * buf[pl.ds(H8 - SC_W + 1, tm), :]
    for k in range(1, SC_W):
        acc = acc + w_ref[k:k + 1, :] * buf[pl.ds(H8 - SC_W + 1 + k, tm), :]
    return acc


def _conv3_t(w_ref, buf, tm):
    acc = w_ref[0:1, :] * buf[pl.ds(SC_W - 1, tm), :]
    for k in range(1, SC_W):
        acc = acc + w_ref[k:k + 1, :] * buf[pl.ds(SC_W - 1 - k, tm), :]
    return acc


def _sc_fwd(proj, w8):
    s = proj.shape[0]
    tm = _tile(s, TM)
    c_bg, c_cg, c_x = SBG0 // C512, SCG0 // C512, SX0 // C512

    def body(bg, cp, cm, xp, xm, w_ref, zb_ref, buf):
        _sc_ubuf(buf, cp, cm, xp, xm, pl.program_id(0) == 0, tm)
        zb_ref[...] = (bg[...] * _conv3(w_ref, buf, tm)).astype(BF16)

    return pl.pallas_call(
        body, name="sc_fwd", grid=(s // tm,),
        in_specs=[_rows(tm, C512, c_bg), _prev(tm, H8, C512, c_cg), _rows(tm, C512, c_cg),
                  _prev(tm, H8, C512, c_x), _rows(tm, C512, c_x), _full((8, C512))],
        out_specs=_rows(tm, C512, 0), out_shape=jax.ShapeDtypeStruct((s, C512), BF16),
        scratch_shapes=[pltpu.VMEM((tm + H8, C512), F32)], compiler_params=_params(("parallel",)),
    )(proj, proj, proj, proj, proj, w8)


def _sc_bwd1(dzb, proj, w8):
    s = proj.shape[0]
    tm = _tile(s, TM)
    c_bg, c_cg, c_x = SBG0 // C512, SCG0 // C512, SX0 // C512

    def body(dz_ref, bg, cp, cm, xp, xm, w_ref, dconv_ref, dbg_ref, dw_ref, acc_ref, buf):
        first = pl.program_id(0) == 0
        _sc_ubuf(buf, cp, cm, xp, xm, first, tm)
        dz = dz_ref[...]
        dbg = dz * _conv3(w_ref, buf, tm)
        dconv = dz * bg[...]
        dconv_ref[...] = dconv
        dbg_ref[...] = dbg.astype(BF16)
        _acc_rows(acc_ref, first, [_colsum(dbg)])
        _acc_rows(dw_ref, first, [_colsum(dconv * buf[pl.ds(H8 - SC_W + 1 + k, tm), :]) for k in range(SC_W)])

    return pl.pallas_call(
        body, name="sc_bwd1", grid=(s // tm,),
        in_specs=[_rows(tm, C512, 0), _rows(tm, C512, c_bg), _prev(tm, H8, C512, c_cg), _rows(tm, C512, c_cg),
                  _prev(tm, H8, C512, c_x), _rows(tm, C512, c_x), _full((8, C512))],
        out_specs=[_rows(tm, C512, 0), _rows(tm, C512, 0), _full((8, C512)), _full((8, C512))],
        out_shape=[jax.ShapeDtypeStruct((s, C512), F32), jax.ShapeDtypeStruct((s, C512), BF16),
                   jax.ShapeDtypeStruct((8, C512), F32), jax.ShapeDtypeStruct((8, C512), F32)],
        scratch_shapes=[pltpu.VMEM((tm + H8, C512), F32)], compiler_params=_params(("arbitrary",)),
    )(dzb, proj, proj, proj, proj, proj, w8)


def _sc_bwd2(dconv, proj, w8):
    s = proj.shape[0]
    tm = _tile(s, TM)
    c_cg, c_x = SCG0 // C512, SX0 // C512
    nt = s // tm

    def body(dm, dn, cm, xm, w_ref, dcg_ref, dx_ref, acc_ref, buf):
        i = pl.program_id(0)
        buf[0:tm, :] = dm[...]
        nxt = dn[...]
        buf[tm:tm + H8, :] = jnp.where(i == nt - 1, jnp.zeros_like(nxt), nxt)
        du = _conv3_t(w_ref, buf, tm)
        dcg = du * xm[...]
        dx = du * cm[...]
        dcg_ref[...] = dcg.astype(BF16)
        dx_ref[...] = dx.astype(BF16)
        _acc_rows(acc_ref, i == 0, [_colsum(dcg), _colsum(dx)])

    return pl.pallas_call(
        body, name="sc_bwd2", grid=(nt,),
        in_specs=[_rows(tm, C512, 0), _next(tm, H8, C512, 0, s), _rows(tm, C512, c_cg), _rows(tm, C512, c_x),
                  _full((8, C512))],
        out_specs=[_rows(tm, C512, 0), _rows(tm, C512, 0), _full((8, C512))],
        out_shape=[jax.ShapeDtypeStruct((s, C512), BF16), jax.ShapeDtypeStruct((s, C512), BF16),
                   jax.ShapeDtypeStruct((8, C512), F32)],
        scratch_shapes=[pltpu.VMEM((tm + H8, C512), F32)], compiler_params=_params(("arbitrary",)),
    )(dconv, dconv, proj, proj, w8)


QLAT = 256
KVLAT = 128


def _rms(x, g):
    r = lax.rsqrt(jnp.mean(x * x, axis=-1, keepdims=True) + RMS_EPS)
    return x * r * g, r


def _rms_bwd(dy, x, g, r):
    u = dy * g
    dx = r * u - x * (r * r * r) * jnp.mean(u * x, axis=-1, keepdims=True)
    return dx, _colsum(dy * x * r)


def _lat_fwd(proj, gq, gkv):
    s = proj.shape[0]
    tm = _tile(s, TM)

    def body(q_ref, kv_ref, gq_ref, gkv_ref, qn_ref, kvn_ref):
        qn_ref[...] = _rms(q_ref[...], gq_ref[...])[0].astype(BF16)
        kvn_ref[...] = _rms(kv_ref[...], gkv_ref[...])[0].astype(BF16)

    return pl.pallas_call(
        body, name="lat_fwd", grid=(s // tm,),
        in_specs=[_rows(tm, QLAT, QL0 // QLAT), _rows(tm, KVLAT, KVL0 // KVLAT), _full((1, QLAT)), _full((1, KVLAT))],
        out_specs=[_rows(tm, QLAT, 0), _rows(tm, KVLAT, 0)],
        out_shape=[jax.ShapeDtypeStruct((s, QLAT), BF16), jax.ShapeDtypeStruct((s, KVLAT), BF16)],
        compiler_params=_params(("parallel",)),
    )(proj, proj, gq, gkv)


def _lat_bwd(dqn, dkvn, proj, gq, gkv):
    s = proj.shape[0]
    tm = _tile(s, TM)

    def body(dqn_ref, dkvn_ref, q_ref, kv_ref, gq_ref, gkv_ref, dq_ref, dkv_ref, accq_ref, acckv_ref):
        first = pl.program_id(0) == 0
        q, kv = q_ref[...], kv_ref[...]
        gqv, gkvv = gq_ref[...], gkv_ref[...]
        dq, dgq = _rms_bwd(dqn_ref[...], q, gqv, _rms(q, gqv)[1])
        dkv, dgkv = _rms_bwd(dkvn_ref[...], kv, gkvv, _rms(kv, gkvv)[1])
        dq_ref[...] = dq.astype(BF16)
        dkv_ref[...] = dkv.astype(BF16)
        _acc_rows(accq_ref, first, [dgq, _colsum(dq)])
        _acc_rows(acckv_ref, first, [dgkv, _colsum(dkv)])

    return pl.pallas_call(
        body, name="lat_bwd", grid=(s // tm,),
        in_specs=[_rows(tm, QLAT, 0), _rows(tm, KVLAT, 0), _rows(tm, QLAT, QL0 // QLAT), _rows(tm, KVLAT, KVL0 // KVLAT),
                  _full((1, QLAT)), _full((1, KVLAT))],
        out_specs=[_rows(tm, QLAT, 0), _rows(tm, KVLAT, 0), _full((8, QLAT)), _full((8, KVLAT))],
        out_shape=[jax.ShapeDtypeStruct((s, QLAT), BF16), jax.ShapeDtypeStruct((s, KVLAT), BF16),
                   jax.ShapeDtypeStruct((8, QLAT), F32), jax.ShapeDtypeStruct((8, KVLAT), F32)],
        compiler_params=_params(("arbitrary",)),
    )(dqn, dkvn, proj, proj, gq, gkv)


def _rope(x, c, s1, s2):
    return x * c + pltpu.roll(x, HEAD_PAD - ROPE // 2, 1) * s1 + pltpu.roll(x, ROPE // 2, 1) * s2


def _rope_t(d, c, s1, s2):
    return d * c + pltpu.roll(d * s1, ROPE // 2, 1) + pltpu.roll(d * s2, HEAD_PAD - ROPE // 2, 1)


def _rope_fwd(qp, kvp, proj, ct, s1t, s2t):
    s = proj.shape[0]
    tm = _tile(s, TM)

    def body(q_ref, kv_ref, kr_ref, c_ref, s1_ref, s2_ref, qo, ko, vo):
        c, s1, s2 = c_ref[...], s1_ref[...], s2_ref[...]
        kr = _rope(kr_ref[...], c, s1, s2)
        for h in range(HEADS):
            cs = slice(h * HEAD_PAD, (h + 1) * HEAD_PAD)
            qo[:, cs] = _rope(q_ref[:, cs], c, s1, s2).astype(BF16)
            ko[:, cs] = (kv_ref[:, 2 * h * HEAD_PAD:(2 * h + 1) * HEAD_PAD] + kr).astype(BF16)
            vo[:, cs] = kv_ref[:, (2 * h + 1) * HEAD_PAD:(2 * h + 2) * HEAD_PAD].astype(BF16)

    tab = _rows(tm, HEAD_PAD, 0)
    return pl.pallas_call(
        body, name="rope_fwd", grid=(s // tm,),
        in_specs=[_rows(tm, HEADS * HEAD_PAD, 0), _rows(tm, 2 * HEADS * HEAD_PAD, 0), _rows(tm, HEAD_PAD, KR0 // HEAD_PAD),
                  tab, tab, tab],
        out_specs=[_rows(tm, HEADS * HEAD_PAD, 0)] * 3,
        out_shape=[jax.ShapeDtypeStruct((s, HEADS * HEAD_PAD), BF16)] * 3,
        compiler_params=_params(("parallel",)),
    )(qp, kvp, proj, ct, s1t, s2t)


def _rope_bwd(dq, dkv, ct, s1t, s2t):
    s = dq.shape[0]
    tm = _tile(s, TM)

    def body(dq_ref, dkv_ref, c_ref, s1_ref, s2_ref, dqo, dkr_ref, acc_ref):
        c, s1, s2 = c_ref[...], s1_ref[...], s2_ref[...]
        tot = dkv_ref[:, 0:HEAD_PAD]
        for h in range(HEADS):
            cs = slice(h * HEAD_PAD, (h + 1) * HEAD_PAD)
            dqo[:, cs] = _rope_t(dq_ref[:, cs], c, s1, s2).astype(BF16)
            if h:
                tot = tot + dkv_ref[:, 2 * h * HEAD_PAD:(2 * h + 1) * HEAD_PAD]
        lane = lax.broadcasted_iota(jnp.int32, (tm, HEAD_PAD), 1)
        dkr = jnp.where((lane >= NOPE) & (lane < QK_DIM), _rope_t(tot, c, s1, s2), 0.0)
        dkr_ref[...] = dkr.astype(BF16)
        _acc_rows(acc_ref, pl.program_id(0) == 0, [_colsum(dkr)])

    tab = _rows(tm, HEAD_PAD, 0)
    return pl.pallas_call(
        body, name="rope_bwd", grid=(s // tm,),
        in_specs=[_rows(tm, HEADS * HEAD_PAD, 0), _rows(tm, 2 * HEADS * HEAD_PAD, 0), tab, tab, tab],
        out_specs=[_rows(tm, HEADS * HEAD_PAD, 0), tab, _full((8, HEAD_PAD))],
        out_shape=[jax.ShapeDtypeStruct((s, HEADS * HEAD_PAD), BF16), jax.ShapeDtypeStruct((s, HEAD_PAD), BF16),
                   jax.ShapeDtypeStruct((8, HEAD_PAD), F32)],
        compiler_params=_params(("arbitrary",)),
    )(dq, dkv, ct, s1t, s2t)


_NT = (((1,), (1,)), ((), ()))
_TN = (((0,), (0,)), ((), ()))
_NN = (((1,), (0,)), ((), ()))


def _tile_rows(ref, t, tq):
    return ref[pl.ds(pl.multiple_of(t * tq, tq), tq), :]


def _attn_fwd(q, k, v):
    s = q.shape[0]
    tq = _tile(s, TQ)
    nq = s // tq

    def body(q_ref, k_ref, v_ref, o_ref, lse_ref):
        qi = pl.program_id(1)
        q_t = q_ref[...]

        def raw(ki):
            return lax.dot_general(_tile_rows(k_ref, ki, tq), q_t, _NT, preferred_element_type=F32)

        def process(ki, st, m, l, acc, masked):
            sc = st * ATT_SCALE
            if masked:
                key = lax.broadcasted_iota(jnp.int32, (tq, tq), 0)
                qry = lax.broadcasted_iota(jnp.int32, (tq, tq), 1)
                sc = jnp.where(key <= qry, sc, -jnp.inf)
            m_new = jnp.maximum(m, jnp.max(sc, axis=0, keepdims=True))
            pt = jnp.exp(sc - m_new)
            a = jnp.exp(m - m_new)
            pv = lax.dot_general(_tile_rows(v_ref, ki, tq), pt.astype(BF16), _TN, preferred_element_type=F32)
            return m_new, a * l + jnp.sum(pt, axis=0, keepdims=True), a * acc + pv

        def loop_body(ki, c):
            nxt = raw(ki + 1)
            return (nxt,) + process(ki, c[0], c[1], c[2], c[3], False)

        init = (raw(0), jnp.full((1, tq), -jnp.inf, F32), jnp.zeros((1, tq), F32), jnp.zeros((HEAD_PAD, tq), F32))
        c = lax.fori_loop(0, qi, loop_body, init)
        m, l, acc = process(qi, c[0], c[1], c[2], c[3], True)
        o_ref[...] = jnp.transpose(acc / l)
        lse_ref[0] = jnp.broadcast_to(m + jnp.log(l), (8, tq))

    qspec = pl.BlockSpec((tq, HEAD_PAD), lambda h, qi: (qi, h))
    kspec = pl.BlockSpec((s, HEAD_PAD), lambda h, qi: (0, h))
    return pl.pallas_call(
        body, name="attn_fwd", grid=(HEADS, nq),
        in_specs=[qspec, kspec, kspec],
        out_specs=[qspec, pl.BlockSpec((1, 8, tq), lambda h, qi: (h, 0, qi))],
        out_shape=[jax.ShapeDtypeStruct((s, HEADS * HEAD_PAD), F32), jax.ShapeDtypeStruct((HEADS, 8, s), F32)],
        compiler_params=_params(("parallel", "parallel")),
    )(q, k, v)


def _attn_prep(d_o, o):
    s = o.shape[0]
    tm = _tile(s, TM)

    def body(do_ref, o_ref, dl_ref, dob_ref):
        for h in range(HEADS):
            cs = slice(h * HEAD_PAD, (h + 1) * HEAD_PAD)
            dov = do_ref[:, cs]
            row = jnp.sum(jnp.transpose(dov * o_ref[:, cs]), axis=0, keepdims=True)
            dl_ref[h] = jnp.broadcast_to(row, (8, tm))
            dob_ref[:, cs] = dov.astype(BF16)

    blk = _rows(tm, HEADS * HEAD_PAD, 0)
    return pl.pallas_call(
        body, name="attn_prep", grid=(s // tm,),
        in_specs=[blk, blk], out_specs=[pl.BlockSpec((HEADS, 8, tm), lambda i: (0, 0, i)), blk],
        out_shape=[jax.ShapeDtypeStruct((HEADS, 8, s), F32), jax.ShapeDtypeStruct((s, HEADS * HEAD_PAD), BF16)],
        compiler_params=_params(("parallel",)),
    )(d_o, o)


def _attn_bwd(q, k, v, d_o, lse, delta):
    s = q.shape[0]
    tq = _tile(s, TQ_BWD)
    nq = s // tq

    def body(q_ref, k_ref, v_ref, do_ref, lse_ref, dl_ref, dq_ref, dkv_ref):
        ki = pl.program_id(1)
        k_t, v_t = k_ref[...], v_ref[...]

        @pl.when(ki == 0)
        def _():
            dq_ref[...] = jnp.zeros_like(dq_ref)

        def raw(qi):
            return (lax.dot_general(k_t, _tile_rows(q_ref, qi, tq), _NT, preferred_element_type=F32),
                    lax.dot_general(v_t, _tile_rows(do_ref, qi, tq), _NT, preferred_element_type=F32))

        def process(qi, st, dpt, dk, dv, masked):
            cols = pl.ds(pl.multiple_of(qi * tq, tq), tq)
            sc = st * ATT_SCALE
            if masked:
                key = lax.broadcasted_iota(jnp.int32, (tq, tq), 0)
                qry = lax.broadcasted_iota(jnp.int32, (tq, tq), 1)
                sc = jnp.where(key <= qry, sc, -jnp.inf)
            pt = jnp.exp(sc - lse_ref[0, 0:1, cols])
            dsb = (pt * (dpt - dl_ref[0, 0:1, cols]) * ATT_SCALE).astype(BF16)
            dv = dv + lax.dot_general(pt.astype(BF16), _tile_rows(do_ref, qi, tq), _NN, preferred_element_type=F32)
            dk = dk + lax.dot_general(dsb, _tile_rows(q_ref, qi, tq), _NN, preferred_element_type=F32)
            dq_ref[cols, :] += lax.dot_general(dsb, k_t, _TN, preferred_element_type=F32)
            return dk, dv

        zero = jnp.zeros((tq, HEAD_PAD), F32)
        dk, dv = process(ki, *raw(ki), zero, zero, True)

        def loop_body(qi, c):
            nxt = raw(jnp.minimum(qi + 1, nq - 1))
            return nxt + process(qi, c[0], c[1], c[2], c[3], False)

        c = lax.fori_loop(ki + 1, nq, loop_body, raw(jnp.minimum(ki + 1, nq - 1)) + (dk, dv))
        dkv_ref[:, 0:HEAD_PAD] = c[2]
        dkv_ref[:, HEAD_PAD:2 * HEAD_PAD] = c[3]

    full = pl.BlockSpec((s, HEAD_PAD), lambda h, ki: (0, h))
    tile = pl.BlockSpec((tq, HEAD_PAD), lambda h, ki: (ki, h))
    stat = pl.BlockSpec((1, 8, s), lambda h, ki: (h, 0, 0))
    return pl.pallas_call(
        body, name="attn_bwd", grid=(HEADS, nq),
        in_specs=[full, tile, tile, full, stat, stat],
        out_specs=[full, pl.BlockSpec((tq, 2 * HEAD_PAD), lambda h, ki: (ki, h))],
        out_shape=[jax.ShapeDtypeStruct((s, HEADS * HEAD_PAD), F32), jax.ShapeDtypeStruct((s, HEADS * 2 * HEAD_PAD), F32)],
        compiler_params=_params(("parallel", "arbitrary")),
    )(q, k, v, d_o, lse, delta)


PH = 16
PG = 128


def _pool_pd(buf, u_main_ref, g, i, tm):
    w = POOL_WINDOWS[g]
    cs = pl.ds(g * PG, PG)
    tot = buf[pl.ds(PH, tm), cs]
    for j in range(1, w):
        tot = tot + buf[pl.ds(PH - j, tm), cs]
    t = i * tm + lax.broadcasted_iota(jnp.int32, (tm, PG), 0)
    cnt = jnp.minimum(t + 1, w).astype(F32)
    return tot / cnt - u_main_ref[:, cs]


def _pool_ubuf(buf, up, um, first, tm):
    p = up[...]
    buf[0:PH, :] = jnp.where(first, jnp.zeros_like(p), p)
    buf[PH:PH + tm, :] = um[...]


def _pool_fwd(proj, w_pool, scale):
    s = proj.shape[0]
    tm = _tile(s, TM)
    cu = PU0 // C512

    def body(up, um, w_ref, sc_ref, zd_ref, buf):
        i = pl.program_id(0)
        _pool_ubuf(buf, up, um, i == 0, tm)
        for g in range(4):
            pd = _pool_pd(buf, um, g, i, tm).astype(BF16)
            e = lax.dot_general(pd, w_ref[g], _NN, preferred_element_type=F32)
            zd_ref[:, g * PG:(g + 1) * PG] = (e * sc_ref[:, g * PG:(g + 1) * PG]).astype(BF16)

    return pl.pallas_call(
        body, name="pool_fwd", grid=(s // tm,),
        in_specs=[_prev(tm, PH, C512, cu), _rows(tm, C512, cu), _full((4, PG, PG)), _full((1, C512))],
        out_specs=_rows(tm, C512, 0), out_shape=jax.ShapeDtypeStruct((s, C512), BF16),
        scratch_shapes=[pltpu.VMEM((tm + PH, C512), F32)], compiler_params=_params(("parallel",)),
    )(proj, proj, w_pool, scale)


def _pool_bwd1(dzd, proj, w_pool, scale):
    s = proj.shape[0]
    tm = _tile(s, TM)
    cu = PU0 // C512

    def body(dz_ref, up, um, w_ref, sc_ref, dpd_ref, dw_ref, acc_ref, buf):
        i = pl.program_id(0)
        first = i == 0
        _pool_ubuf(buf, up, um, first, tm)

        @pl.when(first)
        def _():
            dw_ref[...] = jnp.zeros_like(dw_ref)
            acc_ref[...] = jnp.zeros_like(acc_ref)
        for g in range(4):
            cs = slice(g * PG, (g + 1) * PG)
            pd = _pool_pd(buf, um, g, i, tm).astype(BF16)
            wg = w_ref[g]
            e = lax.dot_general(pd, wg, _NN, preferred_element_type=F32)
            dz = dz_ref[:, cs]
            acc_ref[0:1, cs] += _colsum(dz * e)
            de = (dz * sc_ref[:, cs]).astype(BF16)
            dw_ref[g] += lax.dot_general(pd, de, _TN, preferred_element_type=F32)
            dpd_ref[:, cs] = lax.dot_general(de, wg, _NT, preferred_element_type=F32)

    return pl.pallas_call(
        body, name="pool_bwd1", grid=(s // tm,),
        in_specs=[_rows(tm, C512, 0), _prev(tm, PH, C512, cu), _rows(tm, C512, cu), _full((4, PG, PG)), _full((1, C512))],
        out_specs=[_rows(tm, C512, 0), _full((4, PG, PG)), _full((8, C512))],
        out_shape=[jax.ShapeDtypeStruct((s, C512), F32), jax.ShapeDtypeStruct((4, PG, PG), F32),
                   jax.ShapeDtypeStruct((8, C512), F32)],
        scratch_shapes=[pltpu.VMEM((tm + PH, C512), F32)], compiler_params=_params(("arbitrary",)),
    )(dzd, proj, proj, w_pool, scale)


def _pool_bwd2(dpd, s):
    tm = _tile(s, TM)
    nt = s // tm

    def body(dm, dn, du_ref, acc_ref, buf):
        i = pl.program_id(0)
        buf[0:tm, :] = dm[...]
        nxt = dn[...]
        buf[tm:tm + PH, :] = jnp.where(i == nt - 1, jnp.zeros_like(nxt), nxt)
        t = i * tm + lax.broadcasted_iota(jnp.int32, (tm + PH, PG), 0)
        cols = []
        for g, w in enumerate(POOL_WINDOWS):
            cs = pl.ds(g * PG, PG)
            cnt = jnp.minimum(t + 1, w).astype(F32)
            buf[:, cs] = buf[:, cs] / cnt
        for g, w in enumerate(POOL_WINDOWS):
            cs = pl.ds(g * PG, PG)
            tot = buf[pl.ds(0, tm), cs]
            for j in range(1, w):
                tot = tot + buf[pl.ds(j, tm), cs]
            du = tot - dm[:, cs]
            du_ref[:, cs] = du.astype(BF16)
            cols.append(_colsum(du))
        _acc_rows(acc_ref, i == 0, [jnp.concatenate(cols, axis=1)])

    return pl.pallas_call(
        body, name="pool_bwd2", grid=(nt,),
        in_specs=[_rows(tm, C512, 0), _next(tm, PH, C512, 0, s)],
        out_specs=[_rows(tm, C512, 0), _full((8, C512))],
        out_shape=[jax.ShapeDtypeStruct((s, C512), BF16), jax.ShapeDtypeStruct((8, C512), F32)],
        scratch_shapes=[pltpu.VMEM((tm + PH, C512), F32)], compiler_params=_params(("arbitrary",)),
    )(dpd, dpd)


def _merge_specs(tm):
    return [_rows(tm, C512, 0), _rows(tm, C512, 0), _rows(tm, D, 0), _rows(tm, C512, 0),
            _rows(tm, 4 * D, GATES0 // (4 * D)),
            _full((C512, D)), _full((C512, D)), _full((D, D)), _full((C512, D))]


def _branch_ys(za, zb, o, zd, wa, wb, wc, wd):
    zs = (za[...], zb[...], o[...].astype(BF16), zd[...])
    return [lax.dot_general(z, w[...], _NN, preferred_element_type=F32) for z, w in zip(zs, (wa, wb, wc, wd))]


def _merge_fwd(za, zb, o, zd, proj, wa, wb, wc, wd):
    s = proj.shape[0]
    tm = _tile(s, TM_WIDE)

    def body(za_r, zb_r, o_r, zd_r, g_ref, wa_r, wb_r, wc_r, wd_r, m_ref):
        ys = _branch_ys(za_r, zb_r, o_r, zd_r, wa_r, wb_r, wc_r, wd_r)
        acc = _sig(g_ref[:, 0:D]) * ys[0]
        for b in range(1, 4):
            acc = acc + _sig(g_ref[:, b * D:(b + 1) * D]) * ys[b]
        m_ref[...] = acc.astype(BF16)

    return pl.pallas_call(
        body, name="merge_fwd", grid=(s // tm,), in_specs=_merge_specs(tm),
        out_specs=_rows(tm, D, 0), out_shape=jax.ShapeDtypeStruct((s, D), BF16),
        compiler_params=_params(("parallel",)),
    )(za, zb, o, zd, proj, wa, wb, wc, wd)


def _merge_bwd(dmerged, za, zb, o, zd, proj, wa, wb, wc, wd):
    s = proj.shape[0]
    tm = _tile(s, TM_WIDE)

    def body(dm_ref, za_r, zb_r, o_r, zd_r, g_ref, wa_r, wb_r, wc_r, wd_r,
             dya, dyb, dyc, dyd, dza, dzb, d_o, dzd, dg_ref, acc_ref):
        ys = _branch_ys(za_r, zb_r, o_r, zd_r, wa_r, wb_r, wc_r, wd_r)
        dm = dm_ref[...]
        sums = []
        for b, (dy_ref, dz_ref, w_r) in enumerate(((dya, dza, wa_r), (dyb, dzb, wb_r), (dyc, d_o, wc_r), (dyd, dzd, wd_r))):
            gt = _sig(g_ref[:, b * D:(b + 1) * D])
            dg = dm * ys[b] * gt * (1.0 - gt)
            dg_ref[:, b * D:(b + 1) * D] = dg.astype(BF16)
            sums.append(_colsum(dg))
            dy = (dm * gt).astype(BF16)
            dy_ref[...] = dy
            dz_ref[...] = lax.dot_general(dy, w_r[...], _NT, preferred_element_type=F32)
        _acc_rows(acc_ref, pl.program_id(0) == 0, [jnp.concatenate(sums, axis=1)])

    bf = lambda c: jax.ShapeDtypeStruct((s, c), BF16)
    f32 = lambda c: jax.ShapeDtypeStruct((s, c), F32)
    return pl.pallas_call(
        body, name="merge_bwd", grid=(s // tm,), in_specs=[_rows(tm, D, 0)] + _merge_specs(tm),
        out_specs=[_rows(tm, D, 0)] * 4 + [_rows(tm, C512, 0), _rows(tm, C512, 0), _rows(tm, D, 0), _rows(tm, C512, 0),
                                           _rows(tm, 4 * D, 0), _full((8, 4 * D))],
        out_shape=[bf(D)] * 4 + [f32(C512), f32(C512), f32(D), f32(C512), bf(4 * D), jax.ShapeDtypeStruct((8, 4 * D), F32)],
        compiler_params=_params(("arbitrary",)),
    )(dmerged, za, zb, o, zd, proj, wa, wb, wc, wd)


def _mm_res_ln(a, w, xres, prm, name):
    s, k = a.shape
    tm = _tile(s, TM_WIDE)

    def body(a_ref, w_ref, x_ref, p_ref, y_ref, xn_ref, hn_ref):
        y = lax.dot_general(a_ref[...], w_ref[...], _NN, preferred_element_type=F32)
        y_ref[...] = y
        xh, _ = _ln_stats(ALPHA * x_ref[...] + (1.0 + p_ref[0:1, :]) * y)
        xn = xh * p_ref[3:4, :] + p_ref[4:5, :]
        xn_ref[...] = xn
        hn_ref[...] = (xn * (1.0 + p_ref[1:2, :]) + p_ref[2:3, :]).astype(BF16)

    return pl.pallas_call(
        body, name=name, grid=(s // tm,),
        in_specs=[_rows(tm, k, 0), _full((k, D)), _rows(tm, D, 0), _full((8, D))],
        out_specs=[_rows(tm, D, 0)] * 3,
        out_shape=[jax.ShapeDtypeStruct((s, D), F32), jax.ShapeDtypeStruct((s, D), F32), jax.ShapeDtypeStruct((s, D), BF16)],
        compiler_params=_params(("parallel",)),
    )(a, w, xres, prm)


def _ln_res_bwd(dres_next, dh, xres, y, prm):
    s = xres.shape[0]
    tm = _tile(s, TM)

    def body(dn_ref, dh_ref, x_ref, y_ref, p_ref, dres_ref, dy_ref, acc_ref):
        gam, lng = p_ref[0:1, :], p_ref[3:4, :]
        yv = y_ref[...]
        xh, rstd = _ln_stats(ALPHA * x_ref[...] + (1.0 + gam) * yv)
        xn = xh * lng + p_ref[4:5, :]
        dh_v = dh_ref[...]
        dxn = dn_ref[...] + dh_v * (1.0 + p_ref[1:2, :])
        dr = _ln_bwd(dxn * lng, xh, rstd)
        dres_ref[...] = ALPHA * dr
        dy_ref[...] = ((1.0 + gam) * dr).astype(BF16)
        _acc_rows(acc_ref, pl.program_id(0) == 0,
                  [_colsum(dr * yv), _colsum(dh_v * xn), _colsum(dh_v), _colsum(dxn * xh), _colsum(dxn)])

    return pl.pallas_call(
        body, name="ln_res_bwd", grid=(s // tm,),
        in_specs=[_rows(tm, D, 0)] * 4 + [_full((8, D))],
        out_specs=[_rows(tm, D, 0), _rows(tm, D, 0), _full((8, D))],
        out_shape=[jax.ShapeDtypeStruct((s, D), F32), jax.ShapeDtypeStruct((s, D), BF16), jax.ShapeDtypeStruct((8, D), F32)],
        compiler_params=_params(("arbitrary",)),
    )(dres_next, dh, xres, y, prm)


FC = 16


def _shift_down(cur, prev, k, rowi):
    return jnp.where(rowi >= k, pltpu.roll(cur, k, 0), pltpu.roll(prev, k, 0))


def _shift_up(cur, nxt, k, rowi):
    return jnp.where(rowi < FC - k, pltpu.roll(cur, FC - k, 0), pltpu.roll(nxt, FC - k, 0))


def _conv3_chunk(w, cur, prev, rowi):
    return w[2] * cur + w[1] * _shift_down(cur, prev, 1, rowi) + w[0] * _shift_down(cur, prev, 2, rowi)


def _conv3_t_chunk(w, cur, nxt, rowi):
    return w[2] * cur + w[1] * _shift_up(cur, nxt, 1, rowi) + w[0] * _shift_up(cur, nxt, 2, rowi)


def _chunk_rows(j):
    return pl.ds(pl.multiple_of(j * FC, FC), FC)


def _ffn_chunk_specs(tm, s):
    r = tm // FC
    last = s // FC - 1
    out = []
    for half in (0, 1):
        out.append((pl.BlockSpec((FC, D_FF), lambda i, half=half: (jnp.maximum(i * r - 1, 0), half)),
                    pl.BlockSpec((tm, D_FF), lambda i, half=half: (i, half)),
                    pl.BlockSpec((FC, D_FF), lambda i, half=half: (jnp.minimum((i + 1) * r, last), half)),
                    pl.BlockSpec((8, D_FF), lambda i, half=half: (0, half))))
    return out


def _ffn_fwd(up, w8):
    s = up.shape[0]
    tm = _tile(s, TM_WIDE)
    (pv_s, mv_s, _, wv_s), (pg_s, mg_s, _, wg_s) = _ffn_chunk_specs(tm, s)

    def body(pv, mv, pg, mg, wv_ref, wg_ref, a_ref):
        first = pl.program_id(0) == 0
        lg = 2 * LANE
        rowi = lax.broadcasted_iota(jnp.int32, (FC, lg), 0)
        zero = jnp.zeros((FC, lg), F32)
        for cg in range(D_FF // lg):
            cs = slice(cg * lg, (cg + 1) * lg)
            wv = [wv_ref[k:k + 1, cs] for k in range(SC_W)]
            wg = [wg_ref[k:k + 1, cs] for k in range(SC_W)]

            def step(j, carry, cs=cs, wv=wv, wg=wg):
                rows = _chunk_rows(j)
                xv, xg = mv[rows, cs], mg[rows, cs]
                gate = _conv3_chunk(wg, xg, carry[1], rowi)
                a_ref[rows, cs] = (gate * _sig(gate) * _conv3_chunk(wv, xv, carry[0], rowi)).astype(BF16)
                return xv, xg

            lax.fori_loop(0, tm // (2 * FC), lambda j, c, step=step: step(2 * j + 1, step(2 * j, c)),
                          (jnp.where(first, zero, pv[:, cs]), jnp.where(first, zero, pg[:, cs])))

    return pl.pallas_call(
        body, name="ffn_fwd", grid=(s // tm,), in_specs=[pv_s, mv_s, pg_s, mg_s, wv_s, wg_s],
        out_specs=_rows(tm, D_FF, 0), out_shape=jax.ShapeDtypeStruct((s, D_FF), BF16),
        compiler_params=_params(("parallel",)),
    )(up, up, up, up, w8, w8)


def _ffn_bwd(da, up, w8):
    s = up.shape[0]
    tm = _tile(s, TM_WIDE)
    n, nt = tm // FC, s // tm
    (pv_s, mv_s, nv_s, wv_s), (pg_s, mg_s, ng_s, wg_s) = _ffn_chunk_specs(tm, s)

    def body(dam, dan, pv, mv, nv, pg, mg, ng, wv_ref, wg_ref, dup_ref, dw_ref):
        i = pl.program_id(0)
        first, last = i == 0, i == nt - 1
        rowi = lax.broadcasted_iota(jnp.int32, (FC, LANE), 0)
        zero = jnp.zeros((FC, LANE), F32)

        @pl.when(first)
        def _():
            dw_ref[...] = jnp.zeros_like(dw_ref)

        for cg in range(D_FF // LANE):
            cs = slice(cg * LANE, (cg + 1) * LANE)
            cs_g = slice(D_FF + cg * LANE, D_FF + (cg + 1) * LANE)
            wv = [wv_ref[k:k + 1, cs] for k in range(SC_W)]
            wg = [wg_ref[k:k + 1, cs] for k in range(SC_W)]

            def conv_grads(xv, xg, xpv, xpg, dav, wv=wv, wg=wg):
                val, gate = _conv3_chunk(wv, xv, xpv, rowi), _conv3_chunk(wg, xg, xpg, rowi)
                sg = _sig(gate)
                return dav * gate * sg, dav * val * (sg * (1.0 + gate * (1.0 - sg)))

            def step(j, c, cs=cs, cs_g=cs_g, wv=wv, wg=wg, conv_grads=conv_grads):
                xpv, xpg, dvp, dgp = c[:4]
                rows = _chunk_rows(j)
                xv, xg = mv[rows, cs], mg[rows, cs]
                dv, dg = conv_grads(xv, xg, xpv, xpg, dam[rows, cs])
                prow = _chunk_rows(jnp.maximum(j - 1, 0))
                dup_ref[prow, cs] = _conv3_t_chunk(wv, dvp, dv, rowi).astype(BF16)
                dup_ref[prow, cs_g] = _conv3_t_chunk(wg, dgp, dg, rowi).astype(BF16)
                accs = (c[4] + dv * _shift_down(xv, xpv, 2, rowi), c[5] + dv * _shift_down(xv, xpv, 1, rowi), c[6] + dv * xv,
                        c[7] + dg * _shift_down(xg, xpg, 2, rowi), c[8] + dg * _shift_down(xg, xpg, 1, rowi), c[9] + dg * xg)
                return (xv, xg, dv, dg) + accs

            init = (jnp.where(first, zero, pv[:, cs]), jnp.where(first, zero, pg[:, cs]), zero, zero) + (zero,) * 6
            c = lax.fori_loop(0, n // 2, lambda j, c, step=step: step(2 * j + 1, step(2 * j, c)), init)
            dv_n, dg_n = conv_grads(nv[:, cs], ng[:, cs], c[0], c[1], jnp.where(last, zero, dan[:, cs]))
            dup_ref[tm - FC:tm, cs] = _conv3_t_chunk(wv, c[2], dv_n, rowi).astype(BF16)
            dup_ref[tm - FC:tm, cs_g] = _conv3_t_chunk(wg, c[3], dg_n, rowi).astype(BF16)
            for k in range(SC_W):
                dw_ref[k:k + 1, cs] += _colsum(c[4 + k])
                dw_ref[k:k + 1, cs_g] += _colsum(c[7 + k])

    r = tm // FC
    da_next = pl.BlockSpec((FC, D_FF), lambda i: (jnp.minimum((i + 1) * r, s // FC - 1), 0))
    return pl.pallas_call(
        body, name="ffn_bwd", grid=(nt,),
        in_specs=[_rows(tm, D_FF, 0), da_next, pv_s, mv_s, nv_s, pg_s, mg_s, ng_s, wv_s, wg_s],
        out_specs=[_rows(tm, 2 * D_FF, 0), _full((8, 2 * D_FF))],
        out_shape=[jax.ShapeDtypeStruct((s, 2 * D_FF), BF16), jax.ShapeDtypeStruct((8, 2 * D_FF), F32)],
        compiler_params=_params(("arbitrary",)),
    )(da, da, up, up, up, up, up, up, w8, w8)


def _loss_head(y, target):
    s = y.shape[0]
    tm = _tile(s, TM)

    def body(y_ref, t_ref, dy_ref, l_ref):
        err = y_ref[...] - t_ref[...]
        dy_ref[...] = err * (1.0 / D)
        part = 0.5 * jnp.sum(jnp.mean(err * err, axis=-1, keepdims=True), axis=0, keepdims=True)

        @pl.when(pl.program_id(0) == 0)
        def _():
            l_ref[...] = jnp.zeros_like(l_ref)
        l_ref[...] += part

    return pl.pallas_call(
        body, name="loss_head", grid=(s // tm,),
        in_specs=[_rows(tm, D, 0)] * 2, out_specs=[_rows(tm, D, 0), _full((8, LANE))],
        out_shape=[jax.ShapeDtypeStruct((s, D), F32), jax.ShapeDtypeStruct((8, LANE), F32)],
        compiler_params=_params(("arbitrary",)),
    )(y, target)


def _silu_rows(c_all):
    def body(c_ref, o_ref):
        cv = c_ref[...]
        o_ref[...] = jnp.concatenate([cv * _sig(cv), jnp.zeros((N_DEV, D), F32)], axis=0).astype(BF16)

    return pl.pallas_call(
        body, name="silu_rows", grid=(1,), in_specs=[_full((N_DEV, D))], out_specs=_full((2 * N_DEV, D)),
        out_shape=jax.ShapeDtypeStruct((2 * N_DEV, D), BF16), compiler_params=_params(("arbitrary",)),
    )(c_all)


GRAD_ROWS = 512


def _sum_parts(p_ref, n):
    g = p_ref[0].astype(F32)
    for j in range(1, n):
        g = g + p_ref[j].astype(F32)
    return g


def _adamw(parts, w, m, v, name):
    n = parts.shape[0]
    r, c = w.shape
    tr = GRAD_ROWS
    assert r % tr == 0 and parts.shape[2] == c, (parts.shape, w.shape)

    def body(p_ref, w_ref, m_ref, v_ref, g_out, d_out, m_out, v_out):
        g = _sum_parts(p_ref, n)
        mn = ADAM_B1 * m_ref[...] + (1.0 - ADAM_B1) * g
        vn = ADAM_B2 * v_ref[...] + (1.0 - ADAM_B2) * (g * g)
        m_hat = mn / (1.0 - ADAM_B1 ** ADAM_STEP)
        v_hat = vn / (1.0 - ADAM_B2 ** ADAM_STEP)
        g_out[...] = g
        d_out[...] = -ADAM_LR * (m_hat / (jnp.sqrt(v_hat) + ADAM_EPS) + ADAM_WD * w_ref[...])
        m_out[...] = mn
        v_out[...] = vn

    blk = pl.BlockSpec((tr, c), lambda i: (i, 0))
    return pl.pallas_call(
        body, name=name, grid=(r // tr,),
        in_specs=[pl.BlockSpec((n, tr, c), lambda i: (0, i, 0)), blk, blk, blk], out_specs=[blk] * 4,
        out_shape=[jax.ShapeDtypeStruct((r, c), F32)] * 4, compiler_params=_params(("parallel",)),
    )(parts, w, m, v)


def _sum8(parts, name):
    n, r, c = parts.shape
    tr = GRAD_ROWS
    assert r % tr == 0, r

    def body(p_ref, g_out):
        g_out[...] = _sum_parts(p_ref, n)

    return pl.pallas_call(
        body, name=name, grid=(r // tr,),
        in_specs=[pl.BlockSpec((n, tr, c), lambda i: (0, i, 0))], out_specs=pl.BlockSpec((tr, c), lambda i: (i, 0)),
        out_shape=jax.ShapeDtypeStruct((r, c), F32), compiler_params=_params(("parallel",)),
    )(parts)


def _peers():
    ix, iy, ic = lax.axis_index("x"), lax.axis_index("y"), lax.axis_index("c")
    me = 4 * ix + 2 * iy + ic
    out = []
    for k in range(1, N_DEV):
        px = 1 - ix if (k >> 2) & 1 else ix
        py = 1 - iy if (k >> 1) & 1 else iy
        pc = 1 - ic if k & 1 else ic
        out.append(((px, py, pc), 4 * px + 2 * py + pc))
    return me, out


_HBM = pl.BlockSpec(memory_space=pltpu.HBM)


def _exchange(x, name, gather):
    shape = ((N_DEV,) + x.shape) if gather else x.shape

    def body(x_ref, o_ref, send_sems, recv_sems, local_sem):
        me, peers = _peers()
        src_of = (lambda p: x_ref) if gather else (lambda p: x_ref.at[p])
        local = pltpu.make_async_copy(src_of(me), o_ref.at[me], local_sem)
        local.start()
        sends = []
        for k, (dev, p) in enumerate(peers):
            cp = pltpu.make_async_remote_copy(src_ref=src_of(p), dst_ref=o_ref.at[me], send_sem=send_sems.at[k],
                                              recv_sem=recv_sems.at[k], device_id=dev, device_id_type=MESH)
            cp.start()
            sends.append(cp)
        for k, (dev, p) in enumerate(peers):
            pltpu.make_async_remote_copy(src_ref=src_of(p), dst_ref=o_ref.at[p], send_sem=send_sems.at[k],
                                         recv_sem=recv_sems.at[k], device_id=dev, device_id_type=MESH).wait_recv()
        for cp in sends:
            cp.wait_send()
        local.wait()

    return pl.pallas_call(
        body, name=name, in_specs=[_HBM], out_specs=_HBM, out_shape=jax.ShapeDtypeStruct(shape, x.dtype),
        scratch_shapes=[pltpu.SemaphoreType.DMA((N_DEV - 1,)), pltpu.SemaphoreType.DMA((N_DEV - 1,)),
                        pltpu.SemaphoreType.DMA],
    )(x)


def _gather_two_level(x, name):
    def body(x_ref, o_ref, send_sems, recv_sems, local_sem):
        ix, iy, ic = lax.axis_index("x"), lax.axis_index("y"), lax.axis_index("c")
        me, sibling = (ix, iy, ic), (ix, iy, 1 - ic)
        chips = [(1 - ix, iy), (ix, 1 - iy), (1 - ix, 1 - iy)]

        def slot(px, py, pc):
            return o_ref.at[4 * px + 2 * py + pc]

        def copy(k, block, to, src=None):
            return pltpu.make_async_remote_copy(src_ref=slot(*block) if src is None else src, dst_ref=slot(*block),
                                                send_sem=send_sems.at[k], recv_sem=recv_sems.at[k],
                                                device_id=to, device_id_type=MESH)

        mine = pltpu.make_async_copy(x_ref, slot(*me), local_sem)
        mine.start()
        first = [copy(0, me, sibling, src=x_ref)] + [copy(1 + j, me, (*chip, ic), src=x_ref) for j, chip in enumerate(chips)]
        for cp in first:
            cp.start()
        passed = [copy(4 + j, (*chip, ic), sibling) for j, chip in enumerate(chips)]
        for j, chip in enumerate(chips):
            copy(1 + j, (*chip, ic), me).wait_recv()
            passed[j].start()
        copy(0, sibling, me).wait_recv()
        for j, chip in enumerate(chips):
            copy(4 + j, (*chip, 1 - ic), me).wait_recv()
        for cp in first + passed:
            cp.wait_send()
        mine.wait()

    return pl.pallas_call(
        body, name=name, in_specs=[_HBM], out_specs=_HBM, out_shape=jax.ShapeDtypeStruct((N_DEV,) + x.shape, x.dtype),
        scratch_shapes=[pltpu.SemaphoreType.DMA((N_DEV - 1,)), pltpu.SemaphoreType.DMA((N_DEV - 1,)),
                        pltpu.SemaphoreType.DMA],
    )(x)


def _pack_rows(arrs, row_mult):
    flat = jnp.concatenate([a.reshape(-1) for a in arrs])
    n = flat.shape[0]
    pad = (-n) % (LANE * row_mult)
    if pad:
        flat = jnp.concatenate([flat, jnp.zeros((pad,), flat.dtype)])
    return flat.reshape(-1, LANE)


def _pack_cols(arrs, row_mult):
    flat = jnp.concatenate(arrs, axis=1)
    n = flat.shape[1]
    pad = (-n) % (LANE * row_mult)
    if pad:
        flat = jnp.concatenate([flat, jnp.zeros((flat.shape[0], pad), flat.dtype)], axis=1)
    return flat.reshape(flat.shape[0], -1, LANE)


def _unpack(flat, shapes):
    out, off = [], 0
    lead = flat.shape[:-1]
    for shp in shapes:
        n = 1
        for d_ in shp:
            n *= d_
        out.append(flat[..., off:off + n].reshape(lead + tuple(shp)))
        off += n
    return out


BIG = (("w_conv_out", 2), ("w_sc_out", 2), ("w_uq", 2), ("w_ukv", 2), ("w_mla_out", 2),
       ("w_pool_out", 2), ("w_o", 1), ("w_down", 1))
TAPS = (("conv_dw", 2), ("sc_dw", 2), ("ffn_dw", 2))
SMALL = ("b_ada", "b_in", "conv_ln_g", "conv_ln_b", "q_norm_g", "kv_norm_g", "w_pool", "pool_scale",
         "ln1_g", "ln1_b", "ln2_g", "ln2_b")
WEIGHTS = ("w_ada", "b_ada", "w_in", "b_in", "conv_dw", "conv_ln_g", "conv_ln_b", "w_conv_out", "sc_dw", "w_sc_out",
           "q_norm_g", "w_uq", "kv_norm_g", "w_ukv", "w_mla_out", "w_pool", "pool_scale", "w_pool_out", "w_o",
           "ln1_g", "ln1_b", "w_up", "ffn_dw", "w_down", "ln2_g", "ln2_b")


def _join(g, axis):
    g = jnp.moveaxis(g, 0, axis)
    shp = list(g.shape)
    shp[axis:axis + 2] = [shp[axis] * shp[axis + 1]]
    return g.reshape(shp)


def _split(full, axis):
    shp = list(full.shape)
    shp[axis:axis + 1] = [N_DEV, shp[axis] // N_DEV]
    return jnp.moveaxis(full.reshape(shp), axis, 0)


def _pad_rows(a, rows):
    return jnp.concatenate([a, jnp.zeros((rows - a.shape[0],) + a.shape[1:], a.dtype)], axis=0)


def _proj_cols(w):
    z = lambda n: jnp.zeros(w.shape[:-1] + (n,), w.dtype)
    return jnp.concatenate([w[..., 3488:7584], w[..., 0:2944], z(NOPE), w[..., 2944:2976], z(HEAD_PAD - QK_DIM),
                            w[..., 2976:3488]], axis=-1)


def _proj_cols_inv(w):
    return jnp.concatenate([w[..., CA0:KR0], w[..., KR0 + NOPE:KR0 + QK_DIM], w[..., PU0:NPROJ], w[..., 0:CA0]], axis=-1)


IN_COLS = 7584
IN_SHARD = IN_COLS // N_DEV
IN_SHARD_PAD = 960
UP_SHARD = 2 * D_FF // N_DEV
_IN_CUTS = (2944, 2976, 3488)
_IN_SEGS = ((3488, IN_COLS), (0, 2944), NOPE, (2944, 2976), HEAD_PAD - QK_DIM, (2976, 3488))


def _proj_rows(g):
    pieces = []
    for seg in _IN_SEGS:
        if isinstance(seg, int):
            pieces.append(jnp.zeros((g.shape[0], seg, g.shape[2]), g.dtype))
            continue
        j = seg[0]
        while j < seg[1]:
            dev = j // IN_SHARD
            e = min(seg[1], (dev + 1) * IN_SHARD)
            r0 = dev * IN_SHARD_PAD + j - dev * IN_SHARD
            pieces.append(g[:, r0:r0 + e - j])
            j = e
    return jnp.concatenate(pieces, axis=1)


def _aligned_row(j):
    if j < 2944:
        return CA0 + j
    if j < 2976:
        return KR0 + NOPE + j - 2944
    if j < 3488:
        return PU0 + j - 2976
    return j - 3488


def _proj_rows_split(gt):
    out = []
    for dev in range(N_DEV):
        j0, j1 = dev * IN_SHARD, (dev + 1) * IN_SHARD
        cuts = [j0] + [c for c in _IN_CUTS if j0 < c < j1] + [j1]
        out.append(jnp.concatenate([gt[_aligned_row(a):_aligned_row(a) + b - a] for a, b in zip(cuts[:-1], cuts[1:])], axis=0))
    return jnp.stack(out)


def _pad_axis(a, axis, size):
    shp = list(a.shape)
    shp[axis] = size - shp[axis]
    return jnp.concatenate([a, jnp.zeros(shp, a.dtype)], axis=axis)


def _all_layer_weights(full, small):
    w = {}
    w["w_inT"] = _proj_rows(full["w_inT"])
    w["b_in"] = _proj_cols(small["b_in"])[:, None, :]
    w["conv_w"] = _pad_axis(full["conv_dw"], 1, 32)
    w["sc_w"] = _pad_axis(full["sc_dw"], 1, 8)
    w["ffn_w"] = _pad_axis(full["ffn_dw"], 1, 8)
    w["w_uq"] = _pad_axis(full["w_uq"].reshape(DEPTH, 256, HEADS, QK_DIM), 3, HEAD_PAD).reshape(DEPTH, 256, -1)
    w["w_ukv"] = _pad_axis(full["w_ukv"].reshape(DEPTH, 128, HEADS, 2, NOPE), 4, HEAD_PAD).reshape(DEPTH, 128, -1)
    w["w_c"] = _pad_axis(full["w_mla_out"].reshape(DEPTH, HEADS, NOPE, D), 2, HEAD_PAD).reshape(DEPTH, HEADS * HEAD_PAD, D)
    w["w_a"], w["w_b"], w["w_d"] = full["w_conv_out"], full["w_sc_out"], full["w_pool_out"]
    w["w_o"], w["w_upT"], w["w_down"] = full["w_o"], full["w_upT"], full["w_down"]
    w["w_pool"] = small["w_pool"].astype(BF16)
    for n in ("conv_ln_g", "conv_ln_b", "q_norm_g", "kv_norm_g", "pool_scale"):
        w[n] = small[n][:, None, :]
    return w


def _prm(rows):
    z = jnp.zeros((D,), F32)
    rows = list(rows) + [z] * (8 - len(rows))
    return jnp.stack(rows)


def _layer_fwd(x, h1, w, prm1, prm2, tabs):
    ct, s1t, s2t = tabs
    proj = _mm(h1, w["w_inT"], "nt", "proj_fwd", bias=w["b_in"], tm=1024, tn=768)
    yconv, za = _conv_a_fwd(proj, w["conv_w"], w["conv_ln_g"], w["conv_ln_b"])
    zb = _sc_fwd(proj, w["sc_w"])
    qn, kvn = _lat_fwd(proj, w["q_norm_g"], w["kv_norm_g"])
    qp = _mm(qn, w["w_uq"], "nn", "uq_fwd")
    kvp = _mm(kvn, w["w_ukv"], "nn", "ukv_fwd")
    q, k, v = _rope_fwd(qp, kvp, proj, ct, s1t, s2t)
    o, lse = _attn_fwd(q, k, v)
    zd = _pool_fwd(proj, w["w_pool"], w["pool_scale"])
    merged = _merge_fwd(za, zb, o, zd, proj, w["w_a"], w["w_b"], w["w_c"], w["w_d"])
    mix, x1, h2 = _mm_res_ln(merged, w["w_o"], x, prm1, "o_res_ln")
    up = _mm(h2, w["w_upT"], "nt", "up_fwd", tm=1024, tn=1408)
    a = _ffn_fwd(up, w["ffn_w"])
    ffn, x2, h_next = _mm_res_ln(a, w["w_down"], x1, prm2, "down_res_ln")
    res = dict(x=x, h1=h1, proj=proj, yconv=yconv, za=za, zb=zb, qn=qn, kvn=kvn, q=q, k=k, v=v, o=o, lse=lse, zd=zd,
               merged=merged, mix=mix, x1=x1, h2=h2, up=up, a=a, ffn=ffn)
    return x2, h_next, res


def _layer_bwd(dres_next, dh_next, r, w, prm1, prm2, tabs):
    ct, s1t, s2t = tabs
    s = r["x"].shape[0]
    g = {}
    dres2, dffn, acc2 = _ln_res_bwd(dres_next, dh_next, r["x1"], r["ffn"], prm2)
    da = _mm(dffn, w["w_down"], "nt", "down_bwd_x", tm=1024, tn=1408)
    g["w_down"] = _mm(r["a"], dffn, "tn", "down_bwd_w", tm=1408)
    dup, dffn_w = _ffn_bwd(da, r["up"], w["ffn_w"])
    g["ffn_dw"] = dffn_w[:SC_W]
    dh2 = _mm(dup, w["w_upT"], "nn", "up_bwd_x", tm=1024, tk=1408)
    g["w_upT"] = _mm(dup, r["h2"], "tn", "up_bwd_w", tm=1408)
    dres1, dmix, acc1 = _ln_res_bwd(dres2, dh2, r["x"], r["mix"], prm1)
    dmerged = _mm(dmix, w["w_o"], "nt", "o_bwd_x")
    g["w_o"] = _mm(r["merged"], dmix, "tn", "o_bwd_w")
    (dya, dyb, dyc, dyd, dza, dzb, d_o, dzd, dgates, accg) = _merge_bwd(
        dmerged, r["za"], r["zb"], r["o"], r["zd"], r["proj"], w["w_a"], w["w_b"], w["w_c"], w["w_d"])
    g["w_conv_out"] = _mm(r["za"], dya, "tn", "branch_bwd_w")
    g["w_sc_out"] = _mm(r["zb"], dyb, "tn", "branch_bwd_w")
    g["w_pool_out"] = _mm(r["zd"], dyd, "tn", "branch_bwd_w")
    gwc = _mm(r["o"], dyc, "tn", "mla_out_bwd_w")
    g["w_mla_out"] = gwc.reshape(HEADS, HEAD_PAD, D)[:, :NOPE].reshape(HEADS * NOPE, D)
    dyconv, dconv_w, acca = _conv_a_bwd1(dza, r["yconv"], r["proj"], w["conv_ln_g"], w["conv_ln_b"])
    g["conv_dw"], g["conv_ln_g"], g["conv_ln_b"] = dconv_w[:CONV_W], acca[0], acca[1]
    d_ca, d_cb, acca2 = _conv_a_bwd2(dyconv, r["proj"], w["conv_w"])
    dconv, d_bg, dsc_w, accb1 = _sc_bwd1(dzb, r["proj"], w["sc_w"])
    g["sc_dw"] = dsc_w[:SC_W]
    d_cg, d_sx, accb2 = _sc_bwd2(dconv, r["proj"], w["sc_w"])
    delta, dob = _attn_prep(d_o, r["o"])
    dq, dkv = _attn_bwd(r["q"], r["k"], r["v"], dob, r["lse"], delta)
    dqp, d_kr, acckr = _rope_bwd(dq, dkv, ct, s1t, s2t)
    dqn = _mm(dqp, w["w_uq"], "nt", "uq_bwd_x")
    guq = _mm(r["qn"], dqp, "tn", "uq_bwd_w")
    g["w_uq"] = guq.reshape(256, HEADS, HEAD_PAD)[:, :, :QK_DIM].reshape(256, HEADS * QK_DIM)
    dkvn = _mm(dkv, w["w_ukv"], "nt", "ukv_bwd_x")
    gukv = _mm(r["kvn"], dkv, "tn", "ukv_bwd_w")
    g["w_ukv"] = gukv.reshape(128, HEADS, 2, HEAD_PAD)[..., :NOPE].reshape(128, HEADS * 2 * NOPE)
    d_ql, d_kvl, accq, acckv = _lat_bwd(dqn, dkvn, r["proj"], w["q_norm_g"], w["kv_norm_g"])
    g["q_norm_g"], g["kv_norm_g"] = accq[0], acckv[0]
    dpd, g["w_pool"], accd1 = _pool_bwd1(dzd, r["proj"], w["w_pool"], w["pool_scale"])
    g["pool_scale"] = accd1[0]
    d_pu, accd2 = _pool_bwd2(dpd, s)
    dproj = jnp.concatenate([dgates, d_ca, d_cb, d_bg, d_cg, d_sx, d_ql, d_kvl, d_kr, d_pu], axis=1)
    db = jnp.concatenate([accg[0], acca2[0], acca2[1], accb1[0], accb2[0], accb2[1], accq[1], acckv[1], acckr[0], accd2[0]])
    g["b_in"] = _proj_cols_inv(db)
    dh1 = _mm(dproj, w["w_inT"], "nn", "proj_bwd_x", tm=1024, tk=1536)
    g["w_inT"] = _proj_rows_split(_mm(dproj, r["h1"], "tn", "proj_bwd_w", tm=768))
    g["ln1_g"], g["ln1_b"], g["ln2_g"], g["ln2_b"] = acc1[3], acc1[4], acc2[3], acc2[4]
    return dres1, dh1, g, (acc1, acc2)


def _rope_tables(positions):
    half = ROPE // 2
    inv = 1.0 / (ROPE_THETA ** (jnp.arange(0, ROPE, 2, dtype=F32) / ROPE))
    ang = positions.astype(F32)[:, None] * inv
    cos, sin = jnp.cos(ang), jnp.sin(ang)
    s = positions.shape[0]
    z = lambda n: jnp.zeros((s, n), F32)
    ct = jnp.concatenate([jnp.ones((s, NOPE), F32), cos, cos, z(HEAD_PAD - QK_DIM)], axis=1)
    s1t = jnp.concatenate([z(NOPE), -sin, z(half), z(HEAD_PAD - QK_DIM)], axis=1)
    s2t = jnp.concatenate([z(NOPE), z(half), sin, z(HEAD_PAD - QK_DIM)], axis=1)
    return ct, s1t, s2t


def _local_step(x, mod, positions, full, small, target):
    tabs = _rope_tables(positions)
    stacked = _all_layer_weights(full, small)
    ws = [{n: a[l] for n, a in stacked.items()} for l in range(DEPTH)]
    zero = jnp.zeros((D,), F32)
    prm1s, prm2s = [], []
    for l in range(DEPTH):
        sh1, sc1, g1, sh2, sc2, g2 = (mod[l, j] for j in range(6))
        nxt = (mod[l + 1, 1], mod[l + 1, 0]) if l + 1 < DEPTH else (zero, zero)
        prm1s.append(_prm([g1, sc2, sh2, small["ln1_g"][l], small["ln1_b"][l]]))
        prm2s.append(_prm([g2, nxt[0], nxt[1], small["ln2_g"][l], small["ln2_b"][l]]))
    prm0 = _prm([zero, mod[0, 1], mod[0, 0]])
    h = _mod_fwd(x, prm0)
    res = []
    xc = x
    for l in range(DEPTH):
        xc, h, r = _layer_fwd(xc, h, ws[l], prm1s[l], prm2s[l], tabs)
        res.append(r)
    dy, lacc = _loss_head(xc, target)
    loss = lacc[0, 0]
    dres, dh = dy, jnp.zeros_like(dy)
    grads = [None] * DEPTH
    accs = [None] * DEPTH
    for l in reversed(range(DEPTH)):
        dres, dh, grads[l], accs[l] = _layer_bwd(dres, dh, res[l], ws[l], prm1s[l], prm2s[l], tabs)
    dx, acc0 = _mod_bwd(dres, dh, x, prm0)
    dmod = []
    for l in range(DEPTH):
        acc1, acc2 = accs[l]
        dsc1, dsh1 = (acc0[0], acc0[1]) if l == 0 else (accs[l - 1][1][1], accs[l - 1][1][2])
        dmod.append(jnp.stack([dsh1, dsc1, acc1[0], acc1[2], acc1[1], acc2[0]]))
    return loss, dx, grads, jnp.stack(dmod)


ADA_SHARD = 6 * D // N_DEV


def _step(p):
    me = 4 * lax.axis_index("x") + 2 * lax.axis_index("y") + lax.axis_index("c")
    x, target, positions = p["x"][0], p["loss_target"][0], p["positions"][0]

    tap_shapes = [p[n].shape for n, _ in TAPS] + [(D,)]
    small_g = _exchange(_pack_rows([p[n] for n, _ in TAPS] + [p["c"][0]], 8), "gather_taps", True)
    parts = _unpack(small_g.reshape(N_DEV, -1), tap_shapes)
    full = {n: _join(g, ax) for (n, ax), g in zip(TAPS, parts[:-1])}
    c_all = parts[-1]
    w_in_t = jnp.swapaxes(p["w_in"], 1, 2).astype(BF16)
    w_in_t = jnp.concatenate([w_in_t, jnp.zeros((DEPTH, IN_SHARD_PAD - IN_SHARD, D), BF16)], axis=1)
    w_up_t = jnp.swapaxes(p["w_up"], 1, 2).astype(BF16)
    big_shapes = [p[n].shape for n, _ in BIG] + [w_in_t.shape, w_up_t.shape]
    big_g = _gather_two_level(_pack_rows([p[n].astype(BF16) for n, _ in BIG] + [w_in_t, w_up_t], 16), "gather_weights")
    big_parts = _unpack(big_g.reshape(N_DEV, -1), big_shapes)
    for (n, ax), g in zip(BIG, big_parts):
        full[n] = _join(g, ax)
    full["w_inT"] = _join(big_parts[-2], 1)
    full["w_upT"] = _join(big_parts[-1], 1)
    small = {n: p[n] for n in SMALL}

    c_act = _silu_rows(c_all)
    w_ada_cols = jnp.moveaxis(p["w_ada"], 0, 1).reshape(D, DEPTH * ADA_SHARD)
    b_shard = lax.dynamic_slice_in_dim(p["b_ada"], me * ADA_SHARD, ADA_SHARD, axis=1).reshape(1, DEPTH * ADA_SHARD)
    mod_sh = _mm(c_act, w_ada_cols, "nn", "ada_fwd", bias=b_shard)[:N_DEV]
    mod_x = _exchange(_pack_cols([mod_sh], 8), "scatter_mod", False)
    mod = mod_x.reshape(N_DEV, -1)[:, :DEPTH * ADA_SHARD].reshape(N_DEV, DEPTH, ADA_SHARD)
    mod = jnp.moveaxis(mod, 0, 1).reshape(DEPTH, 6, D)

    loss_local, dx, grads, dmod = _local_step(x, mod, positions, full, small, target)
    loss = lax.psum(loss_local, ("x", "y", "c"))
    gfull = {n: jnp.stack([grads[l][n] for l in range(DEPTH)]) for n in grads[0]}

    out = {"loss": loss, "grad_x": dx[None]}

    def emit(names, g, dlt, mn, vn, shapes):
        for n, gi, di, mi, vi in zip(names, _unpack(g, shapes), _unpack(dlt, shapes), _unpack(mn, shapes), _unpack(vn, shapes)):
            out["grad_" + n], out["delta_" + n], out["new_m_" + n], out["new_v_" + n] = gi, di, mi, vi

    small_parts = [dmod.reshape(DEPTH, 6 * D)] + [gfull[n] for n in SMALL[1:]]
    small_all = _exchange(_pack_rows(small_parts, GRAD_ROWS), "gather_small_grads", True)
    small_shapes = [p[n].shape for n in SMALL]
    sg, sd, sm, sv = _adamw(small_all, *[_pack_rows([p[pre + n] for n in SMALL], GRAD_ROWS) for pre in ("", "m_", "v_")],
                            name="adamw_small")
    emit(SMALL, *[t.reshape(-1) for t in (sg, sd, sm, sv)], small_shapes)

    dmod_all = small_all.reshape(N_DEV, -1)[:, :DEPTH * 6 * D].reshape(N_DEV, DEPTH, 6 * D)
    dmod_sh = lax.dynamic_slice_in_dim(dmod_all, me * ADA_SHARD, ADA_SHARD, axis=2).reshape(N_DEV, DEPTH * ADA_SHARD)
    g_ada = _mm(c_act, _pad_rows(dmod_sh, 2 * N_DEV), "tn", "ada_bwd_w")
    g_ada = jnp.moveaxis(g_ada.reshape(D, DEPTH, ADA_SHARD), 1, 0)
    ag, ad, am, av = _adamw(_pack_rows([g_ada], GRAD_ROWS)[None], *[_pack_rows([p[pre + "w_ada"]], GRAD_ROWS) for pre in ("", "m_", "v_")],
                            name="adamw_ada")
    emit(("w_ada",), *[t.reshape(-1) for t in (ag, ad, am, av)], [p["w_ada"].shape])

    shard_names = [n for n, _ in BIG + TAPS]
    pieces = [_split(gfull[n], ax).reshape(N_DEV, -1) for n, ax in BIG + TAPS]
    recv = _exchange(_pack_cols(pieces, GRAD_ROWS).astype(BF16), "scatter_grads", False)
    bg, bd, bm, bv = _adamw(recv, *[_pack_rows([p[pre + n] for n in shard_names], GRAD_ROWS) for pre in ("", "m_", "v_")],
                            name="adamw_sharded")
    emit(shard_names, *[t.reshape(-1) for t in (bg, bd, bm, bv)], [p[n].shape for n in shard_names])

    t_pieces = [jnp.moveaxis(gfull["w_inT"], 1, 0).reshape(N_DEV, -1),
                jnp.moveaxis(gfull["w_upT"].reshape(DEPTH, N_DEV, UP_SHARD, D), 1, 0).reshape(N_DEV, -1)]
    g_t = _sum8(_exchange(_pack_cols(t_pieces, GRAD_ROWS).astype(BF16), "scatter_grads_t", False), "sum_grads_t")
    g_in_t, g_up_t = _unpack(g_t.reshape(-1), [(DEPTH, IN_SHARD, D), (DEPTH, UP_SHARD, D)])
    for n, gt in (("w_in", g_in_t), ("w_up", g_up_t)):
        shp = p[n].shape
        two_d = (shp[0] * shp[1], shp[2])
        res = _adamw(jnp.swapaxes(gt, 1, 2).reshape((1,) + two_d), *[p[pre + n].reshape(two_d) for pre in ("", "m_", "v_")],
                     name="adamw_" + n)
        for key, t in zip(("grad_", "delta_", "new_m_", "new_v_"), res):
            out[key + n] = t.reshape(shp)
    return out


_ARG_NAMES = ("x", "c", "positions") + WEIGHTS + ("loss_target",) + tuple("m_" + n for n in WEIGHTS) + tuple("v_" + n for n in WEIGHTS)
_OUT_NAMES = ("loss", "grad_x") + tuple(pre + n for pre in ("grad_", "delta_", "new_m_", "new_v_") for n in WEIGHTS)


def kernel(x, c, positions, w_ada, b_ada, w_in, b_in, conv_dw, conv_ln_g, conv_ln_b, w_conv_out, sc_dw, w_sc_out, q_norm_g, w_uq, kv_norm_g, w_ukv, w_mla_out, w_pool, pool_scale, w_pool_out, w_o, ln1_g, ln1_b, w_up, ffn_dw, w_down, ln2_g, ln2_b, loss_target, m_w_ada, m_b_ada, m_w_in, m_b_in, m_conv_dw, m_conv_ln_g, m_conv_ln_b, m_w_conv_out, m_sc_dw, m_w_sc_out, m_q_norm_g, m_w_uq, m_kv_norm_g, m_w_ukv, m_w_mla_out, m_w_pool, m_pool_scale, m_w_pool_out, m_w_o, m_ln1_g, m_ln1_b, m_w_up, m_ffn_dw, m_w_down, m_ln2_g, m_ln2_b, v_w_ada, v_b_ada, v_w_in, v_b_in, v_conv_dw, v_conv_ln_g, v_conv_ln_b, v_w_conv_out, v_sc_dw, v_w_sc_out, v_q_norm_g, v_w_uq, v_kv_norm_g, v_w_ukv, v_w_mla_out, v_w_pool, v_pool_scale, v_w_pool_out, v_w_o, v_ln1_g, v_ln1_b, v_w_up, v_ffn_dw, v_w_down, v_ln2_g, v_ln2_b):
    args = locals()
    out = _step({n: args[n] for n in _ARG_NAMES})
    return tuple(out[n] for n in _OUT_NAMES)
```

```python
import functools

import jax
import jax.numpy as jnp
from jax import lax
from jax.experimental import pallas as pl
from jax.experimental.pallas import tpu as pltpu

F32 = jnp.float32
BF16 = jnp.bfloat16

N_DEV = 8
DEPTH = 4
D = 1024
CONV_W = 31
HEADS = 8
HEAD_PAD = 128
QK_DIM = 96
NOPE = 64
ROPE = 32
ROPE_THETA = 10000.0
D_FF = 2816
LN_EPS = 1e-5
RMS_EPS = 1e-6
ALPHA = (2.0 * DEPTH) ** 0.25
ATT_SCALE = QK_DIM ** -0.5
POOL_WINDOWS = (2, 4, 8, 16)

ADAM_LR = 0.001
ADAM_B1 = 0.9
ADAM_B2 = 0.999
ADAM_EPS = 1e-08
ADAM_WD = 0.01
ADAM_STEP = 10

GATES0 = 0
CA0 = 4096
CB0 = 4608
SBG0 = 5120
SCG0 = 5632
SX0 = 6144
QL0 = 6656
KVL0 = 6912
KR0 = 7040
PU0 = 7168
NPROJ = 7680

LANE = 128
TM = 512
TM_WIDE = 256
TQ = 1024
TQ_BWD = 512
TQ_BWD_Q = 1024
VMEM_LIMIT = 56 * 1024 * 1024

MESH = pl.DeviceIdType.MESH


def _sig(x):
    return 1.0 / (1.0 + jnp.exp(-x))


def _tile(n, pref):
    if n <= pref:
        return n
    t = (pref // LANE) * LANE
    while t >= LANE:
        if n % t == 0:
            return t
        t -= LANE
    raise ValueError(f"no lane-aligned tile for {n}")


def _params(sem):
    return pltpu.CompilerParams(dimension_semantics=sem, vmem_limit_bytes=VMEM_LIMIT)


def _full(shape):
    nd = len(shape)
    return pl.BlockSpec(shape, lambda *_: (0,) * nd)


def _rows(tm, cw, cb):
    return pl.BlockSpec((tm, cw), lambda i: (i, cb))


def _prev(tm, hb, cw, cb):
    r = tm // hb
    return pl.BlockSpec((hb, cw), lambda i: (jnp.maximum(i * r - 1, 0), cb))


def _next(tm, hb, cw, cb, s):
    r = tm // hb
    last = s // hb - 1
    return pl.BlockSpec((hb, cw), lambda i: (jnp.minimum((i + 1) * r, last), cb))


def _acc_rows(ref, first, rows):
    @pl.when(first)
    def _():
        ref[...] = jnp.zeros_like(ref)
    for r, v in enumerate(rows):
        ref[r:r + 1, :] += v


def _colsum(v):
    return jnp.sum(v, axis=0, keepdims=True)


def _ln_stats(r):
    mu = jnp.mean(r, axis=-1, keepdims=True)
    xc = r - mu
    var = jnp.mean(xc * xc, axis=-1, keepdims=True)
    rstd = lax.rsqrt(var + LN_EPS)
    return xc * rstd, rstd


def _ln_bwd(dxh, xh, rstd):
    return rstd * (dxh - jnp.mean(dxh, axis=-1, keepdims=True) - xh * jnp.mean(dxh * xh, axis=-1, keepdims=True))


_DIMS = {"nn": ((1,), (0,)), "nt": ((1,), (1,)), "tn": ((0,), (0,))}


def _mm(a, b, mode, name, *, out_dtype=F32, bias=None, tm=512, tn=1024, tk=2048):
    if mode == "nn":
        (m, k), (k2, n) = a.shape, b.shape
    elif mode == "nt":
        (m, k), (n, k2) = a.shape, b.shape
    else:
        (k, m), (k2, n) = a.shape, b.shape
    assert k == k2, (a.shape, b.shape, mode)
    tm, tn, tk = _tile(m, tm), _tile(n, tn), _tile(k, tk)
    nk = k // tk
    dims = (_DIMS[mode], ((), ()))
    has_bias = bias is not None

    def body(*refs):
        if has_bias:
            a_ref, b_ref, bias_ref, o_ref = refs[:4]
        else:
            a_ref, b_ref, o_ref = refs[:3]
        p = lax.dot_general(a_ref[...].astype(BF16), b_ref[...].astype(BF16), dims, preferred_element_type=F32)

        def finish(r):
            if has_bias:
                r = r + bias_ref[...]
            o_ref[...] = r.astype(out_dtype)

        if nk == 1:
            finish(p)
        else:
            acc = refs[-1]
            kk = pl.program_id(2)

            @pl.when(kk == 0)
            def _():
                acc[...] = p

            @pl.when(kk > 0)
            def _():
                acc[...] += p

            @pl.when(kk == nk - 1)
            def _():
                finish(acc[...])

    if mode == "nn":
        a_spec = pl.BlockSpec((tm, tk), lambda i, j, kk: (i, kk))
        b_spec = pl.BlockSpec((tk, tn), lambda i, j, kk: (kk, j))
    elif mode == "nt":
        a_spec = pl.BlockSpec((tm, tk), lambda i, j, kk: (i, kk))
        b_spec = pl.BlockSpec((tn, tk), lambda i, j, kk: (j, kk))
    else:
        a_spec = pl.BlockSpec((tk, tm), lambda i, j, kk: (kk, i))
        b_spec = pl.BlockSpec((tk, tn), lambda i, j, kk: (kk, j))
    in_specs = [a_spec, b_spec]
    args = [a, b]
    if has_bias:
        in_specs.append(pl.BlockSpec((1, tn), lambda i, j, kk: (0, j)))
        args.append(bias)
    return pl.pallas_call(
        body, name=name, grid=(m // tm, n // tn, nk),
        in_specs=in_specs, out_specs=pl.BlockSpec((tm, tn), lambda i, j, kk: (i, j)),
        out_shape=jax.ShapeDtypeStruct((m, n), out_dtype),
        scratch_shapes=[pltpu.VMEM((tm, tn), F32)] if nk > 1 else [],
        compiler_params=_params(("parallel", "parallel", "arbitrary")),
    )(*args)


def _mod_fwd(x, prm):
    s = x.shape[0]
    tm = _tile(s, TM)

    def body(x_ref, p_ref, h_ref):
        h_ref[...] = (x_ref[...] * (1.0 + p_ref[1:2, :]) + p_ref[2:3, :]).astype(BF16)

    return pl.pallas_call(
        body, name="mod_fwd", grid=(s // tm,),
        in_specs=[_rows(tm, D, 0), _full((8, D))], out_specs=_rows(tm, D, 0),
        out_shape=jax.ShapeDtypeStruct((s, D), BF16), compiler_params=_params(("parallel",)),
    )(x, prm)


def _mod_bwd(dres, dh, x, prm):
    s = x.shape[0]
    tm = _tile(s, TM)

    def body(dres_ref, dh_ref, x_ref, p_ref, dx_ref, acc_ref):
        dh_v = dh_ref[...]
        dx_ref[...] = dres_ref[...] + dh_v * (1.0 + p_ref[1:2, :])
        _acc_rows(acc_ref, pl.program_id(0) == 0, [_colsum(dh_v * x_ref[...]), _colsum(dh_v)])

    return pl.pallas_call(
        body, name="mod_bwd", grid=(s // tm,),
        in_specs=[_rows(tm, D, 0)] * 3 + [_full((8, D))],
        out_specs=[_rows(tm, D, 0), _full((8, D))],
        out_shape=[jax.ShapeDtypeStruct((s, D), F32), jax.ShapeDtypeStruct((8, D), F32)],
        compiler_params=_params(("arbitrary",)),
    )(dres, dh, x, prm)


CA_HALO = 32
C512 = 512


def _glu_buf(buf, ap, am, bp, bm, first, tm):
    glu_p = ap[...] * _sig(bp[...])
    buf[0:CA_HALO, :] = jnp.where(first, jnp.zeros_like(glu_p), glu_p)
    buf[CA_HALO:CA_HALO + tm, :] = am[...] * _sig(bm[...])


def _conv31(w_ref, buf, zs, tm, offs):
    acc = None
    for r in range(8):
        ks = [k for k in range(CONV_W) if offs[k] % 8 == r]
        if not ks:
            continue
        rows = tm if r == 0 else tm + 8
        z = None
        for k in ks:
            t = w_ref[k:k + 1, :] * buf[pl.ds(offs[k] - r, rows), :]
            z = t if z is None else z + t
        if r:
            zs[...] = z
            z = zs[pl.ds(r, tm), :]
        acc = z if acc is None else acc + z
    return acc


_CA_FWD_OFFS = tuple(CA_HALO - CONV_W + 1 + k for k in range(CONV_W))
_CA_BWD_OFFS = tuple(CONV_W - 1 - k for k in range(CONV_W))


def _conv_a_fwd(proj, w32, ln_g, ln_b):
    s = proj.shape[0]
    tm = _tile(s, TM)
    ca, cb = CA0 // C512, CB0 // C512

    def body(ap, am, bp, bm, w_ref, g_ref, b_ref, yc_ref, za_ref, buf, zs):
        _glu_buf(buf, ap, am, bp, bm, pl.program_id(0) == 0, tm)
        acc = _conv31(w_ref, buf, zs, tm, _CA_FWD_OFFS)
        yc_ref[...] = acc
        xh, _ = _ln_stats(acc)
        y = xh * g_ref[...] + b_ref[...]
        za_ref[...] = (y * _sig(y)).astype(BF16)

    return pl.pallas_call(
        body, name="conv_a_fwd", grid=(s // tm,),
        in_specs=[_prev(tm, CA_HALO, C512, ca), _rows(tm, C512, ca), _prev(tm, CA_HALO, C512, cb), _rows(tm, C512, cb),
                  _full((32, C512)), _full((1, C512)), _full((1, C512))],
        out_specs=[_rows(tm, C512, 0), _rows(tm, C512, 0)],
        out_shape=[jax.ShapeDtypeStruct((s, C512), F32), jax.ShapeDtypeStruct((s, C512), BF16)],
        scratch_shapes=[pltpu.VMEM((tm + CA_HALO, C512), F32), pltpu.VMEM((tm + 8, C512), F32)],
        compiler_params=_params(("parallel",)),
    )(proj, proj, proj, proj, w32, ln_g, ln_b)


def _conv_a_bwd1(dza, yconv, proj, ln_g, ln_b):
    s = proj.shape[0]
    tm = _tile(s, TM)
    ca, cb = CA0 // C512, CB0 // C512

    def body(dza_ref, yc_ref, ap, am, bp, bm, g_ref, b_ref, dyc_ref, dw_ref, acc_ref, buf):
        first = pl.program_id(0) == 0
        xh, rstd = _ln_stats(yc_ref[...])
        g = g_ref[...]
        y = xh * g + b_ref[...]
        sg = _sig(y)
        dy = dza_ref[...] * (sg * (1.0 + y * (1.0 - sg)))
        _acc_rows(acc_ref, first, [_colsum(dy * xh), _colsum(dy)])
        dyc = _ln_bwd(dy * g, xh, rstd)
        dyc_ref[...] = dyc
        _glu_buf(buf, ap, am, bp, bm, first, tm)

        @pl.when(first)
        def _():
            dw_ref[...] = jnp.zeros_like(dw_ref)
        for k in range(CONV_W):
            dw_ref[k:k + 1, :] += _colsum(dyc * buf[pl.ds(_CA_FWD_OFFS[k], tm), :])

    return pl.pallas_call(
        body, name="conv_a_bwd1", grid=(s // tm,),
        in_specs=[_rows(tm, C512, 0), _rows(tm, C512, 0),
                  _prev(tm, CA_HALO, C512, ca), _rows(tm, C512, ca), _prev(tm, CA_HALO, C512, cb), _rows(tm, C512, cb),
                  _full((1, C512)), _full((1, C512))],
        out_specs=[_rows(tm, C512, 0), _full((32, C512)), _full((8, C512))],
        out_shape=[jax.ShapeDtypeStruct((s, C512), F32), jax.ShapeDtypeStruct((32, C512), F32),
                   jax.ShapeDtypeStruct((8, C512), F32)],
        scratch_shapes=[pltpu.VMEM((tm + CA_HALO, C512), F32)],
        compiler_params=_params(("arbitrary",)),
    )(dza, yconv, proj, proj, proj, proj, ln_g, ln_b)


def _conv_a_bwd2(dyc, proj, w32):
    s = proj.shape[0]
    tm = _tile(s, TM)
    ca, cb = CA0 // C512, CB0 // C512
    nt = s // tm

    def body(dm, dn, am, bm, w_ref, da_ref, db_ref, acc_ref, buf, zs):
        i = pl.program_id(0)
        buf[0:tm, :] = dm[...]
        nxt = dn[...]
        buf[tm:tm + CA_HALO, :] = jnp.where(i == nt - 1, jnp.zeros_like(nxt), nxt)
        dglu = _conv31(w_ref, buf, zs, tm, _CA_BWD_OFFS)
        sb = _sig(bm[...])
        da = dglu * sb
        db = dglu * am[...] * sb * (1.0 - sb)
        da_ref[...] = da.astype(BF16)
        db_ref[...] = db.astype(BF16)
        _acc_rows(acc_ref, i == 0, [_colsum(da), _colsum(db)])

    return pl.pallas_call(
        body, name="conv_a_bwd2", grid=(nt,),
        in_specs=[_rows(tm, C512, 0), _next(tm, CA_HALO, C512, 0, s), _rows(tm, C512, ca), _rows(tm, C512, cb),
                  _full((32, C512))],
        out_specs=[_rows(tm, C512, 0), _rows(tm, C512, 0), _full((8, C512))],
        out_shape=[jax.ShapeDtypeStruct((s, C512), BF16), jax.ShapeDtypeStruct((s, C512), BF16),
                   jax.ShapeDtypeStruct((8, C512), F32)],
        scratch_shapes=[pltpu.VMEM((tm + CA_HALO, C512), F32), pltpu.VMEM((tm + 8, C512), F32)],
        compiler_params=_params(("arbitrary",)),
    )(dyc, dyc, proj, proj, w32)


H8 = 8
SC_W = 3


def _sc_ubuf(buf, cp, cm, xp, xm, first, tm):
    up = cp[...] * xp[...]
    buf[0:H8, :] = jnp.where(first, jnp.zeros_like(up), up)
    buf[H8:H8 + tm, :] = cm[...] * xm[...]


def _conv3(w_ref, buf, tm):
    acc = w_ref[0:1, :] * buf[pl.ds(H8 - SC_W + 1, tm), :]
    for k in range(1, SC_W):
        acc = acc + w_ref[k:k + 1, :] * buf[pl.ds(H8 - SC_W + 1 + k, tm), :]
    return acc


def _conv3_t(w_ref, buf, tm):
    acc = w_ref[0:1, :] * buf[pl.ds(SC_W - 1, tm), :]
    for k in range(1, SC_W):
        acc = acc + w_ref[k:k + 1, :] * buf[pl.ds(SC_W - 1 - k, tm), :]
    return acc


def _sc_fwd(proj, w8):
    s = proj.shape[0]
    tm = _tile(s, TM)
    c_bg, c_cg, c_x = SBG0 // C512, SCG0 // C512, SX0 // C512

    def body(bg, cp, cm, xp, xm, w_ref, zb_ref, buf):
        _sc_ubuf(buf, cp, cm, xp, xm, pl.program_id(0) == 0, tm)
        zb_ref[...] = (bg[...] * _conv3(w_ref, buf, tm)).astype(BF16)

    return pl.pallas_call(
        body, name="sc_fwd", grid=(s // tm,),
        in_specs=[_rows(tm, C512, c_bg), _prev(tm, H8, C512, c_cg), _rows(tm, C512, c_cg),
                  _prev(tm, H8, C512, c_x), _rows(tm, C512, c_x), _full((8, C512))],
        out_specs=_rows(tm, C512, 0), out_shape=jax.ShapeDtypeStruct((s, C512), BF16),
        scratch_shapes=[pltpu.VMEM((tm + H8, C512), F32)], compiler_params=_params(("parallel",)),
    )(proj, proj, proj, proj, proj, w8)


def _sc_bwd1(dzb, proj, w8):
    s = proj.shape[0]
    tm = _tile(s, TM)
    c_bg, c_cg, c_x = SBG0 // C512, SCG0 // C512, SX0 // C512

    def body(dz_ref, bg, cp, cm, xp, xm, w_ref, dconv_ref, dbg_ref, dw_ref, acc_ref, buf):
        first = pl.program_id(0) == 0
        _sc_ubuf(buf, cp, cm, xp, xm, first, tm)
        dz = dz_ref[...]
        dbg = dz * _conv3(w_ref, buf, tm)
        dconv = dz * bg[...]
        dconv_ref[...] = dconv
        dbg_ref[...] = dbg.astype(BF16)
        _acc_rows(acc_ref, first, [_colsum(dbg)])
        _acc_rows(dw_ref, first, [_colsum(dconv * buf[pl.ds(H8 - SC_W + 1 + k, tm), :]) for k in range(SC_W)])

    return pl.pallas_call(
        body, name="sc_bwd1", grid=(s // tm,),
        in_specs=[_rows(tm, C512, 0), _rows(tm, C512, c_bg), _prev(tm, H8, C512, c_cg), _rows(tm, C512, c_cg),
                  _prev(tm, H8, C512, c_x), _rows(tm, C512, c_x), _full((8, C512))],
        out_specs=[_rows(tm, C512, 0), _rows(tm, C512, 0), _full((8, C512)), _full((8, C512))],
        out_shape=[jax.ShapeDtypeStruct((s, C512), F32), jax.ShapeDtypeStruct((s, C512), BF16),
                   jax.ShapeDtypeStruct((8, C512), F32), jax.ShapeDtypeStruct((8, C512), F32)],
        scratch_shapes=[pltpu.VMEM((tm + H8, C512), F32)], compiler_params=_params(("arbitrary",)),
    )(dzb, proj, proj, proj, proj, proj, w8)


def _sc_bwd2(dconv, proj, w8):
    s = proj.shape[0]
    tm = _tile(s, TM)
    c_cg, c_x = SCG0 // C512, SX0 // C512
    nt = s // tm

    def body(dm, dn, cm, xm, w_ref, dcg_ref, dx_ref, acc_ref, buf):
        i = pl.program_id(0)
        buf[0:tm, :] = dm[...]
        nxt = dn[...]
        buf[tm:tm + H8, :] = jnp.where(i == nt - 1, jnp.zeros_like(nxt), nxt)
        du = _conv3_t(w_ref, buf, tm)
        dcg = du * xm[...]
        dx = du * cm[...]
        dcg_ref[...] = dcg.astype(BF16)
        dx_ref[...] = dx.astype(BF16)
        _acc_rows(acc_ref, i == 0, [_colsum(dcg), _colsum(dx)])

    return pl.pallas_call(
        body, name="sc_bwd2", grid=(nt,),
        in_specs=[_rows(tm, C512, 0), _next(tm, H8, C512, 0, s), _rows(tm, C512, c_cg), _rows(tm, C512, c_x),
                  _full((8, C512))],
        out_specs=[_rows(tm, C512, 0), _rows(tm, C512, 0), _full((8, C512))],
        out_shape=[jax.ShapeDtypeStruct((s, C512), BF16), jax.ShapeDtypeStruct((s, C512), BF16),
                   jax.ShapeDtypeStruct((8, C512), F32)],
        scratch_shapes=[pltpu.VMEM((tm + H8, C512), F32)], compiler_params=_params(("arbitrary",)),
    )(dconv, dconv, proj, proj, w8)


QLAT = 256
KVLAT = 128


def _rms(x, g):
    r = lax.rsqrt(jnp.mean(x * x, axis=-1, keepdims=True) + RMS_EPS)
    return x * r * g, r


def _rms_bwd(dy, x, g, r):
    u = dy * g
    dx = r * u - x * (r * r * r) * jnp.mean(u * x, axis=-1, keepdims=True)
    return dx, _colsum(dy * x * r)


def _lat_fwd(proj, gq, gkv):
    s = proj.shape[0]
    tm = _tile(s, TM)

    def body(q_ref, kv_ref, gq_ref, gkv_ref, qn_ref, kvn_ref):
        qn_ref[...] = _rms(q_ref[...], gq_ref[...])[0].astype(BF16)
        kvn_ref[...] = _rms(kv_ref[...], gkv_ref[...])[0].astype(BF16)

    return pl.pallas_call(
        body, name="lat_fwd", grid=(s // tm,),
        in_specs=[_rows(tm, QLAT, QL0 // QLAT), _rows(tm, KVLAT, KVL0 // KVLAT), _full((1, QLAT)), _full((1, KVLAT))],
        out_specs=[_rows(tm, QLAT, 0), _rows(tm, KVLAT, 0)],
        out_shape=[jax.ShapeDtypeStruct((s, QLAT), BF16), jax.ShapeDtypeStruct((s, KVLAT), BF16)],
        compiler_params=_params(("parallel",)),
    )(proj, proj, gq, gkv)


def _lat_bwd(dqn, dkvn, proj, gq, gkv):
    s = proj.shape[0]
    tm = _tile(s, TM)

    def body(dqn_ref, dkvn_ref, q_ref, kv_ref, gq_ref, gkv_ref, dq_ref, dkv_ref, accq_ref, acckv_ref):
        first = pl.program_id(0) == 0
        q, kv = q_ref[...], kv_ref[...]
        gqv, gkvv = gq_ref[...], gkv_ref[...]
        dq, dgq = _rms_bwd(dqn_ref[...], q, gqv, _rms(q, gqv)[1])
        dkv, dgkv = _rms_bwd(dkvn_ref[...], kv, gkvv, _rms(kv, gkvv)[1])
        dq_ref[...] = dq.astype(BF16)
        dkv_ref[...] = dkv.astype(BF16)
        _acc_rows(accq_ref, first, [dgq, _colsum(dq)])
        _acc_rows(acckv_ref, first, [dgkv, _colsum(dkv)])

    return pl.pallas_call(
        body, name="lat_bwd", grid=(s // tm,),
        in_specs=[_rows(tm, QLAT, 0), _rows(tm, KVLAT, 0), _rows(tm, QLAT, QL0 // QLAT), _rows(tm, KVLAT, KVL0 // KVLAT),
                  _full((1, QLAT)), _full((1, KVLAT))],
        out_specs=[_rows(tm, QLAT, 0), _rows(tm, KVLAT, 0), _full((8, QLAT)), _full((8, KVLAT))],
        out_shape=[jax.ShapeDtypeStruct((s, QLAT), BF16), jax.ShapeDtypeStruct((s, KVLAT), BF16),
                   jax.ShapeDtypeStruct((8, QLAT), F32), jax.ShapeDtypeStruct((8, KVLAT), F32)],
        compiler_params=_params(("arbitrary",)),
    )(dqn, dkvn, proj, proj, gq, gkv)


def _rope(x, c, s1, s2):
    return x * c + pltpu.roll(x, HEAD_PAD - ROPE // 2, 1) * s1 + pltpu.roll(x, ROPE // 2, 1) * s2


def _rope_t(d, c, s1, s2):
    return d * c + pltpu.roll(d * s1, ROPE // 2, 1) + pltpu.roll(d * s2, HEAD_PAD - ROPE // 2, 1)


def _rope_fwd(qp, kvp, proj, ct, s1t, s2t):
    s = proj.shape[0]
    tm = _tile(s, TM)

    def body(q_ref, kv_ref, kr_ref, c_ref, s1_ref, s2_ref, qo, ko, vo):
        c, s1, s2 = c_ref[...], s1_ref[...], s2_ref[...]
        kr = _rope(kr_ref[...], c, s1, s2)
        for h in range(HEADS):
            cs = slice(h * HEAD_PAD, (h + 1) * HEAD_PAD)
            qo[:, cs] = _rope(q_ref[:, cs], c, s1, s2).astype(BF16)
            ko[:, cs] = (kv_ref[:, 2 * h * HEAD_PAD:(2 * h + 1) * HEAD_PAD] + kr).astype(BF16)
            vo[:, cs] = kv_ref[:, (2 * h + 1) * HEAD_PAD:(2 * h + 2) * HEAD_PAD].astype(BF16)

    tab = _rows(tm, HEAD_PAD, 0)
    return pl.pallas_call(
        body, name="rope_fwd", grid=(s // tm,),
        in_specs=[_rows(tm, HEADS * HEAD_PAD, 0), _rows(tm, 2 * HEADS * HEAD_PAD, 0), _rows(tm, HEAD_PAD, KR0 // HEAD_PAD),
                  tab, tab, tab],
        out_specs=[_rows(tm, HEADS * HEAD_PAD, 0)] * 3,
        out_shape=[jax.ShapeDtypeStruct((s, HEADS * HEAD_PAD), BF16)] * 3,
        compiler_params=_params(("parallel",)),
    )(qp, kvp, proj, ct, s1t, s2t)


def _rope_bwd(dq, dkv, ct, s1t, s2t):
    s = dq.shape[0]
    tm = _tile(s, TM)

    def body(dq_ref, dkv_ref, c_ref, s1_ref, s2_ref, dqo, dkr_ref, acc_ref):
        c, s1, s2 = c_ref[...], s1_ref[...], s2_ref[...]
        tot = dkv_ref[:, 0:HEAD_PAD]
        for h in range(HEADS):
            cs = slice(h * HEAD_PAD, (h + 1) * HEAD_PAD)
            dqo[:, cs] = _rope_t(dq_ref[:, cs], c, s1, s2).astype(BF16)
            if h:
                tot = tot + dkv_ref[:, 2 * h * HEAD_PAD:(2 * h + 1) * HEAD_PAD]
        lane = lax.broadcasted_iota(jnp.int32, (tm, HEAD_PAD), 1)
        dkr = jnp.where((lane >= NOPE) & (lane < QK_DIM), _rope_t(tot, c, s1, s2), 0.0)
        dkr_ref[...] = dkr.astype(BF16)
        _acc_rows(acc_ref, pl.program_id(0) == 0, [_colsum(dkr)])

    tab = _rows(tm, HEAD_PAD, 0)
    return pl.pallas_call(
        body, name="rope_bwd", grid=(s // tm,),
        in_specs=[_rows(tm, HEADS * HEAD_PAD, 0), _rows(tm, 2 * HEADS * HEAD_PAD, 0), tab, tab, tab],
        out_specs=[_rows(tm, HEADS * HEAD_PAD, 0), tab, _full((8, HEAD_PAD))],
        out_shape=[jax.ShapeDtypeStruct((s, HEADS * HEAD_PAD), BF16), jax.ShapeDtypeStruct((s, HEAD_PAD), BF16),
                   jax.ShapeDtypeStruct((8, HEAD_PAD), F32)],
        compiler_params=_params(("arbitrary",)),
    )(dq, dkv, ct, s1t, s2t)


_NT = (((1,), (1,)), ((), ()))
_TN = (((0,), (0,)), ((), ()))
_NN = (((1,), (0,)), ((), ()))


def _tile_rows(ref, t, tq):
    return ref[pl.ds(pl.multiple_of(t * tq, tq), tq), :]


def _attn_fwd(q, k, v):
    s = q.shape[0]
    tq = _tile(s, TQ)
    nq = s // tq

    def body(q_ref, k_ref, v_ref, o_ref, lse_ref):
        qi = pl.program_id(1)
        q_t = q_ref[...]

        def raw(ki):
            return lax.dot_general(_tile_rows(k_ref, ki, tq), q_t, _NT, preferred_element_type=F32)

        def process(ki, st, m, l, acc, masked):
            sc = st * ATT_SCALE
            if masked:
                key = lax.broadcasted_iota(jnp.int32, (tq, tq), 0)
                qry = lax.broadcasted_iota(jnp.int32, (tq, tq), 1)
                sc = jnp.where(key <= qry, sc, -jnp.inf)
            m_new = jnp.maximum(m, jnp.max(sc, axis=0, keepdims=True))
            pt = jnp.exp(sc - m_new)
            a = jnp.exp(m - m_new)
            pv = lax.dot_general(_tile_rows(v_ref, ki, tq), pt.astype(BF16), _TN, preferred_element_type=F32)
            return m_new, a * l + jnp.sum(pt, axis=0, keepdims=True), a * acc + pv

        def loop_body(ki, c):
            return process(ki, raw(ki), c[0], c[1], c[2], False)

        init = (jnp.full((1, tq), -jnp.inf, F32), jnp.zeros((1, tq), F32), jnp.zeros((HEAD_PAD, tq), F32))
        c = lax.fori_loop(0, qi, loop_body, init)
        m, l, acc = process(qi, raw(qi), c[0], c[1], c[2], True)
        o_ref[...] = jnp.transpose(acc / l)
        lse_ref[0] = jnp.broadcast_to(m + jnp.log(l), (8, tq))

    qspec = pl.BlockSpec((tq, HEAD_PAD), lambda h, qi: (qi, h))
    kspec = pl.BlockSpec((s, HEAD_PAD), lambda h, qi: (0, h))
    return pl.pallas_call(
        body, name="attn_fwd", grid=(HEADS, nq),
        in_specs=[qspec, kspec, kspec],
        out_specs=[qspec, pl.BlockSpec((1, 8, tq), lambda h, qi: (h, 0, qi))],
        out_shape=[jax.ShapeDtypeStruct((s, HEADS * HEAD_PAD), F32), jax.ShapeDtypeStruct((HEADS, 8, s), F32)],
        compiler_params=_params(("parallel", "parallel")),
    )(q, k, v)


def _attn_prep(d_o, o):
    s = o.shape[0]
    tm = _tile(s, TM)

    def body(do_ref, o_ref, dl_ref, dob_ref):
        for h in range(HEADS):
            cs = slice(h * HEAD_PAD, (h + 1) * HEAD_PAD)
            dov = do_ref[:, cs]
            row = jnp.sum(jnp.transpose(dov * o_ref[:, cs]), axis=0, keepdims=True)
            dl_ref[h] = jnp.broadcast_to(row, (8, tm))
            dob_ref[:, cs] = dov.astype(BF16)

    blk = _rows(tm, HEADS * HEAD_PAD, 0)
    return pl.pallas_call(
        body, name="attn_prep", grid=(s // tm,),
        in_specs=[blk, blk], out_specs=[pl.BlockSpec((HEADS, 8, tm), lambda i: (0, 0, i)), blk],
        out_shape=[jax.ShapeDtypeStruct((HEADS, 8, s), F32), jax.ShapeDtypeStruct((s, HEADS * HEAD_PAD), BF16)],
        compiler_params=_params(("parallel",)),
    )(d_o, o)


def _attn_bwd(q, k, v, d_o, lse, delta):
    s = q.shape[0]
    tk = _tile(s, TQ_BWD)
    tq = _tile(s, TQ_BWD_Q)
    assert tq % tk == 0
    nq = s // tq

    def body(q_ref, k_ref, v_ref, do_ref, lse_ref, dl_ref, dq_ref, dkv_ref):
        ki = pl.program_id(1)
        k_t, v_t = k_ref[...], v_ref[...]
        q0 = (ki * tk) // tq

        @pl.when(ki == 0)
        def _():
            dq_ref[...] = jnp.zeros_like(dq_ref)

        def raw(qi):
            return (lax.dot_general(k_t, _tile_rows(q_ref, qi, tq), _NT, preferred_element_type=F32),
                    lax.dot_general(v_t, _tile_rows(do_ref, qi, tq), _NT, preferred_element_type=F32))

        def process(qi, st, dpt, dk, dv, masked):
            cols = pl.ds(pl.multiple_of(qi * tq, tq), tq)
            sc = st * ATT_SCALE
            if masked:
                key = ki * tk + lax.broadcasted_iota(jnp.int32, (tk, tq), 0)
                qry = qi * tq + lax.broadcasted_iota(jnp.int32, (tk, tq), 1)
                sc = jnp.where(key <= qry, sc, -jnp.inf)
            pt = jnp.exp(sc - lse_ref[0, 0:1, cols])
            dsb = (pt * (dpt - dl_ref[0, 0:1, cols]) * ATT_SCALE).astype(BF16)
            dv = dv + lax.dot_general(pt.astype(BF16), _tile_rows(do_ref, qi, tq), _NN, preferred_element_type=F32)
            dk = dk + lax.dot_general(dsb, _tile_rows(q_ref, qi, tq), _NN, preferred_element_type=F32)
            dq_ref[cols, :] += lax.dot_general(dsb, k_t, _TN, preferred_element_type=F32)
            return dk, dv

        zero = jnp.zeros((tk, HEAD_PAD), F32)
        dk, dv = process(q0, *raw(q0), zero, zero, True)

        def loop_body(qi, c):
            st, dpt = raw(qi)
            return process(qi, st, dpt, c[0], c[1], False)

        c = lax.fori_loop(q0 + 1, nq, loop_body, (dk, dv))
        dkv_ref[:, 0:HEAD_PAD] = c[0]
        dkv_ref[:, HEAD_PAD:2 * HEAD_PAD] = c[1]

    full = pl.BlockSpec((s, HEAD_PAD), lambda h, ki: (0, h))
    tile = pl.BlockSpec((tk, HEAD_PAD), lambda h, ki: (ki, h))
    stat = pl.BlockSpec((1, 8, s), lambda h, ki: (h, 0, 0))
    return pl.pallas_call(
        body, name="attn_bwd", grid=(HEADS, s // tk),
        in_specs=[full, tile, tile, full, stat, stat],
        out_specs=[full, pl.BlockSpec((tk, 2 * HEAD_PAD), lambda h, ki: (ki, h))],
        out_shape=[jax.ShapeDtypeStruct((s, HEADS * HEAD_PAD), F32), jax.ShapeDtypeStruct((s, HEADS * 2 * HEAD_PAD), F32)],
        compiler_params=_params(("parallel", "arbitrary")),
    )(q, k, v, d_o, lse, delta)


PH = 16
PG = 128


def _pool_pd(buf, u_main_ref, g, i, tm):
    w = POOL_WINDOWS[g]
    cs = pl.ds(g * PG, PG)
    tot = buf[pl.ds(PH, tm), cs]
    for j in range(1, w):
        tot = tot + buf[pl.ds(PH - j, tm), cs]
    t = i * tm + lax.broadcasted_iota(jnp.int32, (tm, PG), 0)
    cnt = jnp.minimum(t + 1, w).astype(F32)
    return tot / cnt - u_main_ref[:, cs]


def _pool_ubuf(buf, up, um, first, tm):
    p = up[...]
    buf[0:PH, :] = jnp.where(first, jnp.zeros_like(p), p)
    buf[PH:PH + tm, :] = um[...]


def _pool_fwd(proj, w_pool, scale):
    s = proj.shape[0]
    tm = _tile(s, TM)
    cu = PU0 // C512

    def body(up, um, w_ref, sc_ref, zd_ref, buf):
        i = pl.program_id(0)
        _pool_ubuf(buf, up, um, i == 0, tm)
        for g in range(4):
            pd = _pool_pd(buf, um, g, i, tm).astype(BF16)
            e = lax.dot_general(pd, w_ref[g], _NN, preferred_element_type=F32)
            zd_ref[:, g * PG:(g + 1) * PG] = (e * sc_ref[:, g * PG:(g + 1) * PG]).astype(BF16)

    return pl.pallas_call(
        body, name="pool_fwd", grid=(s // tm,),
        in_specs=[_prev(tm, PH, C512, cu), _rows(tm, C512, cu), _full((4, PG, PG)), _full((1, C512))],
        out_specs=_rows(tm, C512, 0), out_shape=jax.ShapeDtypeStruct((s, C512), BF16),
        scratch_shapes=[pltpu.VMEM((tm + PH, C512), F32)], compiler_params=_params(("parallel",)),
    )(proj, proj, w_pool, scale)


def _pool_bwd1(dzd, proj, w_pool, scale):
    s = proj.shape[0]
    tm = _tile(s, TM)
    cu = PU0 // C512

    def body(dz_ref, up, um, w_ref, sc_ref, dpd_ref, dw_ref, acc_ref, buf):
        i = pl.program_id(0)
        first = i == 0
        _pool_ubuf(buf, up, um, first, tm)

        @pl.when(first)
        def _():
            dw_ref[...] = jnp.zeros_like(dw_ref)
            acc_ref[...] = jnp.zeros_like(acc_ref)
        for g in range(4):
            cs = slice(g * PG, (g + 1) * PG)
            pd = _pool_pd(buf, um, g, i, tm).astype(BF16)
            wg = w_ref[g]
            e = lax.dot_general(pd, wg, _NN, preferred_element_type=F32)
            dz = dz_ref[:, cs]
            acc_ref[0:1, cs] += _colsum(dz * e)
            de = (dz * sc_ref[:, cs]).astype(BF16)
            dw_ref[g] += lax.dot_general(pd, de, _TN, preferred_element_type=F32)
            dpd_ref[:, cs] = lax.dot_general(de, wg, _NT, preferred_element_type=F32)

    return pl.pallas_call(
        body, name="pool_bwd1", grid=(s // tm,),
        in_specs=[_rows(tm, C512, 0), _prev(tm, PH, C512, cu), _rows(tm, C512, cu), _full((4, PG, PG)), _full((1, C512))],
        out_specs=[_rows(tm, C512, 0), _full((4, PG, PG)), _full((8, C512))],
        out_shape=[jax.ShapeDtypeStruct((s, C512), F32), jax.ShapeDtypeStruct((4, PG, PG), F32),
                   jax.ShapeDtypeStruct((8, C512), F32)],
        scratch_shapes=[pltpu.VMEM((tm + PH, C512), F32)], compiler_params=_params(("arbitrary",)),
    )(dzd, proj, proj, w_pool, scale)


def _pool_bwd2(dpd, s):
    tm = _tile(s, TM)
    nt = s // tm

    def body(dm, dn, du_ref, acc_ref, buf):
        i = pl.program_id(0)
        buf[0:tm, :] = dm[...]
        nxt = dn[...]
        buf[tm:tm + PH, :] = jnp.where(i == nt - 1, jnp.zeros_like(nxt), nxt)
        t = i * tm + lax.broadcasted_iota(jnp.int32, (tm + PH, PG), 0)
        cols = []
        for g, w in enumerate(POOL_WINDOWS):
            cs = pl.ds(g * PG, PG)
            cnt = jnp.minimum(t + 1, w).astype(F32)
            buf[:, cs] = buf[:, cs] / cnt
        for g, w in enumerate(POOL_WINDOWS):
            cs = pl.ds(g * PG, PG)
            tot = buf[pl.ds(0, tm), cs]
            for j in range(1, w):
                tot = tot + buf[pl.ds(j, tm), cs]
            du = tot - dm[:, cs]
            du_ref[:, cs] = du.astype(BF16)
            cols.append(_colsum(du))
        _acc_rows(acc_ref, i == 0, [jnp.concatenate(cols, axis=1)])

    return pl.pallas_call(
        body, name="pool_bwd2", grid=(nt,),
        in_specs=[_rows(tm, C512, 0), _next(tm, PH, C512, 0, s)],
        out_specs=[_rows(tm, C512, 0), _full((8, C512))],
        out_shape=[jax.ShapeDtypeStruct((s, C512), BF16), jax.ShapeDtypeStruct((8, C512), F32)],
        scratch_shapes=[pltpu.VMEM((tm + PH, C512), F32)], compiler_params=_params(("arbitrary",)),
    )(dpd, dpd)


def _merge_specs(tm):
    return [_rows(tm, C512, 0), _rows(tm, C512, 0), _rows(tm, D, 0), _rows(tm, C512, 0),
            _rows(tm, 4 * D, GATES0 // (4 * D)),
            _full((C512, D)), _full((C512, D)), _full((D, D)), _full((C512, D))]


def _branch_ys(za, zb, o, zd, wa, wb, wc, wd):
    zs = (za[...], zb[...], o[...].astype(BF16), zd[...])
    return [lax.dot_general(z, w[...], _NN, preferred_element_type=F32) for z, w in zip(zs, (wa, wb, wc, wd))]


def _merge_fwd(za, zb, o, zd, proj, wa, wb, wc, wd):
    s = proj.shape[0]
    tm = _tile(s, TM_WIDE)

    def body(za_r, zb_r, o_r, zd_r, g_ref, wa_r, wb_r, wc_r, wd_r, m_ref):
        ys = _branch_ys(za_r, zb_r, o_r, zd_r, wa_r, wb_r, wc_r, wd_r)
        acc = _sig(g_ref[:, 0:D]) * ys[0]
        for b in range(1, 4):
            acc = acc + _sig(g_ref[:, b * D:(b + 1) * D]) * ys[b]
        m_ref[...] = acc.astype(BF16)

    return pl.pallas_call(
        body, name="merge_fwd", grid=(s // tm,), in_specs=_merge_specs(tm),
        out_specs=_rows(tm, D, 0), out_shape=jax.ShapeDtypeStruct((s, D), BF16),
        compiler_params=_params(("parallel",)),
    )(za, zb, o, zd, proj, wa, wb, wc, wd)


def _merge_bwd(dmerged, za, zb, o, zd, proj, wa, wb, wc, wd):
    s = proj.shape[0]
    tm = _tile(s, TM_WIDE)

    def body(dm_ref, za_r, zb_r, o_r, zd_r, g_ref, wa_r, wb_r, wc_r, wd_r,
             dya, dyb, dyc, dyd, dza, dzb, d_o, dzd, dg_ref, acc_ref):
        ys = _branch_ys(za_r, zb_r, o_r, zd_r, wa_r, wb_r, wc_r, wd_r)
        dm = dm_ref[...]
        sums = []
        for b, (dy_ref, dz_ref, w_r) in enumerate(((dya, dza, wa_r), (dyb, dzb, wb_r), (dyc, d_o, wc_r), (dyd, dzd, wd_r))):
            gt = _sig(g_ref[:, b * D:(b + 1) * D])
            dg = dm * ys[b] * gt * (1.0 - gt)
            dg_ref[:, b * D:(b + 1) * D] = dg.astype(BF16)
            sums.append(_colsum(dg))
            dy = (dm * gt).astype(BF16)
            dy_ref[...] = dy
            dz_ref[...] = lax.dot_general(dy, w_r[...], _NT, preferred_element_type=F32)
        _acc_rows(acc_ref, pl.program_id(0) == 0, [jnp.concatenate(sums, axis=1)])

    bf = lambda c: jax.ShapeDtypeStruct((s, c), BF16)
    f32 = lambda c: jax.ShapeDtypeStruct((s, c), F32)
    return pl.pallas_call(
        body, name="merge_bwd", grid=(s // tm,), in_specs=[_rows(tm, D, 0)] + _merge_specs(tm),
        out_specs=[_rows(tm, D, 0)] * 4 + [_rows(tm, C512, 0), _rows(tm, C512, 0), _rows(tm, D, 0), _rows(tm, C512, 0),
                                           _rows(tm, 4 * D, 0), _full((8, 4 * D))],
        out_shape=[bf(D)] * 4 + [f32(C512), f32(C512), f32(D), f32(C512), bf(4 * D), jax.ShapeDtypeStruct((8, 4 * D), F32)],
        compiler_params=_params(("arbitrary",)),
    )(dmerged, za, zb, o, zd, proj, wa, wb, wc, wd)


def _mm_res_ln(a, w, xres, prm, name):
    s, k = a.shape
    tm = _tile(s, TM_WIDE)

    def body(a_ref, w_ref, x_ref, p_ref, y_ref, xn_ref, hn_ref):
        y = lax.dot_general(a_ref[...], w_ref[...], _NN, preferred_element_type=F32)
        y_ref[...] = y
        xh, _ = _ln_stats(ALPHA * x_ref[...] + (1.0 + p_ref[0:1, :]) * y)
        xn = xh * p_ref[3:4, :] + p_ref[4:5, :]
        xn_ref[...] = xn
        hn_ref[...] = (xn * (1.0 + p_ref[1:2, :]) + p_ref[2:3, :]).astype(BF16)

    return pl.pallas_call(
        body, name=name, grid=(s // tm,),
        in_specs=[_rows(tm, k, 0), _full((k, D)), _rows(tm, D, 0), _full((8, D))],
        out_specs=[_rows(tm, D, 0)] * 3,
        out_shape=[jax.ShapeDtypeStruct((s, D), F32), jax.ShapeDtypeStruct((s, D), F32), jax.ShapeDtypeStruct((s, D), BF16)],
        compiler_params=_params(("parallel",)),
    )(a, w, xres, prm)


def _ln_res_bwd(dres_next, dh, xres, y, prm):
    s = xres.shape[0]
    tm = _tile(s, TM)

    def body(dn_ref, dh_ref, x_ref, y_ref, p_ref, dres_ref, dy_ref, acc_ref):
        gam, lng = p_ref[0:1, :], p_ref[3:4, :]
        yv = y_ref[...]
        xh, rstd = _ln_stats(ALPHA * x_ref[...] + (1.0 + gam) * yv)
        xn = xh * lng + p_ref[4:5, :]
        dh_v = dh_ref[...]
        dxn = dn_ref[...] + dh_v * (1.0 + p_ref[1:2, :])
        dr = _ln_bwd(dxn * lng, xh, rstd)
        dres_ref[...] = ALPHA * dr
        dy_ref[...] = ((1.0 + gam) * dr).astype(BF16)
        _acc_rows(acc_ref, pl.program_id(0) == 0,
                  [_colsum(dr * yv), _colsum(dh_v * xn), _colsum(dh_v), _colsum(dxn * xh), _colsum(dxn)])

    return pl.pallas_call(
        body, name="ln_res_bwd", grid=(s // tm,),
        in_specs=[_rows(tm, D, 0)] * 4 + [_full((8, D))],
        out_specs=[_rows(tm, D, 0), _rows(tm, D, 0), _full((8, D))],
        out_shape=[jax.ShapeDtypeStruct((s, D), F32), jax.ShapeDtypeStruct((s, D), BF16), jax.ShapeDtypeStruct((8, D), F32)],
        compiler_params=_params(("arbitrary",)),
    )(dres_next, dh, xres, y, prm)


FC = 16


def _shift_down(cur, prev, k, rowi):
    return jnp.where(rowi >= k, pltpu.roll(cur, k, 0), pltpu.roll(prev, k, 0))


def _shift_up(cur, nxt, k, rowi):
    return jnp.where(rowi < FC - k, pltpu.roll(cur, FC - k, 0), pltpu.roll(nxt, FC - k, 0))


def _conv3_chunk(w, cur, prev, rowi):
    return w[2] * cur + w[1] * _shift_down(cur, prev, 1, rowi) + w[0] * _shift_down(cur, prev, 2, rowi)


def _conv3_t_chunk(w, cur, nxt, rowi):
    return w[2] * cur + w[1] * _shift_up(cur, nxt, 1, rowi) + w[0] * _shift_up(cur, nxt, 2, rowi)


def _chunk_rows(j):
    return pl.ds(pl.multiple_of(j * FC, FC), FC)


def _ffn_chunk_specs(tm, s):
    r = tm // FC
    last = s // FC - 1
    out = []
    for half in (0, 1):
        out.append((pl.BlockSpec((FC, D_FF), lambda i, half=half: (jnp.maximum(i * r - 1, 0), half)),
                    pl.BlockSpec((tm, D_FF), lambda i, half=half: (i, half)),
                    pl.BlockSpec((FC, D_FF), lambda i, half=half: (jnp.minimum((i + 1) * r, last), half)),
                    pl.BlockSpec((8, D_FF), lambda i, half=half: (0, half))))
    return out


def _ffn_fwd(up, w8):
    s = up.shape[0]
    tm = _tile(s, TM_WIDE)
    (pv_s, mv_s, _, wv_s), (pg_s, mg_s, _, wg_s) = _ffn_chunk_specs(tm, s)

    def body(pv, mv, pg, mg, wv_ref, wg_ref, a_ref):
        first = pl.program_id(0) == 0
        lg = 2 * LANE
        rowi = lax.broadcasted_iota(jnp.int32, (FC, lg), 0)
        zero = jnp.zeros((FC, lg), F32)
        for cg in range(D_FF // lg):
            cs = slice(cg * lg, (cg + 1) * lg)
            wv = [wv_ref[k:k + 1, cs] for k in range(SC_W)]
            wg = [wg_ref[k:k + 1, cs] for k in range(SC_W)]

            def step(j, carry, cs=cs, wv=wv, wg=wg):
                rows = _chunk_rows(j)
                xv, xg = mv[rows, cs], mg[rows, cs]
                gate = _conv3_chunk(wg, xg, carry[1], rowi)
                a_ref[rows, cs] = (gate * _sig(gate) * _conv3_chunk(wv, xv, carry[0], rowi)).astype(BF16)
                return xv, xg

            lax.fori_loop(0, tm // (2 * FC), lambda j, c, step=step: step(2 * j + 1, step(2 * j, c)),
                          (jnp.where(first, zero, pv[:, cs]), jnp.where(first, zero, pg[:, cs])))

    return pl.pallas_call(
        body, name="ffn_fwd", grid=(s // tm,), in_specs=[pv_s, mv_s, pg_s, mg_s, wv_s, wg_s],
        out_specs=_rows(tm, D_FF, 0), out_shape=jax.ShapeDtypeStruct((s, D_FF), BF16),
        compiler_params=_params(("parallel",)),
    )(up, up, up, up, w8, w8)


def _ffn_bwd(da, up, w8):
    s = up.shape[0]
    tm = _tile(s, TM_WIDE)
    n, nt = tm // FC, s // tm
    (pv_s, mv_s, nv_s, wv_s), (pg_s, mg_s, ng_s, wg_s) = _ffn_chunk_specs(tm, s)

    def body(dam, dan, pv, mv, nv, pg, mg, ng, wv_ref, wg_ref, dup_ref, dw_ref):
        i = pl.program_id(0)
        first, last = i == 0, i == nt - 1
        rowi = lax.broadcasted_iota(jnp.int32, (FC, LANE), 0)
        zero = jnp.zeros((FC, LANE), F32)

        @pl.when(first)
        def _():
            dw_ref[...] = jnp.zeros_like(dw_ref)

        for cg in range(D_FF // LANE):
            cs = slice(cg * LANE, (cg + 1) * LANE)
            cs_g = slice(D_FF + cg * LANE, D_FF + (cg + 1) * LANE)
            wv = [wv_ref[k:k + 1, cs] for k in range(SC_W)]
            wg = [wg_ref[k:k + 1, cs] for k in range(SC_W)]

            def conv_grads(xv, xg, xpv, xpg, dav, wv=wv, wg=wg):
                val, gate = _conv3_chunk(wv, xv, xpv, rowi), _conv3_chunk(wg, xg, xpg, rowi)
                sg = _sig(gate)
                return dav * gate * sg, dav * val * (sg * (1.0 + gate * (1.0 - sg)))

            def step(j, c, cs=cs, cs_g=cs_g, wv=wv, wg=wg, conv_grads=conv_grads):
                xpv, xpg, dvp, dgp = c[:4]
                rows = _chunk_rows(j)
                xv, xg = mv[rows, cs], mg[rows, cs]
                dv, dg = conv_grads(xv, xg, xpv, xpg, dam[rows, cs])
                prow = _chunk_rows(jnp.maximum(j - 1, 0))
                dup_ref[prow, cs] = _conv3_t_chunk(wv, dvp, dv, rowi).astype(BF16)
                dup_ref[prow, cs_g] = _conv3_t_chunk(wg, dgp, dg, rowi).astype(BF16)
                accs = (c[4] + dv * _shift_down(xv, xpv, 2, rowi), c[5] + dv * _shift_down(xv, xpv, 1, rowi), c[6] + dv * xv,
                        c[7] + dg * _shift_down(xg, xpg, 2, rowi), c[8] + dg * _shift_down(xg, xpg, 1, rowi), c[9] + dg * xg)
                return (xv, xg, dv, dg) + accs

            init = (jnp.where(first, zero, pv[:, cs]), jnp.where(first, zero, pg[:, cs]), zero, zero) + (zero,) * 6
            c = lax.fori_loop(0, n // 2, lambda j, c, step=step: step(2 * j + 1, step(2 * j, c)), init)
            dv_n, dg_n = conv_grads(nv[:, cs], ng[:, cs], c[0], c[1], jnp.where(last, zero, dan[:, cs]))
            dup_ref[tm - FC:tm, cs] = _conv3_t_chunk(wv, c[2], dv_n, rowi).astype(BF16)
            dup_ref[tm - FC:tm, cs_g] = _conv3_t_chunk(wg, c[3], dg_n, rowi).astype(BF16)
            for k in range(SC_W):
                dw_ref[k:k + 1, cs] += _colsum(c[4 + k])
                dw_ref[k:k + 1, cs_g] += _colsum(c[7 + k])

    r = tm // FC
    da_next = pl.BlockSpec((FC, D_FF), lambda i: (jnp.minimum((i + 1) * r, s // FC - 1), 0))
    return pl.pallas_call(
        body, name="ffn_bwd", grid=(nt,),
        in_specs=[_rows(tm, D_FF, 0), da_next, pv_s, mv_s, nv_s, pg_s, mg_s, ng_s, wv_s, wg_s],
        out_specs=[_rows(tm, 2 * D_FF, 0), _full((8, 2 * D_FF))],
        out_shape=[jax.ShapeDtypeStruct((s, 2 * D_FF), BF16), jax.ShapeDtypeStruct((8, 2 * D_FF), F32)],
        compiler_params=_params(("arbitrary",)),
    )(da, da, up, up, up, up, up, up, w8, w8)


def _loss_head(y, target):
    s = y.shape[0]
    tm = _tile(s, TM)

    def body(y_ref, t_ref, dy_ref, l_ref):
        err = y_ref[...] - t_ref[...]
        dy_ref[...] = err * (1.0 / D)
        part = 0.5 * jnp.sum(jnp.mean(err * err, axis=-1, keepdims=True), axis=0, keepdims=True)

        @pl.when(pl.program_id(0) == 0)
        def _():
            l_ref[...] = jnp.zeros_like(l_ref)
        l_ref[...] += part

    return pl.pallas_call(
        body, name="loss_head", grid=(s // tm,),
        in_specs=[_rows(tm, D, 0)] * 2, out_specs=[_rows(tm, D, 0), _full((8, LANE))],
        out_shape=[jax.ShapeDtypeStruct((s, D), F32), jax.ShapeDtypeStruct((8, LANE), F32)],
        compiler_params=_params(("arbitrary",)),
    )(y, target)


def _silu_rows(c_all):
    def body(c_ref, o_ref):
        cv = c_ref[...]
        o_ref[...] = jnp.concatenate([cv * _sig(cv), jnp.zeros((N_DEV, D), F32)], axis=0).astype(BF16)

    return pl.pallas_call(
        body, name="silu_rows", grid=(1,), in_specs=[_full((N_DEV, D))], out_specs=_full((2 * N_DEV, D)),
        out_shape=jax.ShapeDtypeStruct((2 * N_DEV, D), BF16), compiler_params=_params(("arbitrary",)),
    )(c_all)


GRAD_ROWS = 512


def _sum_parts(p_ref, n):
    g = p_ref[0].astype(F32)
    for j in range(1, n):
        g = g + p_ref[j].astype(F32)
    return g


def _adamw(parts, w, m, v, name):
    n = parts.shape[0]
    r, c = w.shape
    tr = GRAD_ROWS
    assert r % tr == 0 and parts.shape[2] == c, (parts.shape, w.shape)

    def body(p_ref, w_ref, m_ref, v_ref, g_out, d_out, m_out, v_out):
        g = _sum_parts(p_ref, n)
        mn = ADAM_B1 * m_ref[...] + (1.0 - ADAM_B1) * g
        vn = ADAM_B2 * v_ref[...] + (1.0 - ADAM_B2) * (g * g)
        m_hat = mn / (1.0 - ADAM_B1 ** ADAM_STEP)
        v_hat = vn / (1.0 - ADAM_B2 ** ADAM_STEP)
        g_out[...] = g
        d_out[...] = -ADAM_LR * (m_hat / (jnp.sqrt(v_hat) + ADAM_EPS) + ADAM_WD * w_ref[...])
        m_out[...] = mn
        v_out[...] = vn

    blk = pl.BlockSpec((tr, c), lambda i: (i, 0))
    return pl.pallas_call(
        body, name=name, grid=(r // tr,),
        in_specs=[pl.BlockSpec((n, tr, c), lambda i: (0, i, 0)), blk, blk, blk], out_specs=[blk] * 4,
        out_shape=[jax.ShapeDtypeStruct((r, c), F32)] * 4, compiler_params=_params(("parallel",)),
    )(parts, w, m, v)


def _sum8(parts, name):
    n, r, c = parts.shape
    tr = GRAD_ROWS
    assert r % tr == 0, r

    def body(p_ref, g_out):
        g_out[...] = _sum_parts(p_ref, n)

    return pl.pallas_call(
        body, name=name, grid=(r // tr,),
        in_specs=[pl.BlockSpec((n, tr, c), lambda i: (0, i, 0))], out_specs=pl.BlockSpec((tr, c), lambda i: (i, 0)),
        out_shape=jax.ShapeDtypeStruct((r, c), F32), compiler_params=_params(("parallel",)),
    )(parts)


def _peers():
    ix, iy, ic = lax.axis_index("x"), lax.axis_index("y"), lax.axis_index("c")
    me = 4 * ix + 2 * iy + ic
    out = []
    for k in range(1, N_DEV):
        px = 1 - ix if (k >> 2) & 1 else ix
        py = 1 - iy if (k >> 1) & 1 else iy
        pc = 1 - ic if k & 1 else ic
        out.append(((px, py, pc), 4 * px + 2 * py + pc))
    return me, out


_HBM = pl.BlockSpec(memory_space=pltpu.HBM)


def _exchange(x, name, gather):
    shape = ((N_DEV,) + x.shape) if gather else x.shape

    def body(x_ref, o_ref, send_sems, recv_sems, local_sem):
        me, peers = _peers()
        src_of = (lambda p: x_ref) if gather else (lambda p: x_ref.at[p])
        local = pltpu.make_async_copy(src_of(me), o_ref.at[me], local_sem)
        local.start()
        sends = []
        for k, (dev, p) in enumerate(peers):
            cp = pltpu.make_async_remote_copy(src_ref=src_of(p), dst_ref=o_ref.at[me], send_sem=send_sems.at[k],
                                              recv_sem=recv_sems.at[k], device_id=dev, device_id_type=MESH)
            cp.start()
            sends.append(cp)
        for k, (dev, p) in enumerate(peers):
            pltpu.make_async_remote_copy(src_ref=src_of(p), dst_ref=o_ref.at[p], send_sem=send_sems.at[k],
                                         recv_sem=recv_sems.at[k], device_id=dev, device_id_type=MESH).wait_recv()
        for cp in sends:
            cp.wait_send()
        local.wait()

    return pl.pallas_call(
        body, name=name, in_specs=[_HBM], out_specs=_HBM, out_shape=jax.ShapeDtypeStruct(shape, x.dtype),
        scratch_shapes=[pltpu.SemaphoreType.DMA((N_DEV - 1,)), pltpu.SemaphoreType.DMA((N_DEV - 1,)),
                        pltpu.SemaphoreType.DMA],
    )(x)


def _gather_two_level(x, name):
    def body(x_ref, o_ref, send_sems, recv_sems, local_sem):
        ix, iy, ic = lax.axis_index("x"), lax.axis_index("y"), lax.axis_index("c")
        me, sibling = (ix, iy, ic), (ix, iy, 1 - ic)
        chips = [(1 - ix, iy), (ix, 1 - iy), (1 - ix, 1 - iy)]

        def slot(px, py, pc):
            return o_ref.at[4 * px + 2 * py + pc]

        def copy(k, block, to, src=None):
            return pltpu.make_async_remote_copy(src_ref=slot(*block) if src is None else src, dst_ref=slot(*block),
                                                send_sem=send_sems.at[k], recv_sem=recv_sems.at[k],
                                                device_id=to, device_id_type=MESH)

        mine = pltpu.make_async_copy(x_ref, slot(*me), local_sem)
        mine.start()
        first = [copy(0, me, sibling, src=x_ref)] + [copy(1 + j, me, (*chip, ic), src=x_ref) for j, chip in enumerate(chips)]
        for cp in first:
            cp.start()
        passed = [copy(4 + j, (*chip, ic), sibling) for j, chip in enumerate(chips)]
        for j, chip in enumerate(chips):
            copy(1 + j, (*chip, ic), me).wait_recv()
            passed[j].start()
        copy(0, sibling, me).wait_recv()
        for j, chip in enumerate(chips):
            copy(4 + j, (*chip, 1 - ic), me).wait_recv()
        for cp in first + passed:
            cp.wait_send()
        mine.wait()

    return pl.pallas_call(
        body, name=name, in_specs=[_HBM], out_specs=_HBM, out_shape=jax.ShapeDtypeStruct((N_DEV,) + x.shape, x.dtype),
        scratch_shapes=[pltpu.SemaphoreType.DMA((N_DEV - 1,)), pltpu.SemaphoreType.DMA((N_DEV - 1,)),
                        pltpu.SemaphoreType.DMA],
    )(x)


def _pack_rows(arrs, row_mult):
    flat = jnp.concatenate([a.reshape(-1) for a in arrs])
    n = flat.shape[0]
    pad = (-n) % (LANE * row_mult)
    if pad:
        flat = jnp.concatenate([flat, jnp.zeros((pad,), flat.dtype)])
    return flat.reshape(-1, LANE)


def _pack_cols(arrs, row_mult):
    flat = jnp.concatenate(arrs, axis=1)
    n = flat.shape[1]
    pad = (-n) % (LANE * row_mult)
    if pad:
        flat = jnp.concatenate([flat, jnp.zeros((flat.shape[0], pad), flat.dtype)], axis=1)
    return flat.reshape(flat.shape[0], -1, LANE)


def _unpack(flat, shapes):
    out, off = [], 0
    lead = flat.shape[:-1]
    for shp in shapes:
        n = 1
        for d_ in shp:
            n *= d_
        out.append(flat[..., off:off + n].reshape(lead + tuple(shp)))
        off += n
    return out


BIG = (("w_conv_out", 2), ("w_sc_out", 2), ("w_uq", 2), ("w_ukv", 2), ("w_mla_out", 2),
       ("w_pool_out", 2), ("w_o", 1), ("w_down", 1))
TAPS = (("conv_dw", 2), ("sc_dw", 2), ("ffn_dw", 2))
SMALL = ("b_ada", "b_in", "conv_ln_g", "conv_ln_b", "q_norm_g", "kv_norm_g", "w_pool", "pool_scale",
         "ln1_g", "ln1_b", "ln2_g", "ln2_b")
WEIGHTS = ("w_ada", "b_ada", "w_in", "b_in", "conv_dw", "conv_ln_g", "conv_ln_b", "w_conv_out", "sc_dw", "w_sc_out",
           "q_norm_g", "w_uq", "kv_norm_g", "w_ukv", "w_mla_out", "w_pool", "pool_scale", "w_pool_out", "w_o",
           "ln1_g", "ln1_b", "w_up", "ffn_dw", "w_down", "ln2_g", "ln2_b")


def _join(g, axis):
    g = jnp.moveaxis(g, 0, axis)
    shp = list(g.shape)
    shp[axis:axis + 2] = [shp[axis] * shp[axis + 1]]
    return g.reshape(shp)


def _split(full, axis):
    shp = list(full.shape)
    shp[axis:axis + 1] = [N_DEV, shp[axis] // N_DEV]
    return jnp.moveaxis(full.reshape(shp), axis, 0)


def _pad_rows(a, rows):
    return jnp.concatenate([a, jnp.zeros((rows - a.shape[0],) + a.shape[1:], a.dtype)], axis=0)


def _proj_cols(w):
    z = lambda n: jnp.zeros(w.shape[:-1] + (n,), w.dtype)
    return jnp.concatenate([w[..., 3488:7584], w[..., 0:2944], z(NOPE), w[..., 2944:2976], z(HEAD_PAD - QK_DIM),
                            w[..., 2976:3488]], axis=-1)


def _proj_cols_inv(w):
    return jnp.concatenate([w[..., CA0:KR0], w[..., KR0 + NOPE:KR0 + QK_DIM], w[..., PU0:NPROJ], w[..., 0:CA0]], axis=-1)


IN_COLS = 7584
IN_SHARD = IN_COLS // N_DEV
IN_SHARD_PAD = 960
UP_SHARD = 2 * D_FF // N_DEV
_IN_CUTS = (2944, 2976, 3488)
_IN_SEGS = ((3488, IN_COLS), (0, 2944), NOPE, (2944, 2976), HEAD_PAD - QK_DIM, (2976, 3488))


def _proj_rows(g):
    pieces = []
    for seg in _IN_SEGS:
        if isinstance(seg, int):
            pieces.append(jnp.zeros((g.shape[0], seg, g.shape[2]), g.dtype))
            continue
        j = seg[0]
        while j < seg[1]:
            dev = j // IN_SHARD
            e = min(seg[1], (dev + 1) * IN_SHARD)
            r0 = dev * IN_SHARD_PAD + j - dev * IN_SHARD
            pieces.append(g[:, r0:r0 + e - j])
            j = e
    return jnp.concatenate(pieces, axis=1)


def _aligned_row(j):
    if j < 2944:
        return CA0 + j
    if j < 2976:
        return KR0 + NOPE + j - 2944
    if j < 3488:
        return PU0 + j - 2976
    return j - 3488


def _proj_rows_split(gt):
    out = []
    for dev in range(N_DEV):
        j0, j1 = dev * IN_SHARD, (dev + 1) * IN_SHARD
        cuts = [j0] + [c for c in _IN_CUTS if j0 < c < j1] + [j1]
        out.append(jnp.concatenate([gt[_aligned_row(a):_aligned_row(a) + b - a] for a, b in zip(cuts[:-1], cuts[1:])], axis=0))
    return jnp.stack(out)


def _pad_axis(a, axis, size):
    shp = list(a.shape)
    shp[axis] = size - shp[axis]
    return jnp.concatenate([a, jnp.zeros(shp, a.dtype)], axis=axis)


def _all_layer_weights(full, small):
    w = {}
    w["w_inT"] = _proj_rows(full["w_inT"])
    w["b_in"] = _proj_cols(small["b_in"])[:, None, :]
    w["conv_w"] = _pad_axis(full["conv_dw"], 1, 32)
    w["sc_w"] = _pad_axis(full["sc_dw"], 1, 8)
    w["ffn_w"] = _pad_axis(full["ffn_dw"], 1, 8)
    w["w_uq"] = _pad_axis(full["w_uq"].reshape(DEPTH, 256, HEADS, QK_DIM), 3, HEAD_PAD).reshape(DEPTH, 256, -1)
    w["w_ukv"] = _pad_axis(full["w_ukv"].reshape(DEPTH, 128, HEADS, 2, NOPE), 4, HEAD_PAD).reshape(DEPTH, 128, -1)
    w["w_c"] = _pad_axis(full["w_mla_out"].reshape(DEPTH, HEADS, NOPE, D), 2, HEAD_PAD).reshape(DEPTH, HEADS * HEAD_PAD, D)
    w["w_a"], w["w_b"], w["w_d"] = full["w_conv_out"], full["w_sc_out"], full["w_pool_out"]
    w["w_o"], w["w_upT"], w["w_down"] = full["w_o"], full["w_upT"], full["w_down"]
    w["w_pool"] = small["w_pool"].astype(BF16)
    for n in ("conv_ln_g", "conv_ln_b", "q_norm_g", "kv_norm_g", "pool_scale"):
        w[n] = small[n][:, None, :]
    return w


def _prm(rows):
    z = jnp.zeros((D,), F32)
    rows = list(rows) + [z] * (8 - len(rows))
    return jnp.stack(rows)


def _layer_fwd(x, h1, w, prm1, prm2, tabs):
    ct, s1t, s2t = tabs
    proj = _mm(h1, w["w_inT"], "nt", "proj_fwd", bias=w["b_in"], tm=1024, tn=768)
    yconv, za = _conv_a_fwd(proj, w["conv_w"], w["conv_ln_g"], w["conv_ln_b"])
    zb = _sc_fwd(proj, w["sc_w"])
    qn, kvn = _lat_fwd(proj, w["q_norm_g"], w["kv_norm_g"])
    qp = _mm(qn, w["w_uq"], "nn", "uq_fwd")
    kvp = _mm(kvn, w["w_ukv"], "nn", "ukv_fwd")
    q, k, v = _rope_fwd(qp, kvp, proj, ct, s1t, s2t)
    o, lse = _attn_fwd(q, k, v)
    zd = _pool_fwd(proj, w["w_pool"], w["pool_scale"])
    merged = _merge_fwd(za, zb, o, zd, proj, w["w_a"], w["w_b"], w["w_c"], w["w_d"])
    mix, x1, h2 = _mm_res_ln(merged, w["w_o"], x, prm1, "o_res_ln")
    up = _mm(h2, w["w_upT"], "nt", "up_fwd", tm=1024, tn=1408)
    a = _ffn_fwd(up, w["ffn_w"])
    ffn, x2, h_next = _mm_res_ln(a, w["w_down"], x1, prm2, "down_res_ln")
    res = dict(x=x, h1=h1, proj=proj, yconv=yconv, za=za, zb=zb, qn=qn, kvn=kvn, q=q, k=k, v=v, o=o, lse=lse, zd=zd,
               merged=merged, mix=mix, x1=x1, h2=h2, up=up, a=a, ffn=ffn)
    return x2, h_next, res


def _layer_bwd(dres_next, dh_next, r, w, prm1, prm2, tabs):
    ct, s1t, s2t = tabs
    s = r["x"].shape[0]
    g = {}
    dres2, dffn, acc2 = _ln_res_bwd(dres_next, dh_next, r["x1"], r["ffn"], prm2)
    da = _mm(dffn, w["w_down"], "nt", "down_bwd_x", tm=1024, tn=1408)
    g["w_down"] = _mm(r["a"], dffn, "tn", "down_bwd_w", tm=1408)
    dup, dffn_w = _ffn_bwd(da, r["up"], w["ffn_w"])
    g["ffn_dw"] = dffn_w[:SC_W]
    dh2 = _mm(dup, w["w_upT"], "nn", "up_bwd_x", tm=1024, tk=1408)
    g["w_upT"] = _mm(dup, r["h2"], "tn", "up_bwd_w", tm=1408)
    dres1, dmix, acc1 = _ln_res_bwd(dres2, dh2, r["x"], r["mix"], prm1)
    dmerged = _mm(dmix, w["w_o"], "nt", "o_bwd_x")
    g["w_o"] = _mm(r["merged"], dmix, "tn", "o_bwd_w")
    (dya, dyb, dyc, dyd, dza, dzb, d_o, dzd, dgates, accg) = _merge_bwd(
        dmerged, r["za"], r["zb"], r["o"], r["zd"], r["proj"], w["w_a"], w["w_b"], w["w_c"], w["w_d"])
    g["w_conv_out"] = _mm(r["za"], dya, "tn", "branch_bwd_w")
    g["w_sc_out"] = _mm(r["zb"], dyb, "tn", "branch_bwd_w")
    g["w_pool_out"] = _mm(r["zd"], dyd, "tn", "branch_bwd_w")
    gwc = _mm(r["o"], dyc, "tn", "mla_out_bwd_w")
    g["w_mla_out"] = gwc.reshape(HEADS, HEAD_PAD, D)[:, :NOPE].reshape(HEADS * NOPE, D)
    dyconv, dconv_w, acca = _conv_a_bwd1(dza, r["yconv"], r["proj"], w["conv_ln_g"], w["conv_ln_b"])
    g["conv_dw"], g["conv_ln_g"], g["conv_ln_b"] = dconv_w[:CONV_W], acca[0], acca[1]
    d_ca, d_cb, acca2 = _conv_a_bwd2(dyconv, r["proj"], w["conv_w"])
    dconv, d_bg, dsc_w, accb1 = _sc_bwd1(dzb, r["proj"], w["sc_w"])
    g["sc_dw"] = dsc_w[:SC_W]
    d_cg, d_sx, accb2 = _sc_bwd2(dconv, r["proj"], w["sc_w"])
    delta, dob = _attn_prep(d_o, r["o"])
    dq, dkv = _attn_bwd(r["q"], r["k"], r["v"], dob, r["lse"], delta)
    dqp, d_kr, acckr = _rope_bwd(dq, dkv, ct, s1t, s2t)
    dqn = _mm(dqp, w["w_uq"], "nt", "uq_bwd_x")
    guq = _mm(r["qn"], dqp, "tn", "uq_bwd_w")
    g["w_uq"] = guq.reshape(256, HEADS, HEAD_PAD)[:, :, :QK_DIM].reshape(256, HEADS * QK_DIM)
    dkvn = _mm(dkv, w["w_ukv"], "nt", "ukv_bwd_x")
    gukv = _mm(r["kvn"], dkv, "tn", "ukv_bwd_w")
    g["w_ukv"] = gukv.reshape(128, HEADS, 2, HEAD_PAD)[..., :NOPE].reshape(128, HEADS * 2 * NOPE)
    d_ql, d_kvl, accq, acckv = _lat_bwd(dqn, dkvn, r["proj"], w["q_norm_g"], w["kv_norm_g"])
    g["q_norm_g"], g["kv_norm_g"] = accq[0], acckv[0]
    dpd, g["w_pool"], accd1 = _pool_bwd1(dzd, r["proj"], w["w_pool"], w["pool_scale"])
    g["pool_scale"] = accd1[0]
    d_pu, accd2 = _pool_bwd2(dpd, s)
    dproj = jnp.concatenate([dgates, d_ca, d_cb, d_bg, d_cg, d_sx, d_ql, d_kvl, d_kr, d_pu], axis=1)
    db = jnp.concatenate([accg[0], acca2[0], acca2[1], accb1[0], accb2[0], accb2[1], accq[1], acckv[1], acckr[0], accd2[0]])
    g["b_in"] = _proj_cols_inv(db)
    dh1 = _mm(dproj, w["w_inT"], "nn", "proj_bwd_x", tm=1024, tk=1536)
    g["w_inT"] = _proj_rows_split(_mm(dproj, r["h1"], "tn", "proj_bwd_w", tm=768))
    g["ln1_g"], g["ln1_b"], g["ln2_g"], g["ln2_b"] = acc1[3], acc1[4], acc2[3], acc2[4]
    return dres1, dh1, g, (acc1, acc2)


def _rope_tables(positions):
    half = ROPE // 2
    inv = 1.0 / (ROPE_THETA ** (jnp.arange(0, ROPE, 2, dtype=F32) / ROPE))
    ang = positions.astype(F32)[:, None] * inv
    cos, sin = jnp.cos(ang), jnp.sin(ang)
    s = positions.shape[0]
    z = lambda n: jnp.zeros((s, n), F32)
    ct = jnp.concatenate([jnp.ones((s, NOPE), F32), cos, cos, z(HEAD_PAD - QK_DIM)], axis=1)
    s1t = jnp.concatenate([z(NOPE), -sin, z(half), z(HEAD_PAD - QK_DIM)], axis=1)
    s2t = jnp.concatenate([z(NOPE), z(half), sin, z(HEAD_PAD - QK_DIM)], axis=1)
    return ct, s1t, s2t


def _local_step(x, mod, positions, full, small, target):
    tabs = _rope_tables(positions)
    stacked = _all_layer_weights(full, small)
    ws = [{n: a[l] for n, a in stacked.items()} for l in range(DEPTH)]
    zero = jnp.zeros((D,), F32)
    prm1s, prm2s = [], []
    for l in range(DEPTH):
        sh1, sc1, g1, sh2, sc2, g2 = (mod[l, j] for j in range(6))
        nxt = (mod[l + 1, 1], mod[l + 1, 0]) if l + 1 < DEPTH else (zero, zero)
        prm1s.append(_prm([g1, sc2, sh2, small["ln1_g"][l], small["ln1_b"][l]]))
        prm2s.append(_prm([g2, nxt[0], nxt[1], small["ln2_g"][l], small["ln2_b"][l]]))
    prm0 = _prm([zero, mod[0, 1], mod[0, 0]])
    h = _mod_fwd(x, prm0)
    res = []
    xc = x
    for l in range(DEPTH):
        xc, h, r = _layer_fwd(xc, h, ws[l], prm1s[l], prm2s[l], tabs)
        res.append(r)
    dy, lacc = _loss_head(xc, target)
    loss = lacc[0, 0]
    dres, dh = dy, jnp.zeros_like(dy)
    grads = [None] * DEPTH
    accs = [None] * DEPTH
    for l in reversed(range(DEPTH)):
        dres, dh, grads[l], accs[l] = _layer_bwd(dres, dh, res[l], ws[l], prm1s[l], prm2s[l], tabs)
    dx, acc0 = _mod_bwd(dres, dh, x, prm0)
    dmod = []
    for l in range(DEPTH):
        acc1, acc2 = accs[l]
        dsc1, dsh1 = (acc0[0], acc0[1]) if l == 0 else (accs[l - 1][1][1], accs[l - 1][1][2])
        dmod.append(jnp.stack([dsh1, dsc1, acc1[0], acc1[2], acc1[1], acc2[0]]))
    return loss, dx, grads, jnp.stack(dmod)


ADA_SHARD = 6 * D // N_DEV


def _step(p):
    me = 4 * lax.axis_index("x") + 2 * lax.axis_index("y") + lax.axis_index("c")
    x, target, positions = p["x"][0], p["loss_target"][0], p["positions"][0]

    tap_shapes = [p[n].shape for n, _ in TAPS] + [(D,)]
    small_g = _exchange(_pack_rows([p[n] for n, _ in TAPS] + [p["c"][0]], 8), "gather_taps", True)
    parts = _unpack(small_g.reshape(N_DEV, -1), tap_shapes)
    full = {n: _join(g, ax) for (n, ax), g in zip(TAPS, parts[:-1])}
    c_all = parts[-1]
    w_in_t = jnp.swapaxes(p["w_in"], 1, 2).astype(BF16)
    w_in_t = jnp.concatenate([w_in_t, jnp.zeros((DEPTH, IN_SHARD_PAD - IN_SHARD, D), BF16)], axis=1)
    w_up_t = jnp.swapaxes(p["w_up"], 1, 2).astype(BF16)
    big_shapes = [p[n].shape for n, _ in BIG] + [w_in_t.shape, w_up_t.shape]
    big_g = _gather_two_level(_pack_rows([p[n].astype(BF16) for n, _ in BIG] + [w_in_t, w_up_t], 16), "gather_weights")
    big_parts = _unpack(big_g.reshape(N_DEV, -1), big_shapes)
    for (n, ax), g in zip(BIG, big_parts):
        full[n] = _join(g, ax)
    full["w_inT"] = _join(big_parts[-2], 1)
    full["w_upT"] = _join(big_parts[-1], 1)
    small = {n: p[n] for n in SMALL}

    c_act = _silu_rows(c_all)
    w_ada_cols = jnp.moveaxis(p["w_ada"], 0, 1).reshape(D, DEPTH * ADA_SHARD)
    b_shard = lax.dynamic_slice_in_dim(p["b_ada"], me * ADA_SHARD, ADA_SHARD, axis=1).reshape(1, DEPTH * ADA_SHARD)
    mod_sh = _mm(c_act, w_ada_cols, "nn", "ada_fwd", bias=b_shard)[:N_DEV]
    mod_x = _exchange(_pack_cols([mod_sh], 8), "scatter_mod", False)
    mod = mod_x.reshape(N_DEV, -1)[:, :DEPTH * ADA_SHARD].reshape(N_DEV, DEPTH, ADA_SHARD)
    mod = jnp.moveaxis(mod, 0, 1).reshape(DEPTH, 6, D)

    loss_local, dx, grads, dmod = _local_step(x, mod, positions, full, small, target)
    loss = lax.psum(loss_local, ("x", "y", "c"))
    gfull = {n: jnp.stack([grads[l][n] for l in range(DEPTH)]) for n in grads[0]}

    out = {"loss": loss, "grad_x": dx[None]}

    def emit(names, g, dlt, mn, vn, shapes):
        for n, gi, di, mi, vi in zip(names, _unpack(g, shapes), _unpack(dlt, shapes), _unpack(mn, shapes), _unpack(vn, shapes)):
            out["grad_" + n], out["delta_" + n], out["new_m_" + n], out["new_v_" + n] = gi, di, mi, vi

    small_parts = [dmod.reshape(DEPTH, 6 * D)] + [gfull[n] for n in SMALL[1:]]
    small_all = _exchange(_pack_rows(small_parts, GRAD_ROWS), "gather_small_grads", True)
    small_shapes = [p[n].shape for n in SMALL]
    sg, sd, sm, sv = _adamw(small_all, *[_pack_rows([p[pre + n] for n in SMALL], GRAD_ROWS) for pre in ("", "m_", "v_")],
                            name="adamw_small")
    emit(SMALL, *[t.reshape(-1) for t in (sg, sd, sm, sv)], small_shapes)

    dmod_all = small_all.reshape(N_DEV, -1)[:, :DEPTH * 6 * D].reshape(N_DEV, DEPTH, 6 * D)
    dmod_sh = lax.dynamic_slice_in_dim(dmod_all, me * ADA_SHARD, ADA_SHARD, axis=2).reshape(N_DEV, DEPTH * ADA_SHARD)
    g_ada = _mm(c_act, _pad_rows(dmod_sh, 2 * N_DEV), "tn", "ada_bwd_w")
    g_ada = jnp.moveaxis(g_ada.reshape(D, DEPTH, ADA_SHARD), 1, 0)
    ag, ad, am, av = _adamw(_pack_rows([g_ada], GRAD_ROWS)[None], *[_pack_rows([p[pre + "w_ada"]], GRAD_ROWS) for pre in ("", "m_", "v_")],
                            name="adamw_ada")
    emit(("w_ada",), *[t.reshape(-1) for t in (ag, ad, am, av)], [p["w_ada"].shape])

    shard_names = [n for n, _ in BIG + TAPS]
    pieces = [_split(gfull[n], ax).reshape(N_DEV, -1) for n, ax in BIG + TAPS]
    recv = _exchange(_pack_cols(pieces, GRAD_ROWS).astype(BF16), "scatter_grads", False)
    bg, bd, bm, bv = _adamw(recv, *[_pack_rows([p[pre + n] for n in shard_names], GRAD_ROWS) for pre in ("", "m_", "v_")],
                            name="adamw_sharded")
    emit(shard_names, *[t.reshape(-1) for t in (bg, bd, bm, bv)], [p[n].shape for n in shard_names])

    t_pieces = [jnp.moveaxis(gfull["w_inT"], 1, 0).reshape(N_DEV, -1),
                jnp.moveaxis(gfull["w_upT"].reshape(DEPTH, N_DEV, UP_SHARD, D), 1, 0).reshape(N_DEV, -1)]
    g_t = _sum8(_exchange(_pack_cols(t_pieces, GRAD_ROWS).astype(BF16), "scatter_grads_t", False), "sum_grads_t")
    g_in_t, g_up_t = _unpack(g_t.reshape(-1), [(DEPTH, IN_SHARD, D), (DEPTH, UP_SHARD, D)])
    for n, gt in (("w_in", g_in_t), ("w_up", g_up_t)):
        shp = p[n].shape
        two_d = (shp[0] * shp[1], shp[2])
        res = _adamw(jnp.swapaxes(gt, 1, 2).reshape((1,) + two_d), *[p[pre + n].reshape(two_d) for pre in ("", "m_", "v_")],
                     name="adamw_" + n)
        for key, t in zip(("grad_", "delta_", "new_m_", "new_v_"), res):
            out[key + n] = t.reshape(shp)
    return out


_ARG_NAMES = ("x", "c", "positions") + WEIGHTS + ("loss_target",) + tuple("m_" + n for n in WEIGHTS) + tuple("v_" + n for n in WEIGHTS)
_OUT_NAMES = ("loss", "grad_x") + tuple(pre + n for pre in ("grad_", "delta_", "new_m_", "new_v_") for n in WEIGHTS)


def kernel(x, c, positions, w_ada, b_ada, w_in, b_in, conv_dw, conv_ln_g, conv_ln_b, w_conv_out, sc_dw, w_sc_out, q_norm_g, w_uq, kv_norm_g, w_ukv, w_mla_out, w_pool, pool_scale, w_pool_out, w_o, ln1_g, ln1_b, w_up, ffn_dw, w_down, ln2_g, ln2_b, loss_target, m_w_ada, m_b_ada, m_w_in, m_b_in, m_conv_dw, m_conv_ln_g, m_conv_ln_b, m_w_conv_out, m_sc_dw, m_w_sc_out, m_q_norm_g, m_w_uq, m_kv_norm_g, m_w_ukv, m_w_mla_out, m_w_pool, m_pool_scale, m_w_pool_out, m_w_o, m_ln1_g, m_ln1_b, m_w_up, m_ffn_dw, m_w_down, m_ln2_g, m_ln2_b, v_w_ada, v_b_ada, v_w_in, v_b_in, v_conv_dw, v_conv_ln_g, v_conv_ln_b, v_w_conv_out, v_sc_dw, v_w_sc_out, v_q_norm_g, v_w_uq, v_kv_norm_g, v_w_ukv, v_w_mla_out, v_w_pool, v_pool_scale, v_w_pool_out, v_w_o, v_ln1_g, v_ln1_b, v_w_up, v_ffn_dw, v_w_down, v_ln2_g, v_ln2_b):
    args = locals()
    out = _step({n: args[n] for n in _ARG_NAMES})
    return tuple(out[n] for n in _OUT_NAMES)
```

```python
import functools

import jax
import jax.numpy as jnp
from jax import lax
from jax.experimental import pallas as pl
from jax.experimental.pallas import tpu as pltpu

F32 = jnp.float32
BF16 = jnp.bfloat16

N_DEV = 8
DEPTH = 4
D = 1024
CONV_W = 31
HEADS = 8
HEAD_PAD = 128
QK_DIM = 96
NOPE = 64
ROPE = 32
ROPE_THETA = 10000.0
D_FF = 2816
LN_EPS = 1e-5
RMS_EPS = 1e-6
ALPHA = (2.0 * DEPTH) ** 0.25
ATT_SCALE = QK_DIM ** -0.5
POOL_WINDOWS = (2, 4, 8, 16)

ADAM_LR = 0.001
ADAM_B1 = 0.9
ADAM_B2 = 0.999
ADAM_EPS = 1e-08
ADAM_WD = 0.01
ADAM_STEP = 10

GATES0 = 0
CA0 = 4096
CB0 = 4608
SBG0 = 5120
SCG0 = 5632
SX0 = 6144
QL0 = 6656
KVL0 = 6912
KR0 = 7040
PU0 = 7168
NPROJ = 7680

LANE = 128
TM = 512
TM_WIDE = 256
TQ = 1024
TQ_BWD = 512
TQ_BWD_Q = 1024
VMEM_LIMIT = 56 * 1024 * 1024

MESH = pl.DeviceIdType.MESH


def _sig(x):
    return 1.0 / (1.0 + jnp.exp(-x))


def _tile(n, pref):
    if n <= pref:
        return n
    t = (pref // LANE) * LANE
    while t >= LANE:
        if n % t == 0:
            return t
        t -= LANE
    raise ValueError(f"no lane-aligned tile for {n}")


def _params(sem):
    return pltpu.CompilerParams(dimension_semantics=sem, vmem_limit_bytes=VMEM_LIMIT)


def _full(shape):
    nd = len(shape)
    return pl.BlockSpec(shape, lambda *_: (0,) * nd)


def _rows(tm, cw, cb):
    return pl.BlockSpec((tm, cw), lambda i: (i, cb))


def _prev(tm, hb, cw, cb):
    r = tm // hb
    return pl.BlockSpec((hb, cw), lambda i: (jnp.maximum(i * r - 1, 0), cb))


def _next(tm, hb, cw, cb, s):
    r = tm // hb
    last = s // hb - 1
    return pl.BlockSpec((hb, cw), lambda i: (jnp.minimum((i + 1) * r, last), cb))


def _acc_rows(ref, first, rows):
    @pl.when(first)
    def _():
        ref[...] = jnp.zeros_like(ref)
    for r, v in enumerate(rows):
        ref[r:r + 1, :] += v


def _colsum(v):
    return jnp.sum(v, axis=0, keepdims=True)


def _ln_stats(r):
    mu = jnp.mean(r, axis=-1, keepdims=True)
    xc = r - mu
    var = jnp.mean(xc * xc, axis=-1, keepdims=True)
    rstd = lax.rsqrt(var + LN_EPS)
    return xc * rstd, rstd


def _ln_bwd(dxh, xh, rstd):
    return rstd * (dxh - jnp.mean(dxh, axis=-1, keepdims=True) - xh * jnp.mean(dxh * xh, axis=-1, keepdims=True))


_DIMS = {"nn": ((1,), (0,)), "nt": ((1,), (1,)), "tn": ((0,), (0,))}


def _mm(a, b, mode, name, *, out_dtype=F32, bias=None, tm=512, tn=1024, tk=2048):
    if mode == "nn":
        (m, k), (k2, n) = a.shape, b.shape
    elif mode == "nt":
        (m, k), (n, k2) = a.shape, b.shape
    else:
        (k, m), (k2, n) = a.shape, b.shape
    assert k == k2, (a.shape, b.shape, mode)
    tm, tn, tk = _tile(m, tm), _tile(n, tn), _tile(k, tk)
    nk = k // tk
    dims = (_DIMS[mode], ((), ()))
    has_bias = bias is not None

    def body(*refs):
        if has_bias:
            a_ref, b_ref, bias_ref, o_ref = refs[:4]
        else:
            a_ref, b_ref, o_ref = refs[:3]
        p = lax.dot_general(a_ref[...].astype(BF16), b_ref[...].astype(BF16), dims, preferred_element_type=F32)

        def finish(r):
            if has_bias:
                r = r + bias_ref[...]
            o_ref[...] = r.astype(out_dtype)

        if nk == 1:
            finish(p)
        else:
            acc = refs[-1]
            kk = pl.program_id(2)

            @pl.when(kk == 0)
            def _():
                acc[...] = p

            @pl.when(kk > 0)
            def _():
                acc[...] += p

            @pl.when(kk == nk - 1)
            def _():
                finish(acc[...])

    if mode == "nn":
        a_spec = pl.BlockSpec((tm, tk), lambda i, j, kk: (i, kk))
        b_spec = pl.BlockSpec((tk, tn), lambda i, j, kk: (kk, j))
    elif mode == "nt":
        a_spec = pl.BlockSpec((tm, tk), lambda i, j, kk: (i, kk))
        b_spec = pl.BlockSpec((tn, tk), lambda i, j, kk: (j, kk))
    else:
        a_spec = pl.BlockSpec((tk, tm), lambda i, j, kk: (kk, i))
        b_spec = pl.BlockSpec((tk, tn), lambda i, j, kk: (kk, j))
    in_specs = [a_spec, b_spec]
    args = [a, b]
    if has_bias:
        in_specs.append(pl.BlockSpec((1, tn), lambda i, j, kk: (0, j)))
        args.append(bias)
    return pl.pallas_call(
        body, name=name, grid=(m // tm, n // tn, nk),
        in_specs=in_specs, out_specs=pl.BlockSpec((tm, tn), lambda i, j, kk: (i, j)),
        out_shape=jax.ShapeDtypeStruct((m, n), out_dtype),
        scratch_shapes=[pltpu.VMEM((tm, tn), F32)] if nk > 1 else [],
        compiler_params=_params(("parallel", "parallel", "arbitrary")),
    )(*args)


def _mod_fwd(x, prm):
    s = x.shape[0]
    tm = _tile(s, TM)

    def body(x_ref, p_ref, h_ref):
        h_ref[...] = (x_ref[...] * (1.0 + p_ref[1:2, :]) + p_ref[2:3, :]).astype(BF16)

    return pl.pallas_call(
        body, name="mod_fwd", grid=(s // tm,),
        in_specs=[_rows(tm, D, 0), _full((8, D))], out_specs=_rows(tm, D, 0),
        out_shape=jax.ShapeDtypeStruct((s, D), BF16), compiler_params=_params(("parallel",)),
    )(x, prm)


def _mod_bwd(dres, dh, x, prm):
    s = x.shape[0]
    tm = _tile(s, TM)

    def body(dres_ref, dh_ref, x_ref, p_ref, dx_ref, acc_ref):
        dh_v = dh_ref[...]
        dx_ref[...] = dres_ref[...] + dh_v * (1.0 + p_ref[1:2, :])
        _acc_rows(acc_ref, pl.program_id(0) == 0, [_colsum(dh_v * x_ref[...]), _colsum(dh_v)])

    return pl.pallas_call(
        body, name="mod_bwd", grid=(s // tm,),
        in_specs=[_rows(tm, D, 0)] * 3 + [_full((8, D))],
        out_specs=[_rows(tm, D, 0), _full((8, D))],
        out_shape=[jax.ShapeDtypeStruct((s, D), F32), jax.ShapeDtypeStruct((8, D), F32)],
        compiler_params=_params(("arbitrary",)),
    )(dres, dh, x, prm)


CA_HALO = 32
C512 = 512


def _glu_buf(buf, ap, am, bp, bm, first, tm):
    glu_p = ap[...] * _sig(bp[...])
    buf[0:CA_HALO, :] = jnp.where(first, jnp.zeros_like(glu_p), glu_p)
    buf[CA_HALO:CA_HALO + tm, :] = am[...] * _sig(bm[...])


def _conv31(w_ref, buf, zs, tm, offs):
    acc = None
    for r in range(8):
        ks = [k for k in range(CONV_W) if offs[k] % 8 == r]
        if not ks:
            continue
        rows = tm if r == 0 else tm + 8
        z = None
        for k in ks:
            t = w_ref[k:k + 1, :] * buf[pl.ds(offs[k] - r, rows), :]
            z = t if z is None else z + t
        if r:
            zs[...] = z
            z = zs[pl.ds(r, tm), :]
        acc = z if acc is None else acc + z
    return acc


_CA_FWD_OFFS = tuple(CA_HALO - CONV_W + 1 + k for k in range(CONV_W))
_CA_BWD_OFFS = tuple(CONV_W - 1 - k for k in range(CONV_W))


def _conv_a_fwd(proj, w32, ln_g, ln_b):
    s = proj.shape[0]
    tm = _tile(s, TM)
    ca, cb = CA0 // C512, CB0 // C512

    def body(ap, am, bp, bm, w_ref, g_ref, b_ref, yc_ref, za_ref, buf, zs):
        _glu_buf(buf, ap, am, bp, bm, pl.program_id(0) == 0, tm)
        acc = _conv31(w_ref, buf, zs, tm, _CA_FWD_OFFS)
        yc_ref[...] = acc
        xh, _ = _ln_stats(acc)
        y = xh * g_ref[...] + b_ref[...]
        za_ref[...] = (y * _sig(y)).astype(BF16)

    return pl.pallas_call(
        body, name="conv_a_fwd", grid=(s // tm,),
        in_specs=[_prev(tm, CA_HALO, C512, ca), _rows(tm, C512, ca), _prev(tm, CA_HALO, C512, cb), _rows(tm, C512, cb),
                  _full((32, C512)), _full((1, C512)), _full((1, C512))],
        out_specs=[_rows(tm, C512, 0), _rows(tm, C512, 0)],
        out_shape=[jax.ShapeDtypeStruct((s, C512), F32), jax.ShapeDtypeStruct((s, C512), BF16)],
        scratch_shapes=[pltpu.VMEM((tm + CA_HALO, C512), F32), pltpu.VMEM((tm + 8, C512), F32)],
        compiler_params=_params(("parallel",)),
    )(proj, proj, proj, proj, w32, ln_g, ln_b)


def _conv_a_bwd1(dza, yconv, proj, ln_g, ln_b):
    s = proj.shape[0]
    tm = _tile(s, TM)
    ca, cb = CA0 // C512, CB0 // C512

    def body(dza_ref, yc_ref, ap, am, bp, bm, g_ref, b_ref, dyc_ref, dw_ref, acc_ref, buf):
        first = pl.program_id(0) == 0
        xh, rstd = _ln_stats(yc_ref[...])
        g = g_ref[...]
        y = xh * g + b_ref[...]
        sg = _sig(y)
        dy = dza_ref[...] * (sg * (1.0 + y * (1.0 - sg)))
        _acc_rows(acc_ref, first, [_colsum(dy * xh), _colsum(dy)])
        dyc = _ln_bwd(dy * g, xh, rstd)
        dyc_ref[...] = dyc
        _glu_buf(buf, ap, am, bp, bm, first, tm)

        @pl.when(first)
        def _():
            dw_ref[...] = jnp.zeros_like(dw_ref)
        for k in range(CONV_W):
            dw_ref[k:k + 1, :] += _colsum(dyc * buf[pl.ds(_CA_FWD_OFFS[k], tm), :])

    return pl.pallas_call(
        body, name="conv_a_bwd1", grid=(s // tm,),
        in_specs=[_rows(tm, C512, 0), _rows(tm, C512, 0),
                  _prev(tm, CA_HALO, C512, ca), _rows(tm, C512, ca), _prev(tm, CA_HALO, C512, cb), _rows(tm, C512, cb),
                  _full((1, C512)), _full((1, C512))],
        out_specs=[_rows(tm, C512, 0), _full((32, C512)), _full((8, C512))],
        out_shape=[jax.ShapeDtypeStruct((s, C512), F32), jax.ShapeDtypeStruct((32, C512), F32),
                   jax.ShapeDtypeStruct((8, C512), F32)],
        scratch_shapes=[pltpu.VMEM((tm + CA_HALO, C512), F32)],
        compiler_params=_params(("arbitrary",)),
    )(dza, yconv, proj, proj, proj, proj, ln_g, ln_b)


def _conv_a_bwd2(dyc, proj, w32):
    s = proj.shape[0]
    tm = _tile(s, TM)
    ca, cb = CA0 // C512, CB0 // C512
    nt = s // tm

    def body(dm, dn, am, bm, w_ref, da_ref, db_ref, acc_ref, buf, zs):
        i = pl.program_id(0)
        buf[0:tm, :] = dm[...]
        nxt = dn[...]
        buf[tm:tm + CA_HALO, :] = jnp.where(i == nt - 1, jnp.zeros_like(nxt), nxt)
        dglu = _conv31(w_ref, buf, zs, tm, _CA_BWD_OFFS)
        sb = _sig(bm[...])
        da = dglu * sb
        db = dglu * am[...] * sb * (1.0 - sb)
        da_ref[...] = da.astype(BF16)
        db_ref[...] = db.astype(BF16)
        _acc_rows(acc_ref, i == 0, [_colsum(da), _colsum(db)])

    return pl.pallas_call(
        body, name="conv_a_bwd2", grid=(nt,),
        in_specs=[_rows(tm, C512, 0), _next(tm, CA_HALO, C512, 0, s), _rows(tm, C512, ca), _rows(tm, C512, cb),
                  _full((32, C512))],
        out_specs=[_rows(tm, C512, 0), _rows(tm, C512, 0), _full((8, C512))],
        out_shape=[jax.ShapeDtypeStruct((s, C512), BF16), jax.ShapeDtypeStruct((s, C512), BF16),
                   jax.ShapeDtypeStruct((8, C512), F32)],
        scratch_shapes=[pltpu.VMEM((tm + CA_HALO, C512), F32), pltpu.VMEM((tm + 8, C512), F32)],
        compiler_params=_params(("arbitrary",)),
    )(dyc, dyc, proj, proj, w32)


H8 = 8
SC_W = 3


def _sc_ubuf(buf, cp, cm, xp, xm, first, tm):
    up = cp[...] * xp[...]
    buf[0:H8, :] = jnp.where(first, jnp.zeros_like(up), up)
    buf[H8:H8 + tm, :] = cm[...] * xm[...]


def _conv3(w_ref, buf, tm):
    acc = w_ref[0:1, :] * buf[pl.ds(H8 - SC_W + 1, tm), :]
    for k in range(1, SC_W):
        acc = acc + w_ref[k:k + 1, :] * buf[pl.ds(H8 - SC_W + 1 + k, tm), :]
    return acc


def _conv3_t(w_ref, buf, tm):
    acc = w_ref[0:1, :] * buf[pl.ds(SC_W - 1, tm), :]
    for k in range(1, SC_W):
        acc = acc + w_ref[k:k + 1, :] * buf[pl.ds(SC_W - 1 - k, tm), :]
    return acc


def _sc_fwd(proj, w8):
    s = proj.shape[0]
    tm = _tile(s, TM)
    c_bg, c_cg, c_x = SBG0 // C512, SCG0 // C512, SX0 // C512

    def body(bg, cp, cm, xp, xm, w_ref, zb_ref, buf):
        _sc_ubuf(buf, cp, cm, xp, xm, pl.program_id(0) == 0, tm)
        zb_ref[...] = (bg[...] * _conv3(w_ref, buf, tm)).astype(BF16)

    return pl.pallas_call(
        body, name="sc_fwd", grid=(s // tm,),
        in_specs=[_rows(tm, C512, c_bg), _prev(tm, H8, C512, c_cg), _rows(tm, C512, c_cg),
                  _prev(tm, H8, C512, c_x), _rows(tm, C512, c_x), _full((8, C512))],
        out_specs=_rows(tm, C512, 0), out_shape=jax.ShapeDtypeStruct((s, C512), BF16),
        scratch_shapes=[pltpu.VMEM((tm + H8, C512), F32)], compiler_params=_params(("parallel",)),
    )(proj, proj, proj, proj, proj, w8)


def _sc_bwd1(dzb, proj, w8):
    s = proj.shape[0]
    tm = _tile(s, TM)
    c_bg, c_cg, c_x = SBG0 // C512, SCG0 // C512, SX0 // C512

    def body(dz_ref, bg, cp, cm, xp, xm, w_ref, dconv_ref, dbg_ref, dw_ref, acc_ref, buf):
        first = pl.program_id(0) == 0
        _sc_ubuf(buf, cp, cm, xp, xm, first, tm)
        dz = dz_ref[...]
        dbg = dz * _conv3(w_ref, buf, tm)
        dconv = dz * bg[...]
        dconv_ref[...] = dconv
        dbg_ref[...] = dbg.astype(BF16)
        _acc_rows(acc_ref, first, [_colsum(dbg)])
        _acc_rows(dw_ref, first, [_colsum(dconv * buf[pl.ds(H8 - SC_W + 1 + k, tm), :]) for k in range(SC_W)])

    return pl.pallas_call(
        body, name="sc_bwd1", grid=(s // tm,),
        in_specs=[_rows(tm, C512, 0), _rows(tm, C512, c_bg), _prev(tm, H8, C512, c_cg), _rows(tm, C512, c_cg),
                  _prev(tm, H8, C512, c_x), _rows(tm, C512, c_x), _full((8, C512))],
        out_specs=[_rows(tm, C512, 0), _rows(tm, C512, 0), _full((8, C512)), _full((8, C512))],
        out_shape=[jax.ShapeDtypeStruct((s, C512), F32), jax.ShapeDtypeStruct((s, C512), BF16),
                   jax.ShapeDtypeStruct((8, C512), F32), jax.ShapeDtypeStruct((8, C512), F32)],
        scratch_shapes=[pltpu.VMEM((tm + H8, C512), F32)], compiler_params=_params(("arbitrary",)),
    )(dzb, proj, proj, proj, proj, proj, w8)


def _sc_bwd2(dconv, proj, w8):
    s = proj.shape[0]
    tm = _tile(s, TM)
    c_cg, c_x = SCG0 // C512, SX0 // C512
    nt = s // tm

    def body(dm, dn, cm, xm, w_ref, dcg_ref, dx_ref, acc_ref, buf):
        i = pl.program_id(0)
        buf[0:tm, :] = dm[...]
        nxt = dn[...]
        buf[tm:tm + H8, :] = jnp.where(i == nt - 1, jnp.zeros_like(nxt), nxt)
        du = _conv3_t(w_ref, buf, tm)
        dcg = du * xm[...]
        dx = du * cm[...]
        dcg_ref[...] = dcg.astype(BF16)
        dx_ref[...] = dx.astype(BF16)
        _acc_rows(acc_ref, i == 0, [_colsum(dcg), _colsum(dx)])

    return pl.pallas_call(
        body, name="sc_bwd2", grid=(nt,),
        in_specs=[_rows(tm, C512, 0), _next(tm, H8, C512, 0, s), _rows(tm, C512, c_cg), _rows(tm, C512, c_x),
                  _full((8, C512))],
        out_specs=[_rows(tm, C512, 0), _rows(tm, C512, 0), _full((8, C512))],
        out_shape=[jax.ShapeDtypeStruct((s, C512), BF16), jax.ShapeDtypeStruct((s, C512), BF16),
                   jax.ShapeDtypeStruct((8, C512), F32)],
        scratch_shapes=[pltpu.VMEM((tm + H8, C512), F32)], compiler_params=_params(("arbitrary",)),
    )(dconv, dconv, proj, proj, w8)


QLAT = 256
KVLAT = 128


def _rms(x, g):
    r = lax.rsqrt(jnp.mean(x * x, axis=-1, keepdims=True) + RMS_EPS)
    return x * r * g, r


def _rms_bwd(dy, x, g, r):
    u = dy * g
    dx = r * u - x * (r * r * r) * jnp.mean(u * x, axis=-1, keepdims=True)
    return dx, _colsum(dy * x * r)


def _lat_fwd(proj, gq, gkv):
    s = proj.shape[0]
    tm = _tile(s, TM)

    def body(q_ref, kv_ref, gq_ref, gkv_ref, qn_ref, kvn_ref):
        qn_ref[...] = _rms(q_ref[...], gq_ref[...])[0].astype(BF16)
        kvn_ref[...] = _rms(kv_ref[...], gkv_ref[...])[0].astype(BF16)

    return pl.pallas_call(
        body, name="lat_fwd", grid=(s // tm,),
        in_specs=[_rows(tm, QLAT, QL0 // QLAT), _rows(tm, KVLAT, KVL0 // KVLAT), _full((1, QLAT)), _full((1, KVLAT))],
        out_specs=[_rows(tm, QLAT, 0), _rows(tm, KVLAT, 0)],
        out_shape=[jax.ShapeDtypeStruct((s, QLAT), BF16), jax.ShapeDtypeStruct((s, KVLAT), BF16)],
        compiler_params=_params(("parallel",)),
    )(proj, proj, gq, gkv)


def _lat_bwd(dqn, dkvn, proj, gq, gkv):
    s = proj.shape[0]
    tm = _tile(s, TM)

    def body(dqn_ref, dkvn_ref, q_ref, kv_ref, gq_ref, gkv_ref, dq_ref, dkv_ref, accq_ref, acckv_ref):
        first = pl.program_id(0) == 0
        q, kv = q_ref[...], kv_ref[...]
        gqv, gkvv = gq_ref[...], gkv_ref[...]
        dq, dgq = _rms_bwd(dqn_ref[...], q, gqv, _rms(q, gqv)[1])
        dkv, dgkv = _rms_bwd(dkvn_ref[...], kv, gkvv, _rms(kv, gkvv)[1])
        dq_ref[...] = dq.astype(BF16)
        dkv_ref[...] = dkv.astype(BF16)
        _acc_rows(accq_ref, first, [dgq, _colsum(dq)])
        _acc_rows(acckv_ref, first, [dgkv, _colsum(dkv)])

    return pl.pallas_call(
        body, name="lat_bwd", grid=(s // tm,),
        in_specs=[_rows(tm, QLAT, 0), _rows(tm, KVLAT, 0), _rows(tm, QLAT, QL0 // QLAT), _rows(tm, KVLAT, KVL0 // KVLAT),
                  _full((1, QLAT)), _full((1, KVLAT))],
        out_specs=[_rows(tm, QLAT, 0), _rows(tm, KVLAT, 0), _full((8, QLAT)), _full((8, KVLAT))],
        out_shape=[jax.ShapeDtypeStruct((s, QLAT), BF16), jax.ShapeDtypeStruct((s, KVLAT), BF16),
                   jax.ShapeDtypeStruct((8, QLAT), F32), jax.ShapeDtypeStruct((8, KVLAT), F32)],
        compiler_params=_params(("arbitrary",)),
    )(dqn, dkvn, proj, proj, gq, gkv)


def _rope(x, c, s1, s2):
    return x * c + pltpu.roll(x, HEAD_PAD - ROPE // 2, 1) * s1 + pltpu.roll(x, ROPE // 2, 1) * s2


def _rope_t(d, c, s1, s2):
    return d * c + pltpu.roll(d * s1, ROPE // 2, 1) + pltpu.roll(d * s2, HEAD_PAD - ROPE // 2, 1)


def _rope_fwd(qp, kvp, proj, ct, s1t, s2t):
    s = proj.shape[0]
    tm = _tile(s, TM)

    def body(q_ref, kv_ref, kr_ref, c_ref, s1_ref, s2_ref, qo, ko, vo):
        c, s1, s2 = c_ref[...], s1_ref[...], s2_ref[...]
        kr = _rope(kr_ref[...], c, s1, s2)
        for h in range(HEADS):
            cs = slice(h * HEAD_PAD, (h + 1) * HEAD_PAD)
            qo[:, cs] = _rope(q_ref[:, cs], c, s1, s2).astype(BF16)
            ko[:, cs] = (kv_ref[:, 2 * h * HEAD_PAD:(2 * h + 1) * HEAD_PAD] + kr).astype(BF16)
            vo[:, cs] = kv_ref[:, (2 * h + 1) * HEAD_PAD:(2 * h + 2) * HEAD_PAD].astype(BF16)

    tab = _rows(tm, HEAD_PAD, 0)
    return pl.pallas_call(
        body, name="rope_fwd", grid=(s // tm,),
        in_specs=[_rows(tm, HEADS * HEAD_PAD, 0), _rows(tm, 2 * HEADS * HEAD_PAD, 0), _rows(tm, HEAD_PAD, KR0 // HEAD_PAD),
                  tab, tab, tab],
        out_specs=[_rows(tm, HEADS * HEAD_PAD, 0)] * 3,
        out_shape=[jax.ShapeDtypeStruct((s, HEADS * HEAD_PAD), BF16)] * 3,
        compiler_params=_params(("parallel",)),
    )(qp, kvp, proj, ct, s1t, s2t)


def _rope_bwd(dq, dkv, ct, s1t, s2t):
    s = dq.shape[0]
    tm = _tile(s, TM)

    def body(dq_ref, dkv_ref, c_ref, s1_ref, s2_ref, dqo, dkr_ref, acc_ref):
        c, s1, s2 = c_ref[...], s1_ref[...], s2_ref[...]
        tot = dkv_ref[:, 0:HEAD_PAD]
        for h in range(HEADS):
            cs = slice(h * HEAD_PAD, (h + 1) * HEAD_PAD)
            dqo[:, cs] = _rope_t(dq_ref[:, cs], c, s1, s2).astype(BF16)
            if h:
                tot = tot + dkv_ref[:, 2 * h * HEAD_PAD:(2 * h + 1) * HEAD_PAD]
        lane = lax.broadcasted_iota(jnp.int32, (tm, HEAD_PAD), 1)
        dkr = jnp.where((lane >= NOPE) & (lane < QK_DIM), _rope_t(tot, c, s1, s2), 0.0)
        dkr_ref[...] = dkr.astype(BF16)
        _acc_rows(acc_ref, pl.program_id(0) == 0, [_colsum(dkr)])

    tab = _rows(tm, HEAD_PAD, 0)
    return pl.pallas_call(
        body, name="rope_bwd", grid=(s // tm,),
        in_specs=[_rows(tm, HEADS * HEAD_PAD, 0), _rows(tm, 2 * HEADS * HEAD_PAD, 0), tab, tab, tab],
        out_specs=[_rows(tm, HEADS * HEAD_PAD, 0), tab, _full((8, HEAD_PAD))],
        out_shape=[jax.ShapeDtypeStruct((s, HEADS * HEAD_PAD), BF16), jax.ShapeDtypeStruct((s, HEAD_PAD), BF16),
                   jax.ShapeDtypeStruct((8, HEAD_PAD), F32)],
        compiler_params=_params(("arbitrary",)),
    )(dq, dkv, ct, s1t, s2t)


_NT = (((1,), (1,)), ((), ()))
_TN = (((0,), (0,)), ((), ()))
_NN = (((1,), (0,)), ((), ()))


def _tile_rows(ref, t, tq):
    return ref[pl.ds(pl.multiple_of(t * tq, tq), tq), :]


def _attn_fwd(q, k, v):
    s = q.shape[0]
    tq = _tile(s, TQ)
    nq = s // tq

    def body(q_ref, k_ref, v_ref, o_ref, lse_ref):
        qi = pl.program_id(1)
        q_t = q_ref[...]

        def raw(ki):
            return lax.dot_general(_tile_rows(k_ref, ki, tq), q_t, _NT, preferred_element_type=F32)

        def process(ki, st, m, l, acc, masked):
            sc = st * ATT_SCALE
            if masked:
                key = lax.broadcasted_iota(jnp.int32, (tq, tq), 0)
                qry = lax.broadcasted_iota(jnp.int32, (tq, tq), 1)
                sc = jnp.where(key <= qry, sc, -jnp.inf)
            m_new = jnp.maximum(m, jnp.max(sc, axis=0, keepdims=True))
            pt = jnp.exp(sc - m_new)
            a = jnp.exp(m - m_new)
            pv = lax.dot_general(_tile_rows(v_ref, ki, tq), pt.astype(BF16), _TN, preferred_element_type=F32)
            return m_new, a * l + jnp.sum(pt, axis=0, keepdims=True), a * acc + pv

        def loop_body(ki, c):
            return process(ki, raw(ki), c[0], c[1], c[2], False)

        init = (jnp.full((1, tq), -jnp.inf, F32), jnp.zeros((1, tq), F32), jnp.zeros((HEAD_PAD, tq), F32))
        c = lax.fori_loop(0, qi, loop_body, init)
        m, l, acc = process(qi, raw(qi), c[0], c[1], c[2], True)
        o_ref[...] = jnp.transpose(acc / l)
        lse_ref[0] = jnp.broadcast_to(m + jnp.log(l), (8, tq))

    qspec = pl.BlockSpec((tq, HEAD_PAD), lambda h, qi: (qi, h))
    kspec = pl.BlockSpec((s, HEAD_PAD), lambda h, qi: (0, h))
    return pl.pallas_call(
        body, name="attn_fwd", grid=(HEADS, nq),
        in_specs=[qspec, kspec, kspec],
        out_specs=[qspec, pl.BlockSpec((1, 8, tq), lambda h, qi: (h, 0, qi))],
        out_shape=[jax.ShapeDtypeStruct((s, HEADS * HEAD_PAD), F32), jax.ShapeDtypeStruct((HEADS, 8, s), F32)],
        compiler_params=_params(("parallel", "parallel")),
    )(q, k, v)


def _attn_prep(d_o, o):
    s = o.shape[0]
    tm = _tile(s, TM)

    def body(do_ref, o_ref, dl_ref, dob_ref):
        for h in range(HEADS):
            cs = slice(h * HEAD_PAD, (h + 1) * HEAD_PAD)
            dov = do_ref[:, cs]
            row = jnp.sum(jnp.transpose(dov * o_ref[:, cs]), axis=0, keepdims=True)
            dl_ref[h] = jnp.broadcast_to(row, (8, tm))
            dob_ref[:, cs] = dov.astype(BF16)

    blk = _rows(tm, HEADS * HEAD_PAD, 0)
    return pl.pallas_call(
        body, name="attn_prep", grid=(s // tm,),
        in_specs=[blk, blk], out_specs=[pl.BlockSpec((HEADS, 8, tm), lambda i: (0, 0, i)), blk],
        out_shape=[jax.ShapeDtypeStruct((HEADS, 8, s), F32), jax.ShapeDtypeStruct((s, HEADS * HEAD_PAD), BF16)],
        compiler_params=_params(("parallel",)),
    )(d_o, o)


def _attn_bwd(q, k, v, d_o, lse, delta):
    s = q.shape[0]
    tk = _tile(s, TQ_BWD)
    tq = _tile(s, TQ_BWD_Q)
    assert tq in (tk, 2 * tk)
    nq = s // tq

    def body(q_ref, k_ref, v_ref, do_ref, lse_ref, dl_ref, dq_ref, dkv_ref):
        ki = pl.program_id(1)
        k_t, v_t = k_ref[...], v_ref[...]
        q0 = (ki * tk) // tq

        @pl.when(ki == 0)
        def _():
            dq_ref[...] = jnp.zeros_like(dq_ref)

        def tile(off, w, dk, dv, diagonal):
            cols = pl.ds(pl.multiple_of(off, tk), w)
            q_i, do_i = q_ref[cols, :], do_ref[cols, :]
            sc = lax.dot_general(k_t, q_i, _NT, preferred_element_type=F32) * ATT_SCALE
            dpt = lax.dot_general(v_t, do_i, _NT, preferred_element_type=F32)
            if diagonal:
                sc = jnp.where(lax.broadcasted_iota(jnp.int32, (tk, w), 0) <= lax.broadcasted_iota(jnp.int32, (tk, w), 1),
                               sc, -jnp.inf)
            pt = jnp.exp(sc - lse_ref[0, 0:1, cols])
            dsb = (pt * (dpt - dl_ref[0, 0:1, cols]) * ATT_SCALE).astype(BF16)
            dv = dv + lax.dot_general(pt.astype(BF16), do_i, _NN, preferred_element_type=F32)
            dk = dk + lax.dot_general(dsb, q_i, _NN, preferred_element_type=F32)
            dq_ref[cols, :] += lax.dot_general(dsb, k_t, _TN, preferred_element_type=F32)
            return dk, dv

        zero = jnp.zeros((tk, HEAD_PAD), F32)
        dk, dv = tile(ki * tk, tk, zero, zero, True)
        if tq != tk:
            dk, dv = lax.cond((ki * tk) % tq == 0, lambda a, b: tile((ki + 1) * tk, tk, a, b, False), lambda a, b: (a, b), dk, dv)

        def loop_body(qi, c):
            return tile(qi * tq, tq, c[0], c[1], False)

        c = lax.fori_loop(q0 + 1, nq, loop_body, (dk, dv))
        dkv_ref[:, 0:HEAD_PAD] = c[0]
        dkv_ref[:, HEAD_PAD:2 * HEAD_PAD] = c[1]

    full = pl.BlockSpec((s, HEAD_PAD), lambda h, ki: (0, h))
    tile = pl.BlockSpec((tk, HEAD_PAD), lambda h, ki: (ki, h))
    stat = pl.BlockSpec((1, 8, s), lambda h, ki: (h, 0, 0))
    return pl.pallas_call(
        body, name="attn_bwd", grid=(HEADS, s // tk),
        in_specs=[full, tile, tile, full, stat, stat],
        out_specs=[full, pl.BlockSpec((tk, 2 * HEAD_PAD), lambda h, ki: (ki, h))],
        out_shape=[jax.ShapeDtypeStruct((s, HEADS * HEAD_PAD), F32), jax.ShapeDtypeStruct((s, HEADS * 2 * HEAD_PAD), F32)],
        compiler_params=_params(("parallel", "arbitrary")),
    )(q, k, v, d_o, lse, delta)


PH = 16
PG = 128


def _pool_pd(buf, u_main_ref, g, i, tm):
    w = POOL_WINDOWS[g]
    cs = pl.ds(g * PG, PG)
    tot = buf[pl.ds(PH, tm), cs]
    for j in range(1, w):
        tot = tot + buf[pl.ds(PH - j, tm), cs]
    t = i * tm + lax.broadcasted_iota(jnp.int32, (tm, PG), 0)
    cnt = jnp.minimum(t + 1, w).astype(F32)
    return tot / cnt - u_main_ref[:, cs]


def _pool_ubuf(buf, up, um, first, tm):
    p = up[...]
    buf[0:PH, :] = jnp.where(first, jnp.zeros_like(p), p)
    buf[PH:PH + tm, :] = um[...]


def _pool_fwd(proj, w_pool, scale):
    s = proj.shape[0]
    tm = _tile(s, TM)
    cu = PU0 // C512

    def body(up, um, w_ref, sc_ref, zd_ref, buf):
        i = pl.program_id(0)
        _pool_ubuf(buf, up, um, i == 0, tm)
        for g in range(4):
            pd = _pool_pd(buf, um, g, i, tm).astype(BF16)
            e = lax.dot_general(pd, w_ref[g], _NN, preferred_element_type=F32)
            zd_ref[:, g * PG:(g + 1) * PG] = (e * sc_ref[:, g * PG:(g + 1) * PG]).astype(BF16)

    return pl.pallas_call(
        body, name="pool_fwd", grid=(s // tm,),
        in_specs=[_prev(tm, PH, C512, cu), _rows(tm, C512, cu), _full((4, PG, PG)), _full((1, C512))],
        out_specs=_rows(tm, C512, 0), out_shape=jax.ShapeDtypeStruct((s, C512), BF16),
        scratch_shapes=[pltpu.VMEM((tm + PH, C512), F32)], compiler_params=_params(("parallel",)),
    )(proj, proj, w_pool, scale)


def _pool_bwd1(dzd, proj, w_pool, scale):
    s = proj.shape[0]
    tm = _tile(s, TM)
    cu = PU0 // C512

    def body(dz_ref, up, um, w_ref, sc_ref, dpd_ref, dw_ref, acc_ref, buf):
        i = pl.program_id(0)
        first = i == 0
        _pool_ubuf(buf, up, um, first, tm)

        @pl.when(first)
        def _():
            dw_ref[...] = jnp.zeros_like(dw_ref)
            acc_ref[...] = jnp.zeros_like(acc_ref)
        for g in range(4):
            cs = slice(g * PG, (g + 1) * PG)
            pd = _pool_pd(buf, um, g, i, tm).astype(BF16)
            wg = w_ref[g]
            e = lax.dot_general(pd, wg, _NN, preferred_element_type=F32)
            dz = dz_ref[:, cs]
            acc_ref[0:1, cs] += _colsum(dz * e)
            de = (dz * sc_ref[:, cs]).astype(BF16)
            dw_ref[g] += lax.dot_general(pd, de, _TN, preferred_element_type=F32)
            dpd_ref[:, cs] = lax.dot_general(de, wg, _NT, preferred_element_type=F32)

    return pl.pallas_call(
        body, name="pool_bwd1", grid=(s // tm,),
        in_specs=[_rows(tm, C512, 0), _prev(tm, PH, C512, cu), _rows(tm, C512, cu), _full((4, PG, PG)), _full((1, C512))],
        out_specs=[_rows(tm, C512, 0), _full((4, PG, PG)), _full((8, C512))],
        out_shape=[jax.ShapeDtypeStruct((s, C512), F32), jax.ShapeDtypeStruct((4, PG, PG), F32),
                   jax.ShapeDtypeStruct((8, C512), F32)],
        scratch_shapes=[pltpu.VMEM((tm + PH, C512), F32)], compiler_params=_params(("arbitrary",)),
    )(dzd, proj, proj, w_pool, scale)


def _pool_bwd2(dpd, s):
    tm = _tile(s, TM)
    nt = s // tm

    def body(dm, dn, du_ref, acc_ref, buf):
        i = pl.program_id(0)
        buf[0:tm, :] = dm[...]
        nxt = dn[...]
        buf[tm:tm + PH, :] = jnp.where(i == nt - 1, jnp.zeros_like(nxt), nxt)
        t = i * tm + lax.broadcasted_iota(jnp.int32, (tm + PH, PG), 0)
        cols = []
        for g, w in enumerate(POOL_WINDOWS):
            cs = pl.ds(g * PG, PG)
            cnt = jnp.minimum(t + 1, w).astype(F32)
            buf[:, cs] = buf[:, cs] / cnt
        for g, w in enumerate(POOL_WINDOWS):
            cs = pl.ds(g * PG, PG)
            tot = buf[pl.ds(0, tm), cs]
            for j in range(1, w):
                tot = tot + buf[pl.ds(j, tm), cs]
            du = tot - dm[:, cs]
            du_ref[:, cs] = du.astype(BF16)
            cols.append(_colsum(du))
        _acc_rows(acc_ref, i == 0, [jnp.concatenate(cols, axis=1)])

    return pl.pallas_call(
        body, name="pool_bwd2", grid=(nt,),
        in_specs=[_rows(tm, C512, 0), _next(tm, PH, C512, 0, s)],
        out_specs=[_rows(tm, C512, 0), _full((8, C512))],
        out_shape=[jax.ShapeDtypeStruct((s, C512), BF16), jax.ShapeDtypeStruct((8, C512), F32)],
        scratch_shapes=[pltpu.VMEM((tm + PH, C512), F32)], compiler_params=_params(("arbitrary",)),
    )(dpd, dpd)


def _merge_specs(tm):
    return [_rows(tm, C512, 0), _rows(tm, C512, 0), _rows(tm, D, 0), _rows(tm, C512, 0),
            _rows(tm, 4 * D, GATES0 // (4 * D)),
            _full((C512, D)), _full((C512, D)), _full((D, D)), _full((C512, D))]


def _branch_ys(za, zb, o, zd, wa, wb, wc, wd):
    zs = (za[...], zb[...], o[...].astype(BF16), zd[...])
    return [lax.dot_general(z, w[...], _NN, preferred_element_type=F32) for z, w in zip(zs, (wa, wb, wc, wd))]


def _merge_fwd(za, zb, o, zd, proj, wa, wb, wc, wd):
    s = proj.shape[0]
    tm = _tile(s, TM_WIDE)

    def body(za_r, zb_r, o_r, zd_r, g_ref, wa_r, wb_r, wc_r, wd_r, m_ref):
        ys = _branch_ys(za_r, zb_r, o_r, zd_r, wa_r, wb_r, wc_r, wd_r)
        acc = _sig(g_ref[:, 0:D]) * ys[0]
        for b in range(1, 4):
            acc = acc + _sig(g_ref[:, b * D:(b + 1) * D]) * ys[b]
        m_ref[...] = acc.astype(BF16)

    return pl.pallas_call(
        body, name="merge_fwd", grid=(s // tm,), in_specs=_merge_specs(tm),
        out_specs=_rows(tm, D, 0), out_shape=jax.ShapeDtypeStruct((s, D), BF16),
        compiler_params=_params(("parallel",)),
    )(za, zb, o, zd, proj, wa, wb, wc, wd)


def _merge_bwd(dmerged, za, zb, o, zd, proj, wa, wb, wc, wd):
    s = proj.shape[0]
    tm = _tile(s, TM_WIDE)

    def body(dm_ref, za_r, zb_r, o_r, zd_r, g_ref, wa_r, wb_r, wc_r, wd_r,
             dya, dyb, dyc, dyd, dza, dzb, d_o, dzd, dg_ref, acc_ref):
        ys = _branch_ys(za_r, zb_r, o_r, zd_r, wa_r, wb_r, wc_r, wd_r)
        dm = dm_ref[...]
        sums = []
        for b, (dy_ref, dz_ref, w_r) in enumerate(((dya, dza, wa_r), (dyb, dzb, wb_r), (dyc, d_o, wc_r), (dyd, dzd, wd_r))):
            gt = _sig(g_ref[:, b * D:(b + 1) * D])
            dg = dm * ys[b] * gt * (1.0 - gt)
            dg_ref[:, b * D:(b + 1) * D] = dg.astype(BF16)
            sums.append(_colsum(dg))
            dy = (dm * gt).astype(BF16)
            dy_ref[...] = dy
            dz_ref[...] = lax.dot_general(dy, w_r[...], _NT, preferred_element_type=F32)
        _acc_rows(acc_ref, pl.program_id(0) == 0, [jnp.concatenate(sums, axis=1)])

    bf = lambda c: jax.ShapeDtypeStruct((s, c), BF16)
    f32 = lambda c: jax.ShapeDtypeStruct((s, c), F32)
    return pl.pallas_call(
        body, name="merge_bwd", grid=(s // tm,), in_specs=[_rows(tm, D, 0)] + _merge_specs(tm),
        out_specs=[_rows(tm, D, 0)] * 4 + [_rows(tm, C512, 0), _rows(tm, C512, 0), _rows(tm, D, 0), _rows(tm, C512, 0),
                                           _rows(tm, 4 * D, 0), _full((8, 4 * D))],
        out_shape=[bf(D)] * 4 + [f32(C512), f32(C512), f32(D), f32(C512), bf(4 * D), jax.ShapeDtypeStruct((8, 4 * D), F32)],
        compiler_params=_params(("arbitrary",)),
    )(dmerged, za, zb, o, zd, proj, wa, wb, wc, wd)


def _mm_res_ln(a, w, xres, prm, name):
    s, k = a.shape
    tm = _tile(s, TM_WIDE)

    def body(a_ref, w_ref, x_ref, p_ref, y_ref, xn_ref, hn_ref):
        y = lax.dot_general(a_ref[...], w_ref[...], _NN, preferred_element_type=F32)
        y_ref[...] = y
        xh, _ = _ln_stats(ALPHA * x_ref[...] + (1.0 + p_ref[0:1, :]) * y)
        xn = xh * p_ref[3:4, :] + p_ref[4:5, :]
        xn_ref[...] = xn
        hn_ref[...] = (xn * (1.0 + p_ref[1:2, :]) + p_ref[2:3, :]).astype(BF16)

    return pl.pallas_call(
        body, name=name, grid=(s // tm,),
        in_specs=[_rows(tm, k, 0), _full((k, D)), _rows(tm, D, 0), _full((8, D))],
        out_specs=[_rows(tm, D, 0)] * 3,
        out_shape=[jax.ShapeDtypeStruct((s, D), F32), jax.ShapeDtypeStruct((s, D), F32), jax.ShapeDtypeStruct((s, D), BF16)],
        compiler_params=_params(("parallel",)),
    )(a, w, xres, prm)


def _ln_res_bwd(dres_next, dh, xres, y, prm):
    s = xres.shape[0]
    tm = _tile(s, TM)

    def body(dn_ref, dh_ref, x_ref, y_ref, p_ref, dres_ref, dy_ref, acc_ref):
        gam, lng = p_ref[0:1, :], p_ref[3:4, :]
        yv = y_ref[...]
        xh, rstd = _ln_stats(ALPHA * x_ref[...] + (1.0 + gam) * yv)
        xn = xh * lng + p_ref[4:5, :]
        dh_v = dh_ref[...]
        dxn = dn_ref[...] + dh_v * (1.0 + p_ref[1:2, :])
        dr = _ln_bwd(dxn * lng, xh, rstd)
        dres_ref[...] = ALPHA * dr
        dy_ref[...] = ((1.0 + gam) * dr).astype(BF16)
        _acc_rows(acc_ref, pl.program_id(0) == 0,
                  [_colsum(dr * yv), _colsum(dh_v * xn), _colsum(dh_v), _colsum(dxn * xh), _colsum(dxn)])

    return pl.pallas_call(
        body, name="ln_res_bwd", grid=(s // tm,),
        in_specs=[_rows(tm, D, 0)] * 4 + [_full((8, D))],
        out_specs=[_rows(tm, D, 0), _rows(tm, D, 0), _full((8, D))],
        out_shape=[jax.ShapeDtypeStruct((s, D), F32), jax.ShapeDtypeStruct((s, D), BF16), jax.ShapeDtypeStruct((8, D), F32)],
        compiler_params=_params(("arbitrary",)),
    )(dres_next, dh, xres, y, prm)


FC = 16


def _shift_down(cur, prev, k, rowi):
    return jnp.where(rowi >= k, pltpu.roll(cur, k, 0), pltpu.roll(prev, k, 0))


def _shift_up(cur, nxt, k, rowi):
    return jnp.where(rowi < FC - k, pltpu.roll(cur, FC - k, 0), pltpu.roll(nxt, FC - k, 0))


def _conv3_chunk(w, cur, prev, rowi):
    return w[2] * cur + w[1] * _shift_down(cur, prev, 1, rowi) + w[0] * _shift_down(cur, prev, 2, rowi)


def _conv3_t_chunk(w, cur, nxt, rowi):
    return w[2] * cur + w[1] * _shift_up(cur, nxt, 1, rowi) + w[0] * _shift_up(cur, nxt, 2, rowi)


def _chunk_rows(j):
    return pl.ds(pl.multiple_of(j * FC, FC), FC)


def _ffn_chunk_specs(tm, s):
    r = tm // FC
    last = s // FC - 1
    out = []
    for half in (0, 1):
        out.append((pl.BlockSpec((FC, D_FF), lambda i, half=half: (jnp.maximum(i * r - 1, 0), half)),
                    pl.BlockSpec((tm, D_FF), lambda i, half=half: (i, half)),
                    pl.BlockSpec((FC, D_FF), lambda i, half=half: (jnp.minimum((i + 1) * r, last), half)),
                    pl.BlockSpec((8, D_FF), lambda i, half=half: (0, half))))
    return out


def _ffn_fwd(up, w8):
    s = up.shape[0]
    tm = _tile(s, TM_WIDE)
    (pv_s, mv_s, _, wv_s), (pg_s, mg_s, _, wg_s) = _ffn_chunk_specs(tm, s)

    def body(pv, mv, pg, mg, wv_ref, wg_ref, a_ref):
        first = pl.program_id(0) == 0
        lg = 2 * LANE
        rowi = lax.broadcasted_iota(jnp.int32, (FC, lg), 0)
        zero = jnp.zeros((FC, lg), F32)
        for cg in range(D_FF // lg):
            cs = slice(cg * lg, (cg + 1) * lg)
            wv = [wv_ref[k:k + 1, cs] for k in range(SC_W)]
            wg = [wg_ref[k:k + 1, cs] for k in range(SC_W)]

            def step(j, carry, cs=cs, wv=wv, wg=wg):
                rows = _chunk_rows(j)
                xv, xg = mv[rows, cs], mg[rows, cs]
                gate = _conv3_chunk(wg, xg, carry[1], rowi)
                a_ref[rows, cs] = (gate * _sig(gate) * _conv3_chunk(wv, xv, carry[0], rowi)).astype(BF16)
                return xv, xg

            lax.fori_loop(0, tm // (2 * FC), lambda j, c, step=step: step(2 * j + 1, step(2 * j, c)),
                          (jnp.where(first, zero, pv[:, cs]), jnp.where(first, zero, pg[:, cs])))

    return pl.pallas_call(
        body, name="ffn_fwd", grid=(s // tm,), in_specs=[pv_s, mv_s, pg_s, mg_s, wv_s, wg_s],
        out_specs=_rows(tm, D_FF, 0), out_shape=jax.ShapeDtypeStruct((s, D_FF), BF16),
        compiler_params=_params(("parallel",)),
    )(up, up, up, up, w8, w8)


def _ffn_bwd(da, up, w8):
    s = up.shape[0]
    tm = _tile(s, TM_WIDE)
    n, nt = tm // FC, s // tm
    (pv_s, mv_s, nv_s, wv_s), (pg_s, mg_s, ng_s, wg_s) = _ffn_chunk_specs(tm, s)

    def body(dam, dan, pv, mv, nv, pg, mg, ng, wv_ref, wg_ref, dup_ref, dw_ref):
        i = pl.program_id(0)
        first, last = i == 0, i == nt - 1
        rowi = lax.broadcasted_iota(jnp.int32, (FC, LANE), 0)
        zero = jnp.zeros((FC, LANE), F32)

        @pl.when(first)
        def _():
            dw_ref[...] = jnp.zeros_like(dw_ref)

        for cg in range(D_FF // LANE):
            cs = slice(cg * LANE, (cg + 1) * LANE)
            cs_g = slice(D_FF + cg * LANE, D_FF + (cg + 1) * LANE)
            wv = [wv_ref[k:k + 1, cs] for k in range(SC_W)]
            wg = [wg_ref[k:k + 1, cs] for k in range(SC_W)]

            def conv_grads(xv, xg, xpv, xpg, dav, wv=wv, wg=wg):
                val, gate = _conv3_chunk(wv, xv, xpv, rowi), _conv3_chunk(wg, xg, xpg, rowi)
                sg = _sig(gate)
                return dav * gate * sg, dav * val * (sg * (1.0 + gate * (1.0 - sg)))

            def step(j, c, cs=cs, cs_g=cs_g, wv=wv, wg=wg, conv_grads=conv_grads):
                xpv, xpg, dvp, dgp = c[:4]
                rows = _chunk_rows(j)
                xv, xg = mv[rows, cs], mg[rows, cs]
                dv, dg = conv_grads(xv, xg, xpv, xpg, dam[rows, cs])
                prow = _chunk_rows(jnp.maximum(j - 1, 0))
                dup_ref[prow, cs] = _conv3_t_chunk(wv, dvp, dv, rowi).astype(BF16)
                dup_ref[prow, cs_g] = _conv3_t_chunk(wg, dgp, dg, rowi).astype(BF16)
                accs = (c[4] + dv * _shift_down(xv, xpv, 2, rowi), c[5] + dv * _shift_down(xv, xpv, 1, rowi), c[6] + dv * xv,
                        c[7] + dg * _shift_down(xg, xpg, 2, rowi), c[8] + dg * _shift_down(xg, xpg, 1, rowi), c[9] + dg * xg)
                return (xv, xg, dv, dg) + accs

            init = (jnp.where(first, zero, pv[:, cs]), jnp.where(first, zero, pg[:, cs]), zero, zero) + (zero,) * 6
            c = lax.fori_loop(0, n // 2, lambda j, c, step=step: step(2 * j + 1, step(2 * j, c)), init)
            dv_n, dg_n = conv_grads(nv[:, cs], ng[:, cs], c[0], c[1], jnp.where(last, zero, dan[:, cs]))
            dup_ref[tm - FC:tm, cs] = _conv3_t_chunk(wv, c[2], dv_n, rowi).astype(BF16)
            dup_ref[tm - FC:tm, cs_g] = _conv3_t_chunk(wg, c[3], dg_n, rowi).astype(BF16)
            for k in range(SC_W):
                dw_ref[k:k + 1, cs] += _colsum(c[4 + k])
                dw_ref[k:k + 1, cs_g] += _colsum(c[7 + k])

    r = tm // FC
    da_next = pl.BlockSpec((FC, D_FF), lambda i: (jnp.minimum((i + 1) * r, s // FC - 1), 0))
    return pl.pallas_call(
        body, name="ffn_bwd", grid=(nt,),
        in_specs=[_rows(tm, D_FF, 0), da_next, pv_s, mv_s, nv_s, pg_s, mg_s, ng_s, wv_s, wg_s],
        out_specs=[_rows(tm, 2 * D_FF, 0), _full((8, 2 * D_FF))],
        out_shape=[jax.ShapeDtypeStruct((s, 2 * D_FF), BF16), jax.ShapeDtypeStruct((8, 2 * D_FF), F32)],
        compiler_params=_params(("arbitrary",)),
    )(da, da, up, up, up, up, up, up, w8, w8)


def _loss_head(y, target):
    s = y.shape[0]
    tm = _tile(s, TM)

    def body(y_ref, t_ref, dy_ref, l_ref):
        err = y_ref[...] - t_ref[...]
        dy_ref[...] = err * (1.0 / D)
        part = 0.5 * jnp.sum(jnp.mean(err * err, axis=-1, keepdims=True), axis=0, keepdims=True)

        @pl.when(pl.program_id(0) == 0)
        def _():
            l_ref[...] = jnp.zeros_like(l_ref)
        l_ref[...] += part

    return pl.pallas_call(
        body, name="loss_head", grid=(s // tm,),
        in_specs=[_rows(tm, D, 0)] * 2, out_specs=[_rows(tm, D, 0), _full((8, LANE))],
        out_shape=[jax.ShapeDtypeStruct((s, D), F32), jax.ShapeDtypeStruct((8, LANE), F32)],
        compiler_params=_params(("arbitrary",)),
    )(y, target)


def _silu_rows(c_all):
    def body(c_ref, o_ref):
        cv = c_ref[...]
        o_ref[...] = jnp.concatenate([cv * _sig(cv), jnp.zeros((N_DEV, D), F32)], axis=0).astype(BF16)

    return pl.pallas_call(
        body, name="silu_rows", grid=(1,), in_specs=[_full((N_DEV, D))], out_specs=_full((2 * N_DEV, D)),
        out_shape=jax.ShapeDtypeStruct((2 * N_DEV, D), BF16), compiler_params=_params(("arbitrary",)),
    )(c_all)


GRAD_ROWS = 512


def _sum_parts(p_ref, n):
    g = p_ref[0].astype(F32)
    for j in range(1, n):
        g = g + p_ref[j].astype(F32)
    return g


def _adamw(parts, w, m, v, name):
    n = parts.shape[0]
    r, c = w.shape
    tr = GRAD_ROWS
    assert r % tr == 0 and parts.shape[2] == c, (parts.shape, w.shape)

    def body(p_ref, w_ref, m_ref, v_ref, g_out, d_out, m_out, v_out):
        g = _sum_parts(p_ref, n)
        mn = ADAM_B1 * m_ref[...] + (1.0 - ADAM_B1) * g
        vn = ADAM_B2 * v_ref[...] + (1.0 - ADAM_B2) * (g * g)
        m_hat = mn / (1.0 - ADAM_B1 ** ADAM_STEP)
        v_hat = vn / (1.0 - ADAM_B2 ** ADAM_STEP)
        g_out[...] = g
        d_out[...] = -ADAM_LR * (m_hat / (jnp.sqrt(v_hat) + ADAM_EPS) + ADAM_WD * w_ref[...])
        m_out[...] = mn
        v_out[...] = vn

    blk = pl.BlockSpec((tr, c), lambda i: (i, 0))
    return pl.pallas_call(
        body, name=name, grid=(r // tr,),
        in_specs=[pl.BlockSpec((n, tr, c), lambda i: (0, i, 0)), blk, blk, blk], out_specs=[blk] * 4,
        out_shape=[jax.ShapeDtypeStruct((r, c), F32)] * 4, compiler_params=_params(("parallel",)),
    )(parts, w, m, v)


def _sum8(parts, name):
    n, r, c = parts.shape
    tr = GRAD_ROWS
    assert r % tr == 0, r

    def body(p_ref, g_out):
        g_out[...] = _sum_parts(p_ref, n)

    return pl.pallas_call(
        body, name=name, grid=(r // tr,),
        in_specs=[pl.BlockSpec((n, tr, c), lambda i: (0, i, 0))], out_specs=pl.BlockSpec((tr, c), lambda i: (i, 0)),
        out_shape=jax.ShapeDtypeStruct((r, c), F32), compiler_params=_params(("parallel",)),
    )(parts)


def _peers():
    ix, iy, ic = lax.axis_index("x"), lax.axis_index("y"), lax.axis_index("c")
    me = 4 * ix + 2 * iy + ic
    out = []
    for k in range(1, N_DEV):
        px = 1 - ix if (k >> 2) & 1 else ix
        py = 1 - iy if (k >> 1) & 1 else iy
        pc = 1 - ic if k & 1 else ic
        out.append(((px, py, pc), 4 * px + 2 * py + pc))
    return me, out


_HBM = pl.BlockSpec(memory_space=pltpu.HBM)


def _exchange(x, name, gather):
    shape = ((N_DEV,) + x.shape) if gather else x.shape

    def body(x_ref, o_ref, send_sems, recv_sems, local_sem):
        me, peers = _peers()
        src_of = (lambda p: x_ref) if gather else (lambda p: x_ref.at[p])
        local = pltpu.make_async_copy(src_of(me), o_ref.at[me], local_sem)
        local.start()
        sends = []
        for k, (dev, p) in enumerate(peers):
            cp = pltpu.make_async_remote_copy(src_ref=src_of(p), dst_ref=o_ref.at[me], send_sem=send_sems.at[k],
                                              recv_sem=recv_sems.at[k], device_id=dev, device_id_type=MESH)
            cp.start()
            sends.append(cp)
        for k, (dev, p) in enumerate(peers):
            pltpu.make_async_remote_copy(src_ref=src_of(p), dst_ref=o_ref.at[p], send_sem=send_sems.at[k],
                                         recv_sem=recv_sems.at[k], device_id=dev, device_id_type=MESH).wait_recv()
        for cp in sends:
            cp.wait_send()
        local.wait()

    return pl.pallas_call(
        body, name=name, in_specs=[_HBM], out_specs=_HBM, out_shape=jax.ShapeDtypeStruct(shape, x.dtype),
        scratch_shapes=[pltpu.SemaphoreType.DMA((N_DEV - 1,)), pltpu.SemaphoreType.DMA((N_DEV - 1,)),
                        pltpu.SemaphoreType.DMA],
    )(x)


def _gather_two_level(x, name):
    def body(x_ref, o_ref, send_sems, recv_sems, local_sem):
        ix, iy, ic = lax.axis_index("x"), lax.axis_index("y"), lax.axis_index("c")
        me, sibling = (ix, iy, ic), (ix, iy, 1 - ic)
        chips = [(1 - ix, iy), (ix, 1 - iy), (1 - ix, 1 - iy)]

        def slot(px, py, pc):
            return o_ref.at[4 * px + 2 * py + pc]

        def copy(k, block, to, src=None):
            return pltpu.make_async_remote_copy(src_ref=slot(*block) if src is None else src, dst_ref=slot(*block),
                                                send_sem=send_sems.at[k], recv_sem=recv_sems.at[k],
                                                device_id=to, device_id_type=MESH)

        mine = pltpu.make_async_copy(x_ref, slot(*me), local_sem)
        mine.start()
        first = [copy(0, me, sibling, src=x_ref)] + [copy(1 + j, me, (*chip, ic), src=x_ref) for j, chip in enumerate(chips)]
        for cp in first:
            cp.start()
        passed = [copy(4 + j, (*chip, ic), sibling) for j, chip in enumerate(chips)]
        for j, chip in enumerate(chips):
            copy(1 + j, (*chip, ic), me).wait_recv()
            passed[j].start()
        copy(0, sibling, me).wait_recv()
        for j, chip in enumerate(chips):
            copy(4 + j, (*chip, 1 - ic), me).wait_recv()
        for cp in first + passed:
            cp.wait_send()
        mine.wait()

    return pl.pallas_call(
        body, name=name, in_specs=[_HBM], out_specs=_HBM, out_shape=jax.ShapeDtypeStruct((N_DEV,) + x.shape, x.dtype),
        scratch_shapes=[pltpu.SemaphoreType.DMA((N_DEV - 1,)), pltpu.SemaphoreType.DMA((N_DEV - 1,)),
                        pltpu.SemaphoreType.DMA],
    )(x)


def _pack_rows(arrs, row_mult):
    flat = jnp.concatenate([a.reshape(-1) for a in arrs])
    n = flat.shape[0]
    pad = (-n) % (LANE * row_mult)
    if pad:
        flat = jnp.concatenate([flat, jnp.zeros((pad,), flat.dtype)])
    return flat.reshape(-1, LANE)


def _pack_cols(arrs, row_mult):
    flat = jnp.concatenate(arrs, axis=1)
    n = flat.shape[1]
    pad = (-n) % (LANE * row_mult)
    if pad:
        flat = jnp.concatenate([flat, jnp.zeros((flat.shape[0], pad), flat.dtype)], axis=1)
    return flat.reshape(flat.shape[0], -1, LANE)


def _unpack(flat, shapes):
    out, off = [], 0
    lead = flat.shape[:-1]
    for shp in shapes:
        n = 1
        for d_ in shp:
            n *= d_
        out.append(flat[..., off:off + n].reshape(lead + tuple(shp)))
        off += n
    return out


BIG = (("w_conv_out", 2), ("w_sc_out", 2), ("w_uq", 2), ("w_ukv", 2), ("w_mla_out", 2),
       ("w_pool_out", 2), ("w_o", 1), ("w_down", 1))
TAPS = (("conv_dw", 2), ("sc_dw", 2), ("ffn_dw", 2))
SMALL = ("b_ada", "b_in", "conv_ln_g", "conv_ln_b", "q_norm_g", "kv_norm_g", "w_pool", "pool_scale",
         "ln1_g", "ln1_b", "ln2_g", "ln2_b")
WEIGHTS = ("w_ada", "b_ada", "w_in", "b_in", "conv_dw", "conv_ln_g", "conv_ln_b", "w_conv_out", "sc_dw", "w_sc_out",
           "q_norm_g", "w_uq", "kv_norm_g", "w_ukv", "w_mla_out", "w_pool", "pool_scale", "w_pool_out", "w_o",
           "ln1_g", "ln1_b", "w_up", "ffn_dw", "w_down", "ln2_g", "ln2_b")


def _join(g, axis):
    g = jnp.moveaxis(g, 0, axis)
    shp = list(g.shape)
    shp[axis:axis + 2] = [shp[axis] * shp[axis + 1]]
    return g.reshape(shp)


def _split(full, axis):
    shp = list(full.shape)
    shp[axis:axis + 1] = [N_DEV, shp[axis] // N_DEV]
    return jnp.moveaxis(full.reshape(shp), axis, 0)


def _pad_rows(a, rows):
    return jnp.concatenate([a, jnp.zeros((rows - a.shape[0],) + a.shape[1:], a.dtype)], axis=0)


def _proj_cols(w):
    z = lambda n: jnp.zeros(w.shape[:-1] + (n,), w.dtype)
    return jnp.concatenate([w[..., 3488:7584], w[..., 0:2944], z(NOPE), w[..., 2944:2976], z(HEAD_PAD - QK_DIM),
                            w[..., 2976:3488]], axis=-1)


def _proj_cols_inv(w):
    return jnp.concatenate([w[..., CA0:KR0], w[..., KR0 + NOPE:KR0 + QK_DIM], w[..., PU0:NPROJ], w[..., 0:CA0]], axis=-1)


IN_COLS = 7584
IN_SHARD = IN_COLS // N_DEV
IN_SHARD_PAD = 960
UP_SHARD = 2 * D_FF // N_DEV
_IN_CUTS = (2944, 2976, 3488)
_IN_SEGS = ((3488, IN_COLS), (0, 2944), NOPE, (2944, 2976), HEAD_PAD - QK_DIM, (2976, 3488))


def _proj_rows(g):
    pieces = []
    for seg in _IN_SEGS:
        if isinstance(seg, int):
            pieces.append(jnp.zeros((g.shape[0], seg, g.shape[2]), g.dtype))
            continue
        j = seg[0]
        while j < seg[1]:
            dev = j // IN_SHARD
            e = min(seg[1], (dev + 1) * IN_SHARD)
            r0 = dev * IN_SHARD_PAD + j - dev * IN_SHARD
            pieces.append(g[:, r0:r0 + e - j])
            j = e
    return jnp.concatenate(pieces, axis=1)


def _aligned_row(j):
    if j < 2944:
        return CA0 + j
    if j < 2976:
        return KR0 + NOPE + j - 2944
    if j < 3488:
        return PU0 + j - 2976
    return j - 3488


def _proj_rows_split(gt):
    out = []
    for dev in range(N_DEV):
        j0, j1 = dev * IN_SHARD, (dev + 1) * IN_SHARD
        cuts = [j0] + [c for c in _IN_CUTS if j0 < c < j1] + [j1]
        out.append(jnp.concatenate([gt[_aligned_row(a):_aligned_row(a) + b - a] for a, b in zip(cuts[:-1], cuts[1:])], axis=0))
    return jnp.stack(out)


def _pad_axis(a, axis, size):
    shp = list(a.shape)
    shp[axis] = size - shp[axis]
    return jnp.concatenate([a, jnp.zeros(shp, a.dtype)], axis=axis)


def _all_layer_weights(full, small):
    w = {}
    w["w_inT"] = _proj_rows(full["w_inT"])
    w["b_in"] = _proj_cols(small["b_in"])[:, None, :]
    w["conv_w"] = _pad_axis(full["conv_dw"], 1, 32)
    w["sc_w"] = _pad_axis(full["sc_dw"], 1, 8)
    w["ffn_w"] = _pad_axis(full["ffn_dw"], 1, 8)
    w["w_uq"] = _pad_axis(full["w_uq"].reshape(DEPTH, 256, HEADS, QK_DIM), 3, HEAD_PAD).reshape(DEPTH, 256, -1)
    w["w_ukv"] = _pad_axis(full["w_ukv"].reshape(DEPTH, 128, HEADS, 2, NOPE), 4, HEAD_PAD).reshape(DEPTH, 128, -1)
    w["w_c"] = _pad_axis(full["w_mla_out"].reshape(DEPTH, HEADS, NOPE, D), 2, HEAD_PAD).reshape(DEPTH, HEADS * HEAD_PAD, D)
    w["w_a"], w["w_b"], w["w_d"] = full["w_conv_out"], full["w_sc_out"], full["w_pool_out"]
    w["w_o"], w["w_upT"], w["w_down"] = full["w_o"], full["w_upT"], full["w_down"]
    w["w_pool"] = small["w_pool"].astype(BF16)
    for n in ("conv_ln_g", "conv_ln_b", "q_norm_g", "kv_norm_g", "pool_scale"):
        w[n] = small[n][:, None, :]
    return w


def _prm(rows):
    z = jnp.zeros((D,), F32)
    rows = list(rows) + [z] * (8 - len(rows))
    return jnp.stack(rows)


def _layer_fwd(x, h1, w, prm1, prm2, tabs):
    ct, s1t, s2t = tabs
    proj = _mm(h1, w["w_inT"], "nt", "proj_fwd", bias=w["b_in"], tm=1024, tn=768)
    yconv, za = _conv_a_fwd(proj, w["conv_w"], w["conv_ln_g"], w["conv_ln_b"])
    zb = _sc_fwd(proj, w["sc_w"])
    qn, kvn = _lat_fwd(proj, w["q_norm_g"], w["kv_norm_g"])
    qp = _mm(qn, w["w_uq"], "nn", "uq_fwd")
    kvp = _mm(kvn, w["w_ukv"], "nn", "ukv_fwd")
    q, k, v = _rope_fwd(qp, kvp, proj, ct, s1t, s2t)
    o, lse = _attn_fwd(q, k, v)
    zd = _pool_fwd(proj, w["w_pool"], w["pool_scale"])
    merged = _merge_fwd(za, zb, o, zd, proj, w["w_a"], w["w_b"], w["w_c"], w["w_d"])
    mix, x1, h2 = _mm_res_ln(merged, w["w_o"], x, prm1, "o_res_ln")
    up = _mm(h2, w["w_upT"], "nt", "up_fwd", tm=1024, tn=1408)
    a = _ffn_fwd(up, w["ffn_w"])
    ffn, x2, h_next = _mm_res_ln(a, w["w_down"], x1, prm2, "down_res_ln")
    res = dict(x=x, h1=h1, proj=proj, yconv=yconv, za=za, zb=zb, qn=qn, kvn=kvn, q=q, k=k, v=v, o=o, lse=lse, zd=zd,
               merged=merged, mix=mix, x1=x1, h2=h2, up=up, a=a, ffn=ffn)
    return x2, h_next, res


def _layer_bwd(dres_next, dh_next, r, w, prm1, prm2, tabs):
    ct, s1t, s2t = tabs
    s = r["x"].shape[0]
    g = {}
    dres2, dffn, acc2 = _ln_res_bwd(dres_next, dh_next, r["x1"], r["ffn"], prm2)
    da = _mm(dffn, w["w_down"], "nt", "down_bwd_x", tm=1024, tn=1408)
    g["w_down"] = _mm(r["a"], dffn, "tn", "down_bwd_w", tm=1408)
    dup, dffn_w = _ffn_bwd(da, r["up"], w["ffn_w"])
    g["ffn_dw"] = dffn_w[:SC_W]
    dh2 = _mm(dup, w["w_upT"], "nn", "up_bwd_x", tm=1024, tk=1408)
    g["w_upT"] = _mm(dup, r["h2"], "tn", "up_bwd_w", tm=1408)
    dres1, dmix, acc1 = _ln_res_bwd(dres2, dh2, r["x"], r["mix"], prm1)
    dmerged = _mm(dmix, w["w_o"], "nt", "o_bwd_x")
    g["w_o"] = _mm(r["merged"], dmix, "tn", "o_bwd_w")
    (dya, dyb, dyc, dyd, dza, dzb, d_o, dzd, dgates, accg) = _merge_bwd(
        dmerged, r["za"], r["zb"], r["o"], r["zd"], r["proj"], w["w_a"], w["w_b"], w["w_c"], w["w_d"])
    g["w_conv_out"] = _mm(r["za"], dya, "tn", "branch_bwd_w")
    g["w_sc_out"] = _mm(r["zb"], dyb, "tn", "branch_bwd_w")
    g["w_pool_out"] = _mm(r["zd"], dyd, "tn", "branch_bwd_w")
    gwc = _mm(r["o"], dyc, "tn", "mla_out_bwd_w")
    g["w_mla_out"] = gwc.reshape(HEADS, HEAD_PAD, D)[:, :NOPE].reshape(HEADS * NOPE, D)
    dyconv, dconv_w, acca = _conv_a_bwd1(dza, r["yconv"], r["proj"], w["conv_ln_g"], w["conv_ln_b"])
    g["conv_dw"], g["conv_ln_g"], g["conv_ln_b"] = dconv_w[:CONV_W], acca[0], acca[1]
    d_ca, d_cb, acca2 = _conv_a_bwd2(dyconv, r["proj"], w["conv_w"])
    dconv, d_bg, dsc_w, accb1 = _sc_bwd1(dzb, r["proj"], w["sc_w"])
    g["sc_dw"] = dsc_w[:SC_W]
    d_cg, d_sx, accb2 = _sc_bwd2(dconv, r["proj"], w["sc_w"])
    delta, dob = _attn_prep(d_o, r["o"])
    dq, dkv = _attn_bwd(r["q"], r["k"], r["v"], dob, r["lse"], delta)
    dqp, d_kr, acckr = _rope_bwd(dq, dkv, ct, s1t, s2t)
    dqn = _mm(dqp, w["w_uq"], "nt", "uq_bwd_x")
    guq = _mm(r["qn"], dqp, "tn", "uq_bwd_w")
    g["w_uq"] = guq.reshape(256, HEADS, HEAD_PAD)[:, :, :QK_DIM].reshape(256, HEADS * QK_DIM)
    dkvn = _mm(dkv, w["w_ukv"], "nt", "ukv_bwd_x")
    gukv = _mm(r["kvn"], dkv, "tn", "ukv_bwd_w")
    g["w_ukv"] = gukv.reshape(128, HEADS, 2, HEAD_PAD)[..., :NOPE].reshape(128, HEADS * 2 * NOPE)
    d_ql, d_kvl, accq, acckv = _lat_bwd(dqn, dkvn, r["proj"], w["q_norm_g"], w["kv_norm_g"])
    g["q_norm_g"], g["kv_norm_g"] = accq[0], acckv[0]
    dpd, g["w_pool"], accd1 = _pool_bwd1(dzd, r["proj"], w["w_pool"], w["pool_scale"])
    g["pool_scale"] = accd1[0]
    d_pu, accd2 = _pool_bwd2(dpd, s)
    dproj = jnp.concatenate([dgates, d_ca, d_cb, d_bg, d_cg, d_sx, d_ql, d_kvl, d_kr, d_pu], axis=1)
    db = jnp.concatenate([accg[0], acca2[0], acca2[1], accb1[0], accb2[0], accb2[1], accq[1], acckv[1], acckr[0], accd2[0]])
    g["b_in"] = _proj_cols_inv(db)
    dh1 = _mm(dproj, w["w_inT"], "nn", "proj_bwd_x", tm=1024, tk=1536)
    g["w_inT"] = _proj_rows_split(_mm(dproj, r["h1"], "tn", "proj_bwd_w", tm=768))
    g["ln1_g"], g["ln1_b"], g["ln2_g"], g["ln2_b"] = acc1[3], acc1[4], acc2[3], acc2[4]
    return dres1, dh1, g, (acc1, acc2)


def _rope_tables(positions):
    half = ROPE // 2
    inv = 1.0 / (ROPE_THETA ** (jnp.arange(0, ROPE, 2, dtype=F32) / ROPE))
    ang = positions.astype(F32)[:, None] * inv
    cos, sin = jnp.cos(ang), jnp.sin(ang)
    s = positions.shape[0]
    z = lambda n: jnp.zeros((s, n), F32)
    ct = jnp.concatenate([jnp.ones((s, NOPE), F32), cos, cos, z(HEAD_PAD - QK_DIM)], axis=1)
    s1t = jnp.concatenate([z(NOPE), -sin, z(half), z(HEAD_PAD - QK_DIM)], axis=1)
    s2t = jnp.concatenate([z(NOPE), z(half), sin, z(HEAD_PAD - QK_DIM)], axis=1)
    return ct, s1t, s2t


def _local_step(x, mod, positions, full, small, target):
    tabs = _rope_tables(positions)
    stacked = _all_layer_weights(full, small)
    ws = [{n: a[l] for n, a in stacked.items()} for l in range(DEPTH)]
    zero = jnp.zeros((D,), F32)
    prm1s, prm2s = [], []
    for l in range(DEPTH):
        sh1, sc1, g1, sh2, sc2, g2 = (mod[l, j] for j in range(6))
        nxt = (mod[l + 1, 1], mod[l + 1, 0]) if l + 1 < DEPTH else (zero, zero)
        prm1s.append(_prm([g1, sc2, sh2, small["ln1_g"][l], small["ln1_b"][l]]))
        prm2s.append(_prm([g2, nxt[0], nxt[1], small["ln2_g"][l], small["ln2_b"][l]]))
    prm0 = _prm([zero, mod[0, 1], mod[0, 0]])
    h = _mod_fwd(x, prm0)
    res = []
    xc = x
    for l in range(DEPTH):
        xc, h, r = _layer_fwd(xc, h, ws[l], prm1s[l], prm2s[l], tabs)
        res.append(r)
    dy, lacc = _loss_head(xc, target)
    loss = lacc[0, 0]
    dres, dh = dy, jnp.zeros_like(dy)
    grads = [None] * DEPTH
    accs = [None] * DEPTH
    for l in reversed(range(DEPTH)):
        dres, dh, grads[l], accs[l] = _layer_bwd(dres, dh, res[l], ws[l], prm1s[l], prm2s[l], tabs)
    dx, acc0 = _mod_bwd(dres, dh, x, prm0)
    dmod = []
    for l in range(DEPTH):
        acc1, acc2 = accs[l]
        dsc1, dsh1 = (acc0[0], acc0[1]) if l == 0 else (accs[l - 1][1][1], accs[l - 1][1][2])
        dmod.append(jnp.stack([dsh1, dsc1, acc1[0], acc1[2], acc1[1], acc2[0]]))
    return loss, dx, grads, jnp.stack(dmod)


ADA_SHARD = 6 * D // N_DEV


def _step(p):
    me = 4 * lax.axis_index("x") + 2 * lax.axis_index("y") + lax.axis_index("c")
    x, target, positions = p["x"][0], p["loss_target"][0], p["positions"][0]

    tap_shapes = [p[n].shape for n, _ in TAPS] + [(D,)]
    small_g = _exchange(_pack_rows([p[n] for n, _ in TAPS] + [p["c"][0]], 8), "gather_taps", True)
    parts = _unpack(small_g.reshape(N_DEV, -1), tap_shapes)
    full = {n: _join(g, ax) for (n, ax), g in zip(TAPS, parts[:-1])}
    c_all = parts[-1]
    w_in_t = jnp.swapaxes(p["w_in"], 1, 2).astype(BF16)
    w_in_t = jnp.concatenate([w_in_t, jnp.zeros((DEPTH, IN_SHARD_PAD - IN_SHARD, D), BF16)], axis=1)
    w_up_t = jnp.swapaxes(p["w_up"], 1, 2).astype(BF16)
    big_shapes = [p[n].shape for n, _ in BIG] + [w_in_t.shape, w_up_t.shape]
    big_g = _gather_two_level(_pack_rows([p[n].astype(BF16) for n, _ in BIG] + [w_in_t, w_up_t], 16), "gather_weights")
    big_parts = _unpack(big_g.reshape(N_DEV, -1), big_shapes)
    for (n, ax), g in zip(BIG, big_parts):
        full[n] = _join(g, ax)
    full["w_inT"] = _join(big_parts[-2], 1)
    full["w_upT"] = _join(big_parts[-1], 1)
    small = {n: p[n] for n in SMALL}

    c_act = _silu_rows(c_all)
    w_ada_cols = jnp.moveaxis(p["w_ada"], 0, 1).reshape(D, DEPTH * ADA_SHARD)
    b_shard = lax.dynamic_slice_in_dim(p["b_ada"], me * ADA_SHARD, ADA_SHARD, axis=1).reshape(1, DEPTH * ADA_SHARD)
    mod_sh = _mm(c_act, w_ada_cols, "nn", "ada_fwd", bias=b_shard)[:N_DEV]
    mod_x = _exchange(_pack_cols([mod_sh], 8), "scatter_mod", False)
    mod = mod_x.reshape(N_DEV, -1)[:, :DEPTH * ADA_SHARD].reshape(N_DEV, DEPTH, ADA_SHARD)
    mod = jnp.moveaxis(mod, 0, 1).reshape(DEPTH, 6, D)

    loss_local, dx, grads, dmod = _local_step(x, mod, positions, full, small, target)
    loss = lax.psum(loss_local, ("x", "y", "c"))
    gfull = {n: jnp.stack([grads[l][n] for l in range(DEPTH)]) for n in grads[0]}

    out = {"loss": loss, "grad_x": dx[None]}

    def emit(names, g, dlt, mn, vn, shapes):
        for n, gi, di, mi, vi in zip(names, _unpack(g, shapes), _unpack(dlt, shapes), _unpack(mn, shapes), _unpack(vn, shapes)):
            out["grad_" + n], out["delta_" + n], out["new_m_" + n], out["new_v_" + n] = gi, di, mi, vi

    small_parts = [dmod.reshape(DEPTH, 6 * D)] + [gfull[n] for n in SMALL[1:]]
    small_all = _exchange(_pack_rows(small_parts, GRAD_ROWS), "gather_small_grads", True)
    small_shapes = [p[n].shape for n in SMALL]
    sg, sd, sm, sv = _adamw(small_all, *[_pack_rows([p[pre + n] for n in SMALL], GRAD_ROWS) for pre in ("", "m_", "v_")],
                            name="adamw_small")
    emit(SMALL, *[t.reshape(-1) for t in (sg, sd, sm, sv)], small_shapes)

    dmod_all = small_all.reshape(N_DEV, -1)[:, :DEPTH * 6 * D].reshape(N_DEV, DEPTH, 6 * D)
    dmod_sh = lax.dynamic_slice_in_dim(dmod_all, me * ADA_SHARD, ADA_SHARD, axis=2).reshape(N_DEV, DEPTH * ADA_SHARD)
    g_ada = _mm(c_act, _pad_rows(dmod_sh, 2 * N_DEV), "tn", "ada_bwd_w")
    g_ada = jnp.moveaxis(g_ada.reshape(D, DEPTH, ADA_SHARD), 1, 0)
    ag, ad, am, av = _adamw(_pack_rows([g_ada], GRAD_ROWS)[None], *[_pack_rows([p[pre + "w_ada"]], GRAD_ROWS) for pre in ("", "m_", "v_")],
                            name="adamw_ada")
    emit(("w_ada",), *[t.reshape(-1) for t in (ag, ad, am, av)], [p["w_ada"].shape])

    shard_names = [n for n, _ in BIG + TAPS]
    pieces = [_split(gfull[n], ax).reshape(N_DEV, -1) for n, ax in BIG + TAPS]
    recv = _exchange(_pack_cols(pieces, GRAD_ROWS).astype(BF16), "scatter_grads", False)
    bg, bd, bm, bv = _adamw(recv, *[_pack_rows([p[pre + n] for n in shard_names], GRAD_ROWS) for pre in ("", "m_", "v_")],
                            name="adamw_sharded")
    emit(shard_names, *[t.reshape(-1) for t in (bg, bd, bm, bv)], [p[n].shape for n in shard_names])

    t_pieces = [jnp.moveaxis(gfull["w_inT"], 1, 0).reshape(N_DEV, -1),
                jnp.moveaxis(gfull["w_upT"].reshape(DEPTH, N_DEV, UP_SHARD, D), 1, 0).reshape(N_DEV, -1)]
    g_t = _sum8(_exchange(_pack_cols(t_pieces, GRAD_ROWS).astype(BF16), "scatter_grads_t", False), "sum_grads_t")
    g_in_t, g_up_t = _unpack(g_t.reshape(-1), [(DEPTH, IN_SHARD, D), (DEPTH, UP_SHARD, D)])
    for n, gt in (("w_in", g_in_t), ("w_up", g_up_t)):
        shp = p[n].shape
        two_d = (shp[0] * shp[1], shp[2])
        res = _adamw(jnp.swapaxes(gt, 1, 2).reshape((1,) + two_d), *[p[pre + n].reshape(two_d) for pre in ("", "m_", "v_")],
                     name="adamw_" + n)
        for key, t in zip(("grad_", "delta_", "new_m_", "new_v_"), res):
            out[key + n] = t.reshape(shp)
    return out


_ARG_NAMES = ("x", "c", "positions") + WEIGHTS + ("loss_target",) + tuple("m_" + n for n in WEIGHTS) + tuple("v_" + n for n in WEIGHTS)
_OUT_NAMES = ("loss", "grad_x") + tuple(pre + n for pre in ("grad_", "delta_", "new_m_", "new_v_") for n in WEIGHTS)


def kernel(x, c, positions, w_ada, b_ada, w_in, b_in, conv_dw, conv_ln_g, conv_ln_b, w_conv_out, sc_dw, w_sc_out, q_norm_g, w_uq, kv_norm_g, w_ukv, w_mla_out, w_pool, pool_scale, w_pool_out, w_o, ln1_g, ln1_b, w_up, ffn_dw, w_down, ln2_g, ln2_b, loss_target, m_w_ada, m_b_ada, m_w_in, m_b_in, m_conv_dw, m_conv_ln_g, m_conv_ln_b, m_w_conv_out, m_sc_dw, m_w_sc_out, m_q_norm_g, m_w_uq, m_kv_norm_g, m_w_ukv, m_w_mla_out, m_w_pool, m_pool_scale, m_w_pool_out, m_w_o, m_ln1_g, m_ln1_b, m_w_up, m_ffn_dw, m_w_down, m_ln2_g, m_ln2_b, v_w_ada, v_b_ada, v_w_in, v_b_in, v_conv_dw, v_conv_ln_g, v_conv_ln_b, v_w_conv_out, v_sc_dw, v_w_sc_out, v_q_norm_g, v_w_uq, v_kv_norm_g, v_w_ukv, v_w_mla_out, v_w_pool, v_pool_scale, v_w_pool_out, v_w_o, v_ln1_g, v_ln1_b, v_w_up, v_ffn_dw, v_w_down, v_ln2_g, v_ln2_b):
    args = locals()
    out = _step({n: args[n] for n in _ARG_NAMES})
    return tuple(out[n] for n in _OUT_NAMES)
```

```python
import functools

import jax
import jax.numpy as jnp
from jax import lax
from jax.experimental import pallas as pl
from jax.experimental.pallas import tpu as pltpu

F32 = jnp.float32
BF16 = jnp.bfloat16

N_DEV = 8
DEPTH = 4
D = 1024
CONV_W = 31
HEADS = 8
HEAD_PAD = 128
QK_DIM = 96
NOPE = 64
ROPE = 32
ROPE_THETA = 10000.0
D_FF = 2816
LN_EPS = 1e-5
RMS_EPS = 1e-6
ALPHA = (2.0 * DEPTH) ** 0.25
ATT_SCALE = QK_DIM ** -0.5
POOL_WINDOWS = (2, 4, 8, 16)

ADAM_LR = 0.001
ADAM_B1 = 0.9
ADAM_B2 = 0.999
ADAM_EPS = 1e-08
ADAM_WD = 0.01
ADAM_STEP = 10

GATES0 = 0
CA0 = 4096
CB0 = 4608
SBG0 = 5120
SCG0 = 5632
SX0 = 6144
QL0 = 6656
KVL0 = 6912
KR0 = 7040
PU0 = 7168
NPROJ = 7680

LANE = 128
TM = 512
TM_WIDE = 256
TQ = 1024
TQ_BWD = 512
TQ_BWD_Q = 1024
VMEM_LIMIT = 56 * 1024 * 1024

MESH = pl.DeviceIdType.MESH


def _sig(x):
    return 1.0 / (1.0 + jnp.exp(-x))


def _tile(n, pref):
    if n <= pref:
        return n
    t = (pref // LANE) * LANE
    while t >= LANE:
        if n % t == 0:
            return t
        t -= LANE
    raise ValueError(f"no lane-aligned tile for {n}")


def _params(sem):
    return pltpu.CompilerParams(dimension_semantics=sem, vmem_limit_bytes=VMEM_LIMIT)


def _full(shape):
    nd = len(shape)
    return pl.BlockSpec(shape, lambda *_: (0,) * nd)


def _rows(tm, cw, cb):
    return pl.BlockSpec((tm, cw), lambda i: (i, cb))


def _prev(tm, hb, cw, cb):
    r = tm // hb
    return pl.BlockSpec((hb, cw), lambda i: (jnp.maximum(i * r - 1, 0), cb))


def _next(tm, hb, cw, cb, s):
    r = tm // hb
    last = s // hb - 1
    return pl.BlockSpec((hb, cw), lambda i: (jnp.minimum((i + 1) * r, last), cb))


def _acc_rows(ref, first, rows):
    @pl.when(first)
    def _():
        ref[...] = jnp.zeros_like(ref)
    for r, v in enumerate(rows):
        ref[r:r + 1, :] += v


def _colsum(v):
    return jnp.sum(v, axis=0, keepdims=True)


def _ln_stats(r):
    mu = jnp.mean(r, axis=-1, keepdims=True)
    xc = r - mu
    var = jnp.mean(xc * xc, axis=-1, keepdims=True)
    rstd = lax.rsqrt(var + LN_EPS)
    return xc * rstd, rstd


def _ln_bwd(dxh, xh, rstd):
    return rstd * (dxh - jnp.mean(dxh, axis=-1, keepdims=True) - xh * jnp.mean(dxh * xh, axis=-1, keepdims=True))


_DIMS = {"nn": ((1,), (0,)), "nt": ((1,), (1,)), "tn": ((0,), (0,))}


def _mm(a, b, mode, name, *, out_dtype=F32, bias=None, tm=512, tn=1024, tk=2048):
    if mode == "nn":
        (m, k), (k2, n) = a.shape, b.shape
    elif mode == "nt":
        (m, k), (n, k2) = a.shape, b.shape
    else:
        (k, m), (k2, n) = a.shape, b.shape
    assert k == k2, (a.shape, b.shape, mode)
    tm, tn, tk = _tile(m, tm), _tile(n, tn), _tile(k, tk)
    nk = k // tk
    dims = (_DIMS[mode], ((), ()))
    has_bias = bias is not None

    def body(*refs):
        if has_bias:
            a_ref, b_ref, bias_ref, o_ref = refs[:4]
        else:
            a_ref, b_ref, o_ref = refs[:3]
        p = lax.dot_general(a_ref[...].astype(BF16), b_ref[...].astype(BF16), dims, preferred_element_type=F32)

        def finish(r):
            if has_bias:
                r = r + bias_ref[...]
            o_ref[...] = r.astype(out_dtype)

        if nk == 1:
            finish(p)
        else:
            acc = refs[-1]
            kk = pl.program_id(2)

            @pl.when(kk == 0)
            def _():
                acc[...] = p

            @pl.when(kk > 0)
            def _():
                acc[...] += p

            @pl.when(kk == nk - 1)
            def _():
                finish(acc[...])

    if mode == "nn":
        a_spec = pl.BlockSpec((tm, tk), lambda i, j, kk: (i, kk))
        b_spec = pl.BlockSpec((tk, tn), lambda i, j, kk: (kk, j))
    elif mode == "nt":
        a_spec = pl.BlockSpec((tm, tk), lambda i, j, kk: (i, kk))
        b_spec = pl.BlockSpec((tn, tk), lambda i, j, kk: (j, kk))
    else:
        a_spec = pl.BlockSpec((tk, tm), lambda i, j, kk: (kk, i))
        b_spec = pl.BlockSpec((tk, tn), lambda i, j, kk: (kk, j))
    in_specs = [a_spec, b_spec]
    args = [a, b]
    if has_bias:
        in_specs.append(pl.BlockSpec((1, tn), lambda i, j, kk: (0, j)))
        args.append(bias)
    return pl.pallas_call(
        body, name=name, grid=(m // tm, n // tn, nk),
        in_specs=in_specs, out_specs=pl.BlockSpec((tm, tn), lambda i, j, kk: (i, j)),
        out_shape=jax.ShapeDtypeStruct((m, n), out_dtype),
        scratch_shapes=[pltpu.VMEM((tm, tn), F32)] if nk > 1 else [],
        compiler_params=_params(("parallel", "parallel", "arbitrary")),
    )(*args)


def _mod_fwd(x, prm):
    s = x.shape[0]
    tm = _tile(s, TM)

    def body(x_ref, p_ref, h_ref):
        h_ref[...] = (x_ref[...] * (1.0 + p_ref[1:2, :]) + p_ref[2:3, :]).astype(BF16)

    return pl.pallas_call(
        body, name="mod_fwd", grid=(s // tm,),
        in_specs=[_rows(tm, D, 0), _full((8, D))], out_specs=_rows(tm, D, 0),
        out_shape=jax.ShapeDtypeStruct((s, D), BF16), compiler_params=_params(("parallel",)),
    )(x, prm)


def _mod_bwd(dres, dh, x, prm):
    s = x.shape[0]
    tm = _tile(s, TM)

    def body(dres_ref, dh_ref, x_ref, p_ref, dx_ref, acc_ref):
        dh_v = dh_ref[...]
        dx_ref[...] = dres_ref[...] + dh_v * (1.0 + p_ref[1:2, :])
        _acc_rows(acc_ref, pl.program_id(0) == 0, [_colsum(dh_v * x_ref[...]), _colsum(dh_v)])

    return pl.pallas_call(
        body, name="mod_bwd", grid=(s // tm,),
        in_specs=[_rows(tm, D, 0)] * 3 + [_full((8, D))],
        out_specs=[_rows(tm, D, 0), _full((8, D))],
        out_shape=[jax.ShapeDtypeStruct((s, D), F32), jax.ShapeDtypeStruct((8, D), F32)],
        compiler_params=_params(("arbitrary",)),
    )(dres, dh, x, prm)


CA_HALO = 32
C512 = 512


def _glu_buf(buf, ap, am, bp, bm, first, tm):
    glu_p = ap[...] * _sig(bp[...])
    buf[0:CA_HALO, :] = jnp.where(first, jnp.zeros_like(glu_p), glu_p)
    buf[CA_HALO:CA_HALO + tm, :] = am[...] * _sig(bm[...])


def _conv31(w_ref, buf, zs, tm, offs):
    acc = None
    for r in range(8):
        ks = [k for k in range(CONV_W) if offs[k] % 8 == r]
        if not ks:
            continue
        rows = tm if r == 0 else tm + 8
        z = None
        for k in ks:
            t = w_ref[k:k + 1, :] * buf[pl.ds(offs[k] - r, rows), :]
            z = t if z is None else z + t
        if r:
            zs[...] = z
            z = zs[pl.ds(r, tm), :]
        acc = z if acc is None else acc + z
    return acc


_CA_FWD_OFFS = tuple(CA_HALO - CONV_W + 1 + k for k in range(CONV_W))
_CA_BWD_OFFS = tuple(CONV_W - 1 - k for k in range(CONV_W))


def _conv_a_fwd(proj, w32, ln_g, ln_b):
    s = proj.shape[0]
    tm = _tile(s, TM)
    ca, cb = CA0 // C512, CB0 // C512

    def body(ap, am, bp, bm, w_ref, g_ref, b_ref, yc_ref, za_ref, buf, zs):
        _glu_buf(buf, ap, am, bp, bm, pl.program_id(0) == 0, tm)
        acc = _conv31(w_ref, buf, zs, tm, _CA_FWD_OFFS)
        yc_ref[...] = acc
        xh, _ = _ln_stats(acc)
        y = xh * g_ref[...] + b_ref[...]
        za_ref[...] = (y * _sig(y)).astype(BF16)

    return pl.pallas_call(
        body, name="conv_a_fwd", grid=(s // tm,),
        in_specs=[_prev(tm, CA_HALO, C512, ca), _rows(tm, C512, ca), _prev(tm, CA_HALO, C512, cb), _rows(tm, C512, cb),
                  _full((32, C512)), _full((1, C512)), _full((1, C512))],
        out_specs=[_rows(tm, C512, 0), _rows(tm, C512, 0)],
        out_shape=[jax.ShapeDtypeStruct((s, C512), F32), jax.ShapeDtypeStruct((s, C512), BF16)],
        scratch_shapes=[pltpu.VMEM((tm + CA_HALO, C512), F32), pltpu.VMEM((tm + 8, C512), F32)],
        compiler_params=_params(("parallel",)),
    )(proj, proj, proj, proj, w32, ln_g, ln_b)


def _conv_a_bwd1(dza, yconv, proj, ln_g, ln_b):
    s = proj.shape[0]
    tm = _tile(s, TM)
    ca, cb = CA0 // C512, CB0 // C512

    def body(dza_ref, yc_ref, ap, am, bp, bm, g_ref, b_ref, dyc_ref, dw_ref, acc_ref, buf):
        first = pl.program_id(0) == 0
        xh, rstd = _ln_stats(yc_ref[...])
        g = g_ref[...]
        y = xh * g + b_ref[...]
        sg = _sig(y)
        dy = dza_ref[...] * (sg * (1.0 + y * (1.0 - sg)))
        _acc_rows(acc_ref, first, [_colsum(dy * xh), _colsum(dy)])
        dyc = _ln_bwd(dy * g, xh, rstd)
        dyc_ref[...] = dyc
        _glu_buf(buf, ap, am, bp, bm, first, tm)

        @pl.when(first)
        def _():
            dw_ref[...] = jnp.zeros_like(dw_ref)
        for k in range(CONV_W):
            dw_ref[k:k + 1, :] += _colsum(dyc * buf[pl.ds(_CA_FWD_OFFS[k], tm), :])

    return pl.pallas_call(
        body, name="conv_a_bwd1", grid=(s // tm,),
        in_specs=[_rows(tm, C512, 0), _rows(tm, C512, 0),
                  _prev(tm, CA_HALO, C512, ca), _rows(tm, C512, ca), _prev(tm, CA_HALO, C512, cb), _rows(tm, C512, cb),
                  _full((1, C512)), _full((1, C512))],
        out_specs=[_rows(tm, C512, 0), _full((32, C512)), _full((8, C512))],
        out_shape=[jax.ShapeDtypeStruct((s, C512), F32), jax.ShapeDtypeStruct((32, C512), F32),
                   jax.ShapeDtypeStruct((8, C512), F32)],
        scratch_shapes=[pltpu.VMEM((tm + CA_HALO, C512), F32)],
        compiler_params=_params(("arbitrary",)),
    )(dza, yconv, proj, proj, proj, proj, ln_g, ln_b)


def _conv_a_bwd2(dyc, proj, w32):
    s = proj.shape[0]
    tm = _tile(s, TM)
    ca, cb = CA0 // C512, CB0 // C512
    nt = s // tm

    def body(dm, dn, am, bm, w_ref, da_ref, db_ref, acc_ref, buf, zs):
        i = pl.program_id(0)
        buf[0:tm, :] = dm[...]
        nxt = dn[...]
        buf[tm:tm + CA_HALO, :] = jnp.where(i == nt - 1, jnp.zeros_like(nxt), nxt)
        zs[0:tm, :] = _conv31(w_ref, buf, zs, tm, _CA_BWD_OFFS)

        def chunk(j, sums):
            rows = pl.ds(pl.multiple_of(j * 32, 32), 32)
            dglu = zs[rows, :]
            sb = _sig(bm[rows, :])
            da = dglu * sb
            db = dglu * am[rows, :] * sb * (1.0 - sb)
            da_ref[rows, :] = da.astype(BF16)
            db_ref[rows, :] = db.astype(BF16)
            return sums[0] + da, sums[1] + db

        zero = jnp.zeros((32, C512), F32)
        sa, sd = lax.fori_loop(0, tm // 32, chunk, (zero, zero))
        _acc_rows(acc_ref, i == 0, [_colsum(sa), _colsum(sd)])

    return pl.pallas_call(
        body, name="conv_a_bwd2", grid=(nt,),
        in_specs=[_rows(tm, C512, 0), _next(tm, CA_HALO, C512, 0, s), _rows(tm, C512, ca), _rows(tm, C512, cb),
                  _full((32, C512))],
        out_specs=[_rows(tm, C512, 0), _rows(tm, C512, 0), _full((8, C512))],
        out_shape=[jax.ShapeDtypeStruct((s, C512), BF16), jax.ShapeDtypeStruct((s, C512), BF16),
                   jax.ShapeDtypeStruct((8, C512), F32)],
        scratch_shapes=[pltpu.VMEM((tm + CA_HALO, C512), F32), pltpu.VMEM((tm + 8, C512), F32)],
        compiler_params=_params(("arbitrary",)),
    )(dyc, dyc, proj, proj, w32)


H8 = 8
SC_W = 3


def _sc_ubuf(buf, cp, cm, xp, xm, first, tm):
    up = cp[...] * xp[...]
    buf[0:H8, :] = jnp.where(first, jnp.zeros_like(up), up)
    buf[H8:H8 + tm, :] = cm[...] * xm[...]


def _conv3(w_ref, buf, tm):
    acc = w_ref[0:1, :] * buf[pl.ds(H8 - SC_W + 1, tm), :]
    for k in range(1, SC_W):
        acc = acc + w_ref[k:k + 1, :] * buf[pl.ds(H8 - SC_W + 1 + k, tm), :]
    return acc


def _conv3_t(w_ref, buf, tm):
    acc = w_ref[0:1, :] * buf[pl.ds(SC_W - 1, tm), :]
    for k in range(1, SC_W):
        acc = acc + w_ref[k:k + 1, :] * buf[pl.ds(SC_W - 1 - k, tm), :]
    return acc


def _sc_fwd(proj, w8):
    s = proj.shape[0]
    tm = _tile(s, TM)
    c_bg, c_cg, c_x = SBG0 // C512, SCG0 // C512, SX0 // C512

    def body(bg, cp, cm, xp, xm, w_ref, zb_ref, buf):
        _sc_ubuf(buf, cp, cm, xp, xm, pl.program_id(0) == 0, tm)
        zb_ref[...] = (bg[...] * _conv3(w_ref, buf, tm)).astype(BF16)

    return pl.pallas_call(
        body, name="sc_fwd", grid=(s // tm,),
        in_specs=[_rows(tm, C512, c_bg), _prev(tm, H8, C512, c_cg), _rows(tm, C512, c_cg),
                  _prev(tm, H8, C512, c_x), _rows(tm, C512, c_x), _full((8, C512))],
        out_specs=_rows(tm, C512, 0), out_shape=jax.ShapeDtypeStruct((s, C512), BF16),
        scratch_shapes=[pltpu.VMEM((tm + H8, C512), F32)], compiler_params=_params(("parallel",)),
    )(proj, proj, proj, proj, proj, w8)


def _sc_bwd1(dzb, proj, w8):
    s = proj.shape[0]
    tm = _tile(s, TM)
    c_bg, c_cg, c_x = SBG0 // C512, SCG0 // C512, SX0 // C512

    def body(dz_ref, bg, cp, cm, xp, xm, w_ref, dconv_ref, dbg_ref, dw_ref, acc_ref, buf):
        first = pl.program_id(0) == 0
        _sc_ubuf(buf, cp, cm, xp, xm, first, tm)
        dz = dz_ref[...]
        dbg = dz * _conv3(w_ref, buf, tm)
        dconv = dz * bg[...]
        dconv_ref[...] = dconv
        dbg_ref[...] = dbg.astype(BF16)
        _acc_rows(acc_ref, first, [_colsum(dbg)])
        _acc_rows(dw_ref, first, [_colsum(dconv * buf[pl.ds(H8 - SC_W + 1 + k, tm), :]) for k in range(SC_W)])

    return pl.pallas_call(
        body, name="sc_bwd1", grid=(s // tm,),
        in_specs=[_rows(tm, C512, 0), _rows(tm, C512, c_bg), _prev(tm, H8, C512, c_cg), _rows(tm, C512, c_cg),
                  _prev(tm, H8, C512, c_x), _rows(tm, C512, c_x), _full((8, C512))],
        out_specs=[_rows(tm, C512, 0), _rows(tm, C512, 0), _full((8, C512)), _full((8, C512))],
        out_shape=[jax.ShapeDtypeStruct((s, C512), F32), jax.ShapeDtypeStruct((s, C512), BF16),
                   jax.ShapeDtypeStruct((8, C512), F32), jax.ShapeDtypeStruct((8, C512), F32)],
        scratch_shapes=[pltpu.VMEM((tm + H8, C512), F32)], compiler_params=_params(("arbitrary",)),
    )(dzb, proj, proj, proj, proj, proj, w8)


def _sc_bwd2(dconv, proj, w8):
    s = proj.shape[0]
    tm = _tile(s, TM)
    c_cg, c_x = SCG0 // C512, SX0 // C512
    nt = s // tm

    def body(dm, dn, cm, xm, w_ref, dcg_ref, dx_ref, acc_ref, buf):
        i = pl.program_id(0)
        buf[0:tm, :] = dm[...]
        nxt = dn[...]
        buf[tm:tm + H8, :] = jnp.where(i == nt - 1, jnp.zeros_like(nxt), nxt)
        du = _conv3_t(w_ref, buf, tm)
        dcg = du * xm[...]
        dx = du * cm[...]
        dcg_ref[...] = dcg.astype(BF16)
        dx_ref[...] = dx.astype(BF16)
        _acc_rows(acc_ref, i == 0, [_colsum(dcg), _colsum(dx)])

    return pl.pallas_call(
        body, name="sc_bwd2", grid=(nt,),
        in_specs=[_rows(tm, C512, 0), _next(tm, H8, C512, 0, s), _rows(tm, C512, c_cg), _rows(tm, C512, c_x),
                  _full((8, C512))],
        out_specs=[_rows(tm, C512, 0), _rows(tm, C512, 0), _full((8, C512))],
        out_shape=[jax.ShapeDtypeStruct((s, C512), BF16), jax.ShapeDtypeStruct((s, C512), BF16),
                   jax.ShapeDtypeStruct((8, C512), F32)],
        scratch_shapes=[pltpu.VMEM((tm + H8, C512), F32)], compiler_params=_params(("arbitrary",)),
    )(dconv, dconv, proj, proj, w8)


QLAT = 256
KVLAT = 128


def _rms(x, g):
    r = lax.rsqrt(jnp.mean(x * x, axis=-1, keepdims=True) + RMS_EPS)
    return x * r * g, r


def _rms_bwd(dy, x, g, r):
    u = dy * g
    dx = r * u - x * (r * r * r) * jnp.mean(u * x, axis=-1, keepdims=True)
    return dx, _colsum(dy * x * r)


def _lat_fwd(proj, gq, gkv):
    s = proj.shape[0]
    tm = _tile(s, TM)

    def body(q_ref, kv_ref, gq_ref, gkv_ref, qn_ref, kvn_ref):
        qn_ref[...] = _rms(q_ref[...], gq_ref[...])[0].astype(BF16)
        kvn_ref[...] = _rms(kv_ref[...], gkv_ref[...])[0].astype(BF16)

    return pl.pallas_call(
        body, name="lat_fwd", grid=(s // tm,),
        in_specs=[_rows(tm, QLAT, QL0 // QLAT), _rows(tm, KVLAT, KVL0 // KVLAT), _full((1, QLAT)), _full((1, KVLAT))],
        out_specs=[_rows(tm, QLAT, 0), _rows(tm, KVLAT, 0)],
        out_shape=[jax.ShapeDtypeStruct((s, QLAT), BF16), jax.ShapeDtypeStruct((s, KVLAT), BF16)],
        compiler_params=_params(("parallel",)),
    )(proj, proj, gq, gkv)


def _lat_bwd(dqn, dkvn, proj, gq, gkv):
    s = proj.shape[0]
    tm = _tile(s, TM)

    def body(dqn_ref, dkvn_ref, q_ref, kv_ref, gq_ref, gkv_ref, dq_ref, dkv_ref, accq_ref, acckv_ref):
        first = pl.program_id(0) == 0
        q, kv = q_ref[...], kv_ref[...]
        gqv, gkvv = gq_ref[...], gkv_ref[...]
        dq, dgq = _rms_bwd(dqn_ref[...], q, gqv, _rms(q, gqv)[1])
        dkv, dgkv = _rms_bwd(dkvn_ref[...], kv, gkvv, _rms(kv, gkvv)[1])
        dq_ref[...] = dq.astype(BF16)
        dkv_ref[...] = dkv.astype(BF16)
        _acc_rows(accq_ref, first, [dgq, _colsum(dq)])
        _acc_rows(acckv_ref, first, [dgkv, _colsum(dkv)])

    return pl.pallas_call(
        body, name="lat_bwd", grid=(s // tm,),
        in_specs=[_rows(tm, QLAT, 0), _rows(tm, KVLAT, 0), _rows(tm, QLAT, QL0 // QLAT), _rows(tm, KVLAT, KVL0 // KVLAT),
                  _full((1, QLAT)), _full((1, KVLAT))],
        out_specs=[_rows(tm, QLAT, 0), _rows(tm, KVLAT, 0), _full((8, QLAT)), _full((8, KVLAT))],
        out_shape=[jax.ShapeDtypeStruct((s, QLAT), BF16), jax.ShapeDtypeStruct((s, KVLAT), BF16),
                   jax.ShapeDtypeStruct((8, QLAT), F32), jax.ShapeDtypeStruct((8, KVLAT), F32)],
        compiler_params=_params(("arbitrary",)),
    )(dqn, dkvn, proj, proj, gq, gkv)


def _rope(x, c, s1, s2):
    return x * c + pltpu.roll(x, HEAD_PAD - ROPE // 2, 1) * s1 + pltpu.roll(x, ROPE // 2, 1) * s2


def _rope_t(d, c, s1, s2):
    return d * c + pltpu.roll(d * s1, ROPE // 2, 1) + pltpu.roll(d * s2, HEAD_PAD - ROPE // 2, 1)


def _rope_fwd(qp, kvp, proj, ct, s1t, s2t):
    s = proj.shape[0]
    tm = _tile(s, TM)

    def body(q_ref, kv_ref, kr_ref, c_ref, s1_ref, s2_ref, qo, ko, vo):
        c, s1, s2 = c_ref[...], s1_ref[...], s2_ref[...]
        kr = _rope(kr_ref[...], c, s1, s2)
        for h in range(HEADS):
            cs = slice(h * HEAD_PAD, (h + 1) * HEAD_PAD)
            qo[:, cs] = _rope(q_ref[:, cs], c, s1, s2).astype(BF16)
            ko[:, cs] = (kv_ref[:, 2 * h * HEAD_PAD:(2 * h + 1) * HEAD_PAD] + kr).astype(BF16)
            vo[:, cs] = kv_ref[:, (2 * h + 1) * HEAD_PAD:(2 * h + 2) * HEAD_PAD].astype(BF16)

    tab = _rows(tm, HEAD_PAD, 0)
    return pl.pallas_call(
        body, name="rope_fwd", grid=(s // tm,),
        in_specs=[_rows(tm, HEADS * HEAD_PAD, 0), _rows(tm, 2 * HEADS * HEAD_PAD, 0), _rows(tm, HEAD_PAD, KR0 // HEAD_PAD),
                  tab, tab, tab],
        out_specs=[_rows(tm, HEADS * HEAD_PAD, 0)] * 3,
        out_shape=[jax.ShapeDtypeStruct((s, HEADS * HEAD_PAD), BF16)] * 3,
        compiler_params=_params(("parallel",)),
    )(qp, kvp, proj, ct, s1t, s2t)


def _rope_bwd(dq, dkv, ct, s1t, s2t):
    s = dq.shape[0]
    tm = _tile(s, TM)

    def body(dq_ref, dkv_ref, c_ref, s1_ref, s2_ref, dqo, dkr_ref, acc_ref):
        c, s1, s2 = c_ref[...], s1_ref[...], s2_ref[...]
        tot = dkv_ref[:, 0:HEAD_PAD]
        for h in range(HEADS):
            cs = slice(h * HEAD_PAD, (h + 1) * HEAD_PAD)
            dqo[:, cs] = _rope_t(dq_ref[:, cs], c, s1, s2).astype(BF16)
            if h:
                tot = tot + dkv_ref[:, 2 * h * HEAD_PAD:(2 * h + 1) * HEAD_PAD]
        lane = lax.broadcasted_iota(jnp.int32, (tm, HEAD_PAD), 1)
        dkr = jnp.where((lane >= NOPE) & (lane < QK_DIM), _rope_t(tot, c, s1, s2), 0.0)
        dkr_ref[...] = dkr.astype(BF16)
        _acc_rows(acc_ref, pl.program_id(0) == 0, [_colsum(dkr)])

    tab = _rows(tm, HEAD_PAD, 0)
    return pl.pallas_call(
        body, name="rope_bwd", grid=(s // tm,),
        in_specs=[_rows(tm, HEADS * HEAD_PAD, 0), _rows(tm, 2 * HEADS * HEAD_PAD, 0), tab, tab, tab],
        out_specs=[_rows(tm, HEADS * HEAD_PAD, 0), tab, _full((8, HEAD_PAD))],
        out_shape=[jax.ShapeDtypeStruct((s, HEADS * HEAD_PAD), BF16), jax.ShapeDtypeStruct((s, HEAD_PAD), BF16),
                   jax.ShapeDtypeStruct((8, HEAD_PAD), F32)],
        compiler_params=_params(("arbitrary",)),
    )(dq, dkv, ct, s1t, s2t)


_NT = (((1,), (1,)), ((), ()))
_TN = (((0,), (0,)), ((), ()))
_NN = (((1,), (0,)), ((), ()))


def _tile_rows(ref, t, tq):
    return ref[pl.ds(pl.multiple_of(t * tq, tq), tq), :]


def _attn_fwd(q, k, v):
    s = q.shape[0]
    tq = _tile(s, TQ)
    nq = s // tq

    def body(q_ref, k_ref, v_ref, o_ref, lse_ref):
        qi = pl.program_id(1)
        q_t = q_ref[...]

        def raw(ki):
            return lax.dot_general(_tile_rows(k_ref, ki, tq), q_t, _NT, preferred_element_type=F32)

        def process(ki, st, m, l, acc, masked):
            sc = st * ATT_SCALE
            if masked:
                key = lax.broadcasted_iota(jnp.int32, (tq, tq), 0)
                qry = lax.broadcasted_iota(jnp.int32, (tq, tq), 1)
                sc = jnp.where(key <= qry, sc, -jnp.inf)
            m_new = jnp.maximum(m, jnp.max(sc, axis=0, keepdims=True))
            pt = jnp.exp(sc - m_new)
            a = jnp.exp(m - m_new)
            pv = lax.dot_general(_tile_rows(v_ref, ki, tq), pt.astype(BF16), _TN, preferred_element_type=F32)
            return m_new, a * l + jnp.sum(pt, axis=0, keepdims=True), a * acc + pv

        def loop_body(ki, c):
            return process(ki, raw(ki), c[0], c[1], c[2], False)

        init = (jnp.full((1, tq), -jnp.inf, F32), jnp.zeros((1, tq), F32), jnp.zeros((HEAD_PAD, tq), F32))
        c = lax.fori_loop(0, qi, loop_body, init)
        m, l, acc = process(qi, raw(qi), c[0], c[1], c[2], True)
        o_ref[...] = jnp.transpose(acc / l)
        lse_ref[0] = jnp.broadcast_to(m + jnp.log(l), (8, tq))

    qspec = pl.BlockSpec((tq, HEAD_PAD), lambda h, qi: (qi, h))
    kspec = pl.BlockSpec((s, HEAD_PAD), lambda h, qi: (0, h))
    return pl.pallas_call(
        body, name="attn_fwd", grid=(HEADS, nq),
        in_specs=[qspec, kspec, kspec],
        out_specs=[qspec, pl.BlockSpec((1, 8, tq), lambda h, qi: (h, 0, qi))],
        out_shape=[jax.ShapeDtypeStruct((s, HEADS * HEAD_PAD), F32), jax.ShapeDtypeStruct((HEADS, 8, s), F32)],
        compiler_params=_params(("parallel", "parallel")),
    )(q, k, v)


def _attn_prep(d_o, o):
    s = o.shape[0]
    tm = _tile(s, TM)

    def body(do_ref, o_ref, dl_ref, dob_ref):
        for h in range(HEADS):
            cs = slice(h * HEAD_PAD, (h + 1) * HEAD_PAD)
            dov = do_ref[:, cs]
            row = jnp.sum(jnp.transpose(dov * o_ref[:, cs]), axis=0, keepdims=True)
            dl_ref[h] = jnp.broadcast_to(row, (8, tm))
            dob_ref[:, cs] = dov.astype(BF16)

    blk = _rows(tm, HEADS * HEAD_PAD, 0)
    return pl.pallas_call(
        body, name="attn_prep", grid=(s // tm,),
        in_specs=[blk, blk], out_specs=[pl.BlockSpec((HEADS, 8, tm), lambda i: (0, 0, i)), blk],
        out_shape=[jax.ShapeDtypeStruct((HEADS, 8, s), F32), jax.ShapeDtypeStruct((s, HEADS * HEAD_PAD), BF16)],
        compiler_params=_params(("parallel",)),
    )(d_o, o)


def _attn_bwd(q, k, v, d_o, lse, delta):
    s = q.shape[0]
    tk = _tile(s, TQ_BWD)
    tq = _tile(s, TQ_BWD_Q)
    assert tq in (tk, 2 * tk)
    nq = s // tq

    def body(q_ref, k_ref, v_ref, do_ref, lse_ref, dl_ref, dq_ref, dkv_ref):
        ki = pl.program_id(1)
        k_t, v_t = k_ref[...], v_ref[...]
        q0 = (ki * tk) // tq

        @pl.when(ki == 0)
        def _():
            dq_ref[...] = jnp.zeros_like(dq_ref)

        def tile(off, w, dk, dv, diagonal):
            cols = pl.ds(pl.multiple_of(off, tk), w)
            q_i, do_i = q_ref[cols, :], do_ref[cols, :]
            sc = lax.dot_general(k_t, q_i, _NT, preferred_element_type=F32) * ATT_SCALE
            dpt = lax.dot_general(v_t, do_i, _NT, preferred_element_type=F32)
            if diagonal:
                sc = jnp.where(lax.broadcasted_iota(jnp.int32, (tk, w), 0) <= lax.broadcasted_iota(jnp.int32, (tk, w), 1),
                               sc, -jnp.inf)
            pt = jnp.exp(sc - lse_ref[0, 0:1, cols])
            dsb = (pt * (dpt - dl_ref[0, 0:1, cols]) * ATT_SCALE).astype(BF16)
            dv = dv + lax.dot_general(pt.astype(BF16), do_i, _NN, preferred_element_type=F32)
            dk = dk + lax.dot_general(dsb, q_i, _NN, preferred_element_type=F32)
            dq_ref[cols, :] += lax.dot_general(dsb, k_t, _TN, preferred_element_type=F32)
            return dk, dv

        zero = jnp.zeros((tk, HEAD_PAD), F32)
        dk, dv = tile(ki * tk, tk, zero, zero, True)
        if tq != tk:
            dk, dv = lax.cond((ki * tk) % tq == 0, lambda a, b: tile((ki + 1) * tk, tk, a, b, False), lambda a, b: (a, b), dk, dv)

        def loop_body(qi, c):
            return tile(qi * tq, tq, c[0], c[1], False)

        c = lax.fori_loop(q0 + 1, nq, loop_body, (dk, dv))
        dkv_ref[:, 0:HEAD_PAD] = c[0]
        dkv_ref[:, HEAD_PAD:2 * HEAD_PAD] = c[1]

    full = pl.BlockSpec((s, HEAD_PAD), lambda h, ki: (0, h))
    tile = pl.BlockSpec((tk, HEAD_PAD), lambda h, ki: (ki, h))
    stat = pl.BlockSpec((1, 8, s), lambda h, ki: (h, 0, 0))
    return pl.pallas_call(
        body, name="attn_bwd", grid=(HEADS, s // tk),
        in_specs=[full, tile, tile, full, stat, stat],
        out_specs=[full, pl.BlockSpec((tk, 2 * HEAD_PAD), lambda h, ki: (ki, h))],
        out_shape=[jax.ShapeDtypeStruct((s, HEADS * HEAD_PAD), F32), jax.ShapeDtypeStruct((s, HEADS * 2 * HEAD_PAD), F32)],
        compiler_params=_params(("parallel", "arbitrary")),
    )(q, k, v, d_o, lse, delta)


PH = 16
PG = 128


def _pool_pd(buf, u_main_ref, g, i, tm):
    w = POOL_WINDOWS[g]
    cs = pl.ds(g * PG, PG)
    tot = buf[pl.ds(PH, tm), cs]
    for j in range(1, w):
        tot = tot + buf[pl.ds(PH - j, tm), cs]
    t = i * tm + lax.broadcasted_iota(jnp.int32, (tm, PG), 0)
    cnt = jnp.minimum(t + 1, w).astype(F32)
    return tot / cnt - u_main_ref[:, cs]


def _pool_ubuf(buf, up, um, first, tm):
    p = up[...]
    buf[0:PH, :] = jnp.where(first, jnp.zeros_like(p), p)
    buf[PH:PH + tm, :] = um[...]


def _pool_fwd(proj, w_pool, scale):
    s = proj.shape[0]
    tm = _tile(s, TM)
    cu = PU0 // C512

    def body(up, um, w_ref, sc_ref, zd_ref, buf):
        i = pl.program_id(0)
        _pool_ubuf(buf, up, um, i == 0, tm)
        for g in range(4):
            pd = _pool_pd(buf, um, g, i, tm).astype(BF16)
            e = lax.dot_general(pd, w_ref[g], _NN, preferred_element_type=F32)
            zd_ref[:, g * PG:(g + 1) * PG] = (e * sc_ref[:, g * PG:(g + 1) * PG]).astype(BF16)

    return pl.pallas_call(
        body, name="pool_fwd", grid=(s // tm,),
        in_specs=[_prev(tm, PH, C512, cu), _rows(tm, C512, cu), _full((4, PG, PG)), _full((1, C512))],
        out_specs=_rows(tm, C512, 0), out_shape=jax.ShapeDtypeStruct((s, C512), BF16),
        scratch_shapes=[pltpu.VMEM((tm + PH, C512), F32)], compiler_params=_params(("parallel",)),
    )(proj, proj, w_pool, scale)


def _pool_bwd1(dzd, proj, w_pool, scale):
    s = proj.shape[0]
    tm = _tile(s, TM)
    cu = PU0 // C512

    def body(dz_ref, up, um, w_ref, sc_ref, dpd_ref, dw_ref, acc_ref, buf):
        i = pl.program_id(0)
        first = i == 0
        _pool_ubuf(buf, up, um, first, tm)

        @pl.when(first)
        def _():
            dw_ref[...] = jnp.zeros_like(dw_ref)
            acc_ref[...] = jnp.zeros_like(acc_ref)
        for g in range(4):
            cs = slice(g * PG, (g + 1) * PG)
            pd = _pool_pd(buf, um, g, i, tm).astype(BF16)
            wg = w_ref[g]
            e = lax.dot_general(pd, wg, _NN, preferred_element_type=F32)
            dz = dz_ref[:, cs]
            acc_ref[0:1, cs] += _colsum(dz * e)
            de = (dz * sc_ref[:, cs]).astype(BF16)
            dw_ref[g] += lax.dot_general(pd, de, _TN, preferred_element_type=F32)
            dpd_ref[:, cs] = lax.dot_general(de, wg, _NT, preferred_element_type=F32)

    return pl.pallas_call(
        body, name="pool_bwd1", grid=(s // tm,),
        in_specs=[_rows(tm, C512, 0), _prev(tm, PH, C512, cu), _rows(tm, C512, cu), _full((4, PG, PG)), _full((1, C512))],
        out_specs=[_rows(tm, C512, 0), _full((4, PG, PG)), _full((8, C512))],
        out_shape=[jax.ShapeDtypeStruct((s, C512), F32), jax.ShapeDtypeStruct((4, PG, PG), F32),
                   jax.ShapeDtypeStruct((8, C512), F32)],
        scratch_shapes=[pltpu.VMEM((tm + PH, C512), F32)], compiler_params=_params(("arbitrary",)),
    )(dzd, proj, proj, w_pool, scale)


def _pool_bwd2(dpd, s):
    tm = _tile(s, TM)
    nt = s // tm

    def body(dm, dn, du_ref, acc_ref, buf):
        i = pl.program_id(0)
        buf[0:tm, :] = dm[...]
        nxt = dn[...]
        buf[tm:tm + PH, :] = jnp.where(i == nt - 1, jnp.zeros_like(nxt), nxt)
        t = i * tm + lax.broadcasted_iota(jnp.int32, (tm + PH, PG), 0)
        cols = []
        for g, w in enumerate(POOL_WINDOWS):
            cs = pl.ds(g * PG, PG)
            cnt = jnp.minimum(t + 1, w).astype(F32)
            buf[:, cs] = buf[:, cs] / cnt
        for g, w in enumerate(POOL_WINDOWS):
            cs = pl.ds(g * PG, PG)
            tot = buf[pl.ds(0, tm), cs]
            for j in range(1, w):
                tot = tot + buf[pl.ds(j, tm), cs]
            du = tot - dm[:, cs]
            du_ref[:, cs] = du.astype(BF16)
            cols.append(_colsum(du))
        _acc_rows(acc_ref, i == 0, [jnp.concatenate(cols, axis=1)])

    return pl.pallas_call(
        body, name="pool_bwd2", grid=(nt,),
        in_specs=[_rows(tm, C512, 0), _next(tm, PH, C512, 0, s)],
        out_specs=[_rows(tm, C512, 0), _full((8, C512))],
        out_shape=[jax.ShapeDtypeStruct((s, C512), BF16), jax.ShapeDtypeStruct((8, C512), F32)],
        scratch_shapes=[pltpu.VMEM((tm + PH, C512), F32)], compiler_params=_params(("arbitrary",)),
    )(dpd, dpd)


def _merge_specs(tm):
    return [_rows(tm, C512, 0), _rows(tm, C512, 0), _rows(tm, D, 0), _rows(tm, C512, 0),
            _rows(tm, 4 * D, GATES0 // (4 * D)),
            _full((C512, D)), _full((C512, D)), _full((D, D)), _full((C512, D))]


def _branch_ys(za, zb, o, zd, wa, wb, wc, wd):
    zs = (za[...], zb[...], o[...].astype(BF16), zd[...])
    return [lax.dot_general(z, w[...], _NN, preferred_element_type=F32) for z, w in zip(zs, (wa, wb, wc, wd))]


def _merge_fwd(za, zb, o, zd, proj, wa, wb, wc, wd):
    s = proj.shape[0]
    tm = _tile(s, TM_WIDE)

    def body(za_r, zb_r, o_r, zd_r, g_ref, wa_r, wb_r, wc_r, wd_r, m_ref):
        ys = _branch_ys(za_r, zb_r, o_r, zd_r, wa_r, wb_r, wc_r, wd_r)
        acc = _sig(g_ref[:, 0:D]) * ys[0]
        for b in range(1, 4):
            acc = acc + _sig(g_ref[:, b * D:(b + 1) * D]) * ys[b]
        m_ref[...] = acc.astype(BF16)

    return pl.pallas_call(
        body, name="merge_fwd", grid=(s // tm,), in_specs=_merge_specs(tm),
        out_specs=_rows(tm, D, 0), out_shape=jax.ShapeDtypeStruct((s, D), BF16),
        compiler_params=_params(("parallel",)),
    )(za, zb, o, zd, proj, wa, wb, wc, wd)


def _merge_bwd(dmerged, za, zb, o, zd, proj, wa, wb, wc, wd):
    s = proj.shape[0]
    tm = _tile(s, TM_WIDE)

    def body(dm_ref, za_r, zb_r, o_r, zd_r, g_ref, wa_r, wb_r, wc_r, wd_r,
             dya, dyb, dyc, dyd, dza, dzb, d_o, dzd, dg_ref, acc_ref):
        ys = _branch_ys(za_r, zb_r, o_r, zd_r, wa_r, wb_r, wc_r, wd_r)
        dm = dm_ref[...]
        sums = []
        for b, (dy_ref, dz_ref, w_r) in enumerate(((dya, dza, wa_r), (dyb, dzb, wb_r), (dyc, d_o, wc_r), (dyd, dzd, wd_r))):
            gt = _sig(g_ref[:, b * D:(b + 1) * D])
            dg = dm * ys[b] * gt * (1.0 - gt)
            dg_ref[:, b * D:(b + 1) * D] = dg.astype(BF16)
            sums.append(_colsum(dg))
            dy = (dm * gt).astype(BF16)
            dy_ref[...] = dy
            dz_ref[...] = lax.dot_general(dy, w_r[...], _NT, preferred_element_type=F32)
        _acc_rows(acc_ref, pl.program_id(0) == 0, [jnp.concatenate(sums, axis=1)])

    bf = lambda c: jax.ShapeDtypeStruct((s, c), BF16)
    f32 = lambda c: jax.ShapeDtypeStruct((s, c), F32)
    return pl.pallas_call(
        body, name="merge_bwd", grid=(s // tm,), in_specs=[_rows(tm, D, 0)] + _merge_specs(tm),
        out_specs=[_rows(tm, D, 0)] * 4 + [_rows(tm, C512, 0), _rows(tm, C512, 0), _rows(tm, D, 0), _rows(tm, C512, 0),
                                           _rows(tm, 4 * D, 0), _full((8, 4 * D))],
        out_shape=[bf(D)] * 4 + [f32(C512), f32(C512), f32(D), f32(C512), bf(4 * D), jax.ShapeDtypeStruct((8, 4 * D), F32)],
        compiler_params=_params(("arbitrary",)),
    )(dmerged, za, zb, o, zd, proj, wa, wb, wc, wd)


def _mm_res_ln(a, w, xres, prm, name):
    s, k = a.shape
    tm = _tile(s, TM_WIDE)

    def body(a_ref, w_ref, x_ref, p_ref, y_ref, xn_ref, hn_ref):
        y = lax.dot_general(a_ref[...], w_ref[...], _NN, preferred_element_type=F32)
        y_ref[...] = y
        xh, _ = _ln_stats(ALPHA * x_ref[...] + (1.0 + p_ref[0:1, :]) * y)
        xn = xh * p_ref[3:4, :] + p_ref[4:5, :]
        xn_ref[...] = xn
        hn_ref[...] = (xn * (1.0 + p_ref[1:2, :]) + p_ref[2:3, :]).astype(BF16)

    return pl.pallas_call(
        body, name=name, grid=(s // tm,),
        in_specs=[_rows(tm, k, 0), _full((k, D)), _rows(tm, D, 0), _full((8, D))],
        out_specs=[_rows(tm, D, 0)] * 3,
        out_shape=[jax.ShapeDtypeStruct((s, D), F32), jax.ShapeDtypeStruct((s, D), F32), jax.ShapeDtypeStruct((s, D), BF16)],
        compiler_params=_params(("parallel",)),
    )(a, w, xres, prm)


def _ln_res_bwd(dres_next, dh, xres, y, prm):
    s = xres.shape[0]
    tm = _tile(s, TM)

    def body(dn_ref, dh_ref, x_ref, y_ref, p_ref, dres_ref, dy_ref, acc_ref):
        gam, lng = p_ref[0:1, :], p_ref[3:4, :]
        yv = y_ref[...]
        xh, rstd = _ln_stats(ALPHA * x_ref[...] + (1.0 + gam) * yv)
        xn = xh * lng + p_ref[4:5, :]
        dh_v = dh_ref[...]
        dxn = dn_ref[...] + dh_v * (1.0 + p_ref[1:2, :])
        dr = _ln_bwd(dxn * lng, xh, rstd)
        dres_ref[...] = ALPHA * dr
        dy_ref[...] = ((1.0 + gam) * dr).astype(BF16)
        _acc_rows(acc_ref, pl.program_id(0) == 0,
                  [_colsum(dr * yv), _colsum(dh_v * xn), _colsum(dh_v), _colsum(dxn * xh), _colsum(dxn)])

    return pl.pallas_call(
        body, name="ln_res_bwd", grid=(s // tm,),
        in_specs=[_rows(tm, D, 0)] * 4 + [_full((8, D))],
        out_specs=[_rows(tm, D, 0), _rows(tm, D, 0), _full((8, D))],
        out_shape=[jax.ShapeDtypeStruct((s, D), F32), jax.ShapeDtypeStruct((s, D), BF16), jax.ShapeDtypeStruct((8, D), F32)],
        compiler_params=_params(("arbitrary",)),
    )(dres_next, dh, xres, y, prm)


FC = 16


def _shift_down(cur, prev, k, rowi):
    return jnp.where(rowi >= k, pltpu.roll(cur, k, 0), pltpu.roll(prev, k, 0))


def _shift_up(cur, nxt, k, rowi):
    return jnp.where(rowi < FC - k, pltpu.roll(cur, FC - k, 0), pltpu.roll(nxt, FC - k, 0))


def _conv3_chunk(w, cur, prev, rowi):
    return w[2] * cur + w[1] * _shift_down(cur, prev, 1, rowi) + w[0] * _shift_down(cur, prev, 2, rowi)


def _conv3_t_chunk(w, cur, nxt, rowi):
    return w[2] * cur + w[1] * _shift_up(cur, nxt, 1, rowi) + w[0] * _shift_up(cur, nxt, 2, rowi)


def _chunk_rows(j):
    return pl.ds(pl.multiple_of(j * FC, FC), FC)


def _ffn_chunk_specs(tm, s):
    r = tm // FC
    last = s // FC - 1
    out = []
    for half in (0, 1):
        out.append((pl.BlockSpec((FC, D_FF), lambda i, half=half: (jnp.maximum(i * r - 1, 0), half)),
                    pl.BlockSpec((tm, D_FF), lambda i, half=half: (i, half)),
                    pl.BlockSpec((FC, D_FF), lambda i, half=half: (jnp.minimum((i + 1) * r, last), half)),
                    pl.BlockSpec((8, D_FF), lambda i, half=half: (0, half))))
    return out


def _ffn_fwd(up, w8):
    s = up.shape[0]
    tm = _tile(s, TM_WIDE)
    (pv_s, mv_s, _, wv_s), (pg_s, mg_s, _, wg_s) = _ffn_chunk_specs(tm, s)

    def body(pv, mv, pg, mg, wv_ref, wg_ref, a_ref):
        first = pl.program_id(0) == 0
        lg = 2 * LANE
        rowi = lax.broadcasted_iota(jnp.int32, (FC, lg), 0)
        zero = jnp.zeros((FC, lg), F32)
        for cg in range(D_FF // lg):
            cs = slice(cg * lg, (cg + 1) * lg)
            wv = [wv_ref[k:k + 1, cs] for k in range(SC_W)]
            wg = [wg_ref[k:k + 1, cs] for k in range(SC_W)]

            def step(j, carry, cs=cs, wv=wv, wg=wg):
                rows = _chunk_rows(j)
                xv, xg = mv[rows, cs], mg[rows, cs]
                gate = _conv3_chunk(wg, xg, carry[1], rowi)
                a_ref[rows, cs] = (gate * _sig(gate) * _conv3_chunk(wv, xv, carry[0], rowi)).astype(BF16)
                return xv, xg

            lax.fori_loop(0, tm // (2 * FC), lambda j, c, step=step: step(2 * j + 1, step(2 * j, c)),
                          (jnp.where(first, zero, pv[:, cs]), jnp.where(first, zero, pg[:, cs])))

    return pl.pallas_call(
        body, name="ffn_fwd", grid=(s // tm,), in_specs=[pv_s, mv_s, pg_s, mg_s, wv_s, wg_s],
        out_specs=_rows(tm, D_FF, 0), out_shape=jax.ShapeDtypeStruct((s, D_FF), BF16),
        compiler_params=_params(("parallel",)),
    )(up, up, up, up, w8, w8)


def _ffn_bwd(da, up, w8):
    s = up.shape[0]
    tm = _tile(s, TM_WIDE)
    n, nt = tm // FC, s // tm
    (pv_s, mv_s, nv_s, wv_s), (pg_s, mg_s, ng_s, wg_s) = _ffn_chunk_specs(tm, s)

    def body(dam, dan, pv, mv, nv, pg, mg, ng, wv_ref, wg_ref, dup_ref, dw_ref):
        i = pl.program_id(0)
        first, last = i == 0, i == nt - 1
        rowi = lax.broadcasted_iota(jnp.int32, (FC, LANE), 0)
        zero = jnp.zeros((FC, LANE), F32)

        @pl.when(first)
        def _():
            dw_ref[...] = jnp.zeros_like(dw_ref)

        for cg in range(D_FF // LANE):
            cs = slice(cg * LANE, (cg + 1) * LANE)
            cs_g = slice(D_FF + cg * LANE, D_FF + (cg + 1) * LANE)
            wv = [wv_ref[k:k + 1, cs] for k in range(SC_W)]
            wg = [wg_ref[k:k + 1, cs] for k in range(SC_W)]

            def conv_grads(xv, xg, xpv, xpg, dav, wv=wv, wg=wg):
                val, gate = _conv3_chunk(wv, xv, xpv, rowi), _conv3_chunk(wg, xg, xpg, rowi)
                sg = _sig(gate)
                return dav * gate * sg, dav * val * (sg * (1.0 + gate * (1.0 - sg)))

            def step(j, c, cs=cs, cs_g=cs_g, wv=wv, wg=wg, conv_grads=conv_grads):
                xpv, xpg, dvp, dgp = c[:4]
                rows = _chunk_rows(j)
                xv, xg = mv[rows, cs], mg[rows, cs]
                dv, dg = conv_grads(xv, xg, xpv, xpg, dam[rows, cs])
                prow = _chunk_rows(jnp.maximum(j - 1, 0))
                dup_ref[prow, cs] = _conv3_t_chunk(wv, dvp, dv, rowi).astype(BF16)
                dup_ref[prow, cs_g] = _conv3_t_chunk(wg, dgp, dg, rowi).astype(BF16)
                accs = (c[4] + dv * _shift_down(xv, xpv, 2, rowi), c[5] + dv * _shift_down(xv, xpv, 1, rowi), c[6] + dv * xv,
                        c[7] + dg * _shift_down(xg, xpg, 2, rowi), c[8] + dg * _shift_down(xg, xpg, 1, rowi), c[9] + dg * xg)
                return (xv, xg, dv, dg) + accs

            init = (jnp.where(first, zero, pv[:, cs]), jnp.where(first, zero, pg[:, cs]), zero, zero) + (zero,) * 6
            c = lax.fori_loop(0, n // 2, lambda j, c, step=step: step(2 * j + 1, step(2 * j, c)), init)
            dv_n, dg_n = conv_grads(nv[:, cs], ng[:, cs], c[0], c[1], jnp.where(last, zero, dan[:, cs]))
            dup_ref[tm - FC:tm, cs] = _conv3_t_chunk(wv, c[2], dv_n, rowi).astype(BF16)
            dup_ref[tm - FC:tm, cs_g] = _conv3_t_chunk(wg, c[3], dg_n, rowi).astype(BF16)
            for k in range(SC_W):
                dw_ref[k:k + 1, cs] += _colsum(c[4 + k])
                dw_ref[k:k + 1, cs_g] += _colsum(c[7 + k])

    r = tm // FC
    da_next = pl.BlockSpec((FC, D_FF), lambda i: (jnp.minimum((i + 1) * r, s // FC - 1), 0))
    return pl.pallas_call(
        body, name="ffn_bwd", grid=(nt,),
        in_specs=[_rows(tm, D_FF, 0), da_next, pv_s, mv_s, nv_s, pg_s, mg_s, ng_s, wv_s, wg_s],
        out_specs=[_rows(tm, 2 * D_FF, 0), _full((8, 2 * D_FF))],
        out_shape=[jax.ShapeDtypeStruct((s, 2 * D_FF), BF16), jax.ShapeDtypeStruct((8, 2 * D_FF), F32)],
        compiler_params=_params(("arbitrary",)),
    )(da, da, up, up, up, up, up, up, w8, w8)


def _loss_head(y, target):
    s = y.shape[0]
    tm = _tile(s, TM)

    def body(y_ref, t_ref, dy_ref, l_ref):
        err = y_ref[...] - t_ref[...]
        dy_ref[...] = err * (1.0 / D)
        part = 0.5 * jnp.sum(jnp.mean(err * err, axis=-1, keepdims=True), axis=0, keepdims=True)

        @pl.when(pl.program_id(0) == 0)
        def _():
            l_ref[...] = jnp.zeros_like(l_ref)
        l_ref[...] += part

    return pl.pallas_call(
        body, name="loss_head", grid=(s // tm,),
        in_specs=[_rows(tm, D, 0)] * 2, out_specs=[_rows(tm, D, 0), _full((8, LANE))],
        out_shape=[jax.ShapeDtypeStruct((s, D), F32), jax.ShapeDtypeStruct((8, LANE), F32)],
        compiler_params=_params(("arbitrary",)),
    )(y, target)


def _silu_rows(c_all):
    def body(c_ref, o_ref):
        cv = c_ref[...]
        o_ref[...] = jnp.concatenate([cv * _sig(cv), jnp.zeros((N_DEV, D), F32)], axis=0).astype(BF16)

    return pl.pallas_call(
        body, name="silu_rows", grid=(1,), in_specs=[_full((N_DEV, D))], out_specs=_full((2 * N_DEV, D)),
        out_shape=jax.ShapeDtypeStruct((2 * N_DEV, D), BF16), compiler_params=_params(("arbitrary",)),
    )(c_all)


GRAD_ROWS = 512


def _sum_parts(p_ref, n):
    g = p_ref[0].astype(F32)
    for j in range(1, n):
        g = g + p_ref[j].astype(F32)
    return g


def _adamw(parts, w, m, v, name):
    n = parts.shape[0]
    r, c = w.shape
    tr = GRAD_ROWS
    assert r % tr == 0 and parts.shape[2] == c, (parts.shape, w.shape)

    def body(p_ref, w_ref, m_ref, v_ref, g_out, d_out, m_out, v_out):
        g = _sum_parts(p_ref, n)
        mn = ADAM_B1 * m_ref[...] + (1.0 - ADAM_B1) * g
        vn = ADAM_B2 * v_ref[...] + (1.0 - ADAM_B2) * (g * g)
        m_hat = mn / (1.0 - ADAM_B1 ** ADAM_STEP)
        v_hat = vn / (1.0 - ADAM_B2 ** ADAM_STEP)
        g_out[...] = g
        d_out[...] = -ADAM_LR * (m_hat / (jnp.sqrt(v_hat) + ADAM_EPS) + ADAM_WD * w_ref[...])
        m_out[...] = mn
        v_out[...] = vn

    blk = pl.BlockSpec((tr, c), lambda i: (i, 0))
    return pl.pallas_call(
        body, name=name, grid=(r // tr,),
        in_specs=[pl.BlockSpec((n, tr, c), lambda i: (0, i, 0)), blk, blk, blk], out_specs=[blk] * 4,
        out_shape=[jax.ShapeDtypeStruct((r, c), F32)] * 4, compiler_params=_params(("parallel",)),
    )(parts, w, m, v)


def _sum8(parts, name):
    n, r, c = parts.shape
    tr = GRAD_ROWS
    assert r % tr == 0, r

    def body(p_ref, g_out):
        g_out[...] = _sum_parts(p_ref, n)

    return pl.pallas_call(
        body, name=name, grid=(r // tr,),
        in_specs=[pl.BlockSpec((n, tr, c), lambda i: (0, i, 0))], out_specs=pl.BlockSpec((tr, c), lambda i: (i, 0)),
        out_shape=jax.ShapeDtypeStruct((r, c), F32), compiler_params=_params(("parallel",)),
    )(parts)


def _peers():
    ix, iy, ic = lax.axis_index("x"), lax.axis_index("y"), lax.axis_index("c")
    me = 4 * ix + 2 * iy + ic
    out = []
    for k in range(1, N_DEV):
        px = 1 - ix if (k >> 2) & 1 else ix
        py = 1 - iy if (k >> 1) & 1 else iy
        pc = 1 - ic if k & 1 else ic
        out.append(((px, py, pc), 4 * px + 2 * py + pc))
    return me, out


_HBM = pl.BlockSpec(memory_space=pltpu.HBM)


def _exchange(x, name, gather):
    shape = ((N_DEV,) + x.shape) if gather else x.shape

    def body(x_ref, o_ref, send_sems, recv_sems, local_sem):
        me, peers = _peers()
        src_of = (lambda p: x_ref) if gather else (lambda p: x_ref.at[p])
        local = pltpu.make_async_copy(src_of(me), o_ref.at[me], local_sem)
        local.start()
        sends = []
        for k, (dev, p) in enumerate(peers):
            cp = pltpu.make_async_remote_copy(src_ref=src_of(p), dst_ref=o_ref.at[me], send_sem=send_sems.at[k],
                                              recv_sem=recv_sems.at[k], device_id=dev, device_id_type=MESH)
            cp.start()
            sends.append(cp)
        for k, (dev, p) in enumerate(peers):
            pltpu.make_async_remote_copy(src_ref=src_of(p), dst_ref=o_ref.at[p], send_sem=send_sems.at[k],
                                         recv_sem=recv_sems.at[k], device_id=dev, device_id_type=MESH).wait_recv()
        for cp in sends:
            cp.wait_send()
        local.wait()

    return pl.pallas_call(
        body, name=name, in_specs=[_HBM], out_specs=_HBM, out_shape=jax.ShapeDtypeStruct(shape, x.dtype),
        scratch_shapes=[pltpu.SemaphoreType.DMA((N_DEV - 1,)), pltpu.SemaphoreType.DMA((N_DEV - 1,)),
                        pltpu.SemaphoreType.DMA],
    )(x)


def _gather_two_level(x, name):
    def body(x_ref, o_ref, send_sems, recv_sems, local_sem):
        ix, iy, ic = lax.axis_index("x"), lax.axis_index("y"), lax.axis_index("c")
        me, sibling = (ix, iy, ic), (ix, iy, 1 - ic)
        chips = [(1 - ix, iy), (ix, 1 - iy), (1 - ix, 1 - iy)]

        def slot(px, py, pc):
            return o_ref.at[4 * px + 2 * py + pc]

        def copy(k, block, to, src=None):
            return pltpu.make_async_remote_copy(src_ref=slot(*block) if src is None else src, dst_ref=slot(*block),
                                                send_sem=send_sems.at[k], recv_sem=recv_sems.at[k],
                                                device_id=to, device_id_type=MESH)

        mine = pltpu.make_async_copy(x_ref, slot(*me), local_sem)
        mine.start()
        first = [copy(0, me, sibling, src=x_ref)] + [copy(1 + j, me, (*chip, ic), src=x_ref) for j, chip in enumerate(chips)]
        for cp in first:
            cp.start()
        passed = [copy(4 + j, (*chip, ic), sibling) for j, chip in enumerate(chips)]
        for j, chip in enumerate(chips):
            copy(1 + j, (*chip, ic), me).wait_recv()
            passed[j].start()
        copy(0, sibling, me).wait_recv()
        for j, chip in enumerate(chips):
            copy(4 + j, (*chip, 1 - ic), me).wait_recv()
        for cp in first + passed:
            cp.wait_send()
        mine.wait()

    return pl.pallas_call(
        body, name=name, in_specs=[_HBM], out_specs=_HBM, out_shape=jax.ShapeDtypeStruct((N_DEV,) + x.shape, x.dtype),
        scratch_shapes=[pltpu.SemaphoreType.DMA((N_DEV - 1,)), pltpu.SemaphoreType.DMA((N_DEV - 1,)),
                        pltpu.SemaphoreType.DMA],
    )(x)


def _pack_rows(arrs, row_mult):
    flat = jnp.concatenate([a.reshape(-1) for a in arrs])
    n = flat.shape[0]
    pad = (-n) % (LANE * row_mult)
    if pad:
        flat = jnp.concatenate([flat, jnp.zeros((pad,), flat.dtype)])
    return flat.reshape(-1, LANE)


def _pack_cols(arrs, row_mult):
    flat = jnp.concatenate(arrs, axis=1)
    n = flat.shape[1]
    pad = (-n) % (LANE * row_mult)
    if pad:
        flat = jnp.concatenate([flat, jnp.zeros((flat.shape[0], pad), flat.dtype)], axis=1)
    return flat.reshape(flat.shape[0], -1, LANE)


def _unpack(flat, shapes):
    out, off = [], 0
    lead = flat.shape[:-1]
    for shp in shapes:
        n = 1
        for d_ in shp:
            n *= d_
        out.append(flat[..., off:off + n].reshape(lead + tuple(shp)))
        off += n
    return out


BIG = (("w_conv_out", 2), ("w_sc_out", 2), ("w_uq", 2), ("w_ukv", 2), ("w_mla_out", 2),
       ("w_pool_out", 2), ("w_o", 1), ("w_down", 1))
TAPS = (("conv_dw", 2), ("sc_dw", 2), ("ffn_dw", 2))
SMALL = ("b_ada", "b_in", "conv_ln_g", "conv_ln_b", "q_norm_g", "kv_norm_g", "w_pool", "pool_scale",
         "ln1_g", "ln1_b", "ln2_g", "ln2_b")
WEIGHTS = ("w_ada", "b_ada", "w_in", "b_in", "conv_dw", "conv_ln_g", "conv_ln_b", "w_conv_out", "sc_dw", "w_sc_out",
           "q_norm_g", "w_uq", "kv_norm_g", "w_ukv", "w_mla_out", "w_pool", "pool_scale", "w_pool_out", "w_o",
           "ln1_g", "ln1_b", "w_up", "ffn_dw", "w_down", "ln2_g", "ln2_b")


def _join(g, axis):
    g = jnp.moveaxis(g, 0, axis)
    shp = list(g.shape)
    shp[axis:axis + 2] = [shp[axis] * shp[axis + 1]]
    return g.reshape(shp)


def _split(full, axis):
    shp = list(full.shape)
    shp[axis:axis + 1] = [N_DEV, shp[axis] // N_DEV]
    return jnp.moveaxis(full.reshape(shp), axis, 0)


def _pad_rows(a, rows):
    return jnp.concatenate([a, jnp.zeros((rows - a.shape[0],) + a.shape[1:], a.dtype)], axis=0)


def _proj_cols(w):
    z = lambda n: jnp.zeros(w.shape[:-1] + (n,), w.dtype)
    return jnp.concatenate([w[..., 3488:7584], w[..., 0:2944], z(NOPE), w[..., 2944:2976], z(HEAD_PAD - QK_DIM),
                            w[..., 2976:3488]], axis=-1)


def _proj_cols_inv(w):
    return jnp.concatenate([w[..., CA0:KR0], w[..., KR0 + NOPE:KR0 + QK_DIM], w[..., PU0:NPROJ], w[..., 0:CA0]], axis=-1)


IN_COLS = 7584
IN_SHARD = IN_COLS // N_DEV
IN_SHARD_PAD = 960
UP_SHARD = 2 * D_FF // N_DEV
_IN_CUTS = (2944, 2976, 3488)
_IN_SEGS = ((3488, IN_COLS), (0, 2944), NOPE, (2944, 2976), HEAD_PAD - QK_DIM, (2976, 3488))


def _proj_rows(g):
    pieces = []
    for seg in _IN_SEGS:
        if isinstance(seg, int):
            pieces.append(jnp.zeros((g.shape[0], seg, g.shape[2]), g.dtype))
            continue
        j = seg[0]
        while j < seg[1]:
            dev = j // IN_SHARD
            e = min(seg[1], (dev + 1) * IN_SHARD)
            r0 = dev * IN_SHARD_PAD + j - dev * IN_SHARD
            pieces.append(g[:, r0:r0 + e - j])
            j = e
    return jnp.concatenate(pieces, axis=1)


def _aligned_row(j):
    if j < 2944:
        return CA0 + j
    if j < 2976:
        return KR0 + NOPE + j - 2944
    if j < 3488:
        return PU0 + j - 2976
    return j - 3488


def _proj_rows_split(gt):
    out = []
    for dev in range(N_DEV):
        j0, j1 = dev * IN_SHARD, (dev + 1) * IN_SHARD
        cuts = [j0] + [c for c in _IN_CUTS if j0 < c < j1] + [j1]
        out.append(jnp.concatenate([gt[_aligned_row(a):_aligned_row(a) + b - a] for a, b in zip(cuts[:-1], cuts[1:])], axis=0))
    return jnp.stack(out)


def _pad_axis(a, axis, size):
    shp = list(a.shape)
    shp[axis] = size - shp[axis]
    return jnp.concatenate([a, jnp.zeros(shp, a.dtype)], axis=axis)


def _all_layer_weights(full, small):
    w = {}
    w["w_inT"] = _proj_rows(full["w_inT"])
    w["b_in"] = _proj_cols(small["b_in"])[:, None, :]
    w["conv_w"] = _pad_axis(full["conv_dw"], 1, 32)
    w["sc_w"] = _pad_axis(full["sc_dw"], 1, 8)
    w["ffn_w"] = _pad_axis(full["ffn_dw"], 1, 8)
    w["w_uq"] = _pad_axis(full["w_uq"].reshape(DEPTH, 256, HEADS, QK_DIM), 3, HEAD_PAD).reshape(DEPTH, 256, -1)
    w["w_ukv"] = _pad_axis(full["w_ukv"].reshape(DEPTH, 128, HEADS, 2, NOPE), 4, HEAD_PAD).reshape(DEPTH, 128, -1)
    w["w_c"] = _pad_axis(full["w_mla_out"].reshape(DEPTH, HEADS, NOPE, D), 2, HEAD_PAD).reshape(DEPTH, HEADS * HEAD_PAD, D)
    w["w_a"], w["w_b"], w["w_d"] = full["w_conv_out"], full["w_sc_out"], full["w_pool_out"]
    w["w_o"], w["w_upT"], w["w_down"] = full["w_o"], full["w_upT"], full["w_down"]
    w["w_pool"] = small["w_pool"].astype(BF16)
    for n in ("conv_ln_g", "conv_ln_b", "q_norm_g", "kv_norm_g", "pool_scale"):
        w[n] = small[n][:, None, :]
    return w


def _prm(rows):
    z = jnp.zeros((D,), F32)
    rows = list(rows) + [z] * (8 - len(rows))
    return jnp.stack(rows)


def _layer_fwd(x, h1, w, prm1, prm2, tabs):
    ct, s1t, s2t = tabs
    proj = _mm(h1, w["w_inT"], "nt", "proj_fwd", bias=w["b_in"], tm=1024, tn=768)
    yconv, za = _conv_a_fwd(proj, w["conv_w"], w["conv_ln_g"], w["conv_ln_b"])
    zb = _sc_fwd(proj, w["sc_w"])
    qn, kvn = _lat_fwd(proj, w["q_norm_g"], w["kv_norm_g"])
    qp = _mm(qn, w["w_uq"], "nn", "uq_fwd")
    kvp = _mm(kvn, w["w_ukv"], "nn", "ukv_fwd")
    q, k, v = _rope_fwd(qp, kvp, proj, ct, s1t, s2t)
    o, lse = _attn_fwd(q, k, v)
    zd = _pool_fwd(proj, w["w_pool"], w["pool_scale"])
    merged = _merge_fwd(za, zb, o, zd, proj, w["w_a"], w["w_b"], w["w_c"], w["w_d"])
    mix, x1, h2 = _mm_res_ln(merged, w["w_o"], x, prm1, "o_res_ln")
    up = _mm(h2, w["w_upT"], "nt", "up_fwd", tm=1024, tn=1408)
    a = _ffn_fwd(up, w["ffn_w"])
    ffn, x2, h_next = _mm_res_ln(a, w["w_down"], x1, prm2, "down_res_ln")
    res = dict(x=x, h1=h1, proj=proj, yconv=yconv, za=za, zb=zb, qn=qn, kvn=kvn, q=q, k=k, v=v, o=o, lse=lse, zd=zd,
               merged=merged, mix=mix, x1=x1, h2=h2, up=up, a=a, ffn=ffn)
    return x2, h_next, res


def _layer_bwd(dres_next, dh_next, r, w, prm1, prm2, tabs):
    ct, s1t, s2t = tabs
    s = r["x"].shape[0]
    g = {}
    dres2, dffn, acc2 = _ln_res_bwd(dres_next, dh_next, r["x1"], r["ffn"], prm2)
    da = _mm(dffn, w["w_down"], "nt", "down_bwd_x", tm=1024, tn=1408)
    g["w_down"] = _mm(r["a"], dffn, "tn", "down_bwd_w", tm=1408)
    dup, dffn_w = _ffn_bwd(da, r["up"], w["ffn_w"])
    g["ffn_dw"] = dffn_w[:SC_W]
    dh2 = _mm(dup, w["w_upT"], "nn", "up_bwd_x", tm=1024, tk=1408)
    g["w_upT"] = _mm(dup, r["h2"], "tn", "up_bwd_w", tm=1408)
    dres1, dmix, acc1 = _ln_res_bwd(dres2, dh2, r["x"], r["mix"], prm1)
    dmerged = _mm(dmix, w["w_o"], "nt", "o_bwd_x")
    g["w_o"] = _mm(r["merged"], dmix, "tn", "o_bwd_w")
    (dya, dyb, dyc, dyd, dza, dzb, d_o, dzd, dgates, accg) = _merge_bwd(
        dmerged, r["za"], r["zb"], r["o"], r["zd"], r["proj"], w["w_a"], w["w_b"], w["w_c"], w["w_d"])
    g["w_conv_out"] = _mm(r["za"], dya, "tn", "branch_bwd_w")
    g["w_sc_out"] = _mm(r["zb"], dyb, "tn", "branch_bwd_w")
    g["w_pool_out"] = _mm(r["zd"], dyd, "tn", "branch_bwd_w")
    gwc = _mm(r["o"], dyc, "tn", "mla_out_bwd_w")
    g["w_mla_out"] = gwc.reshape(HEADS, HEAD_PAD, D)[:, :NOPE].reshape(HEADS * NOPE, D)
    dyconv, dconv_w, acca = _conv_a_bwd1(dza, r["yconv"], r["proj"], w["conv_ln_g"], w["conv_ln_b"])
    g["conv_dw"], g["conv_ln_g"], g["conv_ln_b"] = dconv_w[:CONV_W], acca[0], acca[1]
    d_ca, d_cb, acca2 = _conv_a_bwd2(dyconv, r["proj"], w["conv_w"])
    dconv, d_bg, dsc_w, accb1 = _sc_bwd1(dzb, r["proj"], w["sc_w"])
    g["sc_dw"] = dsc_w[:SC_W]
    d_cg, d_sx, accb2 = _sc_bwd2(dconv, r["proj"], w["sc_w"])
    delta, dob = _attn_prep(d_o, r["o"])
    dq, dkv = _attn_bwd(r["q"], r["k"], r["v"], dob, r["lse"], delta)
    dqp, d_kr, acckr = _rope_bwd(dq, dkv, ct, s1t, s2t)
    dqn = _mm(dqp, w["w_uq"], "nt", "uq_bwd_x")
    guq = _mm(r["qn"], dqp, "tn", "uq_bwd_w")
    g["w_uq"] = guq.reshape(256, HEADS, HEAD_PAD)[:, :, :QK_DIM].reshape(256, HEADS * QK_DIM)
    dkvn = _mm(dkv, w["w_ukv"], "nt", "ukv_bwd_x")
    gukv = _mm(r["kvn"], dkv, "tn", "ukv_bwd_w")
    g["w_ukv"] = gukv.reshape(128, HEADS, 2, HEAD_PAD)[..., :NOPE].reshape(128, HEADS * 2 * NOPE)
    d_ql, d_kvl, accq, acckv = _lat_bwd(dqn, dkvn, r["proj"], w["q_norm_g"], w["kv_norm_g"])
    g["q_norm_g"], g["kv_norm_g"] = accq[0], acckv[0]
    dpd, g["w_pool"], accd1 = _pool_bwd1(dzd, r["proj"], w["w_pool"], w["pool_scale"])
    g["pool_scale"] = accd1[0]
    d_pu, accd2 = _pool_bwd2(dpd, s)
    dproj = jnp.concatenate([dgates, d_ca, d_cb, d_bg, d_cg, d_sx, d_ql, d_kvl, d_kr, d_pu], axis=1)
    db = jnp.concatenate([accg[0], acca2[0], acca2[1], accb1[0], accb2[0], accb2[1], accq[1], acckv[1], acckr[0], accd2[0]])
    g["b_in"] = _proj_cols_inv(db)
    dh1 = _mm(dproj, w["w_inT"], "nn", "proj_bwd_x", tm=1024, tk=1536)
    g["w_inT"] = _proj_rows_split(_mm(dproj, r["h1"], "tn", "proj_bwd_w", tm=768))
    g["ln1_g"], g["ln1_b"], g["ln2_g"], g["ln2_b"] = acc1[3], acc1[4], acc2[3], acc2[4]
    return dres1, dh1, g, (acc1, acc2)


def _rope_tables(positions):
    half = ROPE // 2
    inv = 1.0 / (ROPE_THETA ** (jnp.arange(0, ROPE, 2, dtype=F32) / ROPE))
    ang = positions.astype(F32)[:, None] * inv
    cos, sin = jnp.cos(ang), jnp.sin(ang)
    s = positions.shape[0]
    z = lambda n: jnp.zeros((s, n), F32)
    ct = jnp.concatenate([jnp.ones((s, NOPE), F32), cos, cos, z(HEAD_PAD - QK_DIM)], axis=1)
    s1t = jnp.concatenate([z(NOPE), -sin, z(half), z(HEAD_PAD - QK_DIM)], axis=1)
    s2t = jnp.concatenate([z(NOPE), z(half), sin, z(HEAD_PAD - QK_DIM)], axis=1)
    return ct, s1t, s2t


def _local_step(x, mod, positions, full, small, target):
    tabs = _rope_tables(positions)
    stacked = _all_layer_weights(full, small)
    ws = [{n: a[l] for n, a in stacked.items()} for l in range(DEPTH)]
    zero = jnp.zeros((D,), F32)
    prm1s, prm2s = [], []
    for l in range(DEPTH):
        sh1, sc1, g1, sh2, sc2, g2 = (mod[l, j] for j in range(6))
        nxt = (mod[l + 1, 1], mod[l + 1, 0]) if l + 1 < DEPTH else (zero, zero)
        prm1s.append(_prm([g1, sc2, sh2, small["ln1_g"][l], small["ln1_b"][l]]))
        prm2s.append(_prm([g2, nxt[0], nxt[1], small["ln2_g"][l], small["ln2_b"][l]]))
    prm0 = _prm([zero, mod[0, 1], mod[0, 0]])
    h = _mod_fwd(x, prm0)
    res = []
    xc = x
    for l in range(DEPTH):
        xc, h, r = _layer_fwd(xc, h, ws[l], prm1s[l], prm2s[l], tabs)
        res.append(r)
    dy, lacc = _loss_head(xc, target)
    loss = lacc[0, 0]
    dres, dh = dy, jnp.zeros_like(dy)
    grads = [None] * DEPTH
    accs = [None] * DEPTH
    for l in reversed(range(DEPTH)):
        dres, dh, grads[l], accs[l] = _layer_bwd(dres, dh, res[l], ws[l], prm1s[l], prm2s[l], tabs)
    dx, acc0 = _mod_bwd(dres, dh, x, prm0)
    dmod = []
    for l in range(DEPTH):
        acc1, acc2 = accs[l]
        dsc1, dsh1 = (acc0[0], acc0[1]) if l == 0 else (accs[l - 1][1][1], accs[l - 1][1][2])
        dmod.append(jnp.stack([dsh1, dsc1, acc1[0], acc1[2], acc1[1], acc2[0]]))
    return loss, dx, grads, jnp.stack(dmod)


ADA_SHARD = 6 * D // N_DEV


def _step(p):
    me = 4 * lax.axis_index("x") + 2 * lax.axis_index("y") + lax.axis_index("c")
    x, target, positions = p["x"][0], p["loss_target"][0], p["positions"][0]

    tap_shapes = [p[n].shape for n, _ in TAPS] + [(D,)]
    small_g = _exchange(_pack_rows([p[n] for n, _ in TAPS] + [p["c"][0]], 8), "gather_taps", True)
    parts = _unpack(small_g.reshape(N_DEV, -1), tap_shapes)
    full = {n: _join(g, ax) for (n, ax), g in zip(TAPS, parts[:-1])}
    c_all = parts[-1]
    w_in_t = jnp.swapaxes(p["w_in"], 1, 2).astype(BF16)
    w_in_t = jnp.concatenate([w_in_t, jnp.zeros((DEPTH, IN_SHARD_PAD - IN_SHARD, D), BF16)], axis=1)
    w_up_t = jnp.swapaxes(p["w_up"], 1, 2).astype(BF16)
    big_shapes = [p[n].shape for n, _ in BIG] + [w_in_t.shape, w_up_t.shape]
    big_g = _gather_two_level(_pack_rows([p[n].astype(BF16) for n, _ in BIG] + [w_in_t, w_up_t], 16), "gather_weights")
    big_parts = _unpack(big_g.reshape(N_DEV, -1), big_shapes)
    for (n, ax), g in zip(BIG, big_parts):
        full[n] = _join(g, ax)
    full["w_inT"] = _join(big_parts[-2], 1)
    full["w_upT"] = _join(big_parts[-1], 1)
    small = {n: p[n] for n in SMALL}

    c_act = _silu_rows(c_all)
    w_ada_cols = jnp.moveaxis(p["w_ada"], 0, 1).reshape(D, DEPTH * ADA_SHARD)
    b_shard = lax.dynamic_slice_in_dim(p["b_ada"], me * ADA_SHARD, ADA_SHARD, axis=1).reshape(1, DEPTH * ADA_SHARD)
    mod_sh = _mm(c_act, w_ada_cols, "nn", "ada_fwd", bias=b_shard)[:N_DEV]
    mod_x = _exchange(_pack_cols([mod_sh], 8), "scatter_mod", False)
    mod = mod_x.reshape(N_DEV, -1)[:, :DEPTH * ADA_SHARD].reshape(N_DEV, DEPTH, ADA_SHARD)
    mod = jnp.moveaxis(mod, 0, 1).reshape(DEPTH, 6, D)

    loss_local, dx, grads, dmod = _local_step(x, mod, positions, full, small, target)
    loss = lax.psum(loss_local, ("x", "y", "c"))
    gfull = {n: jnp.stack([grads[l][n] for l in range(DEPTH)]) for n in grads[0]}

    out = {"loss": loss, "grad_x": dx[None]}

    def emit(names, g, dlt, mn, vn, shapes):
        for n, gi, di, mi, vi in zip(names, _unpack(g, shapes), _unpack(dlt, shapes), _unpack(mn, shapes), _unpack(vn, shapes)):
            out["grad_" + n], out["delta_" + n], out["new_m_" + n], out["new_v_" + n] = gi, di, mi, vi

    small_parts = [dmod.reshape(DEPTH, 6 * D)] + [gfull[n] for n in SMALL[1:]]
    small_all = _exchange(_pack_rows(small_parts, GRAD_ROWS), "gather_small_grads", True)
    small_shapes = [p[n].shape for n in SMALL]
    sg, sd, sm, sv = _adamw(small_all, *[_pack_rows([p[pre + n] for n in SMALL], GRAD_ROWS) for pre in ("", "m_", "v_")],
                            name="adamw_small")
    emit(SMALL, *[t.reshape(-1) for t in (sg, sd, sm, sv)], small_shapes)

    dmod_all = small_all.reshape(N_DEV, -1)[:, :DEPTH * 6 * D].reshape(N_DEV, DEPTH, 6 * D)
    dmod_sh = lax.dynamic_slice_in_dim(dmod_all, me * ADA_SHARD, ADA_SHARD, axis=2).reshape(N_DEV, DEPTH * ADA_SHARD)
    g_ada = _mm(c_act, _pad_rows(dmod_sh, 2 * N_DEV), "tn", "ada_bwd_w")
    g_ada = jnp.moveaxis(g_ada.reshape(D, DEPTH, ADA_SHARD), 1, 0)
    ag, ad, am, av = _adamw(_pack_rows([g_ada], GRAD_ROWS)[None], *[_pack_rows([p[pre + "w_ada"]], GRAD_ROWS) for pre in ("", "m_", "v_")],
                            name="adamw_ada")
    emit(("w_ada",), *[t.reshape(-1) for t in (ag, ad, am, av)], [p["w_ada"].shape])

    shard_names = [n for n, _ in BIG + TAPS]
    pieces = [_split(gfull[n], ax).reshape(N_DEV, -1) for n, ax in BIG + TAPS]
    recv = _exchange(_pack_cols(pieces, GRAD_ROWS).astype(BF16), "scatter_grads", False)
    bg, bd, bm, bv = _adamw(recv, *[_pack_rows([p[pre + n] for n in shard_names], GRAD_ROWS) for pre in ("", "m_", "v_")],
                            name="adamw_sharded")
    emit(shard_names, *[t.reshape(-1) for t in (bg, bd, bm, bv)], [p[n].shape for n in shard_names])

    t_pieces = [jnp.moveaxis(gfull["w_inT"], 1, 0).reshape(N_DEV, -1),
                jnp.moveaxis(gfull["w_upT"].reshape(DEPTH, N_DEV, UP_SHARD, D), 1, 0).reshape(N_DEV, -1)]
    g_t = _sum8(_exchange(_pack_cols(t_pieces, GRAD_ROWS).astype(BF16), "scatter_grads_t", False), "sum_grads_t")
    g_in_t, g_up_t = _unpack(g_t.reshape(-1), [(DEPTH, IN_SHARD, D), (DEPTH, UP_SHARD, D)])
    for n, gt in (("w_in", g_in_t), ("w_up", g_up_t)):
        shp = p[n].shape
        two_d = (shp[0] * shp[1], shp[2])
        res = _adamw(jnp.swapaxes(gt, 1, 2).reshape((1,) + two_d), *[p[pre + n].reshape(two_d) for pre in ("", "m_", "v_")],
                     name="adamw_" + n)
        for key, t in zip(("grad_", "delta_", "new_m_", "new_v_"), res):
            out[key + n] = t.reshape(shp)
    return out


_ARG_NAMES = ("x", "c", "positions") + WEIGHTS + ("loss_target",) + tuple("m_" + n for n in WEIGHTS) + tuple("v_" + n for n in WEIGHTS)
_OUT_NAMES = ("loss", "grad_x") + tuple(pre + n for pre in ("grad_", "delta_", "new_m_", "new_v_") for n in WEIGHTS)


def kernel(x, c, positions, w_ada, b_ada, w_in, b_in, conv_dw, conv_ln_g, conv_ln_b, w_conv_out, sc_dw, w_sc_out, q_norm_g, w_uq, kv_norm_g, w_ukv, w_mla_out, w_pool, pool_scale, w_pool_out, w_o, ln1_g, ln1_b, w_up, ffn_dw, w_down, ln2_g, ln2_b, loss_target, m_w_ada, m_b_ada, m_w_in, m_b_in, m_conv_dw, m_conv_ln_g, m_conv_ln_b, m_w_conv_out, m_sc_dw, m_w_sc_out, m_q_norm_g, m_w_uq, m_kv_norm_g, m_w_ukv, m_w_mla_out, m_w_pool, m_pool_scale, m_w_pool_out, m_w_o, m_ln1_g, m_ln1_b, m_w_up, m_ffn_dw, m_w_down, m_ln2_g, m_ln2_b, v_w_ada, v_b_ada, v_w_in, v_b_in, v_conv_dw, v_conv_ln_g, v_conv_ln_b, v_w_conv_out, v_sc_dw, v_w_sc_out, v_q_norm_g, v_w_uq, v_kv_norm_g, v_w_ukv, v_w_mla_out, v_w_pool, v_pool_scale, v_w_pool_out, v_w_o, v_ln1_g, v_ln1_b, v_w_up, v_ffn_dw, v_w_down, v_ln2_g, v_ln2_b):
    args = locals()
    out = _step({n: args[n] for n in _ARG_NAMES})
    return tuple(out[n] for n in _OUT_NAMES)
```
